```python
import jax, jax.numpy as jnp
from jax import lax
import numpy as np

D_MODEL = 2048
BATCH = 16
SEQ = 256
DEPTH = 2
DEC_BATCH = 4
DEC_SEQ = 1024
PAST_LEN = 512

GRID_W = 64
HEAD_DIM = 128
ATT_HEADS = 8
ATT_KV_HEADS = 2
ATT_GROUPS = ATT_HEADS // ATT_KV_HEADS
ATT_W = ATT_HEADS * HEAD_DIM
ATT_KV_W = ATT_KV_HEADS * HEAD_DIM
ML_HEADS = 4
ML_DK = 128
ML_DV = 128
ML_W = ML_HEADS * ML_DV
FO_GROUPS = 4
FO_GC = 128
FO_W = FO_GROUPS * FO_GC
D_MIX = ATT_W + ML_W + FO_W
N_GATES = 4 * ML_HEADS
PROJ_SIZES = (ATT_W, ATT_KV_W, ATT_KV_W, ATT_W, ML_W, ML_W, ML_W, ML_W, ML_W, N_GATES, FO_W, FO_W)
D_PROJ = 2 * ATT_W + 2 * ATT_KV_W + 5 * ML_W + N_GATES + 2 * FO_W
CONV_W = 3
CHUNK = 64
Q_BLOCK = 128
ROPE_BASE = 10000.0
EPS = 1e-6

kernel_name = 'hybrid_attn_mlstm_fourier_diffusion_step'


def rms_norm(x, w):
    xf = x.astype(jnp.float32)
    y = xf * lax.rsqrt(jnp.mean(xf * xf, axis=-1, keepdims=True) + EPS)
    return (y * w.astype(jnp.float32)).astype(x.dtype)


def rope_2d(x, n_rows):
    half = x.shape[-1] // 2
    inv_freq = ROPE_BASE ** (-jnp.arange(0, half, 2, dtype=jnp.float32) / half)
    rows = jnp.repeat(jnp.arange(n_rows, dtype=jnp.float32), GRID_W)
    cols = jnp.tile(jnp.arange(GRID_W, dtype=jnp.float32), n_rows)
    xf = x.astype(jnp.float32)

    def rot(xh, pos):
        ang = pos[:, None] * inv_freq
        cos = jnp.cos(ang)[:, None, :]
        sin = jnp.sin(ang)[:, None, :]
        x1, x2 = jnp.split(xh, 2, axis=-1)
        return jnp.concatenate([x1 * cos - x2 * sin, x1 * sin + x2 * cos], axis=-1)

    out = jnp.concatenate([rot(xf[..., :half], rows), rot(xf[..., half:], cols)], axis=-1)
    return out.astype(x.dtype)


def short_conv(x, w, b):
    T = x.shape[1]
    pad = CONV_W // 2
    xp = jnp.pad(x, ((0, 0), (pad, pad), (0, 0)))
    return sum(xp[:, j:j + T] * w[j] for j in range(CONV_W)) + b


def mlstm_chunkwise(q, k, v, ig, lf, C0, n0, m0):
    B, H, T, _ = q.shape
    DV = v.shape[-1]
    nc = T // CHUNK
    to_chunks = lambda a: jnp.moveaxis(a.reshape(B, H, nc, CHUNK, *a.shape[3:]), 2, 0)
    causal = jnp.tril(jnp.ones((CHUNK, CHUNK), dtype=bool))

    def step(carry, inp):
        C, n, m = carry
        qc, kc, vc, ic, fc = inp
        b = jnp.cumsum(fc, axis=-1)
        dmat = jnp.where(causal, b[..., :, None] - b[..., None, :] + ic[..., None, :], -jnp.inf)
        inter = b + m[..., None]
        m_row = jnp.maximum(inter, jnp.max(dmat, axis=-1))
        s = jnp.einsum('bhld,bhsd->bhls', qc, kc) * jnp.exp(dmat - m_row[..., None])
        w_inter = jnp.exp(inter - m_row)
        num = w_inter[..., None] * jnp.einsum('bhld,bhde->bhle', qc, C) + jnp.einsum('bhls,bhse->bhle', s, vc)
        den = w_inter * jnp.einsum('bhld,bhd->bhl', qc, n) + jnp.sum(s, axis=-1)
        h = num / jnp.maximum(jnp.abs(den), jnp.exp(-m_row))[..., None]
        b_last = b[..., -1]
        g = b_last[..., None] - b + ic
        m_new = jnp.maximum(b_last + m, jnp.max(g, axis=-1))
        w_k = jnp.exp(g - m_new[..., None])
        decay = jnp.exp(b_last + m - m_new)
        C_new = decay[..., None, None] * C + jnp.einsum('bhs,bhsd,bhse->bhde', w_k, kc, vc)
        n_new = decay[..., None] * n + jnp.einsum('bhs,bhsd->bhd', w_k, kc)
        return (C_new, n_new, m_new), h

    init = (C0.astype(jnp.float32), n0.astype(jnp.float32), m0.astype(jnp.float32))
    (C, n, m), hs = lax.scan(step, init, (to_chunks(q), to_chunks(k), to_chunks(v), to_chunks(ig), to_chunks(lf)))
    h = jnp.moveaxis(hs, 0, 2).reshape(B, H, T, DV)
    return h, (C, n, m)


def mlstm_bidir(q, k, v, gates, C0, n0, m0):
    B, T = gates.shape[:2]
    g = gates.reshape(B, T, 4, ML_HEADS).transpose(2, 0, 3, 1)
    ig_f, lf_f = g[0], jax.nn.log_sigmoid(g[1])
    ig_b, lf_b = g[2], jax.nn.log_sigmoid(g[3])
    h_f, (Cf, nf, mf) = mlstm_chunkwise(q, k, v, ig_f, lf_f, C0[:, 0], n0[:, 0], m0[:, 0])
    h_b, (Cb, nb, mb) = mlstm_chunkwise(jnp.flip(q, 2), jnp.flip(k, 2), jnp.flip(v, 2),
                                        jnp.flip(ig_b, -1), jnp.flip(lf_b, -1), C0[:, 1], n0[:, 1], m0[:, 1])
    h = h_f + jnp.flip(h_b, 2)
    return h, (jnp.stack([Cf, Cb], axis=1), jnp.stack([nf, nb], axis=1), jnp.stack([mf, mb], axis=1))


def block_attention(q, k, v):
    B, T, KVH, G, HD = q.shape
    nb = T // Q_BLOCK
    qb = jnp.swapaxes(q.reshape(B, nb, Q_BLOCK, KVH, G, HD), 0, 1)

    def one_block(qi):
        s = jnp.einsum('bqhgd,bkhd->bhgqk', qi, k).astype(jnp.float32) * (HD ** -0.5)
        p = jax.nn.softmax(s, axis=-1).astype(v.dtype)
        return jnp.einsum('bhgqk,bkhd->bqhgd', p, v)

    o = lax.map(one_block, qb)
    return jnp.swapaxes(o, 0, 1).reshape(B, T, KVH * G * HD)


def mixer_layer(x, cond, lw, ctx_kv, ml_state, n_rows):
    (norm_w, w_mod, b_mod, w_in, b_if, conv_w, conv_b, q_norm, k_norm, ml_norm, w_fno, w_out) = lw
    B, T, _ = x.shape
    mod = jax.nn.silu(cond) @ w_mod + b_mod
    shift, scale, gate = jnp.split(mod[:, None, :], 3, axis=-1)
    h = rms_norm(x, norm_w) * (1 + scale) + shift
    z = h @ w_in
    (aq, ak, av, ag, mq, mk, mv, mo, mg, mif, fx, fg) = jnp.split(z, np.cumsum(PROJ_SIZES)[:-1].tolist(), axis=-1)

    q = rms_norm(aq.reshape(B, T, ATT_HEADS, HEAD_DIM), q_norm)
    k = rms_norm(ak.reshape(B, T, ATT_KV_HEADS, HEAD_DIM), k_norm)
    v = av.reshape(B, T, ATT_KV_HEADS, HEAD_DIM)
    if ctx_kv is None:
        k_all, v_all = k, v
    else:
        q = rope_2d(q, n_rows)
        k = rope_2d(k, n_rows)
        k_all = jnp.concatenate([ctx_kv[0].astype(k.dtype), k], axis=1)
        v_all = jnp.concatenate([ctx_kv[1].astype(v.dtype), v], axis=1)
    o_att = block_attention(q.reshape(B, T, ATT_KV_HEADS, ATT_GROUPS, HEAD_DIM), k_all, v_all) * jax.nn.silu(ag)

    qk = jax.nn.silu(short_conv(jnp.concatenate([mq, mk], axis=-1), conv_w, conv_b))
    mq_c, mk_c = jnp.split(qk, 2, axis=-1)
    heads = lambda a: a.reshape(B, T, ML_HEADS, -1).transpose(0, 2, 1, 3).astype(jnp.float32)
    gates = mif.astype(jnp.float32) + b_if.astype(jnp.float32)
    hm, (C_f, n_f, m_f) = mlstm_bidir(heads(mq_c), heads(mk_c) * (ML_DK ** -0.5), heads(mv), gates,
                                      ml_state[0], ml_state[1], ml_state[2])
    hm = rms_norm(hm.transpose(0, 2, 1, 3), ml_norm.reshape(ML_HEADS, ML_DV)).astype(x.dtype)
    o_ml = hm.reshape(B, T, ML_W) * jax.nn.sigmoid(mo) * jax.nn.silu(mg)

    f = jnp.fft.fftn(fx.reshape(B, T, FO_GROUPS, FO_GC).astype(jnp.float32), axes=(1, 3), norm='ortho').real
    o_fo = jnp.einsum('btgc,gcd->btgd', f.astype(x.dtype), w_fno).reshape(B, T, FO_W) * jax.nn.silu(fg)

    y = jnp.concatenate([o_att, o_ml, o_fo], axis=-1) @ w_out
    return x + gate * y, (k, v, C_f, n_f, m_f)


def setup_inputs(seed: int = 0) -> dict:
    key = jax.random.key(seed)
    ks = jax.random.split(key, 21)
    nrm = lambda i, shape, s: s * jax.random.normal(ks[i], shape, jnp.float32)
    f_off = jnp.tile(jnp.repeat(jnp.array([0.0, 3.0], jnp.float32), ML_HEADS), 2)
    return {
        'x_prompt': nrm(0, (BATCH, SEQ, D_MODEL), 1.0),
        'x_sample': nrm(1, (DEC_BATCH, DEC_SEQ, D_MODEL), 1.0),
        'cache_k': nrm(2, (DEC_BATCH, DEPTH, PAST_LEN, ATT_KV_HEADS, HEAD_DIM), 1.0),
        'cache_v': nrm(3, (DEC_BATCH, DEPTH, PAST_LEN, ATT_KV_HEADS, HEAD_DIM), 1.0),
        'state_C': nrm(4, (DEC_BATCH, DEPTH, 2, ML_HEADS, ML_DK, ML_DV), 0.5),
        'state_n': nrm(5, (DEC_BATCH, DEPTH, 2, ML_HEADS, ML_DK), 0.5),
        'state_m': nrm(6, (DEC_BATCH, DEPTH, 2, ML_HEADS), 1.0),
        'c': nrm(7, (DEC_BATCH, D_MODEL), 1.0),
        'c_ctx': nrm(8, (D_MODEL,), 1.0),
        'norm_w': 1.0 + nrm(9, (DEPTH, D_MODEL), 0.02),
        'w_mod': nrm(10, (DEPTH, D_MODEL, 3 * D_MODEL), 0.5 * D_MODEL ** -0.5),
        'b_mod': nrm(11, (DEPTH, 3 * D_MODEL), 0.02),
        'w_in': nrm(12, (DEPTH, D_MODEL, D_PROJ), D_MODEL ** -0.5),
        'b_if': f_off + nrm(13, (DEPTH, N_GATES), 0.1),
        'conv_w': nrm(14, (DEPTH, CONV_W, 2 * ML_W), CONV_W ** -0.5),
        'conv_b': nrm(15, (DEPTH, 2 * ML_W), 0.02),
        'q_norm': 1.0 + nrm(16, (DEPTH, HEAD_DIM), 0.02),
        'k_norm': 1.0 + nrm(17, (DEPTH, HEAD_DIM), 0.02),
        'ml_norm': 1.0 + nrm(18, (DEPTH, ML_W), 0.02),
        'w_fno': nrm(19, (DEPTH, FO_GROUPS, FO_GC, FO_GC), FO_GC ** -0.5),
        'w_out': nrm(20, (DEPTH, D_MIX, D_MODEL), D_MIX ** -0.5),
    }


def reference(x_prompt, x_sample, cache_k, cache_v, state_C, state_n, state_m, c, c_ctx,
              norm_w, w_mod, b_mod, w_in, b_if, conv_w, conv_b, q_norm, k_norm, ml_norm, w_fno, w_out):
    Bp = x_prompt.shape[0]
    zero_state = (jnp.zeros((Bp, 2, ML_HEADS, ML_DK, ML_DV), jnp.float32),
                  jnp.zeros((Bp, 2, ML_HEADS, ML_DK), jnp.float32),
                  jnp.zeros((Bp, 2, ML_HEADS), jnp.float32))
    xp = x_prompt
    new_k, new_v, new_C, new_n, new_m = [], [], [], [], []
    for l in range(DEPTH):
        lw = (norm_w[l], w_mod[l], b_mod[l], w_in[l], b_if[l], conv_w[l], conv_b[l],
              q_norm[l], k_norm[l], ml_norm[l], w_fno[l], w_out[l])
        xp, (k_l, v_l, C_l, n_l, m_l) = mixer_layer(xp, c_ctx[None, :], lw, None, zero_state, None)
        new_k.append(k_l)
        new_v.append(v_l)
        new_C.append(C_l)
        new_n.append(n_l)
        new_m.append(m_l)

    n_rows = x_sample.shape[1] // GRID_W
    xs = x_sample
    for l in range(DEPTH):
        lw = (norm_w[l], w_mod[l], b_mod[l], w_in[l], b_if[l], conv_w[l], conv_b[l],
              q_norm[l], k_norm[l], ml_norm[l], w_fno[l], w_out[l])
        xs, _ = mixer_layer(xs, c, lw, (cache_k[:, l], cache_v[:, l]),
                            (state_C[:, l], state_n[:, l], state_m[:, l]), n_rows)

    return (xp, xs, jnp.stack(new_k, axis=1), jnp.stack(new_v, axis=1), jnp.stack(new_C, axis=1),
            jnp.stack(new_n, axis=1), jnp.stack(new_m, axis=1))
```

```python
import functools

import numpy as np
import jax
import jax.numpy as jnp
from jax import lax
from jax.experimental import pallas as pl
from jax.experimental.pallas import tpu as pltpu

D_MODEL = 2048
DEPTH = 2
GRID_W = 64
HEAD_DIM = 128
ATT_HEADS = 8
ATT_KV_HEADS = 2
ATT_GROUPS = ATT_HEADS // ATT_KV_HEADS
ATT_W = ATT_HEADS * HEAD_DIM
ATT_KV_W = ATT_KV_HEADS * HEAD_DIM
ML_HEADS = 4
ML_DK = 128
ML_DV = 128
ML_W = ML_HEADS * ML_DV
FO_GROUPS = 4
FO_GC = 128
FO_W = FO_GROUPS * FO_GC
D_MIX = ATT_W + ML_W + FO_W
N_GATES = 4 * ML_HEADS
D_PROJ = 2 * ATT_W + 2 * ATT_KV_W + 5 * ML_W + N_GATES + 2 * FO_W
CONV_W = 3
ROPE_BASE = 10000.0
EPS = 1e-6

LANES = 128
OFF_AQ = 0
OFF_AK = OFF_AQ + ATT_W
OFF_AV = OFF_AK + ATT_KV_W
OFF_AG = OFF_AV + ATT_KV_W
OFF_MQ = OFF_AG + ATT_W
OFF_MK = OFF_MQ + ML_W
OFF_MV = OFF_MK + ML_W
OFF_MO = OFF_MV + ML_W
OFF_MG = OFF_MO + ML_W
OFF_IF = OFF_MG + ML_W
OFF_FX = OFF_IF + N_GATES
OFF_FG = OFF_FX + FO_W
FO_SHIFT = OFF_FX % LANES

PROJ_TN = 7 * LANES
Z_W = 7 * PROJ_TN
ROW_TILE = 1024
ML_CHUNK = 256
ATT_TQ = 256
VMEM_LIMIT = 56 * 1024 * 1024

BF16 = jnp.bfloat16
F32 = jnp.float32


def _cparams(*sem):
    return pltpu.CompilerParams(dimension_semantics=sem, vmem_limit_bytes=VMEM_LIMIT)


def _silu(x):
    return x * (1.0 / (1.0 + jnp.exp(-x)))


def _sigmoid(x):
    return 1.0 / (1.0 + jnp.exp(-x))


def _rms(x, w):
    ms = jnp.mean(x * x, axis=-1, keepdims=True)
    return x * lax.rsqrt(ms + EPS) * w


MOD_TN = 768


def _mod_kernel(cond_ref, w_ref, b_ref, o_ref):
    a = _silu(cond_ref[...]).astype(BF16)
    o_ref[...] = jnp.dot(a, w_ref[...].astype(BF16), preferred_element_type=F32) + b_ref[...]


def _modulation(cond8, w_mod, b_mod):
    n = 3 * D_MODEL
    return pl.pallas_call(
        _mod_kernel,
        grid=(DEPTH, n // MOD_TN),
        in_specs=[
            pl.BlockSpec((8, D_MODEL), lambda l, j: (0, 0)),
            pl.BlockSpec((None, D_MODEL, MOD_TN), lambda l, j: (l, 0, j)),
            pl.BlockSpec((None, 1, MOD_TN), lambda l, j: (l, 0, j)),
        ],
        out_specs=pl.BlockSpec((None, 8, MOD_TN), lambda l, j: (l, 0, j)),
        out_shape=jax.ShapeDtypeStruct((DEPTH, 8, n), F32),
        compiler_params=_cparams("parallel", "parallel"),
        name="modulation",
    )(cond8, w_mod, b_mod.reshape(DEPTH, 1, n))


def _inproj_kernel(x_ref, mod_ref, nw_ref, w_ref, z_ref, h_ref, *, per_row_mod):
    i = pl.program_id(0)
    j = pl.program_id(1)

    @pl.when(j == 0)
    def _():
        row = 1 + i if per_row_mod else 0
        shift = mod_ref[pl.ds(row, 1), 0:D_MODEL]
        scale = mod_ref[pl.ds(row, 1), D_MODEL:2 * D_MODEL]
        nw = nw_ref[...]
        rows = 256

        def body(r, carry):
            sl = pl.ds(pl.multiple_of(r * rows, rows), rows)
            y = _rms(x_ref[sl, :], nw)
            h_ref[sl, :] = (y * (1.0 + scale) + shift).astype(BF16)
            return carry

        lax.fori_loop(0, ROW_TILE // rows, body, 0)

    z_ref[...] = jnp.dot(h_ref[...], w_ref[...].astype(BF16), preferred_element_type=F32)


def _inproj(x2d, mod_l, norm_w_l, w_in, layer, *, per_row_mod):
    m = x2d.shape[0]
    return pl.pallas_call(
        functools.partial(_inproj_kernel, per_row_mod=per_row_mod),
        grid=(m // ROW_TILE, Z_W // PROJ_TN),
        in_specs=[
            pl.BlockSpec((ROW_TILE, D_MODEL), lambda i, j: (i, 0)),
            pl.BlockSpec((8, 3 * D_MODEL), lambda i, j: (0, 0)),
            pl.BlockSpec((1, D_MODEL), lambda i, j: (0, 0)),
            pl.BlockSpec((None, D_MODEL, PROJ_TN), lambda i, j: (layer, 0, j)),
        ],
        out_specs=pl.BlockSpec((ROW_TILE, PROJ_TN), lambda i, j: (i, j)),
        out_shape=jax.ShapeDtypeStruct((m, Z_W), F32),
        scratch_shapes=[pltpu.VMEM((ROW_TILE, D_MODEL), BF16)],
        compiler_params=_cparams("parallel", "arbitrary"),
        name="inproj",
    )(x2d, mod_l, norm_w_l.reshape(1, D_MODEL), w_in)


def _rope(x, cos, sin):
    lane = lax.broadcasted_iota(jnp.int32, x.shape, 1)
    quarter = HEAD_DIM // 4
    partner = jnp.where((lane % (2 * quarter)) < quarter,
                        pltpu.roll(x, HEAD_DIM - quarter, axis=1), pltpu.roll(x, quarter, axis=1))
    return x * cos + partner * sin


def _attn_kernel(*refs, t, latent):
    if latent:
        (zq_ref, zk_ref, zv_ref, zg_ref, qn_ref, kn_ref, cos_ref, sin_ref, ck_ref, cv_ref,
         o_ref, knew_ref, kb_ref) = refs
    else:
        zq_ref, zk_ref, zv_ref, zg_ref, qn_ref, kn_ref, o_ref, knew_ref, kb_ref = refs
    qi = pl.program_id(2)

    @pl.when(qi == 0)
    def _():
        k = _rms(zk_ref[...], kn_ref[...])
        if latent:
            k = _rope(k, cos_ref[...], sin_ref[...])
        knew_ref[...] = k
        kb_ref[...] = k.astype(BF16)

    sm_scale = HEAD_DIM ** -0.5
    nt = (((1,), (1,)), ((), ()))
    kb = kb_ref[...]
    vb = zv_ref[...].astype(BF16)
    if latent:
        ckb = ck_ref[...].astype(BF16)
        cvb = cv_ref[...].astype(BF16)
        rows = pl.ds(pl.multiple_of(qi * ATT_TQ, ATT_TQ), ATT_TQ)
        cos_q = cos_ref[rows, :]
        sin_q = sin_ref[rows, :]
    for g in range(ATT_GROUPS):
        cols = slice(g * HEAD_DIM, (g + 1) * HEAD_DIM)
        q = _rms(zq_ref[:, cols], qn_ref[...])
        if latent:
            q = _rope(q, cos_q, sin_q)
        qb = q.astype(BF16)
        s = lax.dot_general(qb, kb, nt, preferred_element_type=F32) * sm_scale
        mx = jnp.max(s, axis=-1, keepdims=True)
        if latent:
            sc = lax.dot_general(qb, ckb, nt, preferred_element_type=F32) * sm_scale
            mx = jnp.maximum(mx, jnp.max(sc, axis=-1, keepdims=True))
            pc = jnp.exp(sc - mx)
        p = jnp.exp(s - mx)
        den = jnp.sum(p, axis=-1, keepdims=True)
        o = jnp.dot(p.astype(BF16), vb, preferred_element_type=F32)
        if latent:
            den = den + jnp.sum(pc, axis=-1, keepdims=True)
            o = o + jnp.dot(pc.astype(BF16), cvb, preferred_element_type=F32)
        o = o * (1.0 / den)
        o_ref[:, cols] = (o * _silu(zg_ref[:, cols])).astype(BF16)


def _attention(z, q_norm_l, k_norm_l, *, b, t, rope=None, cache=None):
    latent = rope is not None
    nq = t // ATT_TQ
    gw = ATT_GROUPS * HEAD_DIM
    in_specs = [
        pl.BlockSpec((ATT_TQ, gw), lambda bi, h, qi: (bi * nq + qi, OFF_AQ // gw + h)),
        pl.BlockSpec((t, HEAD_DIM), lambda bi, h, qi: (bi, OFF_AK // HEAD_DIM + h)),
        pl.BlockSpec((t, HEAD_DIM), lambda bi, h, qi: (bi, OFF_AV // HEAD_DIM + h)),
        pl.BlockSpec((ATT_TQ, gw), lambda bi, h, qi: (bi * nq + qi, OFF_AG // gw + h)),
        pl.BlockSpec((1, HEAD_DIM), lambda bi, h, qi: (0, 0)),
        pl.BlockSpec((1, HEAD_DIM), lambda bi, h, qi: (0, 0)),
    ]
    args = [z, z, z, z, q_norm_l.reshape(1, HEAD_DIM), k_norm_l.reshape(1, HEAD_DIM)]
    if latent:
        cos, sin = rope
        ck, cv, layer = cache
        past = ck.shape[2]
        in_specs += [
            pl.BlockSpec((t, HEAD_DIM), lambda bi, h, qi: (0, 0)),
            pl.BlockSpec((t, HEAD_DIM), lambda bi, h, qi: (0, 0)),
            pl.BlockSpec((None, None, past, HEAD_DIM), lambda bi, h, qi: (bi, layer, 0, h)),
            pl.BlockSpec((None, None, past, HEAD_DIM), lambda bi, h, qi: (bi, layer, 0, h)),
        ]
        args += [cos, sin, ck, cv]
    return pl.pallas_call(
        functools.partial(_attn_kernel, t=t, latent=latent),
        grid=(b, ATT_KV_HEADS, nq),
        in_specs=in_specs,
        out_specs=[
            pl.BlockSpec((ATT_TQ, gw), lambda bi, h, qi: (bi * nq + qi, h)),
            pl.BlockSpec((t, HEAD_DIM), lambda bi, h, qi: (bi, h)),
        ],
        out_shape=[
            jax.ShapeDtypeStruct((b * t, ATT_W), BF16),
            jax.ShapeDtypeStruct((b * t, ATT_KV_W), F32),
        ],
        scratch_shapes=[pltpu.VMEM((t, HEAD_DIM), BF16)],
        compiler_params=_cparams("parallel", "parallel", "arbitrary"),
        name="attention",
    )(*args)


def _split3_dot(tri, x):
    hi = x.astype(BF16)
    r1 = x - hi.astype(F32)
    mid = r1.astype(BF16)
    lo = (r1 - mid.astype(F32)).astype(BF16)
    d = lambda a: jnp.dot(tri, a, preferred_element_type=F32)
    return d(hi) + d(mid) + d(lo)


def _shift_rows(x, direction):
    n = x.shape[0]
    row = lax.broadcasted_iota(jnp.int32, x.shape, 0)
    if direction > 0:
        return jnp.where(row == 0, 0.0, pltpu.roll(x, 1, axis=0))
    return jnp.where(row == n - 1, 0.0, pltpu.roll(x, n - 1, axis=0))


def _conv_silu(x, w_ref, b_ref):
    y = _shift_rows(x, 1) * w_ref[0:1, :] + x * w_ref[1:2, :] + _shift_rows(x, -1) * w_ref[2:3, :] + b_ref[...]
    return _silu(y)


def _mlstm_kernel(*refs, t, zero_init):
    if zero_init:
        (zq_ref, zk_ref, zv_ref, zo_ref, zg_ref, zif_ref, bif_ref, cwq_ref, cwk_ref, cbq_ref, cbk_ref, nw_ref,
         o_ref, c_out, n_out, m_out) = refs
    else:
        (zq_ref, zk_ref, zv_ref, zo_ref, zg_ref, zif_ref, bif_ref, cwq_ref, cwk_ref, cbq_ref, cbk_ref, nw_ref,
         c0_ref, n0_ref, m0_ref, o_ref, c_out, n_out, m_out) = refs
    head = pl.program_id(1)
    L = ML_CHUNK
    nc = t // L

    q = _conv_silu(zq_ref[...], cwq_ref, cbq_ref).astype(BF16)
    k = _conv_silu(zk_ref[...], cwk_ref, cbk_ref) * (ML_DK ** -0.5)
    kb = k.astype(BF16)
    lane = lax.broadcasted_iota(jnp.int32, (t, LANES), 1)
    v_ext = jnp.concatenate([zv_ref[...].astype(BF16), (lane == 0).astype(BF16)], axis=1)

    gates = zif_ref[...] + bif_ref[...]
    logsig = jnp.minimum(gates, 0.0) - jnp.log1p(jnp.exp(-jnp.abs(gates)))
    col = lambda a, idx: jnp.sum(jnp.where(lane == idx, a, 0.0), axis=1, keepdims=True)
    g4 = jnp.where(lane == 0, col(gates, head),
                   jnp.where(lane == 1, col(logsig, ML_HEADS + head),
                             jnp.where(lane == 2, col(gates, 2 * ML_HEADS + head),
                                       jnp.where(lane == 3, col(logsig, 3 * ML_HEADS + head), 0.0))))

    r_i = lax.broadcasted_iota(jnp.int32, (L, L), 0)
    c_i = lax.broadcasted_iota(jnp.int32, (L, L), 1)
    lower = r_i >= c_i
    upper = r_i <= c_i
    tril = lower.astype(BF16)
    triu = upper.astype(BF16)
    nt = (((1,), (1,)), ((), ()))

    def chunk_step(c, reverse, c_ext, m):
        sl = slice(c * L, (c + 1) * L)
        qc, kc, kbc, vc, g4c = q[sl], k[sl], kb[sl], v_ext[sl], g4[sl]
        cum = _split3_dot(triu if reverse else tril, g4c)
        g4t = g4c.T
        cumt = cum.T
        i_lane, f_lane = (2, 3) if reverse else (0, 1)
        b_col, b_row = cum[:, f_lane:f_lane + 1], cumt[f_lane:f_lane + 1, :]
        i_col, i_row = g4c[:, i_lane:i_lane + 1], g4t[i_lane:i_lane + 1, :]
        b_end = b_col[0:1, :] if reverse else b_col[L - 1:L, :]
        dmat = jnp.where(upper if reverse else lower, b_col - b_row + i_row, -jnp.inf)
        inter = b_col + m
        m_row = jnp.maximum(inter, jnp.max(dmat, axis=1, keepdims=True))
        qk = lax.dot_general(qc, kbc, nt, preferred_element_type=F32)
        s = (qk * jnp.exp(dmat - m_row)).astype(BF16)
        w_inter = jnp.exp(inter - m_row)
        nd = (w_inter * jnp.dot(qc, c_ext.astype(BF16), preferred_element_type=F32)
              + jnp.dot(s, vc, preferred_element_type=F32))
        num, den = nd[:, :ML_DV], nd[:, ML_DV:ML_DV + 1]
        h = num / jnp.maximum(jnp.abs(den), jnp.exp(-m_row))
        g = b_end - b_col + i_col
        m_new = jnp.maximum(b_end + m, jnp.max(g, axis=0, keepdims=True))
        kw_t = (kc * jnp.exp(g - m_new)).T.astype(BF16)
        c_new = jnp.exp(b_end + m - m_new) * c_ext + jnp.dot(kw_t, vc, preferred_element_type=F32)
        return h, c_new, m_new

    hs = []
    for d in range(2):
        if zero_init:
            c_ext = jnp.zeros((ML_DK, 2 * ML_DV), F32)
            m = jnp.zeros((1, 1), F32)
        else:
            lane_c = lax.broadcasted_iota(jnp.int32, (ML_DK, ML_DV), 1)
            row_c = lax.broadcasted_iota(jnp.int32, (ML_DK, ML_DV), 0)
            n0_col = jnp.sum(jnp.where(lane_c == row_c, n0_ref[d:d + 1, :], 0.0), axis=1, keepdims=True)
            c_ext = jnp.concatenate([c0_ref[d], jnp.where(lane_c == 0, n0_col, 0.0)], axis=1)
            m = m0_ref[d:d + 1, :]
        h_parts = [None] * nc
        for step in range(nc):
            c = nc - 1 - step if d else step
            h_parts[c], c_ext, m = chunk_step(c, bool(d), c_ext, m)
        hs.append(jnp.concatenate(h_parts, axis=0) if nc > 1 else h_parts[0])
        c_out[d] = c_ext[:, :ML_DV]
        n_out[d:d + 1, :] = c_ext[:, ML_DV:].T[0:1, :]
        m_out[d:d + 1, :] = m

    hm = _rms(hs[0] + hs[1], nw_ref[...])
    o_ref[...] = (hm * _sigmoid(zo_ref[...]) * _silu(zg_ref[...])).astype(BF16)


def _mlstm(z, b_if_l, conv_w_l, conv_b_l, ml_norm_l, *, b, t, state=None):
    zero_init = state is None
    blk = lambda off: pl.BlockSpec((t, LANES), lambda bi, h: (bi, off // LANES + h))
    in_specs = [
        blk(OFF_MQ), blk(OFF_MK), blk(OFF_MV), blk(OFF_MO), blk(OFF_MG),
        pl.BlockSpec((t, LANES), lambda bi, h: (bi, OFF_IF // LANES)),
        pl.BlockSpec((1, LANES), lambda bi, h: (0, 0)),
        pl.BlockSpec((CONV_W, LANES), lambda bi, h: (0, h)),
        pl.BlockSpec((CONV_W, LANES), lambda bi, h: (0, ML_HEADS + h)),
        pl.BlockSpec((1, LANES), lambda bi, h: (0, h)),
        pl.BlockSpec((1, LANES), lambda bi, h: (0, ML_HEADS + h)),
        pl.BlockSpec((1, LANES), lambda bi, h: (0, h)),
    ]
    bif = jnp.pad(b_if_l, (0, LANES - N_GATES)).reshape(1, LANES)
    cb = conv_b_l.reshape(1, 2 * ML_W)
    args = [z, z, z, z, z, z, bif, conv_w_l, conv_w_l, cb, cb, ml_norm_l.reshape(1, ML_W)]
    if not zero_init:
        c0, n0, m0, layer = state
        in_specs += [
            pl.BlockSpec((None, None, 2, None, ML_DK, ML_DV), lambda bi, h: (bi, layer, 0, h, 0, 0)),
            pl.BlockSpec((None, None, None, 2, ML_DK), lambda bi, h: (bi, layer, h, 0, 0)),
            pl.BlockSpec((None, None, None, 2, 1), lambda bi, h: (bi, layer, h, 0, 0)),
        ]
        args += [c0, n0, m0]
    return pl.pallas_call(
        functools.partial(_mlstm_kernel, t=t, zero_init=zero_init),
        grid=(b, ML_HEADS),
        in_specs=in_specs,
        out_specs=[
            pl.BlockSpec((t, ML_DV), lambda bi, h: (bi, h)),
            pl.BlockSpec((None, 2, None, ML_DK, ML_DV), lambda bi, h: (bi, 0, h, 0, 0)),
            pl.BlockSpec((None, None, 2, ML_DK), lambda bi, h: (bi, h, 0, 0)),
            pl.BlockSpec((None, None, 2, 1), lambda bi, h: (bi, h, 0, 0)),
        ],
        out_shape=[
            jax.ShapeDtypeStruct((b * t, ML_W), BF16),
            jax.ShapeDtypeStruct((b, 2, ML_HEADS, ML_DK, ML_DV), F32),
            jax.ShapeDtypeStruct((b, ML_HEADS, 2, ML_DK), F32),
            jax.ShapeDtypeStruct((b, ML_HEADS, 2, 1), F32),
        ],
        compiler_params=_cparams("parallel", "parallel"),
        name="mlstm",
    )(*args)


def _dft_tables(t):
    def cs(n):
        idx = np.arange(n, dtype=np.int64)
        ang = 2.0 * np.pi * ((idx[:, None] * idx[None, :]) % n).astype(np.float64) / n
        return np.cos(ang), np.sin(ang)
    cc, sc = cs(FO_GC)
    ct, st = cs(t)
    return (np.concatenate([cc, sc], axis=1).astype(np.float32),
            np.concatenate([ct, -st], axis=1).astype(np.float32))


FO_RT = 256


def _fourier_kernel(zx_ref, zga_ref, zgb_ref, wc_ref, wt_ref, wf_ref, o_ref, y_ref, *, t):
    x = zx_ref[:, FO_SHIFT:FO_SHIFT + FO_W]
    wc = wc_ref[...].astype(BF16)
    for g in range(FO_GROUPS):
        cols = slice(g * FO_GC, (g + 1) * FO_GC)
        y = jnp.dot(x[:, cols].astype(BF16), wc, preferred_element_type=F32)
        y_ref[0:t, cols] = y[:, :FO_GC].astype(BF16)
        y_ref[t:2 * t, cols] = y[:, FO_GC:].astype(BF16)
    scale = (t * FO_GC) ** -0.5
    for r in range(t // FO_RT):
        rows = slice(r * FO_RT, (r + 1) * FO_RT)
        f = jnp.dot(wt_ref[rows, :].astype(BF16), y_ref[...], preferred_element_type=F32) * scale
        fg = jnp.concatenate([zga_ref[rows, FO_SHIFT:], zgb_ref[rows, :FO_SHIFT]], axis=1)
        for g in range(FO_GROUPS):
            cols = slice(g * FO_GC, (g + 1) * FO_GC)
            og = jnp.dot(f[:, cols].astype(BF16), wf_ref[g].astype(BF16), preferred_element_type=F32)
            o_ref[rows, cols] = (og * _silu(fg[:, cols])).astype(BF16)


def _fourier(z, w_fno_l, *, b, t):
    wc, wt = _dft_tables(t)
    xw = FO_W + LANES
    return pl.pallas_call(
        functools.partial(_fourier_kernel, t=t),
        grid=(b,),
        in_specs=[
            pl.BlockSpec((t, xw), lambda bi: (bi, (OFF_FX - FO_SHIFT) // xw)),
            pl.BlockSpec((t, FO_W), lambda bi: (bi, (OFF_FG - FO_SHIFT) // FO_W)),
            pl.BlockSpec((t, LANES), lambda bi: (bi, (OFF_FG - FO_SHIFT + FO_W) // LANES)),
            pl.BlockSpec((FO_GC, 2 * FO_GC), lambda bi: (0, 0)),
            pl.BlockSpec((t, 2 * t), lambda bi: (0, 0)),
            pl.BlockSpec((FO_GROUPS, FO_GC, FO_GC), lambda bi: (0, 0, 0)),
        ],
        out_specs=pl.BlockSpec((t, FO_W), lambda bi: (bi, 0)),
        out_shape=jax.ShapeDtypeStruct((b * t, FO_W), BF16),
        scratch_shapes=[pltpu.VMEM((2 * t, FO_W), BF16)],
        compiler_params=_cparams("parallel"),
        name="fourier",
    )(z, z, z, jnp.asarray(wc), jnp.asarray(wt), w_fno_l)


OUT_TN = 512


def _outproj_kernel(oa_ref, om_ref, of_ref, wa_ref, wm_ref, wf_ref, x_ref, gate_ref, y_ref, *, per_row_mod):
    row = 1 + pl.program_id(0) if per_row_mod else 0
    gate = gate_ref[pl.ds(row, 1), :]
    y = (jnp.dot(oa_ref[...], wa_ref[...].astype(BF16), preferred_element_type=F32)
         + jnp.dot(om_ref[...], wm_ref[...].astype(BF16), preferred_element_type=F32)
         + jnp.dot(of_ref[...], wf_ref[...].astype(BF16), preferred_element_type=F32))
    y_ref[...] = x_ref[...] + gate * y


def _outproj(o_att, o_ml, o_fo, w_out, layer, x2d, mod_l, *, per_row_mod):
    m = x2d.shape[0]
    return pl.pallas_call(
        functools.partial(_outproj_kernel, per_row_mod=per_row_mod),
        grid=(m // ROW_TILE, D_MODEL // OUT_TN),
        in_specs=[
            pl.BlockSpec((ROW_TILE, ATT_W), lambda i, j: (i, 0)),
            pl.BlockSpec((ROW_TILE, ML_W), lambda i, j: (i, 0)),
            pl.BlockSpec((ROW_TILE, FO_W), lambda i, j: (i, 0)),
            pl.BlockSpec((None, ATT_W, OUT_TN), lambda i, j: (layer, 0, j)),
            pl.BlockSpec((None, ML_W, OUT_TN), lambda i, j: (layer, ATT_W // ML_W, j)),
            pl.BlockSpec((None, FO_W, OUT_TN), lambda i, j: (layer, (ATT_W + ML_W) // FO_W, j)),
            pl.BlockSpec((ROW_TILE, OUT_TN), lambda i, j: (i, j)),
            pl.BlockSpec((8, OUT_TN), lambda i, j: (0, 2 * D_MODEL // OUT_TN + j)),
        ],
        out_specs=pl.BlockSpec((ROW_TILE, OUT_TN), lambda i, j: (i, j)),
        out_shape=jax.ShapeDtypeStruct((m, D_MODEL), F32),
        compiler_params=_cparams("parallel", "parallel"),
        name="outproj",
    )(o_att, o_ml, o_fo, w_out, w_out, w_out, x2d, mod_l)


def _rope_tables(t):
    half = HEAD_DIM // 2
    inv_freq = ROPE_BASE ** (-jnp.arange(0, half, 2, dtype=F32) / half)
    n_rows = t // GRID_W
    rows = jnp.repeat(jnp.arange(n_rows, dtype=F32), GRID_W)
    cols = jnp.tile(jnp.arange(GRID_W, dtype=F32), n_rows)
    ar = rows[:, None] * inv_freq
    ac = cols[:, None] * inv_freq
    cos = jnp.concatenate([jnp.cos(ar), jnp.cos(ar), jnp.cos(ac), jnp.cos(ac)], axis=1)
    sin = jnp.concatenate([-jnp.sin(ar), jnp.sin(ar), -jnp.sin(ac), jnp.sin(ac)], axis=1)
    return cos, sin


def _layer(x2d, mod_l, lw, layer, *, b, t, latent, rope=None, cache=None, state=None):
    norm_w, w_in, b_if, conv_w, conv_b, q_norm, k_norm, ml_norm, w_fno, w_out = lw
    z = _inproj(x2d, mod_l, norm_w, w_in, layer, per_row_mod=latent)
    o_att, k_new = _attention(z, q_norm, k_norm, b=b, t=t, rope=rope, cache=cache)
    o_ml, c_f, n_f, m_f = _mlstm(z, b_if, conv_w, conv_b, ml_norm, b=b, t=t, state=state)
    o_fo = _fourier(z, w_fno, b=b, t=t)
    y = _outproj(o_att, o_ml, o_fo, w_out, layer, x2d, mod_l, per_row_mod=latent)
    v_new = z[:, OFF_AV:OFF_AV + ATT_KV_W]
    return y, (k_new, v_new, c_f, n_f, m_f)


def kernel(x_prompt, x_sample, cache_k, cache_v, state_C, state_n, state_m, c, c_ctx, norm_w, w_mod, b_mod, w_in,
           b_if, conv_w, conv_b, q_norm, k_norm, ml_norm, w_fno, w_out):
    bp, tp, _ = x_prompt.shape
    bs, ts, _ = x_sample.shape
    assert tp == ML_CHUNK and ts == ROW_TILE and (bp * tp) % ROW_TILE == 0 and bs + 1 <= 8
    past = cache_k.shape[2]

    cond8 = jnp.concatenate([c_ctx[None, :], c, jnp.zeros((8 - 1 - bs, D_MODEL), F32)], axis=0)
    mod = _modulation(cond8, w_mod, b_mod)

    rope = _rope_tables(ts)
    ck = cache_k.reshape(bs, DEPTH, past, ATT_KV_W)
    cv = cache_v.reshape(bs, DEPTH, past, ATT_KV_W)
    n0 = jnp.swapaxes(state_n, 2, 3)
    m0 = jnp.swapaxes(state_m, 2, 3)[..., None]

    xp = x_prompt.reshape(bp * tp, D_MODEL)
    xs = x_sample.reshape(bs * ts, D_MODEL)
    new_k, new_v, new_c, new_n, new_m = [], [], [], [], []
    for l in range(DEPTH):
        lw = (norm_w[l], w_in, b_if[l], conv_w[l], conv_b[l], q_norm[l], k_norm[l], ml_norm[l], w_fno[l], w_out)
        xp, (k_l, v_l, c_l, n_l, m_l) = _layer(xp, mod[l], lw, l, b=bp, t=tp, latent=False)
        new_k.append(k_l.reshape(bp, tp, ATT_KV_HEADS, HEAD_DIM))
        new_v.append(v_l.reshape(bp, tp, ATT_KV_HEADS, HEAD_DIM))
        new_c.append(c_l)
        new_n.append(jnp.swapaxes(n_l, 1, 2))
        new_m.append(jnp.swapaxes(m_l[..., 0], 1, 2))
        xs, _ = _layer(xs, mod[l], lw, l, b=bs, t=ts, latent=True, rope=rope,
                       cache=(ck, cv, l), state=(state_C, n0, m0, l))
    return (xp.reshape(bp, tp, D_MODEL), xs.reshape(bs, ts, D_MODEL), jnp.stack(new_k, axis=1),
            jnp.stack(new_v, axis=1), jnp.stack(new_c, axis=1), jnp.stack(new_n, axis=1), jnp.stack(new_m, axis=1))
```

```python
import functools

import numpy as np
import jax
import jax.numpy as jnp
from jax import lax
from jax.experimental import pallas as pl
from jax.experimental.pallas import tpu as pltpu

D_MODEL = 2048
DEPTH = 2
GRID_W = 64
HEAD_DIM = 128
ATT_HEADS = 8
ATT_KV_HEADS = 2
ATT_GROUPS = ATT_HEADS // ATT_KV_HEADS
ATT_W = ATT_HEADS * HEAD_DIM
ATT_KV_W = ATT_KV_HEADS * HEAD_DIM
ML_HEADS = 4
ML_DK = 128
ML_DV = 128
ML_W = ML_HEADS * ML_DV
FO_GROUPS = 4
FO_GC = 128
FO_W = FO_GROUPS * FO_GC
D_MIX = ATT_W + ML_W + FO_W
N_GATES = 4 * ML_HEADS
D_PROJ = 2 * ATT_W + 2 * ATT_KV_W + 5 * ML_W + N_GATES + 2 * FO_W
CONV_W = 3
ROPE_BASE = 10000.0
EPS = 1e-6

LANES = 128
OFF_AQ = 0
OFF_AK = OFF_AQ + ATT_W
OFF_AV = OFF_AK + ATT_KV_W
OFF_AG = OFF_AV + ATT_KV_W
OFF_MQ = OFF_AG + ATT_W
OFF_MK = OFF_MQ + ML_W
OFF_MV = OFF_MK + ML_W
OFF_MO = OFF_MV + ML_W
OFF_MG = OFF_MO + ML_W
OFF_IF = OFF_MG + ML_W
OFF_FX = OFF_IF + N_GATES
OFF_FG = OFF_FX + FO_W
FO_SHIFT = OFF_FX % LANES

PROJ_TN = 7 * LANES
Z_W = 7 * PROJ_TN
ROW_TILE = 1024
ML_CHUNK = 256
ATT_TQ = 256
VMEM_LIMIT = 56 * 1024 * 1024

BF16 = jnp.bfloat16
F32 = jnp.float32
NT_DIMS = (((1,), (1,)), ((), ()))


def _cparams(*sem):
    return pltpu.CompilerParams(dimension_semantics=sem, vmem_limit_bytes=VMEM_LIMIT)


def _silu(x):
    return x * (1.0 / (1.0 + jnp.exp(-x)))


def _sigmoid(x):
    return 1.0 / (1.0 + jnp.exp(-x))


def _rms(x, w):
    ms = jnp.mean(x * x, axis=-1, keepdims=True)
    return x * lax.rsqrt(ms + EPS) * w


MOD_TN = 768


def _mod_kernel(cond_ref, w_ref, b_ref, o_ref):
    a = _silu(cond_ref[...]).astype(BF16)
    o_ref[...] = jnp.dot(a, w_ref[...].astype(BF16), preferred_element_type=F32) + b_ref[...]


def _modulation(cond8, w_mod, b_mod):
    n = 3 * D_MODEL
    return pl.pallas_call(
        _mod_kernel,
        grid=(DEPTH, n // MOD_TN),
        in_specs=[
            pl.BlockSpec((8, D_MODEL), lambda l, j: (0, 0)),
            pl.BlockSpec((None, D_MODEL, MOD_TN), lambda l, j: (l, 0, j)),
            pl.BlockSpec((None, 1, MOD_TN), lambda l, j: (l, 0, j)),
        ],
        out_specs=pl.BlockSpec((None, 8, MOD_TN), lambda l, j: (l, 0, j)),
        out_shape=jax.ShapeDtypeStruct((DEPTH, 8, n), F32),
        compiler_params=_cparams("parallel", "parallel"),
        name="modulation",
    )(cond8, w_mod, b_mod.reshape(DEPTH, 1, n))


def _inproj_kernel(x_ref, mod_ref, nw_ref, w_ref, z_ref, h_ref, *, per_row_mod):
    i = pl.program_id(0)
    j = pl.program_id(1)

    @pl.when(j == 0)
    def _():
        row = 1 + i if per_row_mod else 0
        shift = mod_ref[pl.ds(row, 1), 0:D_MODEL]
        scale = mod_ref[pl.ds(row, 1), D_MODEL:2 * D_MODEL]
        nw = nw_ref[...]
        rows = 256

        def body(r, carry):
            sl = pl.ds(pl.multiple_of(r * rows, rows), rows)
            y = _rms(x_ref[sl, :], nw)
            h_ref[sl, :] = (y * (1.0 + scale) + shift).astype(BF16)
            return carry

        lax.fori_loop(0, ROW_TILE // rows, body, 0)

    z_ref[...] = lax.dot_general(h_ref[...], w_ref[...].astype(BF16), NT_DIMS, preferred_element_type=F32)


def _inproj(x2d, mod_l, norm_w_l, w_in_t, layer, *, per_row_mod):
    m = x2d.shape[0]
    return pl.pallas_call(
        functools.partial(_inproj_kernel, per_row_mod=per_row_mod),
        grid=(m // ROW_TILE, Z_W // PROJ_TN),
        in_specs=[
            pl.BlockSpec((ROW_TILE, D_MODEL), lambda i, j: (i, 0)),
            pl.BlockSpec((8, 3 * D_MODEL), lambda i, j: (0, 0)),
            pl.BlockSpec((1, D_MODEL), lambda i, j: (0, 0)),
            pl.BlockSpec((None, PROJ_TN, D_MODEL), lambda i, j: (layer, j, 0)),
        ],
        out_specs=pl.BlockSpec((ROW_TILE, PROJ_TN), lambda i, j: (i, j)),
        out_shape=jax.ShapeDtypeStruct((m, Z_W), F32),
        scratch_shapes=[pltpu.VMEM((ROW_TILE, D_MODEL), BF16)],
        compiler_params=_cparams("parallel", "arbitrary"),
        name="inproj",
    )(x2d, mod_l, norm_w_l.reshape(1, D_MODEL), w_in_t)


def _rope(x, cos, sin):
    lane = lax.broadcasted_iota(jnp.int32, x.shape, 1)
    quarter = HEAD_DIM // 4
    partner = jnp.where((lane % (2 * quarter)) < quarter,
                        pltpu.roll(x, HEAD_DIM - quarter, axis=1), pltpu.roll(x, quarter, axis=1))
    return x * cos + partner * sin


def _attn_kernel(*refs, t, latent):
    if latent:
        (zq_ref, zk_ref, zv_ref, zg_ref, qn_ref, kn_ref, cos_ref, sin_ref, ck_ref, cv_ref,
         o_ref, knew_ref, kb_ref) = refs
    else:
        zq_ref, zk_ref, zv_ref, zg_ref, qn_ref, kn_ref, o_ref, knew_ref, kb_ref = refs
    qi = pl.program_id(2)

    @pl.when(qi == 0)
    def _():
        k = _rms(zk_ref[...], kn_ref[...])
        if latent:
            k = _rope(k, cos_ref[...], sin_ref[...])
        knew_ref[...] = k
        kb_ref[...] = k.astype(BF16)

    sm_scale = HEAD_DIM ** -0.5
    kb = kb_ref[...]
    vb = zv_ref[...].astype(BF16)
    if latent:
        ckb = ck_ref[...].astype(BF16)
        cvb = cv_ref[...].astype(BF16)
        rows = pl.ds(pl.multiple_of(qi * ATT_TQ, ATT_TQ), ATT_TQ)
        cos_q = cos_ref[rows, :]
        sin_q = sin_ref[rows, :]
    for g in range(ATT_GROUPS):
        cols = slice(g * HEAD_DIM, (g + 1) * HEAD_DIM)
        q = _rms(zq_ref[:, cols], qn_ref[...])
        if latent:
            q = _rope(q, cos_q, sin_q)
        qb = q.astype(BF16)
        s = lax.dot_general(qb, kb, NT_DIMS, preferred_element_type=F32) * sm_scale
        mx = jnp.max(s, axis=-1, keepdims=True)
        if latent:
            sc = lax.dot_general(qb, ckb, NT_DIMS, preferred_element_type=F32) * sm_scale
            mx = jnp.maximum(mx, jnp.max(sc, axis=-1, keepdims=True))
            pc = jnp.exp(sc - mx)
        p = jnp.exp(s - mx)
        den = jnp.sum(p, axis=-1, keepdims=True)
        o = jnp.dot(p.astype(BF16), vb, preferred_element_type=F32)
        if latent:
            den = den + jnp.sum(pc, axis=-1, keepdims=True)
            o = o + jnp.dot(pc.astype(BF16), cvb, preferred_element_type=F32)
        o = o * (1.0 / den)
        o_ref[:, cols] = (o * _silu(zg_ref[:, cols])).astype(BF16)


def _attention(z, q_norm_l, k_norm_l, *, b, t, rope=None, cache=None):
    latent = rope is not None
    nq = t // ATT_TQ
    gw = ATT_GROUPS * HEAD_DIM
    in_specs = [
        pl.BlockSpec((ATT_TQ, gw), lambda bi, h, qi: (bi * nq + qi, OFF_AQ // gw + h)),
        pl.BlockSpec((t, HEAD_DIM), lambda bi, h, qi: (bi, OFF_AK // HEAD_DIM + h)),
        pl.BlockSpec((t, HEAD_DIM), lambda bi, h, qi: (bi, OFF_AV // HEAD_DIM + h)),
        pl.BlockSpec((ATT_TQ, gw), lambda bi, h, qi: (bi * nq + qi, OFF_AG // gw + h)),
        pl.BlockSpec((1, HEAD_DIM), lambda bi, h, qi: (0, 0)),
        pl.BlockSpec((1, HEAD_DIM), lambda bi, h, qi: (0, 0)),
    ]
    args = [z, z, z, z, q_norm_l.reshape(1, HEAD_DIM), k_norm_l.reshape(1, HEAD_DIM)]
    if latent:
        cos, sin = rope
        ck, cv, layer = cache
        past = ck.shape[2]
        in_specs += [
            pl.BlockSpec((t, HEAD_DIM), lambda bi, h, qi: (0, 0)),
            pl.BlockSpec((t, HEAD_DIM), lambda bi, h, qi: (0, 0)),
            pl.BlockSpec((None, None, past, HEAD_DIM), lambda bi, h, qi: (bi, layer, 0, h)),
            pl.BlockSpec((None, None, past, HEAD_DIM), lambda bi, h, qi: (bi, layer, 0, h)),
        ]
        args += [cos, sin, ck, cv]
    return pl.pallas_call(
        functools.partial(_attn_kernel, t=t, latent=latent),
        grid=(b, ATT_KV_HEADS, nq),
        in_specs=in_specs,
        out_specs=[
            pl.BlockSpec((ATT_TQ, gw), lambda bi, h, qi: (bi * nq + qi, h)),
            pl.BlockSpec((t, HEAD_DIM), lambda bi, h, qi: (bi, h)),
        ],
        out_shape=[
            jax.ShapeDtypeStruct((b * t, ATT_W), BF16),
            jax.ShapeDtypeStruct((b * t, ATT_KV_W), F32),
        ],
        scratch_shapes=[pltpu.VMEM((t, HEAD_DIM), BF16)],
        compiler_params=_cparams("parallel", "parallel", "arbitrary"),
        name="attention",
    )(*args)


def _split3_dot(tri, x):
    hi = x.astype(BF16)
    r1 = x - hi.astype(F32)
    mid = r1.astype(BF16)
    lo = (r1 - mid.astype(F32)).astype(BF16)
    d = lambda a: jnp.dot(tri, a, preferred_element_type=F32)
    return d(hi) + d(mid) + d(lo)


def _shift_rows(x, direction):
    n = x.shape[0]
    row = lax.broadcasted_iota(jnp.int32, x.shape, 0)
    if direction > 0:
        return jnp.where(row == 0, 0.0, pltpu.roll(x, 1, axis=0))
    return jnp.where(row == n - 1, 0.0, pltpu.roll(x, n - 1, axis=0))


def _conv_silu(x, w, b):
    y = _shift_rows(x, 1) * w[0:1, :] + x * w[1:2, :] + _shift_rows(x, -1) * w[2:3, :] + b
    return _silu(y)


def _mlstm_kernel(*refs, t, zero_init):
    if zero_init:
        z_ref, zif_ref, bif_ref, cw_ref, cb_ref, nw_ref, o_ref, c_out, n_out, m_out = refs
    else:
        z_ref, zif_ref, bif_ref, cw_ref, cb_ref, nw_ref, c0_ref, n0_ref, m0_ref, o_ref, c_out, n_out, m_out = refs
    L = ML_CHUNK
    nc = t // L
    chunk = lambda a, c: a[c * L:(c + 1) * L]

    gates = zif_ref[...] + bif_ref[...]
    logsig = jnp.minimum(gates, 0.0) - jnp.log1p(jnp.exp(-jnp.abs(gates)))
    gates_t = gates.T
    r_i = lax.broadcasted_iota(jnp.int32, (L, L), 0)
    c_i = lax.broadcasted_iota(jnp.int32, (L, L), 1)
    masks = (r_i >= c_i, r_i <= c_i)
    tris = tuple(mk.astype(BF16) for mk in masks)
    cums = [[_split3_dot(tris[d], chunk(logsig, c)) for c in range(nc)] for d in range(2)]
    cums_t = [[x.T for x in row] for row in cums]
    ones_col = (lax.broadcasted_iota(jnp.int32, (t, LANES), 1) == 0).astype(BF16)
    eye = (lax.broadcasted_iota(jnp.int32, (ML_DK, ML_DV), 0) == lax.broadcasted_iota(jnp.int32, (ML_DK, ML_DV), 1))
    lane0 = lax.broadcasted_iota(jnp.int32, (ML_DK, ML_DV), 1) == 0

    def local_part(qk_c, k_c, v_c, b_col, b_row, i_col, i_row, d):
        dmat = jnp.where(masks[d], b_col - b_row + i_row, -jnp.inf)
        m_loc = jnp.max(dmat, axis=1, keepdims=True)
        s = (qk_c * jnp.exp(dmat - m_loc)).astype(BF16)
        bm = jnp.dot(s, v_c, preferred_element_type=F32)
        b_end = b_col[0:1, :] if d else b_col[L - 1:L, :]
        g = b_end - b_col + i_col
        g_max = jnp.max(g, axis=0, keepdims=True)
        u = jnp.dot((k_c * jnp.exp(g - g_max)).T.astype(BF16), v_c, preferred_element_type=F32)
        return m_loc, bm, b_end, g_max, u

    def scan_step(q_c, b_col, loc, c_ext, m):
        m_loc, bm, b_end, g_max, u = loc
        inter = b_col + m
        m_row = jnp.maximum(inter, m_loc)
        nd = jnp.exp(m_loc - m_row) * bm
        if c_ext is not None:
            nd = nd + jnp.exp(inter - m_row) * jnp.dot(q_c, c_ext.astype(BF16), preferred_element_type=F32)
        h = nd[:, :ML_DV] / jnp.maximum(jnp.abs(nd[:, ML_DV:ML_DV + 1]), jnp.exp(-m_row))
        m_new = jnp.maximum(b_end + m, g_max)
        c_new = jnp.exp(g_max - m_new) * u
        if c_ext is not None:
            c_new = c_new + jnp.exp(b_end + m - m_new) * c_ext
        return h, c_new, m_new

    for hd in range(ML_HEADS):
        cols = slice(hd * ML_DK, (hd + 1) * ML_DK)
        kcols = slice(ML_W + hd * ML_DK, ML_W + (hd + 1) * ML_DK)
        vcols = slice(2 * ML_W + hd * ML_DV, 2 * ML_W + (hd + 1) * ML_DV)
        ocols = slice(3 * ML_W + hd * ML_DV, 3 * ML_W + (hd + 1) * ML_DV)
        gcols = slice(4 * ML_W + hd * ML_DV, 4 * ML_W + (hd + 1) * ML_DV)
        q = _conv_silu(z_ref[:, cols], cw_ref[:, cols], cb_ref[:, cols]).astype(BF16)
        k = _conv_silu(z_ref[:, kcols], cw_ref[:, kcols], cb_ref[:, kcols]) * (ML_DK ** -0.5)
        kb = k.astype(BF16)
        v_ext = jnp.concatenate([z_ref[:, vcols].astype(BF16), ones_col], axis=1)
        qk = [lax.dot_general(chunk(q, c), chunk(kb, c), NT_DIMS, preferred_element_type=F32) for c in range(nc)]

        h_sum = None
        for d in range(2):
            i_lane = 2 * d * ML_HEADS + hd
            f_lane = (2 * d + 1) * ML_HEADS + hd
            b_cols = [cums[d][c][:, f_lane:f_lane + 1] for c in range(nc)]
            locs = [local_part(qk[c], chunk(k, c), chunk(v_ext, c), b_cols[c],
                               cums_t[d][c][f_lane:f_lane + 1, :], chunk(gates, c)[:, i_lane:i_lane + 1],
                               gates_t[i_lane:i_lane + 1, c * L:(c + 1) * L], d) for c in range(nc)]
            if zero_init:
                c_ext, m = None, jnp.zeros((1, 1), F32)
            else:
                n0_col = jnp.sum(jnp.where(eye, n0_ref[d, hd:hd + 1, :], 0.0), axis=1, keepdims=True)
                c_ext = jnp.concatenate([c0_ref[d, hd], jnp.where(lane0, n0_col, 0.0)], axis=1)
                m = m0_ref[d:d + 1, hd:hd + 1]
            h_parts = [None] * nc
            for step in range(nc):
                c = nc - 1 - step if d else step
                h_parts[c], c_ext, m = scan_step(chunk(q, c), b_cols[c], locs[c], c_ext, m)
            h_dir = jnp.concatenate(h_parts, axis=0) if nc > 1 else h_parts[0]
            h_sum = h_dir if h_sum is None else h_sum + h_dir
            c_out[d, hd] = c_ext[:, :ML_DV]
            n_out[d, hd:hd + 1, :] = c_ext[:, ML_DV:].T[0:1, :]
            m_out[d:d + 1, hd:hd + 1] = m

        hm = _rms(h_sum, nw_ref[:, cols])
        o_ref[:, cols] = (hm * _sigmoid(z_ref[:, ocols]) * _silu(z_ref[:, gcols])).astype(BF16)


def _mlstm(z, b_if_l, conv_w_l, conv_b_l, ml_norm_l, *, b, t, state=None):
    zero_init = state is None
    zw = 5 * ML_W
    in_specs = [
        pl.BlockSpec((t, zw), lambda bi: (bi, OFF_MQ // zw)),
        pl.BlockSpec((t, LANES), lambda bi: (bi, OFF_IF // LANES)),
        pl.BlockSpec((1, LANES), lambda bi: (0, 0)),
        pl.BlockSpec((CONV_W, 2 * ML_W), lambda bi: (0, 0)),
        pl.BlockSpec((1, 2 * ML_W), lambda bi: (0, 0)),
        pl.BlockSpec((1, ML_W), lambda bi: (0, 0)),
    ]
    bif = jnp.pad(b_if_l, (0, LANES - N_GATES)).reshape(1, LANES)
    args = [z, z, bif, conv_w_l, conv_b_l.reshape(1, 2 * ML_W), ml_norm_l.reshape(1, ML_W)]
    if not zero_init:
        c0, n0, m0, layer = state
        in_specs += [
            pl.BlockSpec((None, None, 2, ML_HEADS, ML_DK, ML_DV), lambda bi: (bi, layer, 0, 0, 0, 0)),
            pl.BlockSpec((None, None, 2, ML_HEADS, ML_DK), lambda bi: (bi, layer, 0, 0, 0)),
            pl.BlockSpec((None, None, 2, ML_HEADS), lambda bi: (bi, layer, 0, 0)),
        ]
        args += [c0, n0, m0]
    return pl.pallas_call(
        functools.partial(_mlstm_kernel, t=t, zero_init=zero_init),
        grid=(b,),
        in_specs=in_specs,
        out_specs=[
            pl.BlockSpec((t, ML_W), lambda bi: (bi, 0)),
            pl.BlockSpec((None, 2, ML_HEADS, ML_DK, ML_DV), lambda bi: (bi, 0, 0, 0, 0)),
            pl.BlockSpec((None, 2, ML_HEADS, ML_DK), lambda bi: (bi, 0, 0, 0)),
            pl.BlockSpec((None, 2, ML_HEADS), lambda bi: (bi, 0, 0)),
        ],
        out_shape=[
            jax.ShapeDtypeStruct((b * t, ML_W), BF16),
            jax.ShapeDtypeStruct((b, 2, ML_HEADS, ML_DK, ML_DV), F32),
            jax.ShapeDtypeStruct((b, 2, ML_HEADS, ML_DK), F32),
            jax.ShapeDtypeStruct((b, 2, ML_HEADS), F32),
        ],
        compiler_params=_cparams("parallel"),
        name="mlstm",
    )(*args)


def _dft_tables(t):
    def cs(n):
        idx = np.arange(n, dtype=np.int64)
        ang = 2.0 * np.pi * ((idx[:, None] * idx[None, :]) % n).astype(np.float64) / n
        return np.cos(ang), np.sin(ang)
    cc, sc = cs(FO_GC)
    ct, st = cs(t)
    return (np.concatenate([cc, sc], axis=1).astype(np.float32),
            np.concatenate([ct, -st], axis=1).astype(np.float32))


FO_RT = 256


def _fourier_kernel(zx_ref, zga_ref, zgb_ref, wc_ref, wt_ref, wf_ref, o_ref, y_ref, *, t):
    x = zx_ref[:, FO_SHIFT:FO_SHIFT + FO_W]
    wc = wc_ref[...].astype(BF16)
    for g in range(FO_GROUPS):
        cols = slice(g * FO_GC, (g + 1) * FO_GC)
        y = jnp.dot(x[:, cols].astype(BF16), wc, preferred_element_type=F32)
        y_ref[0:t, cols] = y[:, :FO_GC].astype(BF16)
        y_ref[t:2 * t, cols] = y[:, FO_GC:].astype(BF16)
    scale = (t * FO_GC) ** -0.5
    for r in range(t // FO_RT):
        rows = slice(r * FO_RT, (r + 1) * FO_RT)
        f = jnp.dot(wt_ref[rows, :].astype(BF16), y_ref[...], preferred_element_type=F32) * scale
        fg = jnp.concatenate([zga_ref[rows, FO_SHIFT:], zgb_ref[rows, :FO_SHIFT]], axis=1)
        for g in range(FO_GROUPS):
            cols = slice(g * FO_GC, (g + 1) * FO_GC)
            og = jnp.dot(f[:, cols].astype(BF16), wf_ref[g].astype(BF16), preferred_element_type=F32)
            o_ref[rows, cols] = (og * _silu(fg[:, cols])).astype(BF16)


def _fourier(z, w_fno_l, *, b, t):
    wc, wt = _dft_tables(t)
    xw = FO_W + LANES
    return pl.pallas_call(
        functools.partial(_fourier_kernel, t=t),
        grid=(b,),
        in_specs=[
            pl.BlockSpec((t, xw), lambda bi: (bi, (OFF_FX - FO_SHIFT) // xw)),
            pl.BlockSpec((t, FO_W), lambda bi: (bi, (OFF_FG - FO_SHIFT) // FO_W)),
            pl.BlockSpec((t, LANES), lambda bi: (bi, (OFF_FG - FO_SHIFT + FO_W) // LANES)),
            pl.BlockSpec((FO_GC, 2 * FO_GC), lambda bi: (0, 0)),
            pl.BlockSpec((t, 2 * t), lambda bi: (0, 0)),
            pl.BlockSpec((FO_GROUPS, FO_GC, FO_GC), lambda bi: (0, 0, 0)),
        ],
        out_specs=pl.BlockSpec((t, FO_W), lambda bi: (bi, 0)),
        out_shape=jax.ShapeDtypeStruct((b * t, FO_W), BF16),
        scratch_shapes=[pltpu.VMEM((2 * t, FO_W), BF16)],
        compiler_params=_cparams("parallel"),
        name="fourier",
    )(z, z, z, jnp.asarray(wc), jnp.asarray(wt), w_fno_l)


OUT_TN = 512


def _outproj_kernel(oa_ref, om_ref, of_ref, wa_ref, wm_ref, wf_ref, x_ref, gate_ref, y_ref, *, per_row_mod):
    row = 1 + pl.program_id(0) if per_row_mod else 0
    gate = gate_ref[pl.ds(row, 1), :]
    y = (jnp.dot(oa_ref[...], wa_ref[...].astype(BF16), preferred_element_type=F32)
         + jnp.dot(om_ref[...], wm_ref[...].astype(BF16), preferred_element_type=F32)
         + jnp.dot(of_ref[...], wf_ref[...].astype(BF16), preferred_element_type=F32))
    y_ref[...] = x_ref[...] + gate * y


def _outproj(o_att, o_ml, o_fo, w_out, layer, x2d, mod_l, *, per_row_mod):
    m = x2d.shape[0]
    return pl.pallas_call(
        functools.partial(_outproj_kernel, per_row_mod=per_row_mod),
        grid=(m // ROW_TILE, D_MODEL // OUT_TN),
        in_specs=[
            pl.BlockSpec((ROW_TILE, ATT_W), lambda i, j: (i, 0)),
            pl.BlockSpec((ROW_TILE, ML_W), lambda i, j: (i, 0)),
            pl.BlockSpec((ROW_TILE, FO_W), lambda i, j: (i, 0)),
            pl.BlockSpec((None, ATT_W, OUT_TN), lambda i, j: (layer, 0, j)),
            pl.BlockSpec((None, ML_W, OUT_TN), lambda i, j: (layer, ATT_W // ML_W, j)),
            pl.BlockSpec((None, FO_W, OUT_TN), lambda i, j: (layer, (ATT_W + ML_W) // FO_W, j)),
            pl.BlockSpec((ROW_TILE, OUT_TN), lambda i, j: (i, j)),
            pl.BlockSpec((8, OUT_TN), lambda i, j: (0, 2 * D_MODEL // OUT_TN + j)),
        ],
        out_specs=pl.BlockSpec((ROW_TILE, OUT_TN), lambda i, j: (i, j)),
        out_shape=jax.ShapeDtypeStruct((m, D_MODEL), F32),
        compiler_params=_cparams("parallel", "parallel"),
        name="outproj",
    )(o_att, o_ml, o_fo, w_out, w_out, w_out, x2d, mod_l)


def _rope_tables(t):
    half = HEAD_DIM // 2
    inv_freq = ROPE_BASE ** (-jnp.arange(0, half, 2, dtype=F32) / half)
    n_rows = t // GRID_W
    rows = jnp.repeat(jnp.arange(n_rows, dtype=F32), GRID_W)
    cols = jnp.tile(jnp.arange(GRID_W, dtype=F32), n_rows)
    ar = rows[:, None] * inv_freq
    ac = cols[:, None] * inv_freq
    cos = jnp.concatenate([jnp.cos(ar), jnp.cos(ar), jnp.cos(ac), jnp.cos(ac)], axis=1)
    sin = jnp.concatenate([-jnp.sin(ar), jnp.sin(ar), -jnp.sin(ac), jnp.sin(ac)], axis=1)
    return cos, sin


def _layer(x2d, mod_l, lw, layer, *, b, t, latent, rope=None, cache=None, state=None):
    norm_w, w_in_t, b_if, conv_w, conv_b, q_norm, k_norm, ml_norm, w_fno, w_out = lw
    z = _inproj(x2d, mod_l, norm_w, w_in_t, layer, per_row_mod=latent)
    o_att, k_new = _attention(z, q_norm, k_norm, b=b, t=t, rope=rope, cache=cache)
    o_ml, c_f, n_f, m_f = _mlstm(z, b_if, conv_w, conv_b, ml_norm, b=b, t=t, state=state)
    o_fo = _fourier(z, w_fno, b=b, t=t)
    y = _outproj(o_att, o_ml, o_fo, w_out, layer, x2d, mod_l, per_row_mod=latent)
    v_new = z[:, OFF_AV:OFF_AV + ATT_KV_W]
    return y, (k_new, v_new, c_f, n_f, m_f)


def kernel(x_prompt, x_sample, cache_k, cache_v, state_C, state_n, state_m, c, c_ctx, norm_w, w_mod, b_mod, w_in,
           b_if, conv_w, conv_b, q_norm, k_norm, ml_norm, w_fno, w_out):
    bp, tp, _ = x_prompt.shape
    bs, ts, _ = x_sample.shape
    assert tp == ML_CHUNK and ts == ROW_TILE and (bp * tp) % ROW_TILE == 0 and bs + 1 <= 8
    past = cache_k.shape[2]

    cond8 = jnp.concatenate([c_ctx[None, :], c, jnp.zeros((8 - 1 - bs, D_MODEL), F32)], axis=0)
    mod = _modulation(cond8, w_mod, b_mod)

    rope = _rope_tables(ts)
    ck = cache_k.reshape(bs, DEPTH, past, ATT_KV_W)
    cv = cache_v.reshape(bs, DEPTH, past, ATT_KV_W)
    w_in_t = jnp.swapaxes(w_in, 1, 2)

    xp = x_prompt.reshape(bp * tp, D_MODEL)
    xs = x_sample.reshape(bs * ts, D_MODEL)
    new_k, new_v, new_c, new_n, new_m = [], [], [], [], []
    for l in range(DEPTH):
        lw = (norm_w[l], w_in_t, b_if[l], conv_w[l], conv_b[l], q_norm[l], k_norm[l], ml_norm[l], w_fno[l], w_out)
        xp, (k_l, v_l, c_l, n_l, m_l) = _layer(xp, mod[l], lw, l, b=bp, t=tp, latent=False)
        new_k.append(k_l.reshape(bp, tp, ATT_KV_HEADS, HEAD_DIM))
        new_v.append(v_l.reshape(bp, tp, ATT_KV_HEADS, HEAD_DIM))
        new_c.append(c_l)
        new_n.append(n_l)
        new_m.append(m_l)
        xs, _ = _layer(xs, mod[l], lw, l, b=bs, t=ts, latent=True, rope=rope,
                       cache=(ck, cv, l), state=(state_C, state_n, state_m, l))
    return (xp.reshape(bp, tp, D_MODEL), xs.reshape(bs, ts, D_MODEL), jnp.stack(new_k, axis=1),
            jnp.stack(new_v, axis=1), jnp.stack(new_c, axis=1), jnp.stack(new_n, axis=1), jnp.stack(new_m, axis=1))
```

```python
import functools

import numpy as np
import jax
import jax.numpy as jnp
from jax import lax
from jax.experimental import pallas as pl
from jax.experimental.pallas import tpu as pltpu

D_MODEL = 2048
DEPTH = 2
GRID_W = 64
HEAD_DIM = 128
ATT_HEADS = 8
ATT_KV_HEADS = 2
ATT_GROUPS = ATT_HEADS // ATT_KV_HEADS
ATT_W = ATT_HEADS * HEAD_DIM
ATT_KV_W = ATT_KV_HEADS * HEAD_DIM
ML_HEADS = 4
ML_DK = 128
ML_DV = 128
ML_W = ML_HEADS * ML_DV
FO_GROUPS = 4
FO_GC = 128
FO_W = FO_GROUPS * FO_GC
D_MIX = ATT_W + ML_W + FO_W
N_GATES = 4 * ML_HEADS
D_PROJ = 2 * ATT_W + 2 * ATT_KV_W + 5 * ML_W + N_GATES + 2 * FO_W
CONV_W = 3
ROPE_BASE = 10000.0
EPS = 1e-6

LANES = 128
OFF_AQ = 0
OFF_AK = OFF_AQ + ATT_W
OFF_AV = OFF_AK + ATT_KV_W
OFF_AG = OFF_AV + ATT_KV_W
OFF_MQ = OFF_AG + ATT_W
OFF_MK = OFF_MQ + ML_W
OFF_MV = OFF_MK + ML_W
OFF_MO = OFF_MV + ML_W
OFF_MG = OFF_MO + ML_W
OFF_IF = OFF_MG + ML_W
OFF_FX = OFF_IF + N_GATES
OFF_FG = OFF_FX + FO_W
FO_SHIFT = OFF_FX % LANES

PROJ_TN = 7 * LANES
Z_W = 7 * PROJ_TN
ROW_TILE = 1024
ML_CHUNK = 256
ATT_TQ = 256
VMEM_LIMIT = 56 * 1024 * 1024

BF16 = jnp.bfloat16
F32 = jnp.float32
NT_DIMS = (((1,), (1,)), ((), ()))


def _cparams(*sem):
    return pltpu.CompilerParams(dimension_semantics=sem, vmem_limit_bytes=VMEM_LIMIT)


def _silu(x):
    return x * (1.0 / (1.0 + jnp.exp(-x)))


def _sigmoid(x):
    return 1.0 / (1.0 + jnp.exp(-x))


def _rms(x, w):
    ms = jnp.mean(x * x, axis=-1, keepdims=True)
    return x * lax.rsqrt(ms + EPS) * w


MOD_TN = 768


def _mod_kernel(cond_ref, w_ref, b_ref, o_ref):
    a = _silu(cond_ref[...]).astype(BF16)
    o_ref[...] = jnp.dot(a, w_ref[...].astype(BF16), preferred_element_type=F32) + b_ref[...]


def _modulation(cond8, w_mod, b_mod):
    n = 3 * D_MODEL
    return pl.pallas_call(
        _mod_kernel,
        grid=(DEPTH, n // MOD_TN),
        in_specs=[
            pl.BlockSpec((8, D_MODEL), lambda l, j: (0, 0)),
            pl.BlockSpec((None, D_MODEL, MOD_TN), lambda l, j: (l, 0, j)),
            pl.BlockSpec((None, 1, MOD_TN), lambda l, j: (l, 0, j)),
        ],
        out_specs=pl.BlockSpec((None, 8, MOD_TN), lambda l, j: (l, 0, j)),
        out_shape=jax.ShapeDtypeStruct((DEPTH, 8, n), F32),
        compiler_params=_cparams("parallel", "parallel"),
        name="modulation",
    )(cond8, w_mod, b_mod.reshape(DEPTH, 1, n))


def _inproj_kernel(x_ref, mod_ref, nw_ref, w_ref, z_ref, h_ref, *, per_row_mod):
    i = pl.program_id(0)
    j = pl.program_id(1)

    @pl.when(j == 0)
    def _():
        row = 1 + i if per_row_mod else 0
        shift = mod_ref[pl.ds(row, 1), 0:D_MODEL]
        scale = mod_ref[pl.ds(row, 1), D_MODEL:2 * D_MODEL]
        nw = nw_ref[...]
        rows = 256

        def body(r, carry):
            sl = pl.ds(pl.multiple_of(r * rows, rows), rows)
            y = _rms(x_ref[sl, :], nw)
            h_ref[sl, :] = (y * (1.0 + scale) + shift).astype(BF16)
            return carry

        lax.fori_loop(0, ROW_TILE // rows, body, 0)

    z = lax.dot_general(h_ref[...], w_ref[...].astype(BF16), NT_DIMS, preferred_element_type=F32)
    last = Z_W // PROJ_TN - 1

    @pl.when(j < last)
    def _():
        z_ref[...] = z

    @pl.when(j == last)
    def _():
        col = lax.broadcasted_iota(jnp.int32, z.shape, 1)
        z_ref[...] = jnp.where(col < D_PROJ - last * PROJ_TN, z, 0.0)


def _inproj(x2d, mod_l, norm_w_l, w_in_t, layer, *, per_row_mod):
    m = x2d.shape[0]
    return pl.pallas_call(
        functools.partial(_inproj_kernel, per_row_mod=per_row_mod),
        grid=(m // ROW_TILE, Z_W // PROJ_TN),
        in_specs=[
            pl.BlockSpec((ROW_TILE, D_MODEL), lambda i, j: (i, 0)),
            pl.BlockSpec((8, 3 * D_MODEL), lambda i, j: (0, 0)),
            pl.BlockSpec((1, D_MODEL), lambda i, j: (0, 0)),
            pl.BlockSpec((None, PROJ_TN, D_MODEL), lambda i, j: (layer, j, 0)),
        ],
        out_specs=pl.BlockSpec((ROW_TILE, PROJ_TN), lambda i, j: (i, j)),
        out_shape=jax.ShapeDtypeStruct((m, Z_W), F32),
        scratch_shapes=[pltpu.VMEM((ROW_TILE, D_MODEL), BF16)],
        compiler_params=_cparams("parallel", "arbitrary"),
        name="inproj",
    )(x2d, mod_l, norm_w_l.reshape(1, D_MODEL), w_in_t)


def _rope(x, cos, sin):
    lane = lax.broadcasted_iota(jnp.int32, x.shape, 1)
    quarter = HEAD_DIM // 4
    partner = jnp.where((lane % (2 * quarter)) < quarter,
                        pltpu.roll(x, HEAD_DIM - quarter, axis=1), pltpu.roll(x, quarter, axis=1))
    return x * cos + partner * sin


LOG2E = 1.4426950408889634


def _attn_kernel(*refs, t, hk, latent):
    zq_ref, zk_ref, zv_ref = refs[:3]
    zg_refs = refs[3:3 + hk]
    rest = refs[3 + hk:]
    if latent:
        qn_ref, kn_ref, cos_ref, sin_ref, ck_ref, cv_ref, o_ref, knew_ref, kall_ref, vall_ref = rest
        past = ck_ref.shape[0]
    else:
        qn_ref, kn_ref, o_ref, knew_ref, kall_ref, vall_ref = rest
        past = 0
    qi = pl.program_id(2)

    @pl.when(qi == 0)
    def _():
        for j in range(hk):
            kc = slice(j * HEAD_DIM, (j + 1) * HEAD_DIM)
            k = _rms(zk_ref[:, kc], kn_ref[...])
            if latent:
                k = _rope(k, cos_ref[...], sin_ref[...])
                kall_ref[j, 0:past, :] = ck_ref[:, kc].astype(BF16)
                vall_ref[j, 0:past, :] = cv_ref[:, kc].astype(BF16)
            knew_ref[:, kc] = k
            kall_ref[j, past:past + t, :] = k.astype(BF16)
            vall_ref[j, past:past + t, :] = zv_ref[:, kc].astype(BF16)

    q_scale = (HEAD_DIM ** -0.5) * LOG2E
    if latent:
        rows = pl.ds(pl.multiple_of(qi * ATT_TQ, ATT_TQ), ATT_TQ)
        cos_q = cos_ref[rows, :]
        sin_q = sin_ref[rows, :]
    for j in range(hk):
        kb = kall_ref[j]
        vb = vall_ref[j]
        for g in range(ATT_GROUPS):
            cols = slice((j * ATT_GROUPS + g) * HEAD_DIM, (j * ATT_GROUPS + g + 1) * HEAD_DIM)
            gcols = slice(g * HEAD_DIM, (g + 1) * HEAD_DIM)
            q = _rms(zq_ref[:, cols], qn_ref[...])
            if latent:
                q = _rope(q, cos_q, sin_q)
            qb = (q * q_scale).astype(BF16)
            s = lax.dot_general(qb, kb, NT_DIMS, preferred_element_type=F32)
            p = jnp.exp2(s - jnp.max(s, axis=-1, keepdims=True))
            den = jnp.sum(p, axis=-1, keepdims=True)
            o = jnp.dot(p.astype(BF16), vb, preferred_element_type=F32) * (1.0 / den)
            o_ref[:, cols] = (o * _silu(zg_refs[j][:, gcols])).astype(BF16)


def _attention(z, q_norm_l, k_norm_l, *, b, t, rope=None, cache=None):
    latent = rope is not None
    hk = 1 if latent else ATT_KV_HEADS
    nq = t // ATT_TQ
    gw = ATT_GROUPS * HEAD_DIM
    in_specs = [
        pl.BlockSpec((ATT_TQ, hk * gw), lambda bi, kg, qi: (bi * nq + qi, OFF_AQ // (hk * gw) + kg)),
        pl.BlockSpec((t, hk * HEAD_DIM), lambda bi, kg, qi: (bi, OFF_AK // (hk * HEAD_DIM) + kg)),
        pl.BlockSpec((t, hk * HEAD_DIM), lambda bi, kg, qi: (bi, OFF_AV // (hk * HEAD_DIM) + kg)),
    ]
    in_specs += [pl.BlockSpec((ATT_TQ, gw), lambda bi, kg, qi, j=j: (bi * nq + qi, OFF_AG // gw + kg * hk + j))
                 for j in range(hk)]
    in_specs += [pl.BlockSpec((1, HEAD_DIM), lambda bi, kg, qi: (0, 0))] * 2
    args = [z] * (3 + hk) + [q_norm_l.reshape(1, HEAD_DIM), k_norm_l.reshape(1, HEAD_DIM)]
    tk = t
    if latent:
        cos, sin = rope
        ck, cv, layer = cache
        past = ck.shape[2]
        tk = past + t
        in_specs += [
            pl.BlockSpec((t, HEAD_DIM), lambda bi, kg, qi: (0, 0)),
            pl.BlockSpec((t, HEAD_DIM), lambda bi, kg, qi: (0, 0)),
            pl.BlockSpec((None, None, past, hk * HEAD_DIM), lambda bi, kg, qi: (bi, layer, 0, kg)),
            pl.BlockSpec((None, None, past, hk * HEAD_DIM), lambda bi, kg, qi: (bi, layer, 0, kg)),
        ]
        args += [cos, sin, ck, cv]
    return pl.pallas_call(
        functools.partial(_attn_kernel, t=t, hk=hk, latent=latent),
        grid=(b, ATT_KV_HEADS // hk, nq),
        in_specs=in_specs,
        out_specs=[
            pl.BlockSpec((ATT_TQ, hk * gw), lambda bi, kg, qi: (bi * nq + qi, kg)),
            pl.BlockSpec((t, hk * HEAD_DIM), lambda bi, kg, qi: (bi, kg)),
        ],
        out_shape=[
            jax.ShapeDtypeStruct((b * t, ATT_W), BF16),
            jax.ShapeDtypeStruct((b * t, ATT_KV_W), F32),
        ],
        scratch_shapes=[pltpu.VMEM((hk, tk, HEAD_DIM), BF16), pltpu.VMEM((hk, tk, HEAD_DIM), BF16)],
        compiler_params=_cparams("parallel", "parallel", "arbitrary"),
        name="attention",
    )(*args)


def _split3_dot(tri, x):
    hi = x.astype(BF16)
    r1 = x - hi.astype(F32)
    mid = r1.astype(BF16)
    lo = (r1 - mid.astype(F32)).astype(BF16)
    d = lambda a: jnp.dot(tri, a, preferred_element_type=F32)
    return d(hi) + d(mid) + d(lo)


def _shift_rows(x, direction):
    n = x.shape[0]
    row = lax.broadcasted_iota(jnp.int32, x.shape, 0)
    if direction > 0:
        return jnp.where(row == 0, 0.0, pltpu.roll(x, 1, axis=0))
    return jnp.where(row == n - 1, 0.0, pltpu.roll(x, n - 1, axis=0))


def _conv_silu(x, w, b):
    y = _shift_rows(x, 1) * w[0:1, :] + x * w[1:2, :] + _shift_rows(x, -1) * w[2:3, :] + b
    return _silu(y)


def _mlstm_kernel(*refs, t, zero_init):
    if zero_init:
        z_ref, zif_ref, bif_ref, cw_ref, cb_ref, nw_ref, o_ref, c_out, n_out, m_out = refs
    else:
        z_ref, zif_ref, bif_ref, cw_ref, cb_ref, nw_ref, c0_ref, n0_ref, m0_ref, o_ref, c_out, n_out, m_out = refs
    L = ML_CHUNK
    nc = t // L
    chunk = lambda a, c: a[c * L:(c + 1) * L]

    gates = zif_ref[...] + bif_ref[...]
    logsig = jnp.minimum(gates, 0.0) - jnp.log1p(jnp.exp(-jnp.abs(gates)))
    gates_t = gates.T
    r_i = lax.broadcasted_iota(jnp.int32, (L, L), 0)
    c_i = lax.broadcasted_iota(jnp.int32, (L, L), 1)
    masks = (r_i >= c_i, r_i <= c_i)
    tris = tuple(mk.astype(BF16) for mk in masks)
    cums = [[_split3_dot(tris[d], chunk(logsig, c)) for c in range(nc)] for d in range(2)]
    cums_t = [[x.T for x in row] for row in cums]
    ones_col = (lax.broadcasted_iota(jnp.int32, (t, LANES), 1) == 0).astype(BF16)
    eye = (lax.broadcasted_iota(jnp.int32, (ML_DK, ML_DV), 0) == lax.broadcasted_iota(jnp.int32, (ML_DK, ML_DV), 1))
    lane0 = lax.broadcasted_iota(jnp.int32, (ML_DK, ML_DV), 1) == 0

    def local_part(qk_c, k_c, v_c, b_col, b_row, i_col, i_row, d):
        dmat = jnp.where(masks[d], b_col - b_row + i_row, -jnp.inf)
        m_loc = jnp.max(dmat, axis=1, keepdims=True)
        s = (qk_c * jnp.exp(dmat - m_loc)).astype(BF16)
        bm = jnp.dot(s, v_c, preferred_element_type=F32)
        b_end = b_col[0:1, :] if d else b_col[L - 1:L, :]
        g = b_end - b_col + i_col
        g_max = jnp.max(g, axis=0, keepdims=True)
        u = jnp.dot((k_c * jnp.exp(g - g_max)).T.astype(BF16), v_c, preferred_element_type=F32)
        return m_loc, bm, b_end, g_max, u

    def scan_step(q_c, b_col, loc, c_ext, m):
        m_loc, bm, b_end, g_max, u = loc
        inter = b_col + m
        m_row = jnp.maximum(inter, m_loc)
        nd = jnp.exp(m_loc - m_row) * bm
        if c_ext is not None:
            nd = nd + jnp.exp(inter - m_row) * jnp.dot(q_c, c_ext.astype(BF16), preferred_element_type=F32)
        h = nd[:, :ML_DV] / jnp.maximum(jnp.abs(nd[:, ML_DV:ML_DV + 1]), jnp.exp(-m_row))
        m_new = jnp.maximum(b_end + m, g_max)
        c_new = jnp.exp(g_max - m_new) * u
        if c_ext is not None:
            c_new = c_new + jnp.exp(b_end + m - m_new) * c_ext
        return h, c_new, m_new

    for hd in range(ML_HEADS):
        cols = slice(hd * ML_DK, (hd + 1) * ML_DK)
        kcols = slice(ML_W + hd * ML_DK, ML_W + (hd + 1) * ML_DK)
        vcols = slice(2 * ML_W + hd * ML_DV, 2 * ML_W + (hd + 1) * ML_DV)
        ocols = slice(3 * ML_W + hd * ML_DV, 3 * ML_W + (hd + 1) * ML_DV)
        gcols = slice(4 * ML_W + hd * ML_DV, 4 * ML_W + (hd + 1) * ML_DV)
        q = _conv_silu(z_ref[:, cols], cw_ref[:, cols], cb_ref[:, cols]).astype(BF16)
        k = _conv_silu(z_ref[:, kcols], cw_ref[:, kcols], cb_ref[:, kcols]) * (ML_DK ** -0.5)
        kb = k.astype(BF16)
        v_ext = jnp.concatenate([z_ref[:, vcols].astype(BF16), ones_col], axis=1)
        qk = [lax.dot_general(chunk(q, c), chunk(kb, c), NT_DIMS, preferred_element_type=F32) for c in range(nc)]

        h_sum = None
        for d in range(2):
            i_lane = 2 * d * ML_HEADS + hd
            f_lane = (2 * d + 1) * ML_HEADS + hd
            b_cols = [cums[d][c][:, f_lane:f_lane + 1] for c in range(nc)]
            locs = [local_part(qk[c], chunk(k, c), chunk(v_ext, c), b_cols[c],
                               cums_t[d][c][f_lane:f_lane + 1, :], chunk(gates, c)[:, i_lane:i_lane + 1],
                               gates_t[i_lane:i_lane + 1, c * L:(c + 1) * L], d) for c in range(nc)]
            if zero_init:
                c_ext, m = None, jnp.zeros((1, 1), F32)
            else:
                n0_col = jnp.sum(jnp.where(eye, n0_ref[d, hd:hd + 1, :], 0.0), axis=1, keepdims=True)
                c_ext = jnp.concatenate([c0_ref[d, hd], jnp.where(lane0, n0_col, 0.0)], axis=1)
                m = m0_ref[d:d + 1, hd:hd + 1]
            h_parts = [None] * nc
            for step in range(nc):
                c = nc - 1 - step if d else step
                h_parts[c], c_ext, m = scan_step(chunk(q, c), b_cols[c], locs[c], c_ext, m)
            h_dir = jnp.concatenate(h_parts, axis=0) if nc > 1 else h_parts[0]
            h_sum = h_dir if h_sum is None else h_sum + h_dir
            c_out[d, hd] = c_ext[:, :ML_DV]
            n_out[d, hd:hd + 1, :] = c_ext[:, ML_DV:].T[0:1, :]
            m_out[d:d + 1, hd:hd + 1] = m

        hm = _rms(h_sum, nw_ref[:, cols])
        o_ref[:, cols] = (hm * _sigmoid(z_ref[:, ocols]) * _silu(z_ref[:, gcols])).astype(BF16)


def _mlstm(z, b_if_l, conv_w_l, conv_b_l, ml_norm_l, *, b, t, state=None):
    zero_init = state is None
    zw = 5 * ML_W
    in_specs = [
        pl.BlockSpec((t, zw), lambda bi: (bi, OFF_MQ // zw)),
        pl.BlockSpec((t, LANES), lambda bi: (bi, OFF_IF // LANES)),
        pl.BlockSpec((1, LANES), lambda bi: (0, 0)),
        pl.BlockSpec((CONV_W, 2 * ML_W), lambda bi: (0, 0)),
        pl.BlockSpec((1, 2 * ML_W), lambda bi: (0, 0)),
        pl.BlockSpec((1, ML_W), lambda bi: (0, 0)),
    ]
    bif = jnp.pad(b_if_l, (0, LANES - N_GATES)).reshape(1, LANES)
    args = [z, z, bif, conv_w_l, conv_b_l.reshape(1, 2 * ML_W), ml_norm_l.reshape(1, ML_W)]
    if not zero_init:
        c0, n0, m0, layer = state
        in_specs += [
            pl.BlockSpec((None, None, 2, ML_HEADS, ML_DK, ML_DV), lambda bi: (bi, layer, 0, 0, 0, 0)),
            pl.BlockSpec((None, None, 2, ML_HEADS, ML_DK), lambda bi: (bi, layer, 0, 0, 0)),
            pl.BlockSpec((None, None, 2, ML_HEADS), lambda bi: (bi, layer, 0, 0)),
        ]
        args += [c0, n0, m0]
    return pl.pallas_call(
        functools.partial(_mlstm_kernel, t=t, zero_init=zero_init),
        grid=(b,),
        in_specs=in_specs,
        out_specs=[
            pl.BlockSpec((t, ML_W), lambda bi: (bi, 0)),
            pl.BlockSpec((None, 2, ML_HEADS, ML_DK, ML_DV), lambda bi: (bi, 0, 0, 0, 0)),
            pl.BlockSpec((None, 2, ML_HEADS, ML_DK), lambda bi: (bi, 0, 0, 0)),
            pl.BlockSpec((None, 2, ML_HEADS), lambda bi: (bi, 0, 0)),
        ],
        out_shape=[
            jax.ShapeDtypeStruct((b * t, ML_W), BF16),
            jax.ShapeDtypeStruct((b, 2, ML_HEADS, ML_DK, ML_DV), F32),
            jax.ShapeDtypeStruct((b, 2, ML_HEADS, ML_DK), F32),
            jax.ShapeDtypeStruct((b, 2, ML_HEADS), F32),
        ],
        compiler_params=_cparams("parallel"),
        name="mlstm",
    )(*args)


def _dft_tables(t):
    def cs(n):
        idx = np.arange(n, dtype=np.int64)
        ang = 2.0 * np.pi * ((idx[:, None] * idx[None, :]) % n).astype(np.float64) / n
        return np.cos(ang), np.sin(ang)
    cc, sc = cs(FO_GC)
    ct, st = cs(t)
    return (np.concatenate([cc, sc], axis=1).astype(np.float32),
            np.concatenate([ct, -st], axis=1).astype(np.float32))


FO_RT = 256


def _fourier_kernel(zx_ref, zga_ref, zgb_ref, wc_ref, wt_ref, wf_ref, o_ref, y_ref, *, t):
    x = zx_ref[:, FO_SHIFT:FO_SHIFT + FO_W]
    wc = wc_ref[...].astype(BF16)
    for g in range(FO_GROUPS):
        cols = slice(g * FO_GC, (g + 1) * FO_GC)
        y = jnp.dot(x[:, cols].astype(BF16), wc, preferred_element_type=F32)
        y_ref[0:t, cols] = y[:, :FO_GC].astype(BF16)
        y_ref[t:2 * t, cols] = y[:, FO_GC:].astype(BF16)
    scale = (t * FO_GC) ** -0.5
    for r in range(t // FO_RT):
        rows = slice(r * FO_RT, (r + 1) * FO_RT)
        f = jnp.dot(wt_ref[rows, :].astype(BF16), y_ref[...], preferred_element_type=F32) * scale
        fg = jnp.concatenate([zga_ref[rows, FO_SHIFT:], zgb_ref[rows, :FO_SHIFT]], axis=1)
        for g in range(FO_GROUPS):
            cols = slice(g * FO_GC, (g + 1) * FO_GC)
            og = jnp.dot(f[:, cols].astype(BF16), wf_ref[g].astype(BF16), preferred_element_type=F32)
            o_ref[rows, cols] = (og * _silu(fg[:, cols])).astype(BF16)


def _fourier(z, w_fno_l, *, b, t):
    wc, wt = _dft_tables(t)
    xw = FO_W + LANES
    return pl.pallas_call(
        functools.partial(_fourier_kernel, t=t),
        grid=(b,),
        in_specs=[
            pl.BlockSpec((t, xw), lambda bi: (bi, (OFF_FX - FO_SHIFT) // xw)),
            pl.BlockSpec((t, FO_W), lambda bi: (bi, (OFF_FG - FO_SHIFT) // FO_W)),
            pl.BlockSpec((t, LANES), lambda bi: (bi, (OFF_FG - FO_SHIFT + FO_W) // LANES)),
            pl.BlockSpec((FO_GC, 2 * FO_GC), lambda bi: (0, 0)),
            pl.BlockSpec((t, 2 * t), lambda bi: (0, 0)),
            pl.BlockSpec((FO_GROUPS, FO_GC, FO_GC), lambda bi: (0, 0, 0)),
        ],
        out_specs=pl.BlockSpec((t, FO_W), lambda bi: (bi, 0)),
        out_shape=jax.ShapeDtypeStruct((b * t, FO_W), BF16),
        scratch_shapes=[pltpu.VMEM((2 * t, FO_W), BF16)],
        compiler_params=_cparams("parallel"),
        name="fourier",
    )(z, z, z, jnp.asarray(wc), jnp.asarray(wt), w_fno_l)


OUT_TN = 512


def _outproj_kernel(oa_ref, om_ref, of_ref, wa_ref, wm_ref, wf_ref, x_ref, gate_ref, y_ref, *, per_row_mod):
    row = 1 + pl.program_id(0) if per_row_mod else 0
    gate = gate_ref[pl.ds(row, 1), :]
    y = (jnp.dot(oa_ref[...], wa_ref[...].astype(BF16), preferred_element_type=F32)
         + jnp.dot(om_ref[...], wm_ref[...].astype(BF16), preferred_element_type=F32)
         + jnp.dot(of_ref[...], wf_ref[...].astype(BF16), preferred_element_type=F32))
    y_ref[...] = x_ref[...] + gate * y


def _outproj(o_att, o_ml, o_fo, w_out, layer, x2d, mod_l, *, per_row_mod):
    m = x2d.shape[0]
    return pl.pallas_call(
        functools.partial(_outproj_kernel, per_row_mod=per_row_mod),
        grid=(m // ROW_TILE, D_MODEL // OUT_TN),
        in_specs=[
            pl.BlockSpec((ROW_TILE, ATT_W), lambda i, j: (i, 0)),
            pl.BlockSpec((ROW_TILE, ML_W), lambda i, j: (i, 0)),
            pl.BlockSpec((ROW_TILE, FO_W), lambda i, j: (i, 0)),
            pl.BlockSpec((None, ATT_W, OUT_TN), lambda i, j: (layer, 0, j)),
            pl.BlockSpec((None, ML_W, OUT_TN), lambda i, j: (layer, ATT_W // ML_W, j)),
            pl.BlockSpec((None, FO_W, OUT_TN), lambda i, j: (layer, (ATT_W + ML_W) // FO_W, j)),
            pl.BlockSpec((ROW_TILE, OUT_TN), lambda i, j: (i, j)),
            pl.BlockSpec((8, OUT_TN), lambda i, j: (0, 2 * D_MODEL // OUT_TN + j)),
        ],
        out_specs=pl.BlockSpec((ROW_TILE, OUT_TN), lambda i, j: (i, j)),
        out_shape=jax.ShapeDtypeStruct((m, D_MODEL), F32),
        compiler_params=_cparams("parallel", "parallel"),
        name="outproj",
    )(o_att, o_ml, o_fo, w_out, w_out, w_out, x2d, mod_l)


def _rope_tables(t):
    half = HEAD_DIM // 2
    inv_freq = ROPE_BASE ** (-jnp.arange(0, half, 2, dtype=F32) / half)
    n_rows = t // GRID_W
    rows = jnp.repeat(jnp.arange(n_rows, dtype=F32), GRID_W)
    cols = jnp.tile(jnp.arange(GRID_W, dtype=F32), n_rows)
    ar = rows[:, None] * inv_freq
    ac = cols[:, None] * inv_freq
    cos = jnp.concatenate([jnp.cos(ar), jnp.cos(ar), jnp.cos(ac), jnp.cos(ac)], axis=1)
    sin = jnp.concatenate([-jnp.sin(ar), jnp.sin(ar), -jnp.sin(ac), jnp.sin(ac)], axis=1)
    return cos, sin


def _layer(x2d, mod_l, lw, layer, *, b, t, latent, rope=None, cache=None, state=None):
    norm_w, w_in_t, b_if, conv_w, conv_b, q_norm, k_norm, ml_norm, w_fno, w_out = lw
    z = _inproj(x2d, mod_l, norm_w, w_in_t, layer, per_row_mod=latent)
    o_att, k_new = _attention(z, q_norm, k_norm, b=b, t=t, rope=rope, cache=cache)
    o_ml, c_f, n_f, m_f = _mlstm(z, b_if, conv_w, conv_b, ml_norm, b=b, t=t, state=state)
    o_fo = _fourier(z, w_fno, b=b, t=t)
    y = _outproj(o_att, o_ml, o_fo, w_out, layer, x2d, mod_l, per_row_mod=latent)
    v_new = z[:, OFF_AV:OFF_AV + ATT_KV_W]
    return y, (k_new, v_new, c_f, n_f, m_f)


def kernel(x_prompt, x_sample, cache_k, cache_v, state_C, state_n, state_m, c, c_ctx, norm_w, w_mod, b_mod, w_in,
           b_if, conv_w, conv_b, q_norm, k_norm, ml_norm, w_fno, w_out):
    bp, tp, _ = x_prompt.shape
    bs, ts, _ = x_sample.shape
    assert tp == ML_CHUNK and ts == ROW_TILE and (bp * tp) % ROW_TILE == 0 and bs + 1 <= 8
    past = cache_k.shape[2]

    cond8 = jnp.concatenate([c_ctx[None, :], c, jnp.zeros((8 - 1 - bs, D_MODEL), F32)], axis=0)
    mod = _modulation(cond8, w_mod, b_mod)

    rope = _rope_tables(ts)
    ck = cache_k.reshape(bs, DEPTH, past, ATT_KV_W)
    cv = cache_v.reshape(bs, DEPTH, past, ATT_KV_W)
    w_in_t = jnp.swapaxes(w_in, 1, 2)

    xp = x_prompt.reshape(bp * tp, D_MODEL)
    xs = x_sample.reshape(bs * ts, D_MODEL)
    new_k, new_v, new_c, new_n, new_m = [], [], [], [], []
    for l in range(DEPTH):
        lw = (norm_w[l], w_in_t, b_if[l], conv_w[l], conv_b[l], q_norm[l], k_norm[l], ml_norm[l], w_fno[l], w_out)
        xp, (k_l, v_l, c_l, n_l, m_l) = _layer(xp, mod[l], lw, l, b=bp, t=tp, latent=False)
        new_k.append(k_l.reshape(bp, tp, ATT_KV_HEADS, HEAD_DIM))
        new_v.append(v_l.reshape(bp, tp, ATT_KV_HEADS, HEAD_DIM))
        new_c.append(c_l)
        new_n.append(n_l)
        new_m.append(m_l)
        xs, _ = _layer(xs, mod[l], lw, l, b=bs, t=ts, latent=True, rope=rope,
                       cache=(ck, cv, l), state=(state_C, state_n, state_m, l))
    return (xp.reshape(bp, tp, D_MODEL), xs.reshape(bs, ts, D_MODEL), jnp.stack(new_k, axis=1),
            jnp.stack(new_v, axis=1), jnp.stack(new_c, axis=1), jnp.stack(new_n, axis=1), jnp.stack(new_m, axis=1))
```

```python
import functools

import numpy as np
import jax
import jax.numpy as jnp
from jax import lax
from jax.experimental import pallas as pl
from jax.experimental.pallas import tpu as pltpu

D_MODEL = 2048
DEPTH = 2
GRID_W = 64
HEAD_DIM = 128
ATT_HEADS = 8
ATT_KV_HEADS = 2
ATT_GROUPS = ATT_HEADS // ATT_KV_HEADS
ATT_W = ATT_HEADS * HEAD_DIM
ATT_KV_W = ATT_KV_HEADS * HEAD_DIM
ML_HEADS = 4
ML_DK = 128
ML_DV = 128
ML_W = ML_HEADS * ML_DV
FO_GROUPS = 4
FO_GC = 128
FO_W = FO_GROUPS * FO_GC
D_MIX = ATT_W + ML_W + FO_W
N_GATES = 4 * ML_HEADS
D_PROJ = 2 * ATT_W + 2 * ATT_KV_W + 5 * ML_W + N_GATES + 2 * FO_W
CONV_W = 3
ROPE_BASE = 10000.0
EPS = 1e-6

LANES = 128
OFF_AQ = 0
OFF_AK = OFF_AQ + ATT_W
OFF_AV = OFF_AK + ATT_KV_W
OFF_AG = OFF_AV + ATT_KV_W
OFF_MQ = OFF_AG + ATT_W
OFF_MK = OFF_MQ + ML_W
OFF_MV = OFF_MK + ML_W
OFF_MO = OFF_MV + ML_W
OFF_MG = OFF_MO + ML_W
OFF_IF = OFF_MG + ML_W
OFF_FX = OFF_IF + N_GATES
OFF_FG = OFF_FX + FO_W
FO_SHIFT = OFF_FX % LANES

PROJ_TN = 7 * LANES
Z_W = 7 * PROJ_TN
ROW_TILE = 1024
ML_CHUNK = 256
ML_EXT = ML_DV + 16
ATT_TQ = 256
VMEM_LIMIT = 56 * 1024 * 1024

BF16 = jnp.bfloat16
F32 = jnp.float32
NT_DIMS = (((1,), (1,)), ((), ()))


def _cparams(*sem):
    return pltpu.CompilerParams(dimension_semantics=sem, vmem_limit_bytes=VMEM_LIMIT)


def _silu(x):
    return x * (1.0 / (1.0 + jnp.exp(-x)))


def _sigmoid(x):
    return 1.0 / (1.0 + jnp.exp(-x))


def _rms(x, w):
    ms = jnp.mean(x * x, axis=-1, keepdims=True)
    return x * lax.rsqrt(ms + EPS) * w


MOD_TN = 768


def _mod_kernel(cond_ref, w_ref, b_ref, o_ref):
    a = _silu(cond_ref[...]).astype(BF16)
    o_ref[...] = jnp.dot(a, w_ref[...].astype(BF16), preferred_element_type=F32) + b_ref[...]


def _modulation(cond8, w_mod, b_mod):
    n = 3 * D_MODEL
    return pl.pallas_call(
        _mod_kernel,
        grid=(DEPTH, n // MOD_TN),
        in_specs=[
            pl.BlockSpec((8, D_MODEL), lambda l, j: (0, 0)),
            pl.BlockSpec((None, D_MODEL, MOD_TN), lambda l, j: (l, 0, j)),
            pl.BlockSpec((None, 1, MOD_TN), lambda l, j: (l, 0, j)),
        ],
        out_specs=pl.BlockSpec((None, 8, MOD_TN), lambda l, j: (l, 0, j)),
        out_shape=jax.ShapeDtypeStruct((DEPTH, 8, n), F32),
        compiler_params=_cparams("parallel", "parallel"),
        name="modulation",
    )(cond8, w_mod, b_mod.reshape(DEPTH, 1, n))


def _inproj_kernel(x_ref, mod_ref, nw_ref, w_ref, z_ref, h_ref, *, per_row_mod):
    i = pl.program_id(0)
    j = pl.program_id(1)

    @pl.when(j == 0)
    def _():
        row = 1 + i if per_row_mod else 0
        shift = mod_ref[pl.ds(row, 1), 0:D_MODEL]
        scale = mod_ref[pl.ds(row, 1), D_MODEL:2 * D_MODEL]
        nw = nw_ref[...]
        rows = 256

        def body(r, carry):
            sl = pl.ds(pl.multiple_of(r * rows, rows), rows)
            y = _rms(x_ref[sl, :], nw)
            h_ref[sl, :] = (y * (1.0 + scale) + shift).astype(BF16)
            return carry

        lax.fori_loop(0, ROW_TILE // rows, body, 0)

    z = lax.dot_general(h_ref[...], w_ref[...].astype(BF16), NT_DIMS, preferred_element_type=F32)
    last = Z_W // PROJ_TN - 1

    @pl.when(j < last)
    def _():
        z_ref[...] = z

    @pl.when(j == last)
    def _():
        col = lax.broadcasted_iota(jnp.int32, z.shape, 1)
        z_ref[...] = jnp.where(col < D_PROJ - last * PROJ_TN, z, 0.0)


def _inproj(x2d, mod_l, norm_w_l, w_in_t, layer, *, per_row_mod):
    m = x2d.shape[0]
    return pl.pallas_call(
        functools.partial(_inproj_kernel, per_row_mod=per_row_mod),
        grid=(m // ROW_TILE, Z_W // PROJ_TN),
        in_specs=[
            pl.BlockSpec((ROW_TILE, D_MODEL), lambda i, j: (i, 0)),
            pl.BlockSpec((8, 3 * D_MODEL), lambda i, j: (0, 0)),
            pl.BlockSpec((1, D_MODEL), lambda i, j: (0, 0)),
            pl.BlockSpec((None, PROJ_TN, D_MODEL), lambda i, j: (layer, j, 0)),
        ],
        out_specs=pl.BlockSpec((ROW_TILE, PROJ_TN), lambda i, j: (i, j)),
        out_shape=jax.ShapeDtypeStruct((m, Z_W), F32),
        scratch_shapes=[pltpu.VMEM((ROW_TILE, D_MODEL), BF16)],
        compiler_params=_cparams("parallel", "arbitrary"),
        name="inproj",
    )(x2d, mod_l, norm_w_l.reshape(1, D_MODEL), w_in_t)


def _rope(x, cos, sin):
    lane = lax.broadcasted_iota(jnp.int32, x.shape, 1)
    quarter = HEAD_DIM // 4
    partner = jnp.where((lane % (2 * quarter)) < quarter,
                        pltpu.roll(x, HEAD_DIM - quarter, axis=1), pltpu.roll(x, quarter, axis=1))
    return x * cos + partner * sin


LOG2E = 1.4426950408889634


def _attn_kernel(*refs, t, hk, latent):
    zq_ref, zk_ref, zv_ref = refs[:3]
    zg_refs = refs[3:3 + hk]
    rest = refs[3 + hk:]
    if latent:
        qn_ref, kn_ref, cos_ref, sin_ref, ck_ref, cv_ref, o_ref, knew_ref, kall_ref, vall_ref = rest
        past = ck_ref.shape[0]
    else:
        qn_ref, kn_ref, o_ref, knew_ref, kall_ref, vall_ref = rest
        past = 0
    qi = pl.program_id(2)

    @pl.when(qi == 0)
    def _():
        for j in range(hk):
            kc = slice(j * HEAD_DIM, (j + 1) * HEAD_DIM)
            k = _rms(zk_ref[:, kc], kn_ref[...])
            if latent:
                k = _rope(k, cos_ref[...], sin_ref[...])
                kall_ref[j, 0:past, :] = ck_ref[:, kc].astype(BF16)
                vall_ref[j, 0:past, :] = cv_ref[:, kc].astype(BF16)
            knew_ref[:, kc] = k
            kall_ref[j, past:past + t, :] = k.astype(BF16)
            vall_ref[j, past:past + t, :] = zv_ref[:, kc].astype(BF16)

    q_scale = (HEAD_DIM ** -0.5) * LOG2E
    if latent:
        rows = pl.ds(pl.multiple_of(qi * ATT_TQ, ATT_TQ), ATT_TQ)
        cos_q = cos_ref[rows, :]
        sin_q = sin_ref[rows, :]
    for j in range(hk):
        kb = kall_ref[j]
        vb = vall_ref[j]
        for g in range(ATT_GROUPS):
            cols = slice((j * ATT_GROUPS + g) * HEAD_DIM, (j * ATT_GROUPS + g + 1) * HEAD_DIM)
            gcols = slice(g * HEAD_DIM, (g + 1) * HEAD_DIM)
            q = _rms(zq_ref[:, cols], qn_ref[...])
            if latent:
                q = _rope(q, cos_q, sin_q)
            qb = (q * q_scale).astype(BF16)
            s = lax.dot_general(qb, kb, NT_DIMS, preferred_element_type=F32)
            p = jnp.exp2(s - jnp.max(s, axis=-1, keepdims=True))
            den = jnp.sum(p, axis=-1, keepdims=True)
            o = jnp.dot(p.astype(BF16), vb, preferred_element_type=F32) * (1.0 / den)
            o_ref[:, cols] = (o * _silu(zg_refs[j][:, gcols])).astype(BF16)


def _attention(z, q_norm_l, k_norm_l, *, b, t, rope=None, cache=None):
    latent = rope is not None
    hk = 1 if latent else ATT_KV_HEADS
    nq = t // ATT_TQ
    gw = ATT_GROUPS * HEAD_DIM
    in_specs = [
        pl.BlockSpec((ATT_TQ, hk * gw), lambda bi, kg, qi: (bi * nq + qi, OFF_AQ // (hk * gw) + kg)),
        pl.BlockSpec((t, hk * HEAD_DIM), lambda bi, kg, qi: (bi, OFF_AK // (hk * HEAD_DIM) + kg)),
        pl.BlockSpec((t, hk * HEAD_DIM), lambda bi, kg, qi: (bi, OFF_AV // (hk * HEAD_DIM) + kg)),
    ]
    in_specs += [pl.BlockSpec((ATT_TQ, gw), lambda bi, kg, qi, j=j: (bi * nq + qi, OFF_AG // gw + kg * hk + j))
                 for j in range(hk)]
    in_specs += [pl.BlockSpec((1, HEAD_DIM), lambda bi, kg, qi: (0, 0))] * 2
    args = [z] * (3 + hk) + [q_norm_l.reshape(1, HEAD_DIM), k_norm_l.reshape(1, HEAD_DIM)]
    tk = t
    if latent:
        cos, sin = rope
        ck, cv, layer = cache
        past = ck.shape[2]
        tk = past + t
        in_specs += [
            pl.BlockSpec((t, HEAD_DIM), lambda bi, kg, qi: (0, 0)),
            pl.BlockSpec((t, HEAD_DIM), lambda bi, kg, qi: (0, 0)),
            pl.BlockSpec((None, None, past, hk * HEAD_DIM), lambda bi, kg, qi: (bi, layer, 0, kg)),
            pl.BlockSpec((None, None, past, hk * HEAD_DIM), lambda bi, kg, qi: (bi, layer, 0, kg)),
        ]
        args += [cos, sin, ck, cv]
    return pl.pallas_call(
        functools.partial(_attn_kernel, t=t, hk=hk, latent=latent),
        grid=(b, ATT_KV_HEADS // hk, nq),
        in_specs=in_specs,
        out_specs=[
            pl.BlockSpec((ATT_TQ, hk * gw), lambda bi, kg, qi: (bi * nq + qi, kg)),
            pl.BlockSpec((t, hk * HEAD_DIM), lambda bi, kg, qi: (bi, kg)),
        ],
        out_shape=[
            jax.ShapeDtypeStruct((b * t, ATT_W), BF16),
            jax.ShapeDtypeStruct((b * t, ATT_KV_W), F32),
        ],
        scratch_shapes=[pltpu.VMEM((hk, tk, HEAD_DIM), BF16), pltpu.VMEM((hk, tk, HEAD_DIM), BF16)],
        compiler_params=_cparams("parallel", "parallel", "arbitrary"),
        name="attention",
    )(*args)


def _split3_dot(tri, x):
    hi = x.astype(BF16)
    r1 = x - hi.astype(F32)
    mid = r1.astype(BF16)
    lo = (r1 - mid.astype(F32)).astype(BF16)
    d = lambda a: jnp.dot(tri, a, preferred_element_type=F32)
    return d(hi) + d(mid) + d(lo)


def _shift_rows(x, direction):
    n = x.shape[0]
    row = lax.broadcasted_iota(jnp.int32, x.shape, 0)
    if direction > 0:
        return jnp.where(row == 0, 0.0, pltpu.roll(x, 1, axis=0))
    return jnp.where(row == n - 1, 0.0, pltpu.roll(x, n - 1, axis=0))


def _conv_silu(x, w, b):
    y = _shift_rows(x, 1) * w[0:1, :] + x * w[1:2, :] + _shift_rows(x, -1) * w[2:3, :] + b
    return _silu(y)


def _mlstm_kernel(*refs, t, zero_init):
    if zero_init:
        z_ref, zif_ref, bif_ref, cw_ref, cb_ref, nw_ref, o_ref, c_out, n_out, m_out = refs
    else:
        z_ref, zif_ref, bif_ref, cw_ref, cb_ref, nw_ref, c0_ref, n0_ref, m0_ref, o_ref, c_out, n_out, m_out = refs
    L = ML_CHUNK
    nc = t // L
    chunk = lambda a, c: a[c * L:(c + 1) * L]

    gates = zif_ref[...] + bif_ref[...]
    logsig = jnp.minimum(gates, 0.0) - jnp.log1p(jnp.exp(-jnp.abs(gates)))
    gates_t = gates.T
    r_i = lax.broadcasted_iota(jnp.int32, (L, L), 0)
    c_i = lax.broadcasted_iota(jnp.int32, (L, L), 1)
    lower, upper = r_i >= c_i, r_i <= c_i
    cum_tris = (lower.astype(BF16), upper.astype(BF16))
    dmasks = (upper, lower)
    cums = [[_split3_dot(cum_tris[d], chunk(logsig, c)) for c in range(nc)] for d in range(2)]
    cums_t = [[x.T for x in row] for row in cums]
    ext_rows = (lax.broadcasted_iota(jnp.int32, (ML_EXT - ML_DV, t), 0) == 0).astype(F32)
    lchunk = lambda a, c: a[:, c * L:(c + 1) * L]

    def local_part(qk_c, kb_c, vt_c, vtb_c, b_row, b_col, i_row, i_col, d):
        dm = jnp.where(dmasks[d], b_row - b_col + i_col, -jnp.inf)
        m_loc = jnp.max(dm, axis=0, keepdims=True)
        s = (qk_c * jnp.exp(dm - m_loc)).astype(BF16)
        bm = jnp.dot(vtb_c, s, preferred_element_type=F32)
        b_end = b_row[:, 0:1] if d else b_row[:, L - 1:L]
        g = b_end - b_row + i_row
        g_max = jnp.max(g, axis=1, keepdims=True)
        vw = (vt_c * jnp.exp(g - g_max)).astype(BF16)
        u = jnp.dot(vw, kb_c, preferred_element_type=F32)
        return m_loc, bm, b_end, g_max, u

    def scan_step(qt_c, b_row, loc, c_t, m):
        m_loc, bm, b_end, g_max, u = loc
        inter = b_row + m
        m_row = jnp.maximum(inter, m_loc)
        nd = jnp.exp(m_loc - m_row) * bm
        if c_t is not None:
            nd = nd + jnp.exp(inter - m_row) * jnp.dot(c_t.astype(BF16), qt_c, preferred_element_type=F32)
        h_t = nd[:ML_DV] / jnp.maximum(jnp.abs(nd[ML_DV:ML_DV + 1]), jnp.exp(-m_row))
        m_new = jnp.maximum(b_end + m, g_max)
        c_new = jnp.exp(g_max - m_new) * u
        if c_t is not None:
            c_new = c_new + jnp.exp(b_end + m - m_new) * c_t
        return h_t, c_new, m_new

    for hd in range(ML_HEADS):
        cols = slice(hd * ML_DK, (hd + 1) * ML_DK)
        kcols = slice(ML_W + hd * ML_DK, ML_W + (hd + 1) * ML_DK)
        vcols = slice(2 * ML_W + hd * ML_DV, 2 * ML_W + (hd + 1) * ML_DV)
        ocols = slice(3 * ML_W + hd * ML_DV, 3 * ML_W + (hd + 1) * ML_DV)
        gcols = slice(4 * ML_W + hd * ML_DV, 4 * ML_W + (hd + 1) * ML_DV)
        qt = _conv_silu(z_ref[:, cols], cw_ref[:, cols], cb_ref[:, cols]).T.astype(BF16)
        kb = (_conv_silu(z_ref[:, kcols], cw_ref[:, kcols], cb_ref[:, kcols]) * (ML_DK ** -0.5)).astype(BF16)
        vt = jnp.concatenate([z_ref[:, vcols].T, ext_rows], axis=0)
        vtb = vt.astype(BF16)
        qk = [jnp.dot(chunk(kb, c), lchunk(qt, c), preferred_element_type=F32) for c in range(nc)]

        h_sum = None
        for d in range(2):
            i_lane = 2 * d * ML_HEADS + hd
            f_lane = (2 * d + 1) * ML_HEADS + hd
            b_rows = [cums_t[d][c][f_lane:f_lane + 1, :] for c in range(nc)]
            locs = [local_part(qk[c], chunk(kb, c), lchunk(vt, c), lchunk(vtb, c), b_rows[c],
                               cums[d][c][:, f_lane:f_lane + 1], lchunk(gates_t, c)[i_lane:i_lane + 1, :],
                               chunk(gates, c)[:, i_lane:i_lane + 1], d) for c in range(nc)]
            if zero_init:
                c_t, m = None, jnp.zeros((1, 1), F32)
            else:
                c_t = jnp.concatenate([c0_ref[d, hd].T, n0_ref[d, hd:hd + 1, :],
                                       jnp.zeros((ML_EXT - ML_DV - 1, ML_DK), F32)], axis=0)
                m = m0_ref[d:d + 1, hd:hd + 1]
            h_parts = [None] * nc
            for step in range(nc):
                c = nc - 1 - step if d else step
                h_parts[c], c_t, m = scan_step(lchunk(qt, c), b_rows[c], locs[c], c_t, m)
            h_dir = jnp.concatenate(h_parts, axis=1) if nc > 1 else h_parts[0]
            h_sum = h_dir if h_sum is None else h_sum + h_dir
            c_out[d, hd] = c_t[:ML_DV].T
            n_out[d, hd:hd + 1, :] = c_t[ML_DV:ML_DV + 1]
            m_out[d:d + 1, hd:hd + 1] = m

        hm = _rms(h_sum.T, nw_ref[:, cols])
        o_ref[:, cols] = (hm * _sigmoid(z_ref[:, ocols]) * _silu(z_ref[:, gcols])).astype(BF16)


def _mlstm(z, b_if_l, conv_w_l, conv_b_l, ml_norm_l, *, b, t, state=None):
    zero_init = state is None
    zw = 5 * ML_W
    in_specs = [
        pl.BlockSpec((t, zw), lambda bi: (bi, OFF_MQ // zw)),
        pl.BlockSpec((t, LANES), lambda bi: (bi, OFF_IF // LANES)),
        pl.BlockSpec((1, LANES), lambda bi: (0, 0)),
        pl.BlockSpec((CONV_W, 2 * ML_W), lambda bi: (0, 0)),
        pl.BlockSpec((1, 2 * ML_W), lambda bi: (0, 0)),
        pl.BlockSpec((1, ML_W), lambda bi: (0, 0)),
    ]
    bif = jnp.pad(b_if_l, (0, LANES - N_GATES)).reshape(1, LANES)
    args = [z, z, bif, conv_w_l, conv_b_l.reshape(1, 2 * ML_W), ml_norm_l.reshape(1, ML_W)]
    if not zero_init:
        c0, n0, m0, layer = state
        in_specs += [
            pl.BlockSpec((None, None, 2, ML_HEADS, ML_DK, ML_DV), lambda bi: (bi, layer, 0, 0, 0, 0)),
            pl.BlockSpec((None, None, 2, ML_HEADS, ML_DK), lambda bi: (bi, layer, 0, 0, 0)),
            pl.BlockSpec((None, None, 2, ML_HEADS), lambda bi: (bi, layer, 0, 0)),
        ]
        args += [c0, n0, m0]
    return pl.pallas_call(
        functools.partial(_mlstm_kernel, t=t, zero_init=zero_init),
        grid=(b,),
        in_specs=in_specs,
        out_specs=[
            pl.BlockSpec((t, ML_W), lambda bi: (bi, 0)),
            pl.BlockSpec((None, 2, ML_HEADS, ML_DK, ML_DV), lambda bi: (bi, 0, 0, 0, 0)),
            pl.BlockSpec((None, 2, ML_HEADS, ML_DK), lambda bi: (bi, 0, 0, 0)),
            pl.BlockSpec((None, 2, ML_HEADS), lambda bi: (bi, 0, 0)),
        ],
        out_shape=[
            jax.ShapeDtypeStruct((b * t, ML_W), BF16),
            jax.ShapeDtypeStruct((b, 2, ML_HEADS, ML_DK, ML_DV), F32),
            jax.ShapeDtypeStruct((b, 2, ML_HEADS, ML_DK), F32),
            jax.ShapeDtypeStruct((b, 2, ML_HEADS), F32),
        ],
        compiler_params=_cparams("parallel"),
        name="mlstm",
    )(*args)


def _dft_tables(t):
    def cs(n):
        idx = np.arange(n, dtype=np.int64)
        ang = 2.0 * np.pi * ((idx[:, None] * idx[None, :]) % n).astype(np.float64) / n
        return np.cos(ang), np.sin(ang)
    cc, sc = cs(FO_GC)
    ct, st = cs(t)
    return (np.concatenate([cc, sc], axis=1).astype(np.float32),
            np.concatenate([ct, -st], axis=1).astype(np.float32))


FO_RT = 256


def _fourier_kernel(zx_ref, zga_ref, zgb_ref, wc_ref, wt_ref, wf_ref, o_ref, y_ref, *, t):
    x = zx_ref[:, FO_SHIFT:FO_SHIFT + FO_W]
    wc = wc_ref[...].astype(BF16)
    for g in range(FO_GROUPS):
        cols = slice(g * FO_GC, (g + 1) * FO_GC)
        y = jnp.dot(x[:, cols].astype(BF16), wc, preferred_element_type=F32)
        y_ref[0:t, cols] = y[:, :FO_GC].astype(BF16)
        y_ref[t:2 * t, cols] = y[:, FO_GC:].astype(BF16)
    scale = (t * FO_GC) ** -0.5
    for r in range(t // FO_RT):
        rows = slice(r * FO_RT, (r + 1) * FO_RT)
        f = jnp.dot(wt_ref[rows, :].astype(BF16), y_ref[...], preferred_element_type=F32) * scale
        fg = jnp.concatenate([zga_ref[rows, FO_SHIFT:], zgb_ref[rows, :FO_SHIFT]], axis=1)
        for g in range(FO_GROUPS):
            cols = slice(g * FO_GC, (g + 1) * FO_GC)
            og = jnp.dot(f[:, cols].astype(BF16), wf_ref[g].astype(BF16), preferred_element_type=F32)
            o_ref[rows, cols] = (og * _silu(fg[:, cols])).astype(BF16)


def _fourier(z, w_fno_l, *, b, t):
    wc, wt = _dft_tables(t)
    xw = FO_W + LANES
    return pl.pallas_call(
        functools.partial(_fourier_kernel, t=t),
        grid=(b,),
        in_specs=[
            pl.BlockSpec((t, xw), lambda bi: (bi, (OFF_FX - FO_SHIFT) // xw)),
            pl.BlockSpec((t, FO_W), lambda bi: (bi, (OFF_FG - FO_SHIFT) // FO_W)),
            pl.BlockSpec((t, LANES), lambda bi: (bi, (OFF_FG - FO_SHIFT + FO_W) // LANES)),
            pl.BlockSpec((FO_GC, 2 * FO_GC), lambda bi: (0, 0)),
            pl.BlockSpec((t, 2 * t), lambda bi: (0, 0)),
            pl.BlockSpec((FO_GROUPS, FO_GC, FO_GC), lambda bi: (0, 0, 0)),
        ],
        out_specs=pl.BlockSpec((t, FO_W), lambda bi: (bi, 0)),
        out_shape=jax.ShapeDtypeStruct((b * t, FO_W), BF16),
        scratch_shapes=[pltpu.VMEM((2 * t, FO_W), BF16)],
        compiler_params=_cparams("parallel"),
        name="fourier",
    )(z, z, z, jnp.asarray(wc), jnp.asarray(wt), w_fno_l)


OUT_TN = 512


def _outproj_kernel(oa_ref, om_ref, of_ref, wa_ref, wm_ref, wf_ref, x_ref, gate_ref, y_ref, *, per_row_mod):
    row = 1 + pl.program_id(0) if per_row_mod else 0
    gate = gate_ref[pl.ds(row, 1), :]
    y = (jnp.dot(oa_ref[...], wa_ref[...].astype(BF16), preferred_element_type=F32)
         + jnp.dot(om_ref[...], wm_ref[...].astype(BF16), preferred_element_type=F32)
         + jnp.dot(of_ref[...], wf_ref[...].astype(BF16), preferred_element_type=F32))
    y_ref[...] = x_ref[...] + gate * y


def _outproj(o_att, o_ml, o_fo, w_out, layer, x2d, mod_l, *, per_row_mod):
    m = x2d.shape[0]
    return pl.pallas_call(
        functools.partial(_outproj_kernel, per_row_mod=per_row_mod),
        grid=(m // ROW_TILE, D_MODEL // OUT_TN),
        in_specs=[
            pl.BlockSpec((ROW_TILE, ATT_W), lambda i, j: (i, 0)),
            pl.BlockSpec((ROW_TILE, ML_W), lambda i, j: (i, 0)),
            pl.BlockSpec((ROW_TILE, FO_W), lambda i, j: (i, 0)),
            pl.BlockSpec((None, ATT_W, OUT_TN), lambda i, j: (layer, 0, j)),
            pl.BlockSpec((None, ML_W, OUT_TN), lambda i, j: (layer, ATT_W // ML_W, j)),
            pl.BlockSpec((None, FO_W, OUT_TN), lambda i, j: (layer, (ATT_W + ML_W) // FO_W, j)),
            pl.BlockSpec((ROW_TILE, OUT_TN), lambda i, j: (i, j)),
            pl.BlockSpec((8, OUT_TN), lambda i, j: (0, 2 * D_MODEL // OUT_TN + j)),
        ],
        out_specs=pl.BlockSpec((ROW_TILE, OUT_TN), lambda i, j: (i, j)),
        out_shape=jax.ShapeDtypeStruct((m, D_MODEL), F32),
        compiler_params=_cparams("parallel", "parallel"),
        name="outproj",
    )(o_att, o_ml, o_fo, w_out, w_out, w_out, x2d, mod_l)


def _rope_tables(t):
    half = HEAD_DIM // 2
    inv_freq = ROPE_BASE ** (-jnp.arange(0, half, 2, dtype=F32) / half)
    n_rows = t // GRID_W
    rows = jnp.repeat(jnp.arange(n_rows, dtype=F32), GRID_W)
    cols = jnp.tile(jnp.arange(GRID_W, dtype=F32), n_rows)
    ar = rows[:, None] * inv_freq
    ac = cols[:, None] * inv_freq
    cos = jnp.concatenate([jnp.cos(ar), jnp.cos(ar), jnp.cos(ac), jnp.cos(ac)], axis=1)
    sin = jnp.concatenate([-jnp.sin(ar), jnp.sin(ar), -jnp.sin(ac), jnp.sin(ac)], axis=1)
    return cos, sin


def _layer(x2d, mod_l, lw, layer, *, b, t, latent, rope=None, cache=None, state=None):
    norm_w, w_in_t, b_if, conv_w, conv_b, q_norm, k_norm, ml_norm, w_fno, w_out = lw
    z = _inproj(x2d, mod_l, norm_w, w_in_t, layer, per_row_mod=latent)
    o_att, k_new = _attention(z, q_norm, k_norm, b=b, t=t, rope=rope, cache=cache)
    o_ml, c_f, n_f, m_f = _mlstm(z, b_if, conv_w, conv_b, ml_norm, b=b, t=t, state=state)
    o_fo = _fourier(z, w_fno, b=b, t=t)
    y = _outproj(o_att, o_ml, o_fo, w_out, layer, x2d, mod_l, per_row_mod=latent)
    v_new = z[:, OFF_AV:OFF_AV + ATT_KV_W]
    return y, (k_new, v_new, c_f, n_f, m_f)


def kernel(x_prompt, x_sample, cache_k, cache_v, state_C, state_n, state_m, c, c_ctx, norm_w, w_mod, b_mod, w_in,
           b_if, conv_w, conv_b, q_norm, k_norm, ml_norm, w_fno, w_out):
    bp, tp, _ = x_prompt.shape
    bs, ts, _ = x_sample.shape
    assert tp == ML_CHUNK and ts == ROW_TILE and (bp * tp) % ROW_TILE == 0 and bs + 1 <= 8
    past = cache_k.shape[2]

    cond8 = jnp.concatenate([c_ctx[None, :], c, jnp.zeros((8 - 1 - bs, D_MODEL), F32)], axis=0)
    mod = _modulation(cond8, w_mod, b_mod)

    rope = _rope_tables(ts)
    ck = cache_k.reshape(bs, DEPTH, past, ATT_KV_W)
    cv = cache_v.reshape(bs, DEPTH, past, ATT_KV_W)
    w_in_t = jnp.swapaxes(w_in, 1, 2)

    xp = x_prompt.reshape(bp * tp, D_MODEL)
    xs = x_sample.reshape(bs * ts, D_MODEL)
    new_k, new_v, new_c, new_n, new_m = [], [], [], [], []
    for l in range(DEPTH):
        lw = (norm_w[l], w_in_t, b_if[l], conv_w[l], conv_b[l], q_norm[l], k_norm[l], ml_norm[l], w_fno[l], w_out)
        xp, (k_l, v_l, c_l, n_l, m_l) = _layer(xp, mod[l], lw, l, b=bp, t=tp, latent=False)
        new_k.append(k_l.reshape(bp, tp, ATT_KV_HEADS, HEAD_DIM))
        new_v.append(v_l.reshape(bp, tp, ATT_KV_HEADS, HEAD_DIM))
        new_c.append(c_l)
        new_n.append(n_l)
        new_m.append(m_l)
        xs, _ = _layer(xs, mod[l], lw, l, b=bs, t=ts, latent=True, rope=rope,
                       cache=(ck, cv, l), state=(state_C, state_n, state_m, l))
    return (xp.reshape(bp, tp, D_MODEL), xs.reshape(bs, ts, D_MODEL), jnp.stack(new_k, axis=1),
            jnp.stack(new_v, axis=1), jnp.stack(new_c, axis=1), jnp.stack(new_n, axis=1), jnp.stack(new_m, axis=1))
```

```python
import functools

import numpy as np
import jax
import jax.numpy as jnp
from jax import lax
from jax.experimental import pallas as pl
from jax.experimental.pallas import tpu as pltpu

D_MODEL = 2048
DEPTH = 2
GRID_W = 64
HEAD_DIM = 128
ATT_HEADS = 8
ATT_KV_HEADS = 2
ATT_GROUPS = ATT_HEADS // ATT_KV_HEADS
ATT_W = ATT_HEADS * HEAD_DIM
ATT_KV_W = ATT_KV_HEADS * HEAD_DIM
ML_HEADS = 4
ML_DK = 128
ML_DV = 128
ML_W = ML_HEADS * ML_DV
FO_GROUPS = 4
FO_GC = 128
FO_W = FO_GROUPS * FO_GC
D_MIX = ATT_W + ML_W + FO_W
N_GATES = 4 * ML_HEADS
D_PROJ = 2 * ATT_W + 2 * ATT_KV_W + 5 * ML_W + N_GATES + 2 * FO_W
CONV_W = 3
ROPE_BASE = 10000.0
EPS = 1e-6

LANES = 128
OFF_AQ = 0
OFF_AK = OFF_AQ + ATT_W
OFF_AV = OFF_AK + ATT_KV_W
OFF_AG = OFF_AV + ATT_KV_W
OFF_MQ = OFF_AG + ATT_W
OFF_MK = OFF_MQ + ML_W
OFF_MV = OFF_MK + ML_W
OFF_MO = OFF_MV + ML_W
OFF_MG = OFF_MO + ML_W
OFF_IF = OFF_MG + ML_W
OFF_FX = OFF_IF + N_GATES
OFF_FG = OFF_FX + FO_W
FO_SHIFT = OFF_FX % LANES

PROJ_TN = 1024
Z_W = (D_PROJ // PROJ_TN) * PROJ_TN
Z_TAIL = D_PROJ - Z_W
ROW_TILE = 1024
ML_CHUNK = 256
ML_EXT = ML_DV + 16
ATT_TQ = 256
VMEM_LIMIT = 56 * 1024 * 1024

BF16 = jnp.bfloat16
F32 = jnp.float32
NT_DIMS = (((1,), (1,)), ((), ()))


def _cparams(*sem):
    return pltpu.CompilerParams(dimension_semantics=sem, vmem_limit_bytes=VMEM_LIMIT)


def _silu(x):
    return x * (1.0 / (1.0 + jnp.exp(-x)))


def _sigmoid(x):
    return 1.0 / (1.0 + jnp.exp(-x))


def _rms(x, w):
    ms = jnp.mean(x * x, axis=-1, keepdims=True)
    return x * lax.rsqrt(ms + EPS) * w


MOD_TN = 768


def _mod_kernel(cond_ref, w_ref, b_ref, o_ref):
    a = _silu(cond_ref[...]).astype(BF16)
    o_ref[...] = jnp.dot(a, w_ref[...].astype(BF16), preferred_element_type=F32) + b_ref[...]


def _modulation(cond8, w_mod, b_mod):
    n = 3 * D_MODEL
    return pl.pallas_call(
        _mod_kernel,
        grid=(DEPTH, n // MOD_TN),
        in_specs=[
            pl.BlockSpec((8, D_MODEL), lambda l, j: (0, 0)),
            pl.BlockSpec((None, D_MODEL, MOD_TN), lambda l, j: (l, 0, j)),
            pl.BlockSpec((None, 1, MOD_TN), lambda l, j: (l, 0, j)),
        ],
        out_specs=pl.BlockSpec((None, 8, MOD_TN), lambda l, j: (l, 0, j)),
        out_shape=jax.ShapeDtypeStruct((DEPTH, 8, n), F32),
        compiler_params=_cparams("parallel", "parallel"),
        name="modulation",
    )(cond8, w_mod, b_mod.reshape(DEPTH, 1, n))


NORM_ROWS = 16


def _inproj_kernel(x_ref, mod_ref, nw_ref, w_ref, wt_ref, z_ref, zt_ref, h_ref, *, per_row_mod):
    i = pl.program_id(0)
    j = pl.program_id(1)

    @pl.when(j == 0)
    def _():
        row = 1 + i if per_row_mod else 0
        shift = mod_ref[pl.ds(row, 1), 0:D_MODEL]
        gain = nw_ref[...] * (1.0 + mod_ref[pl.ds(row, 1), D_MODEL:2 * D_MODEL])

        def body(r, carry):
            sl = pl.ds(pl.multiple_of(r * NORM_ROWS, NORM_ROWS), NORM_ROWS)
            x = x_ref[sl, :]
            inv = lax.rsqrt(jnp.mean(x * x, axis=-1, keepdims=True) + EPS)
            h_ref[sl, :] = (x * inv * gain + shift).astype(BF16)
            return carry

        lax.fori_loop(0, ROW_TILE // NORM_ROWS, body, 0, unroll=8)

    z_ref[...] = lax.dot_general(h_ref[...], w_ref[...].astype(BF16), NT_DIMS, preferred_element_type=F32)

    @pl.when(j == Z_W // PROJ_TN - 1)
    def _():
        zt = lax.dot_general(h_ref[...], wt_ref[...].astype(BF16), NT_DIMS, preferred_element_type=F32)
        col = lax.broadcasted_iota(jnp.int32, zt.shape, 1)
        zt_ref[...] = jnp.where(col < Z_TAIL, zt, 0.0)


def _inproj(x2d, mod_l, norm_w_l, w_in_t, layer, *, per_row_mod):
    m = x2d.shape[0]
    return pl.pallas_call(
        functools.partial(_inproj_kernel, per_row_mod=per_row_mod),
        grid=(m // ROW_TILE, Z_W // PROJ_TN),
        in_specs=[
            pl.BlockSpec((ROW_TILE, D_MODEL), lambda i, j: (i, 0)),
            pl.BlockSpec((8, 3 * D_MODEL), lambda i, j: (0, 0)),
            pl.BlockSpec((1, D_MODEL), lambda i, j: (0, 0)),
            pl.BlockSpec((None, PROJ_TN, D_MODEL), lambda i, j: (layer, j, 0)),
            pl.BlockSpec((None, LANES, D_MODEL), lambda i, j: (layer, Z_W // LANES, 0)),
        ],
        out_specs=[
            pl.BlockSpec((ROW_TILE, PROJ_TN), lambda i, j: (i, j)),
            pl.BlockSpec((ROW_TILE, LANES), lambda i, j: (i, 0)),
        ],
        out_shape=[jax.ShapeDtypeStruct((m, Z_W), F32), jax.ShapeDtypeStruct((m, LANES), F32)],
        scratch_shapes=[pltpu.VMEM((ROW_TILE, D_MODEL), BF16)],
        compiler_params=_cparams("parallel", "arbitrary"),
        name="inproj",
    )(x2d, mod_l, norm_w_l.reshape(1, D_MODEL), w_in_t, w_in_t)


def _rope(x, cos, sin):
    lane = lax.broadcasted_iota(jnp.int32, x.shape, 1)
    quarter = HEAD_DIM // 4
    partner = jnp.where((lane % (2 * quarter)) < quarter,
                        pltpu.roll(x, HEAD_DIM - quarter, axis=1), pltpu.roll(x, quarter, axis=1))
    return x * cos + partner * sin


LOG2E = 1.4426950408889634


def _attn_kernel(*refs, t, hk, latent):
    zq_ref, zk_ref, zv_ref = refs[:3]
    zg_refs = refs[3:3 + hk]
    rest = refs[3 + hk:]
    if latent:
        qn_ref, kn_ref, cos_ref, sin_ref, ck_ref, cv_ref, o_ref, knew_ref, kall_ref, vall_ref = rest
        past = ck_ref.shape[0]
    else:
        qn_ref, kn_ref, o_ref, knew_ref, kall_ref, vall_ref = rest
        past = 0
    qi = pl.program_id(2)

    @pl.when(qi == 0)
    def _():
        for j in range(hk):
            kc = slice(j * HEAD_DIM, (j + 1) * HEAD_DIM)
            k = _rms(zk_ref[:, kc], kn_ref[...])
            if latent:
                k = _rope(k, cos_ref[...], sin_ref[...])
                kall_ref[j, 0:past, :] = ck_ref[:, kc].astype(BF16)
                vall_ref[j, 0:past, :] = cv_ref[:, kc].astype(BF16)
            knew_ref[:, kc] = k
            kall_ref[j, past:past + t, :] = k.astype(BF16)
            vall_ref[j, past:past + t, :] = zv_ref[:, kc].astype(BF16)

    q_scale = (HEAD_DIM ** -0.5) * LOG2E
    if latent:
        rows = pl.ds(pl.multiple_of(qi * ATT_TQ, ATT_TQ), ATT_TQ)
        cos_q = cos_ref[rows, :]
        sin_q = sin_ref[rows, :]
    for j in range(hk):
        kb = kall_ref[j]
        vb = vall_ref[j]
        for g in range(ATT_GROUPS):
            cols = slice((j * ATT_GROUPS + g) * HEAD_DIM, (j * ATT_GROUPS + g + 1) * HEAD_DIM)
            gcols = slice(g * HEAD_DIM, (g + 1) * HEAD_DIM)
            q = _rms(zq_ref[:, cols], qn_ref[...])
            if latent:
                q = _rope(q, cos_q, sin_q)
            qb = (q * q_scale).astype(BF16)
            s = lax.dot_general(qb, kb, NT_DIMS, preferred_element_type=F32)
            p = jnp.exp2(s - jnp.max(s, axis=-1, keepdims=True))
            den = jnp.sum(p, axis=-1, keepdims=True)
            o = jnp.dot(p.astype(BF16), vb, preferred_element_type=F32) * (1.0 / den)
            o_ref[:, cols] = (o * _silu(zg_refs[j][:, gcols])).astype(BF16)


def _attention(z, q_norm_l, k_norm_l, *, b, t, rope=None, cache=None):
    latent = rope is not None
    hk = 1 if latent else ATT_KV_HEADS
    nq = t // ATT_TQ
    gw = ATT_GROUPS * HEAD_DIM
    in_specs = [
        pl.BlockSpec((ATT_TQ, hk * gw), lambda bi, kg, qi: (bi * nq + qi, OFF_AQ // (hk * gw) + kg)),
        pl.BlockSpec((t, hk * HEAD_DIM), lambda bi, kg, qi: (bi, OFF_AK // (hk * HEAD_DIM) + kg)),
        pl.BlockSpec((t, hk * HEAD_DIM), lambda bi, kg, qi: (bi, OFF_AV // (hk * HEAD_DIM) + kg)),
    ]
    in_specs += [pl.BlockSpec((ATT_TQ, gw), lambda bi, kg, qi, j=j: (bi * nq + qi, OFF_AG // gw + kg * hk + j))
                 for j in range(hk)]
    in_specs += [pl.BlockSpec((1, HEAD_DIM), lambda bi, kg, qi: (0, 0))] * 2
    args = [z] * (3 + hk) + [q_norm_l.reshape(1, HEAD_DIM), k_norm_l.reshape(1, HEAD_DIM)]
    tk = t
    if latent:
        cos, sin = rope
        ck, cv, layer = cache
        past = ck.shape[2]
        tk = past + t
        in_specs += [
            pl.BlockSpec((t, HEAD_DIM), lambda bi, kg, qi: (0, 0)),
            pl.BlockSpec((t, HEAD_DIM), lambda bi, kg, qi: (0, 0)),
            pl.BlockSpec((None, None, past, hk * HEAD_DIM), lambda bi, kg, qi: (bi, layer, 0, kg)),
            pl.BlockSpec((None, None, past, hk * HEAD_DIM), lambda bi, kg, qi: (bi, layer, 0, kg)),
        ]
        args += [cos, sin, ck, cv]
    return pl.pallas_call(
        functools.partial(_attn_kernel, t=t, hk=hk, latent=latent),
        grid=(b, ATT_KV_HEADS // hk, nq),
        in_specs=in_specs,
        out_specs=[
            pl.BlockSpec((ATT_TQ, hk * gw), lambda bi, kg, qi: (bi * nq + qi, kg)),
            pl.BlockSpec((t, hk * HEAD_DIM), lambda bi, kg, qi: (bi, kg)),
        ],
        out_shape=[
            jax.ShapeDtypeStruct((b * t, ATT_W), BF16),
            jax.ShapeDtypeStruct((b * t, ATT_KV_W), F32),
        ],
        scratch_shapes=[pltpu.VMEM((hk, tk, HEAD_DIM), BF16), pltpu.VMEM((hk, tk, HEAD_DIM), BF16)],
        compiler_params=_cparams("parallel", "parallel", "arbitrary"),
        name="attention",
    )(*args)


def _split3_dot(tri, x):
    hi = x.astype(BF16)
    r1 = x - hi.astype(F32)
    mid = r1.astype(BF16)
    lo = (r1 - mid.astype(F32)).astype(BF16)
    d = lambda a: jnp.dot(tri, a, preferred_element_type=F32)
    return d(hi) + d(mid) + d(lo)


def _shift_rows(x, direction):
    n = x.shape[0]
    row = lax.broadcasted_iota(jnp.int32, x.shape, 0)
    if direction > 0:
        return jnp.where(row == 0, 0.0, pltpu.roll(x, 1, axis=0))
    return jnp.where(row == n - 1, 0.0, pltpu.roll(x, n - 1, axis=0))


def _conv_silu(x, w, b):
    y = _shift_rows(x, 1) * w[0:1, :] + x * w[1:2, :] + _shift_rows(x, -1) * w[2:3, :] + b
    return _silu(y)


def _mlstm_kernel(*refs, t, zero_init):
    if zero_init:
        z_ref, zif_ref, bif_ref, cw_ref, cb_ref, nw_ref, o_ref, c_out, n_out, m_out = refs
    else:
        z_ref, zif_ref, bif_ref, cw_ref, cb_ref, nw_ref, c0_ref, n0_ref, m0_ref, o_ref, c_out, n_out, m_out = refs
    L = ML_CHUNK
    nc = t // L
    chunk = lambda a, c: a[c * L:(c + 1) * L]

    gates = zif_ref[...] + bif_ref[...]
    logsig = jnp.minimum(gates, 0.0) - jnp.log1p(jnp.exp(-jnp.abs(gates)))
    gates_t = gates.T
    r_i = lax.broadcasted_iota(jnp.int32, (L, L), 0)
    c_i = lax.broadcasted_iota(jnp.int32, (L, L), 1)
    lower, upper = r_i >= c_i, r_i <= c_i
    cum_tris = (lower.astype(BF16), upper.astype(BF16))
    dmasks = (upper, lower)
    cums = [[_split3_dot(cum_tris[d], chunk(logsig, c)) for c in range(nc)] for d in range(2)]
    cums_t = [[x.T for x in row] for row in cums]
    ext_rows = (lax.broadcasted_iota(jnp.int32, (ML_EXT - ML_DV, t), 0) == 0).astype(F32)
    lchunk = lambda a, c: a[:, c * L:(c + 1) * L]

    def local_part(qk_c, kb_c, vt_c, vtb_c, b_row, b_col, i_row, i_col, d):
        dm = jnp.where(dmasks[d], b_row - b_col + i_col, -jnp.inf)
        m_loc = jnp.max(dm, axis=0, keepdims=True)
        s = (qk_c * jnp.exp(dm - m_loc)).astype(BF16)
        bm = jnp.dot(vtb_c, s, preferred_element_type=F32)
        b_end = b_row[:, 0:1] if d else b_row[:, L - 1:L]
        g = b_end - b_row + i_row
        g_max = jnp.max(g, axis=1, keepdims=True)
        vw = (vt_c * jnp.exp(g - g_max)).astype(BF16)
        u = jnp.dot(vw, kb_c, preferred_element_type=F32)
        return m_loc, bm, b_end, g_max, u

    def scan_step(qt_c, b_row, loc, c_t, m):
        m_loc, bm, b_end, g_max, u = loc
        inter = b_row + m
        m_row = jnp.maximum(inter, m_loc)
        nd = jnp.exp(m_loc - m_row) * bm
        if c_t is not None:
            nd = nd + jnp.exp(inter - m_row) * jnp.dot(c_t.astype(BF16), qt_c, preferred_element_type=F32)
        h_t = nd[:ML_DV] / jnp.maximum(jnp.abs(nd[ML_DV:ML_DV + 1]), jnp.exp(-m_row))
        m_new = jnp.maximum(b_end + m, g_max)
        c_new = jnp.exp(g_max - m_new) * u
        if c_t is not None:
            c_new = c_new + jnp.exp(b_end + m - m_new) * c_t
        return h_t, c_new, m_new

    for hd in range(ML_HEADS):
        cols = slice(hd * ML_DK, (hd + 1) * ML_DK)
        kcols = slice(ML_W + hd * ML_DK, ML_W + (hd + 1) * ML_DK)
        vcols = slice(2 * ML_W + hd * ML_DV, 2 * ML_W + (hd + 1) * ML_DV)
        ocols = slice(3 * ML_W + hd * ML_DV, 3 * ML_W + (hd + 1) * ML_DV)
        gcols = slice(4 * ML_W + hd * ML_DV, 4 * ML_W + (hd + 1) * ML_DV)
        qt = _conv_silu(z_ref[:, cols], cw_ref[:, cols], cb_ref[:, cols]).T.astype(BF16)
        kb = (_conv_silu(z_ref[:, kcols], cw_ref[:, kcols], cb_ref[:, kcols]) * (ML_DK ** -0.5)).astype(BF16)
        vt = jnp.concatenate([z_ref[:, vcols].T, ext_rows], axis=0)
        vtb = vt.astype(BF16)
        qk = [jnp.dot(chunk(kb, c), lchunk(qt, c), preferred_element_type=F32) for c in range(nc)]

        h_sum = None
        for d in range(2):
            i_lane = 2 * d * ML_HEADS + hd
            f_lane = (2 * d + 1) * ML_HEADS + hd
            b_rows = [cums_t[d][c][f_lane:f_lane + 1, :] for c in range(nc)]
            locs = [local_part(qk[c], chunk(kb, c), lchunk(vt, c), lchunk(vtb, c), b_rows[c],
                               cums[d][c][:, f_lane:f_lane + 1], lchunk(gates_t, c)[i_lane:i_lane + 1, :],
                               chunk(gates, c)[:, i_lane:i_lane + 1], d) for c in range(nc)]
            if zero_init:
                c_t, m = None, jnp.zeros((1, 1), F32)
            else:
                c_t = jnp.concatenate([c0_ref[d, hd].T, n0_ref[d, hd:hd + 1, :],
                                       jnp.zeros((ML_EXT - ML_DV - 1, ML_DK), F32)], axis=0)
                m = m0_ref[d:d + 1, hd:hd + 1]
            h_parts = [None] * nc
            for step in range(nc):
                c = nc - 1 - step if d else step
                h_parts[c], c_t, m = scan_step(lchunk(qt, c), b_rows[c], locs[c], c_t, m)
            h_dir = jnp.concatenate(h_parts, axis=1) if nc > 1 else h_parts[0]
            h_sum = h_dir if h_sum is None else h_sum + h_dir
            c_out[d, hd] = c_t[:ML_DV].T
            n_out[d, hd:hd + 1, :] = c_t[ML_DV:ML_DV + 1]
            m_out[d:d + 1, hd:hd + 1] = m

        hm = _rms(h_sum.T, nw_ref[:, cols])
        o_ref[:, cols] = (hm * _sigmoid(z_ref[:, ocols]) * _silu(z_ref[:, gcols])).astype(BF16)


def _mlstm(z, b_if_l, conv_w_l, conv_b_l, ml_norm_l, *, b, t, state=None):
    zero_init = state is None
    zw = 5 * ML_W
    in_specs = [
        pl.BlockSpec((t, zw), lambda bi: (bi, OFF_MQ // zw)),
        pl.BlockSpec((t, LANES), lambda bi: (bi, OFF_IF // LANES)),
        pl.BlockSpec((1, LANES), lambda bi: (0, 0)),
        pl.BlockSpec((CONV_W, 2 * ML_W), lambda bi: (0, 0)),
        pl.BlockSpec((1, 2 * ML_W), lambda bi: (0, 0)),
        pl.BlockSpec((1, ML_W), lambda bi: (0, 0)),
    ]
    bif = jnp.pad(b_if_l, (0, LANES - N_GATES)).reshape(1, LANES)
    args = [z, z, bif, conv_w_l, conv_b_l.reshape(1, 2 * ML_W), ml_norm_l.reshape(1, ML_W)]
    if not zero_init:
        c0, n0, m0, layer = state
        in_specs += [
            pl.BlockSpec((None, None, 2, ML_HEADS, ML_DK, ML_DV), lambda bi: (bi, layer, 0, 0, 0, 0)),
            pl.BlockSpec((None, None, 2, ML_HEADS, ML_DK), lambda bi: (bi, layer, 0, 0, 0)),
            pl.BlockSpec((None, None, 2, ML_HEADS), lambda bi: (bi, layer, 0, 0)),
        ]
        args += [c0, n0, m0]
    return pl.pallas_call(
        functools.partial(_mlstm_kernel, t=t, zero_init=zero_init),
        grid=(b,),
        in_specs=in_specs,
        out_specs=[
            pl.BlockSpec((t, ML_W), lambda bi: (bi, 0)),
            pl.BlockSpec((None, 2, ML_HEADS, ML_DK, ML_DV), lambda bi: (bi, 0, 0, 0, 0)),
            pl.BlockSpec((None, 2, ML_HEADS, ML_DK), lambda bi: (bi, 0, 0, 0)),
            pl.BlockSpec((None, 2, ML_HEADS), lambda bi: (bi, 0, 0)),
        ],
        out_shape=[
            jax.ShapeDtypeStruct((b * t, ML_W), BF16),
            jax.ShapeDtypeStruct((b, 2, ML_HEADS, ML_DK, ML_DV), F32),
            jax.ShapeDtypeStruct((b, 2, ML_HEADS, ML_DK), F32),
            jax.ShapeDtypeStruct((b, 2, ML_HEADS), F32),
        ],
        compiler_params=_cparams("parallel"),
        name="mlstm",
    )(*args)


def _dft_tables(t):
    def cs(n):
        idx = np.arange(n, dtype=np.int64)
        ang = 2.0 * np.pi * ((idx[:, None] * idx[None, :]) % n).astype(np.float64) / n
        return np.cos(ang), np.sin(ang)
    cc, sc = cs(FO_GC)
    ct, st = cs(t)
    return (np.concatenate([cc, sc], axis=1).astype(np.float32),
            np.concatenate([ct, -st], axis=1).astype(np.float32))


FO_RT = 256


def _fourier_kernel(zx_ref, zga_ref, zgb_ref, wc_ref, wt_ref, wf_ref, o_ref, y_ref, *, t):
    x = zx_ref[:, FO_SHIFT:FO_SHIFT + FO_W]
    wc = wc_ref[...].astype(BF16)
    for g in range(FO_GROUPS):
        cols = slice(g * FO_GC, (g + 1) * FO_GC)
        y = jnp.dot(x[:, cols].astype(BF16), wc, preferred_element_type=F32)
        y_ref[0:t, cols] = y[:, :FO_GC].astype(BF16)
        y_ref[t:2 * t, cols] = y[:, FO_GC:].astype(BF16)
    scale = (t * FO_GC) ** -0.5
    for r in range(t // FO_RT):
        rows = slice(r * FO_RT, (r + 1) * FO_RT)
        f = jnp.dot(wt_ref[rows, :].astype(BF16), y_ref[...], preferred_element_type=F32) * scale
        fg = jnp.concatenate([zga_ref[rows, FO_SHIFT:], zgb_ref[rows, :FO_SHIFT]], axis=1)
        for g in range(FO_GROUPS):
            cols = slice(g * FO_GC, (g + 1) * FO_GC)
            og = jnp.dot(f[:, cols].astype(BF16), wf_ref[g].astype(BF16), preferred_element_type=F32)
            o_ref[rows, cols] = (og * _silu(fg[:, cols])).astype(BF16)


def _fourier(z, z_tail, w_fno_l, *, b, t):
    assert OFF_FG - FO_SHIFT + FO_W == Z_W and Z_TAIL == FO_SHIFT
    wc, wt = _dft_tables(t)
    xw = FO_W + LANES
    return pl.pallas_call(
        functools.partial(_fourier_kernel, t=t),
        grid=(b,),
        in_specs=[
            pl.BlockSpec((t, xw), lambda bi: (bi, (OFF_FX - FO_SHIFT) // xw)),
            pl.BlockSpec((t, FO_W), lambda bi: (bi, (OFF_FG - FO_SHIFT) // FO_W)),
            pl.BlockSpec((t, LANES), lambda bi: (bi, 0)),
            pl.BlockSpec((FO_GC, 2 * FO_GC), lambda bi: (0, 0)),
            pl.BlockSpec((t, 2 * t), lambda bi: (0, 0)),
            pl.BlockSpec((FO_GROUPS, FO_GC, FO_GC), lambda bi: (0, 0, 0)),
        ],
        out_specs=pl.BlockSpec((t, FO_W), lambda bi: (bi, 0)),
        out_shape=jax.ShapeDtypeStruct((b * t, FO_W), BF16),
        scratch_shapes=[pltpu.VMEM((2 * t, FO_W), BF16)],
        compiler_params=_cparams("parallel"),
        name="fourier",
    )(z, z, z_tail, jnp.asarray(wc), jnp.asarray(wt), w_fno_l)


OUT_TN = 1024
OUT_TM = 512


def _outproj_kernel(oa_ref, om_ref, of_ref, wa_ref, wm_ref, wf_ref, x_ref, gate_ref, y_ref, wb_ref, *, per_row_mod):
    i = pl.program_id(1)

    @pl.when(i == 0)
    def _():
        wb_ref[0:ATT_W, :] = wa_ref[...].astype(BF16)
        wb_ref[ATT_W:ATT_W + ML_W, :] = wm_ref[...].astype(BF16)
        wb_ref[ATT_W + ML_W:D_MIX, :] = wf_ref[...].astype(BF16)

    row = 1 + (i * OUT_TM) // ROW_TILE if per_row_mod else 0
    gate = gate_ref[pl.ds(row, 1), :]
    y = (jnp.dot(oa_ref[...], wb_ref[0:ATT_W, :], preferred_element_type=F32)
         + jnp.dot(om_ref[...], wb_ref[ATT_W:ATT_W + ML_W, :], preferred_element_type=F32)
         + jnp.dot(of_ref[...], wb_ref[ATT_W + ML_W:D_MIX, :], preferred_element_type=F32))
    y_ref[...] = x_ref[...] + gate * y


def _outproj(o_att, o_ml, o_fo, w_out, layer, x2d, mod_l, *, per_row_mod):
    m = x2d.shape[0]
    return pl.pallas_call(
        functools.partial(_outproj_kernel, per_row_mod=per_row_mod),
        grid=(D_MODEL // OUT_TN, m // OUT_TM),
        in_specs=[
            pl.BlockSpec((OUT_TM, ATT_W), lambda n, i: (i, 0)),
            pl.BlockSpec((OUT_TM, ML_W), lambda n, i: (i, 0)),
            pl.BlockSpec((OUT_TM, FO_W), lambda n, i: (i, 0)),
            pl.BlockSpec((None, ATT_W, OUT_TN), lambda n, i: (layer, 0, n)),
            pl.BlockSpec((None, ML_W, OUT_TN), lambda n, i: (layer, ATT_W // ML_W, n)),
            pl.BlockSpec((None, FO_W, OUT_TN), lambda n, i: (layer, (ATT_W + ML_W) // FO_W, n)),
            pl.BlockSpec((OUT_TM, OUT_TN), lambda n, i: (i, n)),
            pl.BlockSpec((8, OUT_TN), lambda n, i: (0, 2 * D_MODEL // OUT_TN + n)),
        ],
        out_specs=pl.BlockSpec((OUT_TM, OUT_TN), lambda n, i: (i, n)),
        out_shape=jax.ShapeDtypeStruct((m, D_MODEL), F32),
        scratch_shapes=[pltpu.VMEM((D_MIX, OUT_TN), BF16)],
        compiler_params=_cparams("parallel", "arbitrary"),
        name="outproj",
    )(o_att, o_ml, o_fo, w_out, w_out, w_out, x2d, mod_l)


def _rope_tables(t):
    half = HEAD_DIM // 2
    inv_freq = ROPE_BASE ** (-jnp.arange(0, half, 2, dtype=F32) / half)
    n_rows = t // GRID_W
    rows = jnp.repeat(jnp.arange(n_rows, dtype=F32), GRID_W)
    cols = jnp.tile(jnp.arange(GRID_W, dtype=F32), n_rows)
    ar = rows[:, None] * inv_freq
    ac = cols[:, None] * inv_freq
    cos = jnp.concatenate([jnp.cos(ar), jnp.cos(ar), jnp.cos(ac), jnp.cos(ac)], axis=1)
    sin = jnp.concatenate([-jnp.sin(ar), jnp.sin(ar), -jnp.sin(ac), jnp.sin(ac)], axis=1)
    return cos, sin


def _layer(x2d, mod_l, lw, layer, *, b, t, latent, rope=None, cache=None, state=None):
    norm_w, w_in_t, b_if, conv_w, conv_b, q_norm, k_norm, ml_norm, w_fno, w_out = lw
    z, z_tail = _inproj(x2d, mod_l, norm_w, w_in_t, layer, per_row_mod=latent)
    o_att, k_new = _attention(z, q_norm, k_norm, b=b, t=t, rope=rope, cache=cache)
    o_ml, c_f, n_f, m_f = _mlstm(z, b_if, conv_w, conv_b, ml_norm, b=b, t=t, state=state)
    o_fo = _fourier(z, z_tail, w_fno, b=b, t=t)
    y = _outproj(o_att, o_ml, o_fo, w_out, layer, x2d, mod_l, per_row_mod=latent)
    v_new = z[:, OFF_AV:OFF_AV + ATT_KV_W]
    return y, (k_new, v_new, c_f, n_f, m_f)


def kernel(x_prompt, x_sample, cache_k, cache_v, state_C, state_n, state_m, c, c_ctx, norm_w, w_mod, b_mod, w_in,
           b_if, conv_w, conv_b, q_norm, k_norm, ml_norm, w_fno, w_out):
    bp, tp, _ = x_prompt.shape
    bs, ts, _ = x_sample.shape
    assert tp == ML_CHUNK and ts == ROW_TILE and (bp * tp) % ROW_TILE == 0 and bs + 1 <= 8
    past = cache_k.shape[2]

    cond8 = jnp.concatenate([c_ctx[None, :], c, jnp.zeros((8 - 1 - bs, D_MODEL), F32)], axis=0)
    mod = _modulation(cond8, w_mod, b_mod)

    rope = _rope_tables(ts)
    ck = cache_k.reshape(bs, DEPTH, past, ATT_KV_W)
    cv = cache_v.reshape(bs, DEPTH, past, ATT_KV_W)
    w_in_t = jnp.swapaxes(w_in, 1, 2)

    xp = x_prompt.reshape(bp * tp, D_MODEL)
    xs = x_sample.reshape(bs * ts, D_MODEL)
    new_k, new_v, new_c, new_n, new_m = [], [], [], [], []
    for l in range(DEPTH):
        lw = (norm_w[l], w_in_t, b_if[l], conv_w[l], conv_b[l], q_norm[l], k_norm[l], ml_norm[l], w_fno[l], w_out)
        xp, (k_l, v_l, c_l, n_l, m_l) = _layer(xp, mod[l], lw, l, b=bp, t=tp, latent=False)
        new_k.append(k_l.reshape(bp, tp, ATT_KV_HEADS, HEAD_DIM))
        new_v.append(v_l.reshape(bp, tp, ATT_KV_HEADS, HEAD_DIM))
        new_c.append(c_l)
        new_n.append(n_l)
        new_m.append(m_l)
        xs, _ = _layer(xs, mod[l], lw, l, b=bs, t=ts, latent=True, rope=rope,
                       cache=(ck, cv, l), state=(state_C, state_n, state_m, l))
    return (xp.reshape(bp, tp, D_MODEL), xs.reshape(bs, ts, D_MODEL), jnp.stack(new_k, axis=1),
            jnp.stack(new_v, axis=1), jnp.stack(new_c, axis=1), jnp.stack(new_n, axis=1), jnp.stack(new_m, axis=1))
```

```python
import functools

import numpy as np
import jax
import jax.numpy as jnp
from jax import lax
from jax.experimental import pallas as pl
from jax.experimental.pallas import tpu as pltpu

D_MODEL = 2048
DEPTH = 2
GRID_W = 64
HEAD_DIM = 128
ATT_HEADS = 8
ATT_KV_HEADS = 2
ATT_GROUPS = ATT_HEADS // ATT_KV_HEADS
ATT_W = ATT_HEADS * HEAD_DIM
ATT_KV_W = ATT_KV_HEADS * HEAD_DIM
ML_HEADS = 4
ML_DK = 128
ML_DV = 128
ML_W = ML_HEADS * ML_DV
FO_GROUPS = 4
FO_GC = 128
FO_W = FO_GROUPS * FO_GC
D_MIX = ATT_W + ML_W + FO_W
N_GATES = 4 * ML_HEADS
D_PROJ = 2 * ATT_W + 2 * ATT_KV_W + 5 * ML_W + N_GATES + 2 * FO_W
CONV_W = 3
ROPE_BASE = 10000.0
EPS = 1e-6

LANES = 128
OFF_AQ = 0
OFF_AK = OFF_AQ + ATT_W
OFF_AV = OFF_AK + ATT_KV_W
OFF_AG = OFF_AV + ATT_KV_W
OFF_MQ = OFF_AG + ATT_W
OFF_MK = OFF_MQ + ML_W
OFF_MV = OFF_MK + ML_W
OFF_MO = OFF_MV + ML_W
OFF_MG = OFF_MO + ML_W
OFF_IF = OFF_MG + ML_W
OFF_FX = OFF_IF + N_GATES
OFF_FG = OFF_FX + FO_W
FO_SHIFT = OFF_FX % LANES

PROJ_TN = 1024
Z_W = (D_PROJ // PROJ_TN) * PROJ_TN
Z_TAIL = D_PROJ - Z_W
ROW_TILE = 1024
ML_CHUNK = 256
ML_EXT = ML_DV + 16
ATT_TQ = 256
VMEM_LIMIT = 56 * 1024 * 1024

BF16 = jnp.bfloat16
F32 = jnp.float32
NT_DIMS = (((1,), (1,)), ((), ()))


def _cparams(*sem):
    return pltpu.CompilerParams(dimension_semantics=sem, vmem_limit_bytes=VMEM_LIMIT)


def _silu(x):
    return x * (1.0 / (1.0 + jnp.exp(-x)))


def _sigmoid(x):
    return 1.0 / (1.0 + jnp.exp(-x))


def _rms(x, w):
    ms = jnp.mean(x * x, axis=-1, keepdims=True)
    return x * lax.rsqrt(ms + EPS) * w


MOD_TN = 768


def _mod_kernel(cond_ref, w_ref, b_ref, o_ref):
    a = _silu(cond_ref[...]).astype(BF16)
    o_ref[...] = jnp.dot(a, w_ref[...].astype(BF16), preferred_element_type=F32) + b_ref[...]


def _modulation(cond8, w_mod, b_mod):
    n = 3 * D_MODEL
    return pl.pallas_call(
        _mod_kernel,
        grid=(DEPTH, n // MOD_TN),
        in_specs=[
            pl.BlockSpec((8, D_MODEL), lambda l, j: (0, 0)),
            pl.BlockSpec((None, D_MODEL, MOD_TN), lambda l, j: (l, 0, j)),
            pl.BlockSpec((None, 1, MOD_TN), lambda l, j: (l, 0, j)),
        ],
        out_specs=pl.BlockSpec((None, 8, MOD_TN), lambda l, j: (l, 0, j)),
        out_shape=jax.ShapeDtypeStruct((DEPTH, 8, n), F32),
        compiler_params=_cparams("parallel", "parallel"),
        name="modulation",
    )(cond8, w_mod, b_mod.reshape(DEPTH, 1, n))


NORM_ROWS = 16


def _inproj_kernel(x_ref, mod_ref, nw_ref, w_ref, wt_ref, z_ref, zt_ref, h_ref, *, per_row_mod):
    i = pl.program_id(0)
    j = pl.program_id(1)

    @pl.when(j == 0)
    def _():
        row = 1 + i if per_row_mod else 0
        shift = mod_ref[pl.ds(row, 1), 0:D_MODEL]
        gain = nw_ref[...] * (1.0 + mod_ref[pl.ds(row, 1), D_MODEL:2 * D_MODEL])

        def body(r, carry):
            sl = pl.ds(pl.multiple_of(r * NORM_ROWS, NORM_ROWS), NORM_ROWS)
            x = x_ref[sl, :]
            inv = lax.rsqrt(jnp.mean(x * x, axis=-1, keepdims=True) + EPS)
            h_ref[sl, :] = (x * inv * gain + shift).astype(BF16)
            return carry

        lax.fori_loop(0, ROW_TILE // NORM_ROWS, body, 0, unroll=8)

    z_ref[...] = lax.dot_general(h_ref[...], w_ref[...].astype(BF16), NT_DIMS, preferred_element_type=F32)

    @pl.when(j == Z_W // PROJ_TN - 1)
    def _():
        zt = lax.dot_general(h_ref[...], wt_ref[...].astype(BF16), NT_DIMS, preferred_element_type=F32)
        col = lax.broadcasted_iota(jnp.int32, zt.shape, 1)
        zt_ref[...] = jnp.where(col < Z_TAIL, zt, 0.0)


def _inproj(x2d, mod_l, norm_w_l, w_in_t, layer, *, per_row_mod):
    m = x2d.shape[0]
    return pl.pallas_call(
        functools.partial(_inproj_kernel, per_row_mod=per_row_mod),
        grid=(m // ROW_TILE, Z_W // PROJ_TN),
        in_specs=[
            pl.BlockSpec((ROW_TILE, D_MODEL), lambda i, j: (i, 0)),
            pl.BlockSpec((8, 3 * D_MODEL), lambda i, j: (0, 0)),
            pl.BlockSpec((1, D_MODEL), lambda i, j: (0, 0)),
            pl.BlockSpec((None, PROJ_TN, D_MODEL), lambda i, j: (layer, j, 0)),
            pl.BlockSpec((None, LANES, D_MODEL), lambda i, j: (layer, Z_W // LANES, 0)),
        ],
        out_specs=[
            pl.BlockSpec((ROW_TILE, PROJ_TN), lambda i, j: (i, j)),
            pl.BlockSpec((ROW_TILE, LANES), lambda i, j: (i, 0)),
        ],
        out_shape=[jax.ShapeDtypeStruct((m, Z_W), F32), jax.ShapeDtypeStruct((m, LANES), F32)],
        scratch_shapes=[pltpu.VMEM((ROW_TILE, D_MODEL), BF16)],
        compiler_params=_cparams("parallel", "arbitrary"),
        name="inproj",
    )(x2d, mod_l, norm_w_l.reshape(1, D_MODEL), w_in_t, w_in_t)


def _rope(x, cos, sin):
    lane = lax.broadcasted_iota(jnp.int32, x.shape, 1)
    quarter = HEAD_DIM // 4
    partner = jnp.where((lane % (2 * quarter)) < quarter,
                        pltpu.roll(x, HEAD_DIM - quarter, axis=1), pltpu.roll(x, quarter, axis=1))
    return x * cos + partner * sin


LOG2E = 1.4426950408889634


def _attn_kernel(*refs, t, hk, latent):
    zq_ref, zk_ref, zv_ref = refs[:3]
    zg_refs = refs[3:3 + hk]
    rest = refs[3 + hk:]
    if latent:
        qn_ref, kn_ref, cos_ref, sin_ref, ck_ref, cv_ref, o_ref, knew_ref, kall_ref, vall_ref = rest
        past = ck_ref.shape[0]
    else:
        qn_ref, kn_ref, o_ref, knew_ref, kall_ref, vall_ref = rest
        past = 0
    qi = pl.program_id(2)

    @pl.when(qi == 0)
    def _():
        for j in range(hk):
            kc = slice(j * HEAD_DIM, (j + 1) * HEAD_DIM)
            k = _rms(zk_ref[:, kc], kn_ref[...])
            if latent:
                k = _rope(k, cos_ref[...], sin_ref[...])
                kall_ref[j, 0:past, :] = ck_ref[:, kc].astype(BF16)
                vall_ref[j, 0:past, :] = cv_ref[:, kc].astype(BF16)
            knew_ref[:, kc] = k
            kall_ref[j, past:past + t, :] = k.astype(BF16)
            vall_ref[j, past:past + t, :] = zv_ref[:, kc].astype(BF16)

    q_scale = (HEAD_DIM ** -0.5) * LOG2E
    if latent:
        rows = pl.ds(pl.multiple_of(qi * ATT_TQ, ATT_TQ), ATT_TQ)
        cos_q = cos_ref[rows, :]
        sin_q = sin_ref[rows, :]
    def scores(hd):
        cols = slice(hd * HEAD_DIM, (hd + 1) * HEAD_DIM)
        q = _rms(zq_ref[:, cols], qn_ref[...])
        if latent:
            q = _rope(q, cos_q, sin_q)
        qb = (q * q_scale).astype(BF16)
        return lax.dot_general(qb, kall_ref[hd // ATT_GROUPS], NT_DIMS, preferred_element_type=F32)

    def finish(hd, s):
        j, g = divmod(hd, ATT_GROUPS)
        cols = slice(hd * HEAD_DIM, (hd + 1) * HEAD_DIM)
        gcols = slice(g * HEAD_DIM, (g + 1) * HEAD_DIM)
        p = jnp.exp2(s - jnp.max(s, axis=-1, keepdims=True))
        den = jnp.sum(p, axis=-1, keepdims=True)
        o = jnp.dot(p.astype(BF16), vall_ref[j], preferred_element_type=F32) * (1.0 / den)
        o_ref[:, cols] = (o * _silu(zg_refs[j][:, gcols])).astype(BF16)

    n_heads = hk * ATT_GROUPS
    s_next = scores(0)
    for hd in range(n_heads):
        s_cur = s_next
        if hd + 1 < n_heads:
            s_next = scores(hd + 1)
        finish(hd, s_cur)


def _attention(z, q_norm_l, k_norm_l, *, b, t, rope=None, cache=None):
    latent = rope is not None
    hk = 1 if latent else ATT_KV_HEADS
    nq = t // ATT_TQ
    gw = ATT_GROUPS * HEAD_DIM
    in_specs = [
        pl.BlockSpec((ATT_TQ, hk * gw), lambda bi, kg, qi: (bi * nq + qi, OFF_AQ // (hk * gw) + kg)),
        pl.BlockSpec((t, hk * HEAD_DIM), lambda bi, kg, qi: (bi, OFF_AK // (hk * HEAD_DIM) + kg)),
        pl.BlockSpec((t, hk * HEAD_DIM), lambda bi, kg, qi: (bi, OFF_AV // (hk * HEAD_DIM) + kg)),
    ]
    in_specs += [pl.BlockSpec((ATT_TQ, gw), lambda bi, kg, qi, j=j: (bi * nq + qi, OFF_AG // gw + kg * hk + j))
                 for j in range(hk)]
    in_specs += [pl.BlockSpec((1, HEAD_DIM), lambda bi, kg, qi: (0, 0))] * 2
    args = [z] * (3 + hk) + [q_norm_l.reshape(1, HEAD_DIM), k_norm_l.reshape(1, HEAD_DIM)]
    tk = t
    if latent:
        cos, sin = rope
        ck, cv, layer = cache
        past = ck.shape[2]
        tk = past + t
        in_specs += [
            pl.BlockSpec((t, HEAD_DIM), lambda bi, kg, qi: (0, 0)),
            pl.BlockSpec((t, HEAD_DIM), lambda bi, kg, qi: (0, 0)),
            pl.BlockSpec((None, None, past, hk * HEAD_DIM), lambda bi, kg, qi: (bi, layer, 0, kg)),
            pl.BlockSpec((None, None, past, hk * HEAD_DIM), lambda bi, kg, qi: (bi, layer, 0, kg)),
        ]
        args += [cos, sin, ck, cv]
    return pl.pallas_call(
        functools.partial(_attn_kernel, t=t, hk=hk, latent=latent),
        grid=(b, ATT_KV_HEADS // hk, nq),
        in_specs=in_specs,
        out_specs=[
            pl.BlockSpec((ATT_TQ, hk * gw), lambda bi, kg, qi: (bi * nq + qi, kg)),
            pl.BlockSpec((t, hk * HEAD_DIM), lambda bi, kg, qi: (bi, kg)),
        ],
        out_shape=[
            jax.ShapeDtypeStruct((b * t, ATT_W), BF16),
            jax.ShapeDtypeStruct((b * t, ATT_KV_W), F32),
        ],
        scratch_shapes=[pltpu.VMEM((hk, tk, HEAD_DIM), BF16), pltpu.VMEM((hk, tk, HEAD_DIM), BF16)],
        compiler_params=_cparams("parallel", "parallel", "arbitrary"),
        name="attention",
    )(*args)


def _split3_dot(tri, x):
    hi = x.astype(BF16)
    r1 = x - hi.astype(F32)
    mid = r1.astype(BF16)
    lo = (r1 - mid.astype(F32)).astype(BF16)
    d = lambda a: jnp.dot(tri, a, preferred_element_type=F32)
    return d(hi) + d(mid) + d(lo)


def _shift_rows(x, direction):
    n = x.shape[0]
    row = lax.broadcasted_iota(jnp.int32, x.shape, 0)
    if direction > 0:
        return jnp.where(row == 0, 0.0, pltpu.roll(x, 1, axis=0))
    return jnp.where(row == n - 1, 0.0, pltpu.roll(x, n - 1, axis=0))


def _conv_silu(x, w, b):
    y = _shift_rows(x, 1) * w[0:1, :] + x * w[1:2, :] + _shift_rows(x, -1) * w[2:3, :] + b
    return _silu(y)


def _mlstm_kernel(*refs, t, zero_init):
    if zero_init:
        z_ref, zif_ref, bif_ref, cw_ref, cb_ref, nw_ref, o_ref, c_out, n_out, m_out = refs
    else:
        z_ref, zif_ref, bif_ref, cw_ref, cb_ref, nw_ref, c0_ref, n0_ref, m0_ref, o_ref, c_out, n_out, m_out = refs
    L = ML_CHUNK
    nc = t // L
    chunk = lambda a, c: a[c * L:(c + 1) * L]

    gates = zif_ref[...] + bif_ref[...]
    logsig = jnp.minimum(gates, 0.0) - jnp.log1p(jnp.exp(-jnp.abs(gates)))
    gates_t = gates.T
    r_i = lax.broadcasted_iota(jnp.int32, (L, L), 0)
    c_i = lax.broadcasted_iota(jnp.int32, (L, L), 1)
    lower, upper = r_i >= c_i, r_i <= c_i
    cum_tris = (lower.astype(BF16), upper.astype(BF16))
    dmasks = (upper, lower)
    cums = [[_split3_dot(cum_tris[d], chunk(logsig, c)) for c in range(nc)] for d in range(2)]
    cums_t = [[x.T for x in row] for row in cums]
    ext_rows = (lax.broadcasted_iota(jnp.int32, (ML_EXT - ML_DV, t), 0) == 0).astype(F32)
    lchunk = lambda a, c: a[:, c * L:(c + 1) * L]

    def local_part(qk_c, kb_c, vt_c, vtb_c, b_row, b_col, i_row, i_col, d):
        dm = jnp.where(dmasks[d], b_row - b_col + i_col, -jnp.inf)
        m_loc = jnp.max(dm, axis=0, keepdims=True)
        s = (qk_c * jnp.exp(dm - m_loc)).astype(BF16)
        bm = jnp.dot(vtb_c, s, preferred_element_type=F32)
        b_end = b_row[:, 0:1] if d else b_row[:, L - 1:L]
        g = b_end - b_row + i_row
        g_max = jnp.max(g, axis=1, keepdims=True)
        vw = (vt_c * jnp.exp(g - g_max)).astype(BF16)
        u = jnp.dot(vw, kb_c, preferred_element_type=F32)
        return m_loc, bm, b_end, g_max, u

    def scan_step(qt_c, b_row, loc, c_t, m):
        m_loc, bm, b_end, g_max, u = loc
        inter = b_row + m
        m_row = jnp.maximum(inter, m_loc)
        nd = jnp.exp(m_loc - m_row) * bm
        if c_t is not None:
            nd = nd + jnp.exp(inter - m_row) * jnp.dot(c_t.astype(BF16), qt_c, preferred_element_type=F32)
        h_t = nd[:ML_DV] / jnp.maximum(jnp.abs(nd[ML_DV:ML_DV + 1]), jnp.exp(-m_row))
        m_new = jnp.maximum(b_end + m, g_max)
        c_new = jnp.exp(g_max - m_new) * u
        if c_t is not None:
            c_new = c_new + jnp.exp(b_end + m - m_new) * c_t
        return h_t, c_new, m_new

    chains = [(hd, d) for hd in range(ML_HEADS) for d in range(2)]
    qts, b_rows, locs, state = {}, {}, {}, {}
    for hd in range(ML_HEADS):
        cols = slice(hd * ML_DK, (hd + 1) * ML_DK)
        kcols = slice(ML_W + hd * ML_DK, ML_W + (hd + 1) * ML_DK)
        vcols = slice(2 * ML_W + hd * ML_DV, 2 * ML_W + (hd + 1) * ML_DV)
        qt = _conv_silu(z_ref[:, cols], cw_ref[:, cols], cb_ref[:, cols]).T.astype(BF16)
        kb = (_conv_silu(z_ref[:, kcols], cw_ref[:, kcols], cb_ref[:, kcols]) * (ML_DK ** -0.5)).astype(BF16)
        vt = jnp.concatenate([z_ref[:, vcols].T, ext_rows], axis=0)
        vtb = vt.astype(BF16)
        qk = [jnp.dot(chunk(kb, c), lchunk(qt, c), preferred_element_type=F32) for c in range(nc)]
        qts[hd] = qt
        for d in range(2):
            i_lane = 2 * d * ML_HEADS + hd
            f_lane = (2 * d + 1) * ML_HEADS + hd
            b_rows[hd, d] = [cums_t[d][c][f_lane:f_lane + 1, :] for c in range(nc)]
            locs[hd, d] = [local_part(qk[c], chunk(kb, c), lchunk(vt, c), lchunk(vtb, c), b_rows[hd, d][c],
                                      cums[d][c][:, f_lane:f_lane + 1], lchunk(gates_t, c)[i_lane:i_lane + 1, :],
                                      chunk(gates, c)[:, i_lane:i_lane + 1], d) for c in range(nc)]
            if zero_init:
                state[hd, d] = (None, jnp.zeros((1, 1), F32))
            else:
                c_t = jnp.concatenate([c0_ref[d, hd].T, n0_ref[d, hd:hd + 1, :],
                                       jnp.zeros((ML_EXT - ML_DV - 1, ML_DK), F32)], axis=0)
                state[hd, d] = (c_t, m0_ref[d:d + 1, hd:hd + 1])

    h_parts = {ch: [None] * nc for ch in chains}
    for step in range(nc):
        for hd, d in chains:
            c = nc - 1 - step if d else step
            c_t, m = state[hd, d]
            h_parts[hd, d][c], c_t, m = scan_step(lchunk(qts[hd], c), b_rows[hd, d][c], locs[hd, d][c], c_t, m)
            state[hd, d] = (c_t, m)

    for hd, d in chains:
        c_t, m = state[hd, d]
        c_out[d, hd] = c_t[:ML_DV].T
        n_out[d, hd:hd + 1, :] = c_t[ML_DV:ML_DV + 1]
        m_out[d:d + 1, hd:hd + 1] = m
    for hd in range(ML_HEADS):
        cols = slice(hd * ML_DK, (hd + 1) * ML_DK)
        ocols = slice(3 * ML_W + hd * ML_DV, 3 * ML_W + (hd + 1) * ML_DV)
        gcols = slice(4 * ML_W + hd * ML_DV, 4 * ML_W + (hd + 1) * ML_DV)
        h_fwd, h_bwd = (jnp.concatenate(h_parts[hd, d], axis=1) if nc > 1 else h_parts[hd, d][0] for d in range(2))
        h_sum = h_fwd + h_bwd
        hm = _rms(h_sum.T, nw_ref[:, cols])
        o_ref[:, cols] = (hm * _sigmoid(z_ref[:, ocols]) * _silu(z_ref[:, gcols])).astype(BF16)


def _mlstm(z, b_if_l, conv_w_l, conv_b_l, ml_norm_l, *, b, t, state=None):
    zero_init = state is None
    zw = 5 * ML_W
    in_specs = [
        pl.BlockSpec((t, zw), lambda bi: (bi, OFF_MQ // zw)),
        pl.BlockSpec((t, LANES), lambda bi: (bi, OFF_IF // LANES)),
        pl.BlockSpec((1, LANES), lambda bi: (0, 0)),
        pl.BlockSpec((CONV_W, 2 * ML_W), lambda bi: (0, 0)),
        pl.BlockSpec((1, 2 * ML_W), lambda bi: (0, 0)),
        pl.BlockSpec((1, ML_W), lambda bi: (0, 0)),
    ]
    bif = jnp.pad(b_if_l, (0, LANES - N_GATES)).reshape(1, LANES)
    args = [z, z, bif, conv_w_l, conv_b_l.reshape(1, 2 * ML_W), ml_norm_l.reshape(1, ML_W)]
    if not zero_init:
        c0, n0, m0, layer = state
        in_specs += [
            pl.BlockSpec((None, None, 2, ML_HEADS, ML_DK, ML_DV), lambda bi: (bi, layer, 0, 0, 0, 0)),
            pl.BlockSpec((None, None, 2, ML_HEADS, ML_DK), lambda bi: (bi, layer, 0, 0, 0)),
            pl.BlockSpec((None, None, 2, ML_HEADS), lambda bi: (bi, layer, 0, 0)),
        ]
        args += [c0, n0, m0]
    return pl.pallas_call(
        functools.partial(_mlstm_kernel, t=t, zero_init=zero_init),
        grid=(b,),
        in_specs=in_specs,
        out_specs=[
            pl.BlockSpec((t, ML_W), lambda bi: (bi, 0)),
            pl.BlockSpec((None, 2, ML_HEADS, ML_DK, ML_DV), lambda bi: (bi, 0, 0, 0, 0)),
            pl.BlockSpec((None, 2, ML_HEADS, ML_DK), lambda bi: (bi, 0, 0, 0)),
            pl.BlockSpec((None, 2, ML_HEADS), lambda bi: (bi, 0, 0)),
        ],
        out_shape=[
            jax.ShapeDtypeStruct((b * t, ML_W), BF16),
            jax.ShapeDtypeStruct((b, 2, ML_HEADS, ML_DK, ML_DV), F32),
            jax.ShapeDtypeStruct((b, 2, ML_HEADS, ML_DK), F32),
            jax.ShapeDtypeStruct((b, 2, ML_HEADS), F32),
        ],
        compiler_params=_cparams("parallel"),
        name="mlstm",
    )(*args)


def _dft_tables(t):
    def cs(n):
        idx = np.arange(n, dtype=np.int64)
        ang = 2.0 * np.pi * ((idx[:, None] * idx[None, :]) % n).astype(np.float64) / n
        return np.cos(ang), np.sin(ang)
    cc, sc = cs(FO_GC)
    ct, st = cs(t)
    return (np.concatenate([cc, sc], axis=1).astype(np.float32),
            np.concatenate([ct, -st], axis=1).astype(np.float32))


FO_RT = 256


def _fourier_kernel(zx_ref, zga_ref, zgb_ref, wc_ref, wt_ref, wf_ref, o_ref, y_ref, *, t):
    x = zx_ref[:, FO_SHIFT:FO_SHIFT + FO_W]
    wc = wc_ref[...].astype(BF16)
    for g in range(FO_GROUPS):
        cols = slice(g * FO_GC, (g + 1) * FO_GC)
        y = jnp.dot(x[:, cols].astype(BF16), wc, preferred_element_type=F32)
        y_ref[0:t, cols] = y[:, :FO_GC].astype(BF16)
        y_ref[t:2 * t, cols] = y[:, FO_GC:].astype(BF16)
    scale = (t * FO_GC) ** -0.5
    for r in range(t // FO_RT):
        rows = slice(r * FO_RT, (r + 1) * FO_RT)
        f = jnp.dot(wt_ref[rows, :].astype(BF16), y_ref[...], preferred_element_type=F32) * scale
        fg = jnp.concatenate([zga_ref[rows, FO_SHIFT:], zgb_ref[rows, :FO_SHIFT]], axis=1)
        for g in range(FO_GROUPS):
            cols = slice(g * FO_GC, (g + 1) * FO_GC)
            og = jnp.dot(f[:, cols].astype(BF16), wf_ref[g].astype(BF16), preferred_element_type=F32)
            o_ref[rows, cols] = (og * _silu(fg[:, cols])).astype(BF16)


def _fourier(z, z_tail, w_fno_l, *, b, t):
    assert OFF_FG - FO_SHIFT + FO_W == Z_W and Z_TAIL == FO_SHIFT
    wc, wt = _dft_tables(t)
    xw = FO_W + LANES
    return pl.pallas_call(
        functools.partial(_fourier_kernel, t=t),
        grid=(b,),
        in_specs=[
            pl.BlockSpec((t, xw), lambda bi: (bi, (OFF_FX - FO_SHIFT) // xw)),
            pl.BlockSpec((t, FO_W), lambda bi: (bi, (OFF_FG - FO_SHIFT) // FO_W)),
            pl.BlockSpec((t, LANES), lambda bi: (bi, 0)),
            pl.BlockSpec((FO_GC, 2 * FO_GC), lambda bi: (0, 0)),
            pl.BlockSpec((t, 2 * t), lambda bi: (0, 0)),
            pl.BlockSpec((FO_GROUPS, FO_GC, FO_GC), lambda bi: (0, 0, 0)),
        ],
        out_specs=pl.BlockSpec((t, FO_W), lambda bi: (bi, 0)),
        out_shape=jax.ShapeDtypeStruct((b * t, FO_W), BF16),
        scratch_shapes=[pltpu.VMEM((2 * t, FO_W), BF16)],
        compiler_params=_cparams("parallel"),
        name="fourier",
    )(z, z, z_tail, jnp.asarray(wc), jnp.asarray(wt), w_fno_l)


OUT_TN = 1024
OUT_TM = 512


def _outproj_kernel(oa_ref, om_ref, of_ref, wa_ref, wm_ref, wf_ref, x_ref, gate_ref, y_ref, wb_ref, *, per_row_mod):
    i = pl.program_id(1)

    @pl.when(i == 0)
    def _():
        wb_ref[0:ATT_W, :] = wa_ref[...].astype(BF16)
        wb_ref[ATT_W:ATT_W + ML_W, :] = wm_ref[...].astype(BF16)
        wb_ref[ATT_W + ML_W:D_MIX, :] = wf_ref[...].astype(BF16)

    row = 1 + (i * OUT_TM) // ROW_TILE if per_row_mod else 0
    gate = gate_ref[pl.ds(row, 1), :]
    y = (jnp.dot(oa_ref[...], wb_ref[0:ATT_W, :], preferred_element_type=F32)
         + jnp.dot(om_ref[...], wb_ref[ATT_W:ATT_W + ML_W, :], preferred_element_type=F32)
         + jnp.dot(of_ref[...], wb_ref[ATT_W + ML_W:D_MIX, :], preferred_element_type=F32))
    y_ref[...] = x_ref[...] + gate * y


def _outproj(o_att, o_ml, o_fo, w_out, layer, x2d, mod_l, *, per_row_mod):
    m = x2d.shape[0]
    return pl.pallas_call(
        functools.partial(_outproj_kernel, per_row_mod=per_row_mod),
        grid=(D_MODEL // OUT_TN, m // OUT_TM),
        in_specs=[
            pl.BlockSpec((OUT_TM, ATT_W), lambda n, i: (i, 0)),
            pl.BlockSpec((OUT_TM, ML_W), lambda n, i: (i, 0)),
            pl.BlockSpec((OUT_TM, FO_W), lambda n, i: (i, 0)),
            pl.BlockSpec((None, ATT_W, OUT_TN), lambda n, i: (layer, 0, n)),
            pl.BlockSpec((None, ML_W, OUT_TN), lambda n, i: (layer, ATT_W // ML_W, n)),
            pl.BlockSpec((None, FO_W, OUT_TN), lambda n, i: (layer, (ATT_W + ML_W) // FO_W, n)),
            pl.BlockSpec((OUT_TM, OUT_TN), lambda n, i: (i, n)),
            pl.BlockSpec((8, OUT_TN), lambda n, i: (0, 2 * D_MODEL // OUT_TN + n)),
        ],
        out_specs=pl.BlockSpec((OUT_TM, OUT_TN), lambda n, i: (i, n)),
        out_shape=jax.ShapeDtypeStruct((m, D_MODEL), F32),
        scratch_shapes=[pltpu.VMEM((D_MIX, OUT_TN), BF16)],
        compiler_params=_cparams("parallel", "arbitrary"),
        name="outproj",
    )(o_att, o_ml, o_fo, w_out, w_out, w_out, x2d, mod_l)


def _rope_tables(t):
    half = HEAD_DIM // 2
    inv_freq = ROPE_BASE ** (-jnp.arange(0, half, 2, dtype=F32) / half)
    n_rows = t // GRID_W
    rows = jnp.repeat(jnp.arange(n_rows, dtype=F32), GRID_W)
    cols = jnp.tile(jnp.arange(GRID_W, dtype=F32), n_rows)
    ar = rows[:, None] * inv_freq
    ac = cols[:, None] * inv_freq
    cos = jnp.concatenate([jnp.cos(ar), jnp.cos(ar), jnp.cos(ac), jnp.cos(ac)], axis=1)
    sin = jnp.concatenate([-jnp.sin(ar), jnp.sin(ar), -jnp.sin(ac), jnp.sin(ac)], axis=1)
    return cos, sin


def _layer(x2d, mod_l, lw, layer, *, b, t, latent, rope=None, cache=None, state=None):
    norm_w, w_in_t, b_if, conv_w, conv_b, q_norm, k_norm, ml_norm, w_fno, w_out = lw
    z, z_tail = _inproj(x2d, mod_l, norm_w, w_in_t, layer, per_row_mod=latent)
    o_att, k_new = _attention(z, q_norm, k_norm, b=b, t=t, rope=rope, cache=cache)
    o_ml, c_f, n_f, m_f = _mlstm(z, b_if, conv_w, conv_b, ml_norm, b=b, t=t, state=state)
    o_fo = _fourier(z, z_tail, w_fno, b=b, t=t)
    y = _outproj(o_att, o_ml, o_fo, w_out, layer, x2d, mod_l, per_row_mod=latent)
    v_new = z[:, OFF_AV:OFF_AV + ATT_KV_W]
    return y, (k_new, v_new, c_f, n_f, m_f)


def kernel(x_prompt, x_sample, cache_k, cache_v, state_C, state_n, state_m, c, c_ctx, norm_w, w_mod, b_mod, w_in,
           b_if, conv_w, conv_b, q_norm, k_norm, ml_norm, w_fno, w_out):
    bp, tp, _ = x_prompt.shape
    bs, ts, _ = x_sample.shape
    assert tp == ML_CHUNK and ts == ROW_TILE and (bp * tp) % ROW_TILE == 0 and bs + 1 <= 8
    past = cache_k.shape[2]

    cond8 = jnp.concatenate([c_ctx[None, :], c, jnp.zeros((8 - 1 - bs, D_MODEL), F32)], axis=0)
    mod = _modulation(cond8, w_mod, b_mod)

    rope = _rope_tables(ts)
    ck = cache_k.reshape(bs, DEPTH, past, ATT_KV_W)
    cv = cache_v.reshape(bs, DEPTH, past, ATT_KV_W)
    w_in_t = jnp.swapaxes(w_in, 1, 2)

    xp = x_prompt.reshape(bp * tp, D_MODEL)
    xs = x_sample.reshape(bs * ts, D_MODEL)
    new_k, new_v, new_c, new_n, new_m = [], [], [], [], []
    for l in range(DEPTH):
        lw = (norm_w[l], w_in_t, b_if[l], conv_w[l], conv_b[l], q_norm[l], k_norm[l], ml_norm[l], w_fno[l], w_out)
        xp, (k_l, v_l, c_l, n_l, m_l) = _layer(xp, mod[l], lw, l, b=bp, t=tp, latent=False)
        new_k.append(k_l.reshape(bp, tp, ATT_KV_HEADS, HEAD_DIM))
        new_v.append(v_l.reshape(bp, tp, ATT_KV_HEADS, HEAD_DIM))
        new_c.append(c_l)
        new_n.append(n_l)
        new_m.append(m_l)
        xs, _ = _layer(xs, mod[l], lw, l, b=bs, t=ts, latent=True, rope=rope,
                       cache=(ck, cv, l), state=(state_C, state_n, state_m, l))
    return (xp.reshape(bp, tp, D_MODEL), xs.reshape(bs, ts, D_MODEL), jnp.stack(new_k, axis=1),
            jnp.stack(new_v, axis=1), jnp.stack(new_c, axis=1), jnp.stack(new_n, axis=1), jnp.stack(new_m, axis=1))
```

```python
import functools

import numpy as np
import jax
import jax.numpy as jnp
from jax import lax
from jax.experimental import pallas as pl
from jax.experimental.pallas import tpu as pltpu

D_MODEL = 2048
DEPTH = 2
GRID_W = 64
HEAD_DIM = 128
ATT_HEADS = 8
ATT_KV_HEADS = 2
ATT_GROUPS = ATT_HEADS // ATT_KV_HEADS
ATT_W = ATT_HEADS * HEAD_DIM
ATT_KV_W = ATT_KV_HEADS * HEAD_DIM
ML_HEADS = 4
ML_DK = 128
ML_DV = 128
ML_W = ML_HEADS * ML_DV
FO_GROUPS = 4
FO_GC = 128
FO_W = FO_GROUPS * FO_GC
D_MIX = ATT_W + ML_W + FO_W
N_GATES = 4 * ML_HEADS
D_PROJ = 2 * ATT_W + 2 * ATT_KV_W + 5 * ML_W + N_GATES + 2 * FO_W
CONV_W = 3
ROPE_BASE = 10000.0
EPS = 1e-6

LANES = 128
OFF_AQ = 0
OFF_AK = OFF_AQ + ATT_W
OFF_AV = OFF_AK + ATT_KV_W
OFF_AG = OFF_AV + ATT_KV_W
OFF_MQ = OFF_AG + ATT_W
OFF_MK = OFF_MQ + ML_W
OFF_MV = OFF_MK + ML_W
OFF_MO = OFF_MV + ML_W
OFF_MG = OFF_MO + ML_W
OFF_IF = OFF_MG + ML_W
OFF_FX = OFF_IF + N_GATES
OFF_FG = OFF_FX + FO_W
FO_SHIFT = OFF_FX % LANES

PROJ_TN = 1024
Z_W = (D_PROJ // PROJ_TN) * PROJ_TN
Z_TAIL = D_PROJ - Z_W
ROW_TILE = 1024
ML_CHUNK = 256
ML_EXT = ML_DV + 16
ATT_TQ = 256
VMEM_LIMIT = 56 * 1024 * 1024

BF16 = jnp.bfloat16
F32 = jnp.float32
NT_DIMS = (((1,), (1,)), ((), ()))


def _cparams(*sem):
    return pltpu.CompilerParams(dimension_semantics=sem, vmem_limit_bytes=VMEM_LIMIT)


def _silu(x):
    return x * (1.0 / (1.0 + jnp.exp(-x)))


def _sigmoid(x):
    return 1.0 / (1.0 + jnp.exp(-x))


def _rms(x, w):
    ms = jnp.mean(x * x, axis=-1, keepdims=True)
    return x * lax.rsqrt(ms + EPS) * w


MOD_TN = 768


def _mod_kernel(cond_ref, w_ref, b_ref, o_ref):
    a = _silu(cond_ref[...]).astype(BF16)
    o_ref[...] = jnp.dot(a, w_ref[...].astype(BF16), preferred_element_type=F32) + b_ref[...]


def _modulation(cond8, w_mod, b_mod):
    n = 3 * D_MODEL
    return pl.pallas_call(
        _mod_kernel,
        grid=(DEPTH, n // MOD_TN),
        in_specs=[
            pl.BlockSpec((8, D_MODEL), lambda l, j: (0, 0)),
            pl.BlockSpec((None, D_MODEL, MOD_TN), lambda l, j: (l, 0, j)),
            pl.BlockSpec((None, 1, MOD_TN), lambda l, j: (l, 0, j)),
        ],
        out_specs=pl.BlockSpec((None, 8, MOD_TN), lambda l, j: (l, 0, j)),
        out_shape=jax.ShapeDtypeStruct((DEPTH, 8, n), F32),
        compiler_params=_cparams("parallel", "parallel"),
        name="modulation",
    )(cond8, w_mod, b_mod.reshape(DEPTH, 1, n))


NORM_ROWS = 16


def _inproj_kernel(x_ref, mod_ref, nw_ref, w_ref, wt_ref, z_ref, zt_ref, h_ref, *, per_row_mod):
    i = pl.program_id(0)
    j = pl.program_id(1)

    @pl.when(j == 0)
    def _():
        row = 1 + i if per_row_mod else 0
        shift = mod_ref[pl.ds(row, 1), 0:D_MODEL]
        gain = nw_ref[...] * (1.0 + mod_ref[pl.ds(row, 1), D_MODEL:2 * D_MODEL])

        def body(r, carry):
            sl = pl.ds(pl.multiple_of(r * NORM_ROWS, NORM_ROWS), NORM_ROWS)
            x = x_ref[sl, :]
            inv = lax.rsqrt(jnp.mean(x * x, axis=-1, keepdims=True) + EPS)
            h_ref[sl, :] = (x * inv * gain + shift).astype(BF16)
            return carry

        lax.fori_loop(0, ROW_TILE // NORM_ROWS, body, 0, unroll=8)

    z_ref[...] = lax.dot_general(h_ref[...], w_ref[...].astype(BF16), NT_DIMS, preferred_element_type=F32)

    @pl.when(j == Z_W // PROJ_TN - 1)
    def _():
        zt = lax.dot_general(h_ref[...], wt_ref[...].astype(BF16), NT_DIMS, preferred_element_type=F32)
        col = lax.broadcasted_iota(jnp.int32, zt.shape, 1)
        zt_ref[...] = jnp.where(col < Z_TAIL, zt, 0.0)


def _inproj(x2d, mod_l, norm_w_l, w_in_t, layer, *, per_row_mod):
    m = x2d.shape[0]
    return pl.pallas_call(
        functools.partial(_inproj_kernel, per_row_mod=per_row_mod),
        grid=(m // ROW_TILE, Z_W // PROJ_TN),
        in_specs=[
            pl.BlockSpec((ROW_TILE, D_MODEL), lambda i, j: (i, 0)),
            pl.BlockSpec((8, 3 * D_MODEL), lambda i, j: (0, 0)),
            pl.BlockSpec((1, D_MODEL), lambda i, j: (0, 0)),
            pl.BlockSpec((None, PROJ_TN, D_MODEL), lambda i, j: (layer, j, 0)),
            pl.BlockSpec((None, LANES, D_MODEL), lambda i, j: (layer, Z_W // LANES, 0)),
        ],
        out_specs=[
            pl.BlockSpec((ROW_TILE, PROJ_TN), lambda i, j: (i, j)),
            pl.BlockSpec((ROW_TILE, LANES), lambda i, j: (i, 0)),
        ],
        out_shape=[jax.ShapeDtypeStruct((m, Z_W), F32), jax.ShapeDtypeStruct((m, LANES), F32)],
        scratch_shapes=[pltpu.VMEM((ROW_TILE, D_MODEL), BF16)],
        compiler_params=_cparams("parallel", "arbitrary"),
        name="inproj",
    )(x2d, mod_l, norm_w_l.reshape(1, D_MODEL), w_in_t, w_in_t)


def _rope(x, cos, sin):
    lane = lax.broadcasted_iota(jnp.int32, x.shape, 1)
    quarter = HEAD_DIM // 4
    partner = jnp.where((lane % (2 * quarter)) < quarter,
                        pltpu.roll(x, HEAD_DIM - quarter, axis=1), pltpu.roll(x, quarter, axis=1))
    return x * cos + partner * sin


LOG2E = 1.4426950408889634
LN2 = 0.6931471805599453


def _attn_kernel(*refs, t, hk, latent):
    zq_ref, zk_ref, zv_ref = refs[:3]
    zg_refs = refs[3:3 + hk]
    rest = refs[3 + hk:]
    if latent:
        qn_ref, kn_ref, cos_ref, sin_ref, ck_ref, cv_ref, o_ref, kall_ref, vall_ref = rest
        past = ck_ref.shape[0]
    else:
        qn_ref, kn_ref, o_ref, knew_ref, vnew_ref, kall_ref, vall_ref = rest
        past = 0
    qi = pl.program_id(2)

    @pl.when(qi == 0)
    def _():
        for j in range(hk):
            kc = slice(j * HEAD_DIM, (j + 1) * HEAD_DIM)
            k = _rms(zk_ref[:, kc], kn_ref[...])
            v = zv_ref[:, kc]
            if latent:
                k = _rope(k, cos_ref[...], sin_ref[...])
                kall_ref[j, 0:past, :] = ck_ref[:, kc].astype(BF16)
                vall_ref[j, 0:past, :] = cv_ref[:, kc].astype(BF16)
            else:
                knew_ref[:, j, :] = k
                vnew_ref[:, j, :] = v
            kall_ref[j, past:past + t, :] = k.astype(BF16)
            vall_ref[j, past:past + t, :] = v.astype(BF16)

    q_scale = (HEAD_DIM ** -0.5) * LOG2E
    if latent:
        rows = pl.ds(pl.multiple_of(qi * ATT_TQ, ATT_TQ), ATT_TQ)
        cos_q = cos_ref[rows, :]
        sin_q = sin_ref[rows, :]
    def scores(hd):
        cols = slice(hd * HEAD_DIM, (hd + 1) * HEAD_DIM)
        q = _rms(zq_ref[:, cols], qn_ref[...])
        if latent:
            q = _rope(q, cos_q, sin_q)
        qb = (q * q_scale).astype(BF16)
        return lax.dot_general(qb, kall_ref[hd // ATT_GROUPS], NT_DIMS, preferred_element_type=F32)

    def finish(hd, s):
        j, g = divmod(hd, ATT_GROUPS)
        cols = slice(hd * HEAD_DIM, (hd + 1) * HEAD_DIM)
        gcols = slice(g * HEAD_DIM, (g + 1) * HEAD_DIM)
        p = jnp.exp2(s - jnp.max(s, axis=-1, keepdims=True))
        den = jnp.sum(p, axis=-1, keepdims=True)
        o = jnp.dot(p.astype(BF16), vall_ref[j], preferred_element_type=F32) * (1.0 / den)
        o_ref[:, cols] = (o * _silu(zg_refs[j][:, gcols])).astype(BF16)

    n_heads = hk * ATT_GROUPS
    s_next = scores(0)
    for hd in range(n_heads):
        s_cur = s_next
        if hd + 1 < n_heads:
            s_next = scores(hd + 1)
        finish(hd, s_cur)


def _attention(z, q_norm_l, k_norm_l, *, b, t, rope=None, cache=None):
    latent = rope is not None
    hk = 1 if latent else ATT_KV_HEADS
    nq = t // ATT_TQ
    gw = ATT_GROUPS * HEAD_DIM
    in_specs = [
        pl.BlockSpec((ATT_TQ, hk * gw), lambda bi, kg, qi: (bi * nq + qi, OFF_AQ // (hk * gw) + kg)),
        pl.BlockSpec((t, hk * HEAD_DIM), lambda bi, kg, qi: (bi, OFF_AK // (hk * HEAD_DIM) + kg)),
        pl.BlockSpec((t, hk * HEAD_DIM), lambda bi, kg, qi: (bi, OFF_AV // (hk * HEAD_DIM) + kg)),
    ]
    in_specs += [pl.BlockSpec((ATT_TQ, gw), lambda bi, kg, qi, j=j: (bi * nq + qi, OFF_AG // gw + kg * hk + j))
                 for j in range(hk)]
    in_specs += [pl.BlockSpec((1, HEAD_DIM), lambda bi, kg, qi: (0, 0))] * 2
    args = [z] * (3 + hk) + [q_norm_l.reshape(1, HEAD_DIM), k_norm_l.reshape(1, HEAD_DIM)]
    tk = t
    if latent:
        cos, sin = rope
        ck, cv, layer = cache
        past = ck.shape[2]
        tk = past + t
        in_specs += [
            pl.BlockSpec((t, HEAD_DIM), lambda bi, kg, qi: (0, 0)),
            pl.BlockSpec((t, HEAD_DIM), lambda bi, kg, qi: (0, 0)),
            pl.BlockSpec((None, None, past, hk * HEAD_DIM), lambda bi, kg, qi: (bi, layer, 0, kg)),
            pl.BlockSpec((None, None, past, hk * HEAD_DIM), lambda bi, kg, qi: (bi, layer, 0, kg)),
        ]
        args += [cos, sin, ck, cv]
    out_specs = [pl.BlockSpec((ATT_TQ, hk * gw), lambda bi, kg, qi: (bi * nq + qi, kg))]
    out_shape = [jax.ShapeDtypeStruct((b * t, ATT_W), BF16)]
    if not latent:
        kv_spec = pl.BlockSpec((None, t, ATT_KV_HEADS, HEAD_DIM), lambda bi, kg, qi: (bi, 0, 0, 0))
        out_specs += [kv_spec, kv_spec]
        out_shape += [jax.ShapeDtypeStruct((b, t, ATT_KV_HEADS, HEAD_DIM), F32)] * 2
    return pl.pallas_call(
        functools.partial(_attn_kernel, t=t, hk=hk, latent=latent),
        grid=(b, ATT_KV_HEADS // hk, nq),
        in_specs=in_specs,
        out_specs=out_specs,
        out_shape=out_shape,
        scratch_shapes=[pltpu.VMEM((hk, tk, HEAD_DIM), BF16), pltpu.VMEM((hk, tk, HEAD_DIM), BF16)],
        compiler_params=_cparams("parallel", "parallel", "arbitrary"),
        name="attention",
    )(*args)


def _split3_dot(tri, x):
    hi = x.astype(BF16)
    r1 = x - hi.astype(F32)
    mid = r1.astype(BF16)
    lo = (r1 - mid.astype(F32)).astype(BF16)
    d = lambda a: jnp.dot(tri, a, preferred_element_type=F32)
    return d(hi) + d(mid) + d(lo)


def _shift_rows(x, direction):
    n = x.shape[0]
    row = lax.broadcasted_iota(jnp.int32, x.shape, 0)
    if direction > 0:
        return jnp.where(row == 0, 0.0, pltpu.roll(x, 1, axis=0))
    return jnp.where(row == n - 1, 0.0, pltpu.roll(x, n - 1, axis=0))


def _conv_silu(x, w, b):
    y = _shift_rows(x, 1) * w[0:1, :] + x * w[1:2, :] + _shift_rows(x, -1) * w[2:3, :] + b
    return _silu(y)


def _mlstm_kernel(*refs, t, zero_init):
    if zero_init:
        z_ref, zif_ref, bif_ref, cw_ref, cb_ref, nw_ref, o_ref, c_out, n_out, m_out = refs
    else:
        z_ref, zif_ref, bif_ref, cw_ref, cb_ref, nw_ref, c0_ref, n0_ref, m0_ref, o_ref, c_out, n_out, m_out = refs
    L = ML_CHUNK
    nc = t // L
    chunk = lambda a, c: a[c * L:(c + 1) * L]

    gates = zif_ref[...] + bif_ref[...]
    logsig = (jnp.minimum(gates, 0.0) - jnp.log1p(jnp.exp(-jnp.abs(gates)))) * LOG2E
    gates = gates * LOG2E
    gates_t = gates.T
    r_i = lax.broadcasted_iota(jnp.int32, (L, L), 0)
    c_i = lax.broadcasted_iota(jnp.int32, (L, L), 1)
    lower, upper = r_i >= c_i, r_i <= c_i
    cum_tris = (lower.astype(BF16), upper.astype(BF16))
    dmasks = (upper, lower)
    cums = [[_split3_dot(cum_tris[d], chunk(logsig, c)) for c in range(nc)] for d in range(2)]
    cums_t = [[x.T for x in row] for row in cums]
    ext_rows = (lax.broadcasted_iota(jnp.int32, (ML_EXT - ML_DV, t), 0) == 0).astype(F32)
    lchunk = lambda a, c: a[:, c * L:(c + 1) * L]

    def running_max(a, d):
        rows = a.shape[0]
        if rows < 8:
            a = jnp.concatenate([a] * (8 // rows), axis=0)
        lane = lax.broadcasted_iota(jnp.int32, a.shape, 1)
        k = 1
        while k < L:
            if d:
                shifted = jnp.where(lane < L - k, pltpu.roll(a, L - k, axis=1), -jnp.inf)
            else:
                shifted = jnp.where(lane >= k, pltpu.roll(a, k, axis=1), -jnp.inf)
            a = jnp.maximum(a, shifted)
            k *= 2
        return a[:rows]

    hsl = lambda d, which: slice((2 * d + which) * ML_HEADS, (2 * d + which + 1) * ML_HEADS)
    a_maxes = [running_max(jnp.concatenate([lchunk(gates_t, c)[hsl(d, 0)] - cums_t[d][c][hsl(d, 1)]
                                            for c in range(nc)], axis=0), d) for d in range(2)]

    def local_part(qk_c, kb_c, vt_c, vtb_c, b_row, b_col, i_row, i_col, a_max, d):
        m_loc = b_row + a_max
        s = (qk_c * jnp.exp2(jnp.where(dmasks[d], (i_col - b_col) - a_max, -jnp.inf))).astype(BF16)
        bm = jnp.dot(vtb_c, s, preferred_element_type=F32)
        b_end = b_row[:, 0:1] if d else b_row[:, L - 1:L]
        g = b_end - b_row + i_row
        g_max = jnp.max(g, axis=1, keepdims=True)
        vw = (vt_c * jnp.exp2(g - g_max)).astype(BF16)
        u = jnp.dot(vw, kb_c, preferred_element_type=F32)
        return m_loc, bm, b_end, g_max, u

    def scan_step(qt_c, b_row, loc, c_t, m):
        m_loc, bm, b_end, g_max, u = loc
        inter = b_row + m
        m_row = jnp.maximum(inter, m_loc)
        nd = jnp.exp2(m_loc - m_row) * bm
        if c_t is not None:
            nd = nd + jnp.exp2(inter - m_row) * jnp.dot(c_t.astype(BF16), qt_c, preferred_element_type=F32)
        h_t = nd[:ML_DV] / jnp.maximum(jnp.abs(nd[ML_DV:ML_DV + 1]), jnp.exp2(-m_row))
        m_new = jnp.maximum(b_end + m, g_max)
        c_new = jnp.exp2(g_max - m_new) * u
        if c_t is not None:
            c_new = c_new + jnp.exp2(b_end + m - m_new) * c_t
        return h_t, c_new, m_new

    chains = [(hd, d) for hd in range(ML_HEADS) for d in range(2)]
    qts, b_rows, locs, state = {}, {}, {}, {}
    for hd in range(ML_HEADS):
        cols = slice(hd * ML_DK, (hd + 1) * ML_DK)
        kcols = slice(ML_W + hd * ML_DK, ML_W + (hd + 1) * ML_DK)
        vcols = slice(2 * ML_W + hd * ML_DV, 2 * ML_W + (hd + 1) * ML_DV)
        qt = _conv_silu(z_ref[:, cols], cw_ref[:, cols], cb_ref[:, cols]).T.astype(BF16)
        kb = (_conv_silu(z_ref[:, kcols], cw_ref[:, kcols], cb_ref[:, kcols]) * (ML_DK ** -0.5)).astype(BF16)
        vt = jnp.concatenate([z_ref[:, vcols].T, ext_rows], axis=0)
        vtb = vt.astype(BF16)
        qk = [jnp.dot(chunk(kb, c), lchunk(qt, c), preferred_element_type=F32) for c in range(nc)]
        qts[hd] = qt
        for d in range(2):
            i_lane = 2 * d * ML_HEADS + hd
            f_lane = (2 * d + 1) * ML_HEADS + hd
            b_rows[hd, d] = [cums_t[d][c][f_lane:f_lane + 1, :] for c in range(nc)]
            locs[hd, d] = [local_part(qk[c], chunk(kb, c), lchunk(vt, c), lchunk(vtb, c), b_rows[hd, d][c],
                                      cums[d][c][:, f_lane:f_lane + 1], lchunk(gates_t, c)[i_lane:i_lane + 1, :],
                                      chunk(gates, c)[:, i_lane:i_lane + 1],
                                      a_maxes[d][c * ML_HEADS + hd:c * ML_HEADS + hd + 1], d) for c in range(nc)]
            if zero_init:
                state[hd, d] = (None, jnp.zeros((1, 1), F32))
            else:
                c_t = jnp.concatenate([c0_ref[d, hd].T, n0_ref[d, hd:hd + 1, :],
                                       jnp.zeros((ML_EXT - ML_DV - 1, ML_DK), F32)], axis=0)
                state[hd, d] = (c_t, m0_ref[d:d + 1, hd:hd + 1] * LOG2E)

    h_parts = {ch: [None] * nc for ch in chains}
    for step in range(nc):
        for hd, d in chains:
            c = nc - 1 - step if d else step
            c_t, m = state[hd, d]
            h_parts[hd, d][c], c_t, m = scan_step(lchunk(qts[hd], c), b_rows[hd, d][c], locs[hd, d][c], c_t, m)
            state[hd, d] = (c_t, m)

    for hd, d in chains:
        c_t, m = state[hd, d]
        c_out[d, hd] = c_t[:ML_DV].T
        n_out[d, hd:hd + 1, :] = c_t[ML_DV:ML_DV + 1]
        m_out[d:d + 1, hd:hd + 1] = m * LN2
    for hd in range(ML_HEADS):
        cols = slice(hd * ML_DK, (hd + 1) * ML_DK)
        ocols = slice(3 * ML_W + hd * ML_DV, 3 * ML_W + (hd + 1) * ML_DV)
        gcols = slice(4 * ML_W + hd * ML_DV, 4 * ML_W + (hd + 1) * ML_DV)
        h_fwd, h_bwd = (jnp.concatenate(h_parts[hd, d], axis=1) if nc > 1 else h_parts[hd, d][0] for d in range(2))
        h_sum = h_fwd + h_bwd
        hm = _rms(h_sum.T, nw_ref[:, cols])
        o_ref[:, cols] = (hm * _sigmoid(z_ref[:, ocols]) * _silu(z_ref[:, gcols])).astype(BF16)


def _mlstm(z, b_if_l, conv_w_l, conv_b_l, ml_norm_l, *, b, t, state=None):
    zero_init = state is None
    zw = 5 * ML_W
    in_specs = [
        pl.BlockSpec((t, zw), lambda bi: (bi, OFF_MQ // zw)),
        pl.BlockSpec((t, LANES), lambda bi: (bi, OFF_IF // LANES)),
        pl.BlockSpec((1, LANES), lambda bi: (0, 0)),
        pl.BlockSpec((CONV_W, 2 * ML_W), lambda bi: (0, 0)),
        pl.BlockSpec((1, 2 * ML_W), lambda bi: (0, 0)),
        pl.BlockSpec((1, ML_W), lambda bi: (0, 0)),
    ]
    bif = jnp.pad(b_if_l, (0, LANES - N_GATES)).reshape(1, LANES)
    args = [z, z, bif, conv_w_l, conv_b_l.reshape(1, 2 * ML_W), ml_norm_l.reshape(1, ML_W)]
    if not zero_init:
        c0, n0, m0, layer = state
        in_specs += [
            pl.BlockSpec((None, None, 2, ML_HEADS, ML_DK, ML_DV), lambda bi: (bi, layer, 0, 0, 0, 0)),
            pl.BlockSpec((None, None, 2, ML_HEADS, ML_DK), lambda bi: (bi, layer, 0, 0, 0)),
            pl.BlockSpec((None, None, 2, ML_HEADS), lambda bi: (bi, layer, 0, 0)),
        ]
        args += [c0, n0, m0]
    return pl.pallas_call(
        functools.partial(_mlstm_kernel, t=t, zero_init=zero_init),
        grid=(b,),
        in_specs=in_specs,
        out_specs=[
            pl.BlockSpec((t, ML_W), lambda bi: (bi, 0)),
            pl.BlockSpec((None, 2, ML_HEADS, ML_DK, ML_DV), lambda bi: (bi, 0, 0, 0, 0)),
            pl.BlockSpec((None, 2, ML_HEADS, ML_DK), lambda bi: (bi, 0, 0, 0)),
            pl.BlockSpec((None, 2, ML_HEADS), lambda bi: (bi, 0, 0)),
        ],
        out_shape=[
            jax.ShapeDtypeStruct((b * t, ML_W), BF16),
            jax.ShapeDtypeStruct((b, 2, ML_HEADS, ML_DK, ML_DV), F32),
            jax.ShapeDtypeStruct((b, 2, ML_HEADS, ML_DK), F32),
            jax.ShapeDtypeStruct((b, 2, ML_HEADS), F32),
        ],
        compiler_params=_cparams("parallel"),
        name="mlstm",
    )(*args)


def _dft_tables(t):
    def cs(n):
        idx = np.arange(n, dtype=np.int64)
        ang = 2.0 * np.pi * ((idx[:, None] * idx[None, :]) % n).astype(np.float64) / n
        return np.cos(ang), np.sin(ang)
    cc, sc = cs(FO_GC)
    ct, st = cs(t)
    return (np.concatenate([cc, sc], axis=1).astype(np.float32),
            np.concatenate([ct, -st], axis=1).astype(np.float32))


FO_RT = 256


def _fourier_kernel(zx_ref, zga_ref, zgb_ref, wc_ref, wt_ref, wf_ref, o_ref, y_ref, *, t, bb):
    wc = wc_ref[...].astype(BF16)
    for bi in range(bb):
        x = zx_ref[bi * t:(bi + 1) * t, FO_SHIFT:FO_SHIFT + FO_W]
        for g in range(FO_GROUPS):
            cols = slice(g * FO_GC, (g + 1) * FO_GC)
            ycols = slice(bi * FO_W + g * FO_GC, bi * FO_W + (g + 1) * FO_GC)
            y = jnp.dot(x[:, cols].astype(BF16), wc, preferred_element_type=F32)
            y_ref[0:t, ycols] = y[:, :FO_GC].astype(BF16)
            y_ref[t:2 * t, ycols] = y[:, FO_GC:].astype(BF16)
    scale = (t * FO_GC) ** -0.5
    for r in range(t // FO_RT):
        rows = slice(r * FO_RT, (r + 1) * FO_RT)
        f = jnp.dot(wt_ref[rows, :].astype(BF16), y_ref[...], preferred_element_type=F32) * scale
        for bi in range(bb):
            orows = slice(bi * t + r * FO_RT, bi * t + (r + 1) * FO_RT)
            fg = jnp.concatenate([zga_ref[orows, FO_SHIFT:], zgb_ref[orows, :FO_SHIFT]], axis=1)
            for g in range(FO_GROUPS):
                cols = slice(g * FO_GC, (g + 1) * FO_GC)
                fcols = slice(bi * FO_W + g * FO_GC, bi * FO_W + (g + 1) * FO_GC)
                og = jnp.dot(f[:, fcols].astype(BF16), wf_ref[g].astype(BF16), preferred_element_type=F32)
                o_ref[orows, cols] = (og * _silu(fg[:, cols])).astype(BF16)


def _fourier(z, z_tail, w_fno_l, *, b, t):
    assert OFF_FG - FO_SHIFT + FO_W == Z_W and Z_TAIL == FO_SHIFT
    wc, wt = _dft_tables(t)
    xw = FO_W + LANES
    bb = max(1, min(b, ROW_TILE // t))
    return pl.pallas_call(
        functools.partial(_fourier_kernel, t=t, bb=bb),
        grid=(b // bb,),
        in_specs=[
            pl.BlockSpec((bb * t, xw), lambda bi: (bi, (OFF_FX - FO_SHIFT) // xw)),
            pl.BlockSpec((bb * t, FO_W), lambda bi: (bi, (OFF_FG - FO_SHIFT) // FO_W)),
            pl.BlockSpec((bb * t, LANES), lambda bi: (bi, 0)),
            pl.BlockSpec((FO_GC, 2 * FO_GC), lambda bi: (0, 0)),
            pl.BlockSpec((t, 2 * t), lambda bi: (0, 0)),
            pl.BlockSpec((FO_GROUPS, FO_GC, FO_GC), lambda bi: (0, 0, 0)),
        ],
        out_specs=pl.BlockSpec((bb * t, FO_W), lambda bi: (bi, 0)),
        out_shape=jax.ShapeDtypeStruct((b * t, FO_W), BF16),
        scratch_shapes=[pltpu.VMEM((2 * t, bb * FO_W), BF16)],
        compiler_params=_cparams("parallel"),
        name="fourier",
    )(z, z, z_tail, jnp.asarray(wc), jnp.asarray(wt), w_fno_l)


OUT_TN = 1024
OUT_TM = 512


def _outproj_kernel(oa_ref, om_ref, of_ref, wa_ref, wm_ref, wf_ref, x_ref, gate_ref, y_ref, wb_ref, *, per_row_mod):
    i = pl.program_id(1)

    @pl.when(i == 0)
    def _():
        wb_ref[0:ATT_W, :] = wa_ref[...].astype(BF16)
        wb_ref[ATT_W:ATT_W + ML_W, :] = wm_ref[...].astype(BF16)
        wb_ref[ATT_W + ML_W:D_MIX, :] = wf_ref[...].astype(BF16)

    row = 1 + (i * OUT_TM) // ROW_TILE if per_row_mod else 0
    gate = gate_ref[pl.ds(row, 1), :]
    y = (jnp.dot(oa_ref[...], wb_ref[0:ATT_W, :], preferred_element_type=F32)
         + jnp.dot(om_ref[...], wb_ref[ATT_W:ATT_W + ML_W, :], preferred_element_type=F32)
         + jnp.dot(of_ref[...], wb_ref[ATT_W + ML_W:D_MIX, :], preferred_element_type=F32))
    y_ref[...] = x_ref[...] + gate * y


def _outproj(o_att, o_ml, o_fo, w_out, layer, x2d, mod_l, *, per_row_mod):
    m = x2d.shape[0]
    return pl.pallas_call(
        functools.partial(_outproj_kernel, per_row_mod=per_row_mod),
        grid=(D_MODEL // OUT_TN, m // OUT_TM),
        in_specs=[
            pl.BlockSpec((OUT_TM, ATT_W), lambda n, i: (i, 0)),
            pl.BlockSpec((OUT_TM, ML_W), lambda n, i: (i, 0)),
            pl.BlockSpec((OUT_TM, FO_W), lambda n, i: (i, 0)),
            pl.BlockSpec((None, ATT_W, OUT_TN), lambda n, i: (layer, 0, n)),
            pl.BlockSpec((None, ML_W, OUT_TN), lambda n, i: (layer, ATT_W // ML_W, n)),
            pl.BlockSpec((None, FO_W, OUT_TN), lambda n, i: (layer, (ATT_W + ML_W) // FO_W, n)),
            pl.BlockSpec((OUT_TM, OUT_TN), lambda n, i: (i, n)),
            pl.BlockSpec((8, OUT_TN), lambda n, i: (0, 2 * D_MODEL // OUT_TN + n)),
        ],
        out_specs=pl.BlockSpec((OUT_TM, OUT_TN), lambda n, i: (i, n)),
        out_shape=jax.ShapeDtypeStruct((m, D_MODEL), F32),
        scratch_shapes=[pltpu.VMEM((D_MIX, OUT_TN), BF16)],
        compiler_params=_cparams("parallel", "arbitrary"),
        name="outproj",
    )(o_att, o_ml, o_fo, w_out, w_out, w_out, x2d, mod_l)


def _rope_tables(t):
    half = HEAD_DIM // 2
    inv_freq = ROPE_BASE ** (-jnp.arange(0, half, 2, dtype=F32) / half)
    n_rows = t // GRID_W
    rows = jnp.repeat(jnp.arange(n_rows, dtype=F32), GRID_W)
    cols = jnp.tile(jnp.arange(GRID_W, dtype=F32), n_rows)
    ar = rows[:, None] * inv_freq
    ac = cols[:, None] * inv_freq
    cos = jnp.concatenate([jnp.cos(ar), jnp.cos(ar), jnp.cos(ac), jnp.cos(ac)], axis=1)
    sin = jnp.concatenate([-jnp.sin(ar), jnp.sin(ar), -jnp.sin(ac), jnp.sin(ac)], axis=1)
    return cos, sin


def _layer(x2d, mod_l, lw, layer, *, b, t, latent, rope=None, cache=None, state=None):
    norm_w, w_in_t, b_if, conv_w, conv_b, q_norm, k_norm, ml_norm, w_fno, w_out = lw
    z, z_tail = _inproj(x2d, mod_l, norm_w, w_in_t, layer, per_row_mod=latent)
    o_att, *kv_new = _attention(z, q_norm, k_norm, b=b, t=t, rope=rope, cache=cache)
    o_ml, c_f, n_f, m_f = _mlstm(z, b_if, conv_w, conv_b, ml_norm, b=b, t=t, state=state)
    o_fo = _fourier(z, z_tail, w_fno, b=b, t=t)
    y = _outproj(o_att, o_ml, o_fo, w_out, layer, x2d, mod_l, per_row_mod=latent)
    return y, (*kv_new, c_f, n_f, m_f)


def kernel(x_prompt, x_sample, cache_k, cache_v, state_C, state_n, state_m, c, c_ctx, norm_w, w_mod, b_mod, w_in,
           b_if, conv_w, conv_b, q_norm, k_norm, ml_norm, w_fno, w_out):
    bp, tp, _ = x_prompt.shape
    bs, ts, _ = x_sample.shape
    assert tp == ML_CHUNK and ts == ROW_TILE and (bp * tp) % ROW_TILE == 0 and bs + 1 <= 8
    past = cache_k.shape[2]

    cond8 = jnp.concatenate([c_ctx[None, :], c, jnp.zeros((8 - 1 - bs, D_MODEL), F32)], axis=0)
    mod = _modulation(cond8, w_mod, b_mod)

    rope = _rope_tables(ts)
    ck = cache_k.reshape(bs, DEPTH, past, ATT_KV_W)
    cv = cache_v.reshape(bs, DEPTH, past, ATT_KV_W)
    w_in_t = jnp.swapaxes(w_in, 1, 2)

    xp = x_prompt.reshape(bp * tp, D_MODEL)
    xs = x_sample.reshape(bs * ts, D_MODEL)
    new_k, new_v, new_c, new_n, new_m = [], [], [], [], []
    for l in range(DEPTH):
        lw = (norm_w[l], w_in_t, b_if[l], conv_w[l], conv_b[l], q_norm[l], k_norm[l], ml_norm[l], w_fno[l], w_out)
        xp, (k_l, v_l, c_l, n_l, m_l) = _layer(xp, mod[l], lw, l, b=bp, t=tp, latent=False)
        new_k.append(k_l)
        new_v.append(v_l)
        new_c.append(c_l)
        new_n.append(n_l)
        new_m.append(m_l)
        xs, _ = _layer(xs, mod[l], lw, l, b=bs, t=ts, latent=True, rope=rope,
                       cache=(ck, cv, l), state=(state_C, state_n, state_m, l))
    return (xp.reshape(bp, tp, D_MODEL), xs.reshape(bs, ts, D_MODEL), jnp.stack(new_k, axis=1),
            jnp.stack(new_v, axis=1), jnp.stack(new_c, axis=1), jnp.stack(new_n, axis=1), jnp.stack(new_m, axis=1))
```

```python
import functools

import numpy as np
import jax
import jax.numpy as jnp
from jax import lax
from jax.experimental import pallas as pl
from jax.experimental.pallas import tpu as pltpu

D_MODEL = 2048
DEPTH = 2
GRID_W = 64
HEAD_DIM = 128
ATT_HEADS = 8
ATT_KV_HEADS = 2
ATT_GROUPS = ATT_HEADS // ATT_KV_HEADS
ATT_W = ATT_HEADS * HEAD_DIM
ATT_KV_W = ATT_KV_HEADS * HEAD_DIM
ML_HEADS = 4
ML_DK = 128
ML_DV = 128
ML_W = ML_HEADS * ML_DV
FO_GROUPS = 4
FO_GC = 128
FO_W = FO_GROUPS * FO_GC
D_MIX = ATT_W + ML_W + FO_W
N_GATES = 4 * ML_HEADS
D_PROJ = 2 * ATT_W + 2 * ATT_KV_W + 5 * ML_W + N_GATES + 2 * FO_W
CONV_W = 3
ROPE_BASE = 10000.0
EPS = 1e-6

LANES = 128
OFF_AQ = 0
OFF_AK = OFF_AQ + ATT_W
OFF_AV = OFF_AK + ATT_KV_W
OFF_AG = OFF_AV + ATT_KV_W
OFF_MQ = OFF_AG + ATT_W
OFF_MK = OFF_MQ + ML_W
OFF_MV = OFF_MK + ML_W
OFF_MO = OFF_MV + ML_W
OFF_MG = OFF_MO + ML_W
OFF_IF = OFF_MG + ML_W
OFF_FX = OFF_IF + N_GATES
OFF_FG = OFF_FX + FO_W
FO_SHIFT = OFF_FX % LANES

PROJ_TN = 1024
Z_W = (D_PROJ // PROJ_TN) * PROJ_TN
Z_TAIL = D_PROJ - Z_W
ROW_TILE = 1024
ML_CHUNK = 256
ML_EXT = ML_DV + 16
ATT_TQ = 256
VMEM_LIMIT = 56 * 1024 * 1024

BF16 = jnp.bfloat16
F32 = jnp.float32
NT_DIMS = (((1,), (1,)), ((), ()))


def _cparams(*sem):
    return pltpu.CompilerParams(dimension_semantics=sem, vmem_limit_bytes=VMEM_LIMIT)


def _silu(x):
    return x * (1.0 / (1.0 + jnp.exp(-x)))


def _sigmoid(x):
    return 1.0 / (1.0 + jnp.exp(-x))


def _rms(x, w):
    ms = jnp.mean(x * x, axis=-1, keepdims=True)
    return x * lax.rsqrt(ms + EPS) * w


MOD_TN = 768


def _mod_kernel(cond_ref, w_ref, b_ref, o_ref):
    a = _silu(cond_ref[...]).astype(BF16)
    o_ref[...] = jnp.dot(a, w_ref[...].astype(BF16), preferred_element_type=F32) + b_ref[...]


def _modulation(cond8, w_mod, b_mod, layer):
    n = 3 * D_MODEL
    return pl.pallas_call(
        _mod_kernel,
        grid=(n // MOD_TN,),
        in_specs=[
            pl.BlockSpec((8, D_MODEL), lambda j: (0, 0)),
            pl.BlockSpec((None, D_MODEL, MOD_TN), lambda j: (layer, 0, j)),
            pl.BlockSpec((None, 1, MOD_TN), lambda j: (layer, 0, j)),
        ],
        out_specs=pl.BlockSpec((8, MOD_TN), lambda j: (0, j)),
        out_shape=jax.ShapeDtypeStruct((8, n), F32),
        compiler_params=_cparams("parallel"),
        name="modulation",
    )(cond8, w_mod, b_mod.reshape(DEPTH, 1, n))


NORM_ROWS = 16


def _inproj_kernel(x_ref, mod_ref, nw_ref, w_ref, wt_ref, z_ref, zt_ref, h_ref, *, per_row_mod):
    i = pl.program_id(0)
    j = pl.program_id(1)

    @pl.when(j == 0)
    def _():
        row = 1 + i if per_row_mod else 0
        shift = mod_ref[pl.ds(row, 1), 0:D_MODEL]
        gain = nw_ref[...] * (1.0 + mod_ref[pl.ds(row, 1), D_MODEL:2 * D_MODEL])

        def body(r, carry):
            sl = pl.ds(pl.multiple_of(r * NORM_ROWS, NORM_ROWS), NORM_ROWS)
            x = x_ref[sl, :]
            inv = lax.rsqrt(jnp.mean(x * x, axis=-1, keepdims=True) + EPS)
            h_ref[sl, :] = (x * inv * gain + shift).astype(BF16)
            return carry

        lax.fori_loop(0, ROW_TILE // NORM_ROWS, body, 0, unroll=8)

    z_ref[...] = lax.dot_general(h_ref[...], w_ref[...].astype(BF16), NT_DIMS, preferred_element_type=F32)

    @pl.when(j == Z_W // PROJ_TN - 1)
    def _():
        zt = lax.dot_general(h_ref[...], wt_ref[...].astype(BF16), NT_DIMS, preferred_element_type=F32)
        col = lax.broadcasted_iota(jnp.int32, zt.shape, 1)
        zt_ref[...] = jnp.where(col < Z_TAIL, zt, 0.0)


def _inproj(x2d, mod_l, norm_w_l, w_in_t, layer, *, per_row_mod):
    m = x2d.shape[0]
    return pl.pallas_call(
        functools.partial(_inproj_kernel, per_row_mod=per_row_mod),
        grid=(m // ROW_TILE, Z_W // PROJ_TN),
        in_specs=[
            pl.BlockSpec((ROW_TILE, D_MODEL), lambda i, j: (i, 0)),
            pl.BlockSpec((8, 3 * D_MODEL), lambda i, j: (0, 0)),
            pl.BlockSpec((1, D_MODEL), lambda i, j: (0, 0)),
            pl.BlockSpec((None, PROJ_TN, D_MODEL), lambda i, j: (layer, j, 0)),
            pl.BlockSpec((None, LANES, D_MODEL), lambda i, j: (layer, Z_W // LANES, 0)),
        ],
        out_specs=[
            pl.BlockSpec((ROW_TILE, PROJ_TN), lambda i, j: (i, j)),
            pl.BlockSpec((ROW_TILE, LANES), lambda i, j: (i, 0)),
        ],
        out_shape=[jax.ShapeDtypeStruct((m, Z_W), F32), jax.ShapeDtypeStruct((m, LANES), F32)],
        scratch_shapes=[pltpu.VMEM((ROW_TILE, D_MODEL), BF16)],
        compiler_params=_cparams("parallel", "arbitrary"),
        name="inproj",
    )(x2d, mod_l, norm_w_l.reshape(1, D_MODEL), w_in_t, w_in_t)


def _rope(x, cos, sin):
    lane = lax.broadcasted_iota(jnp.int32, x.shape, 1)
    quarter = HEAD_DIM // 4
    partner = jnp.where((lane % (2 * quarter)) < quarter,
                        pltpu.roll(x, HEAD_DIM - quarter, axis=1), pltpu.roll(x, quarter, axis=1))
    return x * cos + partner * sin


LOG2E = 1.4426950408889634
LN2 = 0.6931471805599453


def _attn_kernel(*refs, t, tq, hk, latent):
    zq_ref, zk_ref, zv_ref = refs[:3]
    zg_refs = refs[3:3 + hk]
    rest = refs[3 + hk:]
    if latent:
        qn_ref, kn_ref, cos_ref, sin_ref, ck_ref, cv_ref, o_ref, kall_ref, vall_ref = rest
        past = ck_ref.shape[0]
    else:
        qn_ref, kn_ref, o_ref, knew_ref, vnew_ref, kall_ref, vall_ref = rest
        past = 0
    qi = pl.program_id(2)

    @pl.when(qi == 0)
    def _():
        for j in range(hk):
            kc = slice(j * HEAD_DIM, (j + 1) * HEAD_DIM)
            k = _rms(zk_ref[:, kc], kn_ref[...])
            v = zv_ref[:, kc]
            if latent:
                k = _rope(k, cos_ref[...], sin_ref[...])
                kall_ref[j, 0:past, :] = ck_ref[:, kc].astype(BF16)
                vall_ref[j, 0:past, :] = cv_ref[:, kc].astype(BF16)
            else:
                knew_ref[:, j, :] = k
                vnew_ref[:, j, :] = v
            kall_ref[j, past:past + t, :] = k.astype(BF16)
            vall_ref[j, past:past + t, :] = v.astype(BF16)

    q_scale = (HEAD_DIM ** -0.5) * LOG2E
    if latent:
        rows = pl.ds(pl.multiple_of(qi * tq, tq), tq)
        cos_q = cos_ref[rows, :]
        sin_q = sin_ref[rows, :]
    def scores(hd):
        cols = slice(hd * HEAD_DIM, (hd + 1) * HEAD_DIM)
        q = _rms(zq_ref[:, cols], qn_ref[...])
        if latent:
            q = _rope(q, cos_q, sin_q)
        qb = (q * q_scale).astype(BF16)
        return lax.dot_general(qb, kall_ref[hd // ATT_GROUPS], NT_DIMS, preferred_element_type=F32)

    def finish(hd, s):
        j, g = divmod(hd, ATT_GROUPS)
        cols = slice(hd * HEAD_DIM, (hd + 1) * HEAD_DIM)
        gcols = slice(g * HEAD_DIM, (g + 1) * HEAD_DIM)
        p = jnp.exp2(s - jnp.max(s, axis=-1, keepdims=True))
        den = jnp.sum(p, axis=-1, keepdims=True)
        o = jnp.dot(p.astype(BF16), vall_ref[j], preferred_element_type=F32) * (1.0 / den)
        o_ref[:, cols] = (o * _silu(zg_refs[j][:, gcols])).astype(BF16)

    n_heads = hk * ATT_GROUPS
    s_next = scores(0)
    for hd in range(n_heads):
        s_cur = s_next
        if hd + 1 < n_heads:
            s_next = scores(hd + 1)
        finish(hd, s_cur)


def _attention(z, q_norm_l, k_norm_l, *, b, t, rope=None, cache=None):
    latent = rope is not None
    hk = 1 if latent else ATT_KV_HEADS
    tq = min(t, ATT_TQ)
    nq = t // tq
    gw = ATT_GROUPS * HEAD_DIM
    in_specs = [
        pl.BlockSpec((tq, hk * gw), lambda bi, kg, qi: (bi * nq + qi, OFF_AQ // (hk * gw) + kg)),
        pl.BlockSpec((t, hk * HEAD_DIM), lambda bi, kg, qi: (bi, OFF_AK // (hk * HEAD_DIM) + kg)),
        pl.BlockSpec((t, hk * HEAD_DIM), lambda bi, kg, qi: (bi, OFF_AV // (hk * HEAD_DIM) + kg)),
    ]
    in_specs += [pl.BlockSpec((tq, gw), lambda bi, kg, qi, j=j: (bi * nq + qi, OFF_AG // gw + kg * hk + j))
                 for j in range(hk)]
    in_specs += [pl.BlockSpec((1, HEAD_DIM), lambda bi, kg, qi: (0, 0))] * 2
    args = [z] * (3 + hk) + [q_norm_l.reshape(1, HEAD_DIM), k_norm_l.reshape(1, HEAD_DIM)]
    tk = t
    if latent:
        cos, sin = rope
        ck, cv, layer = cache
        past = ck.shape[2]
        tk = past + t
        in_specs += [
            pl.BlockSpec((t, HEAD_DIM), lambda bi, kg, qi: (0, 0)),
            pl.BlockSpec((t, HEAD_DIM), lambda bi, kg, qi: (0, 0)),
            pl.BlockSpec((None, None, past, hk * HEAD_DIM), lambda bi, kg, qi: (bi, layer, 0, kg)),
            pl.BlockSpec((None, None, past, hk * HEAD_DIM), lambda bi, kg, qi: (bi, layer, 0, kg)),
        ]
        args += [cos, sin, ck, cv]
    out_specs = [pl.BlockSpec((tq, hk * gw), lambda bi, kg, qi: (bi * nq + qi, kg))]
    out_shape = [jax.ShapeDtypeStruct((b * t, ATT_W), BF16)]
    if not latent:
        kv_spec = pl.BlockSpec((None, t, ATT_KV_HEADS, HEAD_DIM), lambda bi, kg, qi: (bi, 0, 0, 0))
        out_specs += [kv_spec, kv_spec]
        out_shape += [jax.ShapeDtypeStruct((b, t, ATT_KV_HEADS, HEAD_DIM), F32)] * 2
    return pl.pallas_call(
        functools.partial(_attn_kernel, t=t, tq=tq, hk=hk, latent=latent),
        grid=(b, ATT_KV_HEADS // hk, nq),
        in_specs=in_specs,
        out_specs=out_specs,
        out_shape=out_shape,
        scratch_shapes=[pltpu.VMEM((hk, tk, HEAD_DIM), BF16), pltpu.VMEM((hk, tk, HEAD_DIM), BF16)],
        compiler_params=_cparams("parallel", "parallel", "arbitrary"),
        name="attention",
    )(*args)


def _split3_dot(tri, x):
    hi = x.astype(BF16)
    r1 = x - hi.astype(F32)
    mid = r1.astype(BF16)
    lo = (r1 - mid.astype(F32)).astype(BF16)
    d = lambda a: jnp.dot(tri, a, preferred_element_type=F32)
    return d(hi) + d(mid) + d(lo)


def _shift_rows(x, direction):
    n = x.shape[0]
    row = lax.broadcasted_iota(jnp.int32, x.shape, 0)
    if direction > 0:
        return jnp.where(row == 0, 0.0, pltpu.roll(x, 1, axis=0))
    return jnp.where(row == n - 1, 0.0, pltpu.roll(x, n - 1, axis=0))


def _conv_silu(x, w, b):
    y = _shift_rows(x, 1) * w[0:1, :] + x * w[1:2, :] + _shift_rows(x, -1) * w[2:3, :] + b
    return _silu(y)


def _mlstm_kernel(*refs, t, zero_init, side_mod):
    if side_mod:
        *refs, mo_ref = refs
        cond_ref, wm_ref, bm_ref = refs[6:9]
        refs = refs[:6] + refs[9:]
        _mod_kernel(cond_ref, wm_ref, bm_ref, mo_ref)
    if zero_init:
        z_ref, zif_ref, bif_ref, cw_ref, cb_ref, nw_ref, o_ref, c_out, n_out, m_out = refs
    else:
        z_ref, zif_ref, bif_ref, cw_ref, cb_ref, nw_ref, c0_ref, n0_ref, m0_ref, o_ref, c_out, n_out, m_out = refs
    L = ML_CHUNK
    nc = t // L
    chunk = lambda a, c: a[c * L:(c + 1) * L]

    gates = zif_ref[...] + bif_ref[...]
    logsig = (jnp.minimum(gates, 0.0) - jnp.log1p(jnp.exp(-jnp.abs(gates)))) * LOG2E
    gates = gates * LOG2E
    gates_t = gates.T
    r_i = lax.broadcasted_iota(jnp.int32, (L, L), 0)
    c_i = lax.broadcasted_iota(jnp.int32, (L, L), 1)
    lower, upper = r_i >= c_i, r_i <= c_i
    cum_tris = (lower.astype(BF16), upper.astype(BF16))
    dmasks = (upper, lower)
    cums = [[_split3_dot(cum_tris[d], chunk(logsig, c)) for c in range(nc)] for d in range(2)]
    cums_t = [[x.T for x in row] for row in cums]
    ext_rows = (lax.broadcasted_iota(jnp.int32, (ML_EXT - ML_DV, t), 0) == 0).astype(F32)
    lchunk = lambda a, c: a[:, c * L:(c + 1) * L]

    def running_max(a, d):
        rows = a.shape[0]
        if rows < 8:
            a = jnp.concatenate([a] * (8 // rows), axis=0)
        lane = lax.broadcasted_iota(jnp.int32, a.shape, 1)
        k = 1
        while k < L:
            if d:
                shifted = jnp.where(lane < L - k, pltpu.roll(a, L - k, axis=1), -jnp.inf)
            else:
                shifted = jnp.where(lane >= k, pltpu.roll(a, k, axis=1), -jnp.inf)
            a = jnp.maximum(a, shifted)
            k *= 2
        return a[:rows]

    hsl = lambda d, which: slice((2 * d + which) * ML_HEADS, (2 * d + which + 1) * ML_HEADS)
    a_maxes = [running_max(jnp.concatenate([lchunk(gates_t, c)[hsl(d, 0)] - cums_t[d][c][hsl(d, 1)]
                                            for c in range(nc)], axis=0), d) for d in range(2)]

    def local_part(qk_c, kb_c, vt_c, vtb_c, b_row, b_col, i_row, i_col, a_max, d):
        m_loc = b_row + a_max
        s = (qk_c * jnp.exp2(jnp.where(dmasks[d], (i_col - b_col) - a_max, -jnp.inf))).astype(BF16)
        bm = jnp.dot(vtb_c, s, preferred_element_type=F32)
        b_end = b_row[:, 0:1] if d else b_row[:, L - 1:L]
        g = b_end - b_row + i_row
        g_max = jnp.max(g, axis=1, keepdims=True)
        vw = (vt_c * jnp.exp2(g - g_max)).astype(BF16)
        u = jnp.dot(vw, kb_c, preferred_element_type=F32)
        return m_loc, bm, b_end, g_max, u

    def scan_step(qt_c, b_row, loc, c_t, m):
        m_loc, bm, b_end, g_max, u = loc
        inter = b_row + m
        m_row = jnp.maximum(inter, m_loc)
        nd = jnp.exp2(m_loc - m_row) * bm
        if c_t is not None:
            nd = nd + jnp.exp2(inter - m_row) * jnp.dot(c_t.astype(BF16), qt_c, preferred_element_type=F32)
        h_t = nd[:ML_DV] / jnp.maximum(jnp.abs(nd[ML_DV:ML_DV + 1]), jnp.exp2(-m_row))
        m_new = jnp.maximum(b_end + m, g_max)
        c_new = jnp.exp2(g_max - m_new) * u
        if c_t is not None:
            c_new = c_new + jnp.exp2(b_end + m - m_new) * c_t
        return h_t, c_new, m_new

    chains = [(hd, d) for hd in range(ML_HEADS) for d in range(2)]
    qts, b_rows, locs, state = {}, {}, {}, {}
    for hd in range(ML_HEADS):
        cols = slice(hd * ML_DK, (hd + 1) * ML_DK)
        kcols = slice(ML_W + hd * ML_DK, ML_W + (hd + 1) * ML_DK)
        vcols = slice(2 * ML_W + hd * ML_DV, 2 * ML_W + (hd + 1) * ML_DV)
        qt = _conv_silu(z_ref[:, cols], cw_ref[:, cols], cb_ref[:, cols]).T.astype(BF16)
        kb = (_conv_silu(z_ref[:, kcols], cw_ref[:, kcols], cb_ref[:, kcols]) * (ML_DK ** -0.5)).astype(BF16)
        vt = jnp.concatenate([z_ref[:, vcols].T, ext_rows], axis=0)
        vtb = vt.astype(BF16)
        qk = [jnp.dot(chunk(kb, c), lchunk(qt, c), preferred_element_type=F32) for c in range(nc)]
        qts[hd] = qt
        for d in range(2):
            i_lane = 2 * d * ML_HEADS + hd
            f_lane = (2 * d + 1) * ML_HEADS + hd
            b_rows[hd, d] = [cums_t[d][c][f_lane:f_lane + 1, :] for c in range(nc)]
            locs[hd, d] = [local_part(qk[c], chunk(kb, c), lchunk(vt, c), lchunk(vtb, c), b_rows[hd, d][c],
                                      cums[d][c][:, f_lane:f_lane + 1], lchunk(gates_t, c)[i_lane:i_lane + 1, :],
                                      chunk(gates, c)[:, i_lane:i_lane + 1],
                                      a_maxes[d][c * ML_HEADS + hd:c * ML_HEADS + hd + 1], d) for c in range(nc)]
            if zero_init:
                state[hd, d] = (None, jnp.zeros((1, 1), F32))
            else:
                c_t = jnp.concatenate([c0_ref[d, hd].T, n0_ref[d, hd:hd + 1, :],
                                       jnp.zeros((ML_EXT - ML_DV - 1, ML_DK), F32)], axis=0)
                state[hd, d] = (c_t, m0_ref[d:d + 1, hd:hd + 1] * LOG2E)

    h_parts = {ch: [None] * nc for ch in chains}
    for step in range(nc):
        for hd, d in chains:
            c = nc - 1 - step if d else step
            c_t, m = state[hd, d]
            h_parts[hd, d][c], c_t, m = scan_step(lchunk(qts[hd], c), b_rows[hd, d][c], locs[hd, d][c], c_t, m)
            state[hd, d] = (c_t, m)

    for hd, d in chains:
        c_t, m = state[hd, d]
        c_out[d, hd] = c_t[:ML_DV].T
        n_out[d, hd:hd + 1, :] = c_t[ML_DV:ML_DV + 1]
        m_out[d:d + 1, hd:hd + 1] = m * LN2
    for hd in range(ML_HEADS):
        cols = slice(hd * ML_DK, (hd + 1) * ML_DK)
        ocols = slice(3 * ML_W + hd * ML_DV, 3 * ML_W + (hd + 1) * ML_DV)
        gcols = slice(4 * ML_W + hd * ML_DV, 4 * ML_W + (hd + 1) * ML_DV)
        h_fwd, h_bwd = (jnp.concatenate(h_parts[hd, d], axis=1) if nc > 1 else h_parts[hd, d][0] for d in range(2))
        h_sum = h_fwd + h_bwd
        hm = _rms(h_sum.T, nw_ref[:, cols])
        o_ref[:, cols] = (hm * _sigmoid(z_ref[:, ocols]) * _silu(z_ref[:, gcols])).astype(BF16)


def _mlstm(z, b_if_l, conv_w_l, conv_b_l, ml_norm_l, *, b, t, state=None, side=None):
    zero_init = state is None
    zw = 5 * ML_W
    in_specs = [
        pl.BlockSpec((t, zw), lambda bi: (bi, OFF_MQ // zw)),
        pl.BlockSpec((t, LANES), lambda bi: (bi, OFF_IF // LANES)),
        pl.BlockSpec((1, LANES), lambda bi: (0, 0)),
        pl.BlockSpec((CONV_W, 2 * ML_W), lambda bi: (0, 0)),
        pl.BlockSpec((1, 2 * ML_W), lambda bi: (0, 0)),
        pl.BlockSpec((1, ML_W), lambda bi: (0, 0)),
    ]
    bif = jnp.pad(b_if_l, (0, LANES - N_GATES)).reshape(1, LANES)
    args = [z, z, bif, conv_w_l, conv_b_l.reshape(1, 2 * ML_W), ml_norm_l.reshape(1, ML_W)]
    out_specs = [
        pl.BlockSpec((t, ML_W), lambda bi: (bi, 0)),
        pl.BlockSpec((None, 2, ML_HEADS, ML_DK, ML_DV), lambda bi: (bi, 0, 0, 0, 0)),
        pl.BlockSpec((None, 2, ML_HEADS, ML_DK), lambda bi: (bi, 0, 0, 0)),
        pl.BlockSpec((None, 2, ML_HEADS), lambda bi: (bi, 0, 0)),
    ]
    out_shape = [
        jax.ShapeDtypeStruct((b * t, ML_W), BF16),
        jax.ShapeDtypeStruct((b, 2, ML_HEADS, ML_DK, ML_DV), F32),
        jax.ShapeDtypeStruct((b, 2, ML_HEADS, ML_DK), F32),
        jax.ShapeDtypeStruct((b, 2, ML_HEADS), F32),
    ]
    if side is not None:
        cond8, w_mod, b_mod, side_layer = side
        n = 3 * D_MODEL
        tn = n // b
        assert tn % LANES == 0 and tn * b == n
        in_specs += [
            pl.BlockSpec((8, D_MODEL), lambda bi: (0, 0)),
            pl.BlockSpec((None, D_MODEL, tn), lambda bi: (side_layer, 0, bi)),
            pl.BlockSpec((None, 1, tn), lambda bi: (side_layer, 0, bi)),
        ]
        args += [cond8, w_mod, b_mod.reshape(DEPTH, 1, n)]
        out_specs.append(pl.BlockSpec((8, tn), lambda bi: (0, bi)))
        out_shape.append(jax.ShapeDtypeStruct((8, n), F32))
    if not zero_init:
        c0, n0, m0, layer = state
        in_specs += [
            pl.BlockSpec((None, None, 2, ML_HEADS, ML_DK, ML_DV), lambda bi: (bi, layer, 0, 0, 0, 0)),
            pl.BlockSpec((None, None, 2, ML_HEADS, ML_DK), lambda bi: (bi, layer, 0, 0, 0)),
            pl.BlockSpec((None, None, 2, ML_HEADS), lambda bi: (bi, layer, 0, 0)),
        ]
        args += [c0, n0, m0]
    return pl.pallas_call(
        functools.partial(_mlstm_kernel, t=t, zero_init=zero_init, side_mod=side is not None),
        grid=(b,),
        in_specs=in_specs,
        out_specs=out_specs,
        out_shape=out_shape,
        compiler_params=_cparams("parallel"),
        name="mlstm",
    )(*args)


def _dft_tables(t):
    def cs(n):
        idx = np.arange(n, dtype=np.int64)
        ang = 2.0 * np.pi * ((idx[:, None] * idx[None, :]) % n).astype(np.float64) / n
        return np.cos(ang), np.sin(ang)
    cc, sc = cs(FO_GC)
    ct, st = cs(t)
    return (np.concatenate([cc, sc], axis=1).astype(np.float32),
            np.concatenate([ct, -st], axis=1).astype(np.float32))


FO_RT = 256


def _fourier_kernel(zx_ref, zga_ref, zgb_ref, wc_ref, wt_ref, wf_ref, o_ref, y_ref, *, t, bb):
    wc = wc_ref[...]
    for bi in range(bb):
        x = zx_ref[bi * t:(bi + 1) * t, FO_SHIFT:FO_SHIFT + FO_W]
        for g in range(FO_GROUPS):
            cols = slice(g * FO_GC, (g + 1) * FO_GC)
            ycols = slice(bi * FO_W + g * FO_GC, bi * FO_W + (g + 1) * FO_GC)
            y = jnp.dot(x[:, cols].astype(BF16), wc, preferred_element_type=F32)
            y_ref[0:t, ycols] = y[:, :FO_GC].astype(BF16)
            y_ref[t:2 * t, ycols] = y[:, FO_GC:].astype(BF16)
    scale = (t * FO_GC) ** -0.5
    for r in range(t // FO_RT):
        rows = slice(r * FO_RT, (r + 1) * FO_RT)
        f = jnp.dot(wt_ref[rows, :], y_ref[...], preferred_element_type=F32) * scale
        for bi in range(bb):
            orows = slice(bi * t + r * FO_RT, bi * t + (r + 1) * FO_RT)
            fg = jnp.concatenate([zga_ref[orows, FO_SHIFT:], zgb_ref[orows, :FO_SHIFT]], axis=1)
            for g in range(FO_GROUPS):
                cols = slice(g * FO_GC, (g + 1) * FO_GC)
                fcols = slice(bi * FO_W + g * FO_GC, bi * FO_W + (g + 1) * FO_GC)
                og = jnp.dot(f[:, fcols].astype(BF16), wf_ref[g].astype(BF16), preferred_element_type=F32)
                o_ref[orows, cols] = (og * _silu(fg[:, cols])).astype(BF16)


def _fourier(z, z_tail, w_fno_l, *, b, t):
    assert OFF_FG - FO_SHIFT + FO_W == Z_W and Z_TAIL == FO_SHIFT
    wc, wt = _dft_tables(t)
    xw = FO_W + LANES
    bb = max(1, min(b, ROW_TILE // t))
    return pl.pallas_call(
        functools.partial(_fourier_kernel, t=t, bb=bb),
        grid=(b // bb,),
        in_specs=[
            pl.BlockSpec((bb * t, xw), lambda bi: (bi, (OFF_FX - FO_SHIFT) // xw)),
            pl.BlockSpec((bb * t, FO_W), lambda bi: (bi, (OFF_FG - FO_SHIFT) // FO_W)),
            pl.BlockSpec((bb * t, LANES), lambda bi: (bi, 0)),
            pl.BlockSpec((FO_GC, 2 * FO_GC), lambda bi: (0, 0)),
            pl.BlockSpec((t, 2 * t), lambda bi: (0, 0)),
            pl.BlockSpec((FO_GROUPS, FO_GC, FO_GC), lambda bi: (0, 0, 0)),
        ],
        out_specs=pl.BlockSpec((bb * t, FO_W), lambda bi: (bi, 0)),
        out_shape=jax.ShapeDtypeStruct((b * t, FO_W), BF16),
        scratch_shapes=[pltpu.VMEM((2 * t, bb * FO_W), BF16)],
        compiler_params=_cparams("parallel"),
        name="fourier",
    )(z, z, z_tail, jnp.asarray(wc).astype(BF16), jnp.asarray(wt).astype(BF16), w_fno_l)


OUT_TN = 1024
OUT_TM = 1024


def _outproj_kernel(oa_ref, om_ref, of_ref, wa_ref, wm_ref, wf_ref, x_ref, gate_ref, y_ref, wb_ref, *, per_row_mod):
    i = pl.program_id(1)

    @pl.when(i == 0)
    def _():
        wb_ref[0:ATT_W, :] = wa_ref[...].astype(BF16)
        wb_ref[ATT_W:ATT_W + ML_W, :] = wm_ref[...].astype(BF16)
        wb_ref[ATT_W + ML_W:D_MIX, :] = wf_ref[...].astype(BF16)

    row = 1 + (i * OUT_TM) // ROW_TILE if per_row_mod else 0
    gate = gate_ref[pl.ds(row, 1), :]
    y = (jnp.dot(oa_ref[...], wb_ref[0:ATT_W, :], preferred_element_type=F32)
         + jnp.dot(om_ref[...], wb_ref[ATT_W:ATT_W + ML_W, :], preferred_element_type=F32)
         + jnp.dot(of_ref[...], wb_ref[ATT_W + ML_W:D_MIX, :], preferred_element_type=F32))
    y_ref[...] = x_ref[...] + gate * y


def _outproj(o_att, o_ml, o_fo, w_out, layer, x2d, mod_l, *, per_row_mod):
    m = x2d.shape[0]
    return pl.pallas_call(
        functools.partial(_outproj_kernel, per_row_mod=per_row_mod),
        grid=(D_MODEL // OUT_TN, m // OUT_TM),
        in_specs=[
            pl.BlockSpec((OUT_TM, ATT_W), lambda n, i: (i, 0)),
            pl.BlockSpec((OUT_TM, ML_W), lambda n, i: (i, 0)),
            pl.BlockSpec((OUT_TM, FO_W), lambda n, i: (i, 0)),
            pl.BlockSpec((None, ATT_W, OUT_TN), lambda n, i: (layer, 0, n)),
            pl.BlockSpec((None, ML_W, OUT_TN), lambda n, i: (layer, ATT_W // ML_W, n)),
            pl.BlockSpec((None, FO_W, OUT_TN), lambda n, i: (layer, (ATT_W + ML_W) // FO_W, n)),
            pl.BlockSpec((OUT_TM, OUT_TN), lambda n, i: (i, n)),
            pl.BlockSpec((8, OUT_TN), lambda n, i: (0, 2 * D_MODEL // OUT_TN + n)),
        ],
        out_specs=pl.BlockSpec((OUT_TM, OUT_TN), lambda n, i: (i, n)),
        out_shape=jax.ShapeDtypeStruct((m, D_MODEL), F32),
        scratch_shapes=[pltpu.VMEM((D_MIX, OUT_TN), BF16)],
        compiler_params=_cparams("parallel", "arbitrary"),
        name="outproj",
    )(o_att, o_ml, o_fo, w_out, w_out, w_out, x2d, mod_l)


def _rope_tables(t):
    half = HEAD_DIM // 2
    inv_freq = ROPE_BASE ** (-jnp.arange(0, half, 2, dtype=F32) / half)
    n_rows = t // GRID_W
    rows = jnp.repeat(jnp.arange(n_rows, dtype=F32), GRID_W)
    cols = jnp.tile(jnp.arange(GRID_W, dtype=F32), n_rows)
    ar = rows[:, None] * inv_freq
    ac = cols[:, None] * inv_freq
    cos = jnp.concatenate([jnp.cos(ar), jnp.cos(ar), jnp.cos(ac), jnp.cos(ac)], axis=1)
    sin = jnp.concatenate([-jnp.sin(ar), jnp.sin(ar), -jnp.sin(ac), jnp.sin(ac)], axis=1)
    return cos, sin


def _layer(x2d, mod_l, lw, layer, *, b, t, latent, rope=None, cache=None, state=None, side=None):
    norm_w, w_in_t, b_if, conv_w, conv_b, q_norm, k_norm, ml_norm, w_fno, w_out = lw
    z, z_tail = _inproj(x2d, mod_l, norm_w, w_in_t, layer, per_row_mod=latent)
    o_att, *kv_new = _attention(z, q_norm, k_norm, b=b, t=t, rope=rope, cache=cache)
    o_ml, c_f, n_f, m_f, *side_mod = _mlstm(z, b_if, conv_w, conv_b, ml_norm, b=b, t=t, state=state, side=side)
    o_fo = _fourier(z, z_tail, w_fno, b=b, t=t)
    y = _outproj(o_att, o_ml, o_fo, w_out, layer, x2d, mod_l, per_row_mod=latent)
    return y, (*kv_new, c_f, n_f, m_f), side_mod


def kernel(x_prompt, x_sample, cache_k, cache_v, state_C, state_n, state_m, c, c_ctx, norm_w, w_mod, b_mod, w_in,
           b_if, conv_w, conv_b, q_norm, k_norm, ml_norm, w_fno, w_out):
    bp, tp, _ = x_prompt.shape
    bs, ts, _ = x_sample.shape
    assert tp == ML_CHUNK and ts == ROW_TILE and (bp * tp) % ROW_TILE == 0 and bs + 1 <= 8
    past = cache_k.shape[2]

    cond8 = jnp.concatenate([c_ctx[None, :], c, jnp.zeros((8 - 1 - bs, D_MODEL), F32)], axis=0)
    mod_l = _modulation(cond8, w_mod, b_mod, 0)

    rope = _rope_tables(ts)
    ck = cache_k.reshape(bs, DEPTH, past, ATT_KV_W)
    cv = cache_v.reshape(bs, DEPTH, past, ATT_KV_W)
    w_in_t = jnp.swapaxes(w_in, 1, 2)

    xp = x_prompt.reshape(bp * tp, D_MODEL)
    xs = x_sample.reshape(bs * ts, D_MODEL)
    new_k, new_v, new_c, new_n, new_m = [], [], [], [], []
    for l in range(DEPTH):
        lw = (norm_w[l], w_in_t, b_if[l], conv_w[l], conv_b[l], q_norm[l], k_norm[l], ml_norm[l], w_fno[l], w_out)
        side = (cond8, w_mod, b_mod, l + 1) if l + 1 < DEPTH else None
        xp, (k_l, v_l, c_l, n_l, m_l), mod_next = _layer(xp, mod_l, lw, l, b=bp, t=tp, latent=False, side=side)
        new_k.append(k_l)
        new_v.append(v_l)
        new_c.append(c_l)
        new_n.append(n_l)
        new_m.append(m_l)
        xs, _, _ = _layer(xs, mod_l, lw, l, b=bs, t=ts, latent=True, rope=rope,
                          cache=(ck, cv, l), state=(state_C, state_n, state_m, l))
        if mod_next:
            mod_l = mod_next[0]
    return (xp.reshape(bp, tp, D_MODEL), xs.reshape(bs, ts, D_MODEL), jnp.stack(new_k, axis=1),
            jnp.stack(new_v, axis=1), jnp.stack(new_c, axis=1), jnp.stack(new_n, axis=1), jnp.stack(new_m, axis=1))
```

```python
import functools

import numpy as np
import jax
import jax.numpy as jnp
from jax import lax
from jax.experimental import pallas as pl
from jax.experimental.pallas import tpu as pltpu

D_MODEL = 2048
DEPTH = 2
GRID_W = 64
HEAD_DIM = 128
ATT_HEADS = 8
ATT_KV_HEADS = 2
ATT_GROUPS = ATT_HEADS // ATT_KV_HEADS
ATT_W = ATT_HEADS * HEAD_DIM
ATT_KV_W = ATT_KV_HEADS * HEAD_DIM
ML_HEADS = 4
ML_DK = 128
ML_DV = 128
ML_W = ML_HEADS * ML_DV
FO_GROUPS = 4
FO_GC = 128
FO_W = FO_GROUPS * FO_GC
D_MIX = ATT_W + ML_W + FO_W
N_GATES = 4 * ML_HEADS
D_PROJ = 2 * ATT_W + 2 * ATT_KV_W + 5 * ML_W + N_GATES + 2 * FO_W
CONV_W = 3
ROPE_BASE = 10000.0
EPS = 1e-6

LANES = 128
MXU_N = 256
OFF_AQ = 0
OFF_AK = OFF_AQ + ATT_W
OFF_AV = OFF_AK + ATT_KV_W
OFF_AG = OFF_AV + ATT_KV_W
OFF_MQ = OFF_AG + ATT_W
OFF_MK = OFF_MQ + ML_W
OFF_MV = OFF_MK + ML_W
OFF_MO = OFF_MV + ML_W
OFF_MG = OFF_MO + ML_W
OFF_IF = OFF_MG + ML_W
OFF_FX = OFF_IF + N_GATES
OFF_FG = OFF_FX + FO_W
FO_SHIFT = OFF_FX % LANES

PROJ_TN = 1024
Z_W = (D_PROJ // PROJ_TN) * PROJ_TN
Z_TAIL = D_PROJ - Z_W
ROW_TILE = 1024
ML_CHUNK = 256
ML_EXT = ML_DV + 16
ATT_TQ = 256
VMEM_LIMIT = 56 * 1024 * 1024

BF16 = jnp.bfloat16
F32 = jnp.float32
NT_DIMS = (((1,), (1,)), ((), ()))


def _cparams(*sem):
    return pltpu.CompilerParams(dimension_semantics=sem, vmem_limit_bytes=VMEM_LIMIT)


def _silu(x):
    return x * (1.0 / (1.0 + jnp.exp(-x)))


def _sigmoid(x):
    return 1.0 / (1.0 + jnp.exp(-x))


def _rms(x, w):
    ms = jnp.mean(x * x, axis=-1, keepdims=True)
    return x * lax.rsqrt(ms + EPS) * w


MOD_TN = 768


def _mod_kernel(cond_ref, w_ref, b_ref, o_ref):
    a = _silu(cond_ref[...]).astype(BF16)
    o_ref[...] = jnp.dot(a, w_ref[...].astype(BF16), preferred_element_type=F32) + b_ref[...]


def _modulation(cond8, w_mod, b_mod, layer):
    n = 3 * D_MODEL
    return pl.pallas_call(
        _mod_kernel,
        grid=(n // MOD_TN,),
        in_specs=[
            pl.BlockSpec((8, D_MODEL), lambda j: (0, 0)),
            pl.BlockSpec((None, D_MODEL, MOD_TN), lambda j: (layer, 0, j)),
            pl.BlockSpec((None, 1, MOD_TN), lambda j: (layer, 0, j)),
        ],
        out_specs=pl.BlockSpec((8, MOD_TN), lambda j: (0, j)),
        out_shape=jax.ShapeDtypeStruct((8, n), F32),
        compiler_params=_cparams("parallel"),
        name="modulation",
    )(cond8, w_mod, b_mod.reshape(DEPTH, 1, n))


NORM_ROWS = 16


def _shift_rows(x, direction, period):
    n = x.shape[0]
    pos = lax.broadcasted_iota(jnp.int32, x.shape, 0) % period
    if direction > 0:
        return jnp.where(pos == 0, 0.0, pltpu.roll(x, 1, axis=0))
    return jnp.where(pos == period - 1, 0.0, pltpu.roll(x, n - 1, axis=0))


def _conv_silu(x, w, b, period):
    y = _shift_rows(x, 1, period) * w[0:1, :] + x * w[1:2, :] + _shift_rows(x, -1, period) * w[2:3, :] + b
    return _silu(y)


def _inproj_kernel(x_ref, mod_ref, nw_ref, w_ref, wt_ref, cw_ref, cb_ref, z_ref, zt_ref, h_ref, *, per_row_mod, t):
    i = pl.program_id(0)
    j = pl.program_id(1)

    @pl.when(j == 0)
    def _():
        row = 1 + i if per_row_mod else 0
        shift = mod_ref[pl.ds(row, 1), 0:D_MODEL]
        gain = nw_ref[...] * (1.0 + mod_ref[pl.ds(row, 1), D_MODEL:2 * D_MODEL])

        def body(r, carry):
            sl = pl.ds(pl.multiple_of(r * NORM_ROWS, NORM_ROWS), NORM_ROWS)
            x = x_ref[sl, :]
            inv = lax.rsqrt(jnp.mean(x * x, axis=-1, keepdims=True) + EPS)
            h_ref[sl, :] = (x * inv * gain + shift).astype(BF16)
            return carry

        lax.fori_loop(0, ROW_TILE // NORM_ROWS, body, 0, unroll=8)

    def project(cols=slice(0, PROJ_TN)):
        return lax.dot_general(h_ref[...], w_ref[cols, :].astype(BF16), NT_DIMS, preferred_element_type=F32)

    half = PROJ_TN // 2
    pieces = lambda start: [slice(c, c + MXU_N) for c in range(start, start + half, MXU_N)]
    j_q, j_k, j_o = OFF_MQ // PROJ_TN, OFF_MK // PROJ_TN, OFF_MO // PROJ_TN

    def conv_tile(start, raw_start, conv_off, scale):
        for cols, rcols in zip(pieces(start), pieces(raw_start)):
            cc = slice(conv_off + cols.start - start, conv_off + cols.stop - start)
            y = _conv_silu(project(cols), cw_ref[:, cc], cb_ref[:, cc], t)
            z_ref[:, cols] = y if scale is None else y * scale
            z_ref[:, rcols] = project(rcols)

    @pl.when(j == j_q)
    def _():
        conv_tile(half, 0, 0, None)

    @pl.when(j == j_k)
    def _():
        conv_tile(0, half, ML_W, ML_DK ** -0.5)

    @pl.when(j == j_o)
    def _():
        z_ref[...] = project()
        z_ref[:, 0:half] = _sigmoid(z_ref[:, 0:half]) * _silu(z_ref[:, half:PROJ_TN])

    @pl.when((j != j_q) & (j != j_k) & (j != j_o))
    def _():
        z_ref[...] = project()

    @pl.when(j == Z_W // PROJ_TN - 1)
    def _():
        zt = lax.dot_general(h_ref[...], wt_ref[...].astype(BF16), NT_DIMS, preferred_element_type=F32)
        col = lax.broadcasted_iota(jnp.int32, zt.shape, 1)
        zt_ref[...] = jnp.where(col < Z_TAIL, zt, 0.0)


def _inproj(x2d, mod_l, norm_w_l, w_in_t, layer, conv_w_l, conv_b_l, *, per_row_mod, t):
    m = x2d.shape[0]
    half = PROJ_TN // 2
    assert OFF_MQ % PROJ_TN == half and OFF_MK % PROJ_TN == 0 and OFF_MO % PROJ_TN == 0 and ML_W == half
    assert ROW_TILE % t == 0
    return pl.pallas_call(
        functools.partial(_inproj_kernel, per_row_mod=per_row_mod, t=t),
        grid=(m // ROW_TILE, Z_W // PROJ_TN),
        in_specs=[
            pl.BlockSpec((ROW_TILE, D_MODEL), lambda i, j: (i, 0)),
            pl.BlockSpec((8, 3 * D_MODEL), lambda i, j: (0, 0)),
            pl.BlockSpec((1, D_MODEL), lambda i, j: (0, 0)),
            pl.BlockSpec((None, PROJ_TN, D_MODEL), lambda i, j: (layer, j, 0)),
            pl.BlockSpec((None, LANES, D_MODEL), lambda i, j: (layer, Z_W // LANES, 0)),
            pl.BlockSpec((CONV_W, 2 * ML_W), lambda i, j: (0, 0)),
            pl.BlockSpec((1, 2 * ML_W), lambda i, j: (0, 0)),
        ],
        out_specs=[
            pl.BlockSpec((ROW_TILE, PROJ_TN), lambda i, j: (i, j)),
            pl.BlockSpec((ROW_TILE, LANES), lambda i, j: (i, 0)),
        ],
        out_shape=[jax.ShapeDtypeStruct((m, Z_W), F32), jax.ShapeDtypeStruct((m, LANES), F32)],
        scratch_shapes=[pltpu.VMEM((ROW_TILE, D_MODEL), BF16)],
        compiler_params=_cparams("parallel", "arbitrary"),
        name="inproj",
    )(x2d, mod_l, norm_w_l.reshape(1, D_MODEL), w_in_t, w_in_t, conv_w_l, conv_b_l.reshape(1, 2 * ML_W))


def _rope(x, cos, sin):
    lane = lax.broadcasted_iota(jnp.int32, x.shape, 1)
    quarter = HEAD_DIM // 4
    partner = jnp.where((lane % (2 * quarter)) < quarter,
                        pltpu.roll(x, HEAD_DIM - quarter, axis=1), pltpu.roll(x, quarter, axis=1))
    return x * cos + partner * sin


LOG2E = 1.4426950408889634
LN2 = 0.6931471805599453


def _attn_kernel(*refs, t, tq, hk, latent):
    zq_ref, zk_ref, zv_ref = refs[:3]
    zg_refs = refs[3:3 + hk]
    rest = refs[3 + hk:]
    if latent:
        qn_ref, kn_ref, cos_ref, sin_ref, ck_ref, cv_ref, o_ref, kall_ref, vall_ref = rest
        past = ck_ref.shape[0]
    else:
        qn_ref, kn_ref, o_ref, knew_ref, vnew_ref, kall_ref, vall_ref = rest
        past = 0
    qi = pl.program_id(2)

    @pl.when(qi == 0)
    def _():
        for j in range(hk):
            kc = slice(j * HEAD_DIM, (j + 1) * HEAD_DIM)
            k = _rms(zk_ref[:, kc], kn_ref[...])
            v = zv_ref[:, kc]
            if latent:
                k = _rope(k, cos_ref[...], sin_ref[...])
                kall_ref[j, 0:past, :] = ck_ref[:, kc].astype(BF16)
                vall_ref[j, 0:past, :] = cv_ref[:, kc].astype(BF16)
            else:
                knew_ref[:, j, :] = k
                vnew_ref[:, j, :] = v
            kall_ref[j, past:past + t, :] = k.astype(BF16)
            vall_ref[j, past:past + t, :] = v.astype(BF16)

    q_scale = (HEAD_DIM ** -0.5) * LOG2E
    if latent:
        rows = pl.ds(pl.multiple_of(qi * tq, tq), tq)
        cos_q = cos_ref[rows, :]
        sin_q = sin_ref[rows, :]
    def scores(hd):
        cols = slice(hd * HEAD_DIM, (hd + 1) * HEAD_DIM)
        q = _rms(zq_ref[:, cols], qn_ref[...])
        if latent:
            q = _rope(q, cos_q, sin_q)
        qb = (q * q_scale).astype(BF16)
        return lax.dot_general(qb, kall_ref[hd // ATT_GROUPS], NT_DIMS, preferred_element_type=F32)

    def finish(hd, s):
        j, g = divmod(hd, ATT_GROUPS)
        cols = slice(hd * HEAD_DIM, (hd + 1) * HEAD_DIM)
        gcols = slice(g * HEAD_DIM, (g + 1) * HEAD_DIM)
        p = jnp.exp2(s - jnp.max(s, axis=-1, keepdims=True))
        den = jnp.sum(p, axis=-1, keepdims=True)
        o = jnp.dot(p.astype(BF16), vall_ref[j], preferred_element_type=F32) * (1.0 / den)
        o_ref[:, cols] = (o * _silu(zg_refs[j][:, gcols])).astype(BF16)

    n_heads = hk * ATT_GROUPS
    s_next = scores(0)
    for hd in range(n_heads):
        s_cur = s_next
        if hd + 1 < n_heads:
            s_next = scores(hd + 1)
        finish(hd, s_cur)


def _attention(z, q_norm_l, k_norm_l, *, b, t, rope=None, cache=None):
    latent = rope is not None
    hk = 1 if latent else ATT_KV_HEADS
    tq = min(t, ATT_TQ)
    nq = t // tq
    gw = ATT_GROUPS * HEAD_DIM
    in_specs = [
        pl.BlockSpec((tq, hk * gw), lambda bi, kg, qi: (bi * nq + qi, OFF_AQ // (hk * gw) + kg)),
        pl.BlockSpec((t, hk * HEAD_DIM), lambda bi, kg, qi: (bi, OFF_AK // (hk * HEAD_DIM) + kg)),
        pl.BlockSpec((t, hk * HEAD_DIM), lambda bi, kg, qi: (bi, OFF_AV // (hk * HEAD_DIM) + kg)),
    ]
    in_specs += [pl.BlockSpec((tq, gw), lambda bi, kg, qi, j=j: (bi * nq + qi, OFF_AG // gw + kg * hk + j))
                 for j in range(hk)]
    in_specs += [pl.BlockSpec((1, HEAD_DIM), lambda bi, kg, qi: (0, 0))] * 2
    args = [z] * (3 + hk) + [q_norm_l.reshape(1, HEAD_DIM), k_norm_l.reshape(1, HEAD_DIM)]
    tk = t
    if latent:
        cos, sin = rope
        ck, cv, layer = cache
        past = ck.shape[2]
        tk = past + t
        in_specs += [
            pl.BlockSpec((t, HEAD_DIM), lambda bi, kg, qi: (0, 0)),
            pl.BlockSpec((t, HEAD_DIM), lambda bi, kg, qi: (0, 0)),
            pl.BlockSpec((None, None, past, hk * HEAD_DIM), lambda bi, kg, qi: (bi, layer, 0, kg)),
            pl.BlockSpec((None, None, past, hk * HEAD_DIM), lambda bi, kg, qi: (bi, layer, 0, kg)),
        ]
        args += [cos, sin, ck, cv]
    out_specs = [pl.BlockSpec((tq, hk * gw), lambda bi, kg, qi: (bi * nq + qi, kg))]
    out_shape = [jax.ShapeDtypeStruct((b * t, ATT_W), BF16)]
    if not latent:
        kv_spec = pl.BlockSpec((None, t, ATT_KV_HEADS, HEAD_DIM), lambda bi, kg, qi: (bi, 0, 0, 0))
        out_specs += [kv_spec, kv_spec]
        out_shape += [jax.ShapeDtypeStruct((b, t, ATT_KV_HEADS, HEAD_DIM), F32)] * 2
    return pl.pallas_call(
        functools.partial(_attn_kernel, t=t, tq=tq, hk=hk, latent=latent),
        grid=(b, ATT_KV_HEADS // hk, nq),
        in_specs=in_specs,
        out_specs=out_specs,
        out_shape=out_shape,
        scratch_shapes=[pltpu.VMEM((hk, tk, HEAD_DIM), BF16), pltpu.VMEM((hk, tk, HEAD_DIM), BF16)],
        compiler_params=_cparams("parallel", "parallel", "arbitrary"),
        name="attention",
    )(*args)


def _split3_dot(tri, x):
    hi = x.astype(BF16)
    r1 = x - hi.astype(F32)
    mid = r1.astype(BF16)
    lo = (r1 - mid.astype(F32)).astype(BF16)
    d = lambda a: jnp.dot(tri, a, preferred_element_type=F32)
    return d(hi) + d(mid) + d(lo)


def _mlstm_kernel(*refs, t, zero_init, side_mod):
    if side_mod:
        *refs, mo_ref = refs
        cond_ref, wm_ref, bm_ref = refs[4:7]
        refs = refs[:4] + refs[7:]
        _mod_kernel(cond_ref, wm_ref, bm_ref, mo_ref)
    if zero_init:
        z_ref, zif_ref, bif_ref, nw_ref, o_ref, c_out, n_out, m_out = refs
    else:
        z_ref, zif_ref, bif_ref, nw_ref, c0_ref, n0_ref, m0_ref, o_ref, c_out, n_out, m_out = refs
    L = ML_CHUNK
    nc = t // L
    chunk = lambda a, c: a[c * L:(c + 1) * L]

    gates = zif_ref[...] + bif_ref[...]
    logsig = (jnp.minimum(gates, 0.0) - jnp.log1p(jnp.exp(-jnp.abs(gates)))) * LOG2E
    gates = gates * LOG2E
    gates_t = gates.T
    r_i = lax.broadcasted_iota(jnp.int32, (L, L), 0)
    c_i = lax.broadcasted_iota(jnp.int32, (L, L), 1)
    lower, upper = r_i >= c_i, r_i <= c_i
    cum_tris = (lower.astype(BF16), upper.astype(BF16))
    dmasks = (upper, lower)
    cums = [[_split3_dot(cum_tris[d], chunk(logsig, c)) for c in range(nc)] for d in range(2)]
    cums_t = [[x.T for x in row] for row in cums]
    ext_rows = (lax.broadcasted_iota(jnp.int32, (ML_EXT - ML_DV, t), 0) == 0).astype(F32)
    lchunk = lambda a, c: a[:, c * L:(c + 1) * L]

    def running_max(a, d):
        rows = a.shape[0]
        if rows < 8:
            a = jnp.concatenate([a] * (8 // rows), axis=0)
        lane = lax.broadcasted_iota(jnp.int32, a.shape, 1)
        k = 1
        while k < L:
            if d:
                shifted = jnp.where(lane < L - k, pltpu.roll(a, L - k, axis=1), -jnp.inf)
            else:
                shifted = jnp.where(lane >= k, pltpu.roll(a, k, axis=1), -jnp.inf)
            a = jnp.maximum(a, shifted)
            k *= 2
        return a[:rows]

    hsl = lambda d, which: slice((2 * d + which) * ML_HEADS, (2 * d + which + 1) * ML_HEADS)
    a_maxes = [running_max(jnp.concatenate([lchunk(gates_t, c)[hsl(d, 0)] - cums_t[d][c][hsl(d, 1)]
                                            for c in range(nc)], axis=0), d) for d in range(2)]

    def local_part(qk_c, kb_c, vt_c, vtb_c, b_row, b_col, i_row, i_col, a_max, d):
        m_loc = b_row + a_max
        s = (qk_c * jnp.exp2(jnp.where(dmasks[d], (i_col - b_col) - a_max, -jnp.inf))).astype(BF16)
        bm = jnp.dot(vtb_c, s, preferred_element_type=F32)
        b_end = b_row[:, 0:1] if d else b_row[:, L - 1:L]
        g = b_end - b_row + i_row
        g_max = jnp.max(g, axis=1, keepdims=True)
        vw = (vt_c * jnp.exp2(g - g_max)).astype(BF16)
        u = jnp.dot(vw, kb_c, preferred_element_type=F32)
        return m_loc, bm, b_end, g_max, u

    def scan_step(qt_c, b_row, loc, c_t, m):
        m_loc, bm, b_end, g_max, u = loc
        inter = b_row + m
        m_row = jnp.maximum(inter, m_loc)
        nd = jnp.exp2(m_loc - m_row) * bm
        if c_t is not None:
            nd = nd + jnp.exp2(inter - m_row) * jnp.dot(c_t.astype(BF16), qt_c, preferred_element_type=F32)
        h_t = nd[:ML_DV] / jnp.maximum(jnp.abs(nd[ML_DV:ML_DV + 1]), jnp.exp2(-m_row))
        m_new = jnp.maximum(b_end + m, g_max)
        c_new = jnp.exp2(g_max - m_new) * u
        if c_t is not None:
            c_new = c_new + jnp.exp2(b_end + m - m_new) * c_t
        return h_t, c_new, m_new

    chains = [(hd, d) for hd in range(ML_HEADS) for d in range(2)]
    qts, b_rows, locs, state = {}, {}, {}, {}
    for hd in range(ML_HEADS):
        cols = slice(hd * ML_DK, (hd + 1) * ML_DK)
        kcols = slice(ML_W + hd * ML_DK, ML_W + (hd + 1) * ML_DK)
        vcols = slice(2 * ML_W + hd * ML_DV, 2 * ML_W + (hd + 1) * ML_DV)
        qt = z_ref[:, cols].T.astype(BF16)
        kb = z_ref[:, kcols].astype(BF16)
        vt = jnp.concatenate([z_ref[:, vcols].T, ext_rows], axis=0)
        vtb = vt.astype(BF16)
        qk = [jnp.dot(chunk(kb, c), lchunk(qt, c), preferred_element_type=F32) for c in range(nc)]
        qts[hd] = qt
        for d in range(2):
            i_lane = 2 * d * ML_HEADS + hd
            f_lane = (2 * d + 1) * ML_HEADS + hd
            b_rows[hd, d] = [cums_t[d][c][f_lane:f_lane + 1, :] for c in range(nc)]
            locs[hd, d] = [local_part(qk[c], chunk(kb, c), lchunk(vt, c), lchunk(vtb, c), b_rows[hd, d][c],
                                      cums[d][c][:, f_lane:f_lane + 1], lchunk(gates_t, c)[i_lane:i_lane + 1, :],
                                      chunk(gates, c)[:, i_lane:i_lane + 1],
                                      a_maxes[d][c * ML_HEADS + hd:c * ML_HEADS + hd + 1], d) for c in range(nc)]
            if zero_init:
                state[hd, d] = (None, jnp.zeros((1, 1), F32))
            else:
                c_t = jnp.concatenate([c0_ref[d, hd].T, n0_ref[d, hd:hd + 1, :],
                                       jnp.zeros((ML_EXT - ML_DV - 1, ML_DK), F32)], axis=0)
                state[hd, d] = (c_t, m0_ref[d:d + 1, hd:hd + 1] * LOG2E)

    h_parts = {ch: [None] * nc for ch in chains}
    for step in range(nc):
        for hd, d in chains:
            c = nc - 1 - step if d else step
            c_t, m = state[hd, d]
            h_parts[hd, d][c], c_t, m = scan_step(lchunk(qts[hd], c), b_rows[hd, d][c], locs[hd, d][c], c_t, m)
            state[hd, d] = (c_t, m)

    for hd, d in chains:
        c_t, m = state[hd, d]
        c_out[d, hd] = c_t[:ML_DV].T
        n_out[d, hd:hd + 1, :] = c_t[ML_DV:ML_DV + 1]
        m_out[d:d + 1, hd:hd + 1] = m * LN2
    for hd in range(ML_HEADS):
        cols = slice(hd * ML_DK, (hd + 1) * ML_DK)
        ocols = slice(3 * ML_W + hd * ML_DV, 3 * ML_W + (hd + 1) * ML_DV)
        h_fwd, h_bwd = (jnp.concatenate(h_parts[hd, d], axis=1) if nc > 1 else h_parts[hd, d][0] for d in range(2))
        h_sum = h_fwd + h_bwd
        hm = _rms(h_sum.T, nw_ref[:, cols])
        o_ref[:, cols] = (hm * z_ref[:, ocols]).astype(BF16)


def _mlstm(z, b_if_l, ml_norm_l, *, b, t, state=None, side=None):
    zero_init = state is None
    zw = 5 * ML_W
    in_specs = [
        pl.BlockSpec((t, zw), lambda bi: (bi, OFF_MQ // zw)),
        pl.BlockSpec((t, LANES), lambda bi: (bi, OFF_IF // LANES)),
        pl.BlockSpec((1, LANES), lambda bi: (0, 0)),
        pl.BlockSpec((1, ML_W), lambda bi: (0, 0)),
    ]
    bif = jnp.pad(b_if_l, (0, LANES - N_GATES)).reshape(1, LANES)
    args = [z, z, bif, ml_norm_l.reshape(1, ML_W)]
    out_specs = [
        pl.BlockSpec((t, ML_W), lambda bi: (bi, 0)),
        pl.BlockSpec((None, 2, ML_HEADS, ML_DK, ML_DV), lambda bi: (bi, 0, 0, 0, 0)),
        pl.BlockSpec((None, 2, ML_HEADS, ML_DK), lambda bi: (bi, 0, 0, 0)),
        pl.BlockSpec((None, 2, ML_HEADS), lambda bi: (bi, 0, 0)),
    ]
    out_shape = [
        jax.ShapeDtypeStruct((b * t, ML_W), BF16),
        jax.ShapeDtypeStruct((b, 2, ML_HEADS, ML_DK, ML_DV), F32),
        jax.ShapeDtypeStruct((b, 2, ML_HEADS, ML_DK), F32),
        jax.ShapeDtypeStruct((b, 2, ML_HEADS), F32),
    ]
    if side is not None:
        cond8, w_mod, b_mod, side_layer = side
        n = 3 * D_MODEL
        tn = n // b
        assert tn % LANES == 0 and tn * b == n
        in_specs += [
            pl.BlockSpec((8, D_MODEL), lambda bi: (0, 0)),
            pl.BlockSpec((None, D_MODEL, tn), lambda bi: (side_layer, 0, bi)),
            pl.BlockSpec((None, 1, tn), lambda bi: (side_layer, 0, bi)),
        ]
        args += [cond8, w_mod, b_mod.reshape(DEPTH, 1, n)]
        out_specs.append(pl.BlockSpec((8, tn), lambda bi: (0, bi)))
        out_shape.append(jax.ShapeDtypeStruct((8, n), F32))
    if not zero_init:
        c0, n0, m0, layer = state
        in_specs += [
            pl.BlockSpec((None, None, 2, ML_HEADS, ML_DK, ML_DV), lambda bi: (bi, layer, 0, 0, 0, 0)),
            pl.BlockSpec((None, None, 2, ML_HEADS, ML_DK), lambda bi: (bi, layer, 0, 0, 0)),
            pl.BlockSpec((None, None, 2, ML_HEADS), lambda bi: (bi, layer, 0, 0)),
        ]
        args += [c0, n0, m0]
    return pl.pallas_call(
        functools.partial(_mlstm_kernel, t=t, zero_init=zero_init, side_mod=side is not None),
        grid=(b,),
        in_specs=in_specs,
        out_specs=out_specs,
        out_shape=out_shape,
        compiler_params=_cparams("parallel"),
        name="mlstm",
    )(*args)


def _dft_tables(t):
    def cs(n):
        idx = np.arange(n, dtype=np.int64)
        ang = 2.0 * np.pi * ((idx[:, None] * idx[None, :]) % n).astype(np.float64) / n
        return np.cos(ang), np.sin(ang)
    cc, sc = cs(FO_GC)
    ct, st = cs(t)
    return (np.concatenate([cc, sc], axis=1).astype(np.float32),
            np.concatenate([ct, -st], axis=1).astype(np.float32))


FO_RT = 256


def _fourier_kernel(zx_ref, zga_ref, zgb_ref, wc_ref, wt_ref, wf_ref, o_ref, y_ref, *, t, bb):
    wc = wc_ref[...]
    for bi in range(bb):
        x = zx_ref[bi * t:(bi + 1) * t, FO_SHIFT:FO_SHIFT + FO_W]
        for g in range(FO_GROUPS):
            cols = slice(g * FO_GC, (g + 1) * FO_GC)
            ycols = slice(bi * FO_W + g * FO_GC, bi * FO_W + (g + 1) * FO_GC)
            y = jnp.dot(x[:, cols].astype(BF16), wc, preferred_element_type=F32)
            y_ref[0:t, ycols] = y[:, :FO_GC].astype(BF16)
            y_ref[t:2 * t, ycols] = y[:, FO_GC:].astype(BF16)
    scale = (t * FO_GC) ** -0.5
    for r in range(t // FO_RT):
        rows = slice(r * FO_RT, (r + 1) * FO_RT)
        f = jnp.dot(wt_ref[rows, :], y_ref[...], preferred_element_type=F32) * scale
        for bi in range(bb):
            orows = slice(bi * t + r * FO_RT, bi * t + (r + 1) * FO_RT)
            fg = jnp.concatenate([zga_ref[orows, FO_SHIFT:], zgb_ref[orows, :FO_SHIFT]], axis=1)
            for g in range(FO_GROUPS):
                cols = slice(g * FO_GC, (g + 1) * FO_GC)
                fcols = slice(bi * FO_W + g * FO_GC, bi * FO_W + (g + 1) * FO_GC)
                og = jnp.dot(f[:, fcols].astype(BF16), wf_ref[g].astype(BF16), preferred_element_type=F32)
                o_ref[orows, cols] = (og * _silu(fg[:, cols])).astype(BF16)


def _fourier(z, z_tail, w_fno_l, *, b, t):
    assert OFF_FG - FO_SHIFT + FO_W == Z_W and Z_TAIL == FO_SHIFT
    wc, wt = _dft_tables(t)
    xw = FO_W + LANES
    bb = max(1, min(b, ROW_TILE // t))
    return pl.pallas_call(
        functools.partial(_fourier_kernel, t=t, bb=bb),
        grid=(b // bb,),
        in_specs=[
            pl.BlockSpec((bb * t, xw), lambda bi: (bi, (OFF_FX - FO_SHIFT) // xw)),
            pl.BlockSpec((bb * t, FO_W), lambda bi: (bi, (OFF_FG - FO_SHIFT) // FO_W)),
            pl.BlockSpec((bb * t, LANES), lambda bi: (bi, 0)),
            pl.BlockSpec((FO_GC, 2 * FO_GC), lambda bi: (0, 0)),
            pl.BlockSpec((t, 2 * t), lambda bi: (0, 0)),
            pl.BlockSpec((FO_GROUPS, FO_GC, FO_GC), lambda bi: (0, 0, 0)),
        ],
        out_specs=pl.BlockSpec((bb * t, FO_W), lambda bi: (bi, 0)),
        out_shape=jax.ShapeDtypeStruct((b * t, FO_W), BF16),
        scratch_shapes=[pltpu.VMEM((2 * t, bb * FO_W), BF16)],
        compiler_params=_cparams("parallel"),
        name="fourier",
    )(z, z, z_tail, jnp.asarray(wc).astype(BF16), jnp.asarray(wt).astype(BF16), w_fno_l)


OUT_TN = 1024
OUT_TM = 1024


def _outproj_kernel(oa_ref, om_ref, of_ref, wa_ref, wm_ref, wf_ref, x_ref, gate_ref, y_ref, wb_ref, *, per_row_mod):
    i = pl.program_id(1)

    @pl.when(i == 0)
    def _():
        wb_ref[0:ATT_W, :] = wa_ref[...].astype(BF16)
        wb_ref[ATT_W:ATT_W + ML_W, :] = wm_ref[...].astype(BF16)
        wb_ref[ATT_W + ML_W:D_MIX, :] = wf_ref[...].astype(BF16)

    row = 1 + (i * OUT_TM) // ROW_TILE if per_row_mod else 0
    gate = gate_ref[pl.ds(row, 1), :]
    y = (jnp.dot(oa_ref[...], wb_ref[0:ATT_W, :], preferred_element_type=F32)
         + jnp.dot(om_ref[...], wb_ref[ATT_W:ATT_W + ML_W, :], preferred_element_type=F32)
         + jnp.dot(of_ref[...], wb_ref[ATT_W + ML_W:D_MIX, :], preferred_element_type=F32))
    y_ref[...] = x_ref[...] + gate * y


def _outproj(o_att, o_ml, o_fo, w_out, layer, x2d, mod_l, *, per_row_mod):
    m = x2d.shape[0]
    return pl.pallas_call(
        functools.partial(_outproj_kernel, per_row_mod=per_row_mod),
        grid=(D_MODEL // OUT_TN, m // OUT_TM),
        in_specs=[
            pl.BlockSpec((OUT_TM, ATT_W), lambda n, i: (i, 0)),
            pl.BlockSpec((OUT_TM, ML_W), lambda n, i: (i, 0)),
            pl.BlockSpec((OUT_TM, FO_W), lambda n, i: (i, 0)),
            pl.BlockSpec((None, ATT_W, OUT_TN), lambda n, i: (layer, 0, n)),
            pl.BlockSpec((None, ML_W, OUT_TN), lambda n, i: (layer, ATT_W // ML_W, n)),
            pl.BlockSpec((None, FO_W, OUT_TN), lambda n, i: (layer, (ATT_W + ML_W) // FO_W, n)),
            pl.BlockSpec((OUT_TM, OUT_TN), lambda n, i: (i, n)),
            pl.BlockSpec((8, OUT_TN), lambda n, i: (0, 2 * D_MODEL // OUT_TN + n)),
        ],
        out_specs=pl.BlockSpec((OUT_TM, OUT_TN), lambda n, i: (i, n)),
        out_shape=jax.ShapeDtypeStruct((m, D_MODEL), F32),
        scratch_shapes=[pltpu.VMEM((D_MIX, OUT_TN), BF16)],
        compiler_params=_cparams("parallel", "arbitrary"),
        name="outproj",
    )(o_att, o_ml, o_fo, w_out, w_out, w_out, x2d, mod_l)


def _rope_tables(t):
    half = HEAD_DIM // 2
    inv_freq = ROPE_BASE ** (-jnp.arange(0, half, 2, dtype=F32) / half)
    n_rows = t // GRID_W
    rows = jnp.repeat(jnp.arange(n_rows, dtype=F32), GRID_W)
    cols = jnp.tile(jnp.arange(GRID_W, dtype=F32), n_rows)
    ar = rows[:, None] * inv_freq
    ac = cols[:, None] * inv_freq
    cos = jnp.concatenate([jnp.cos(ar), jnp.cos(ar), jnp.cos(ac), jnp.cos(ac)], axis=1)
    sin = jnp.concatenate([-jnp.sin(ar), jnp.sin(ar), -jnp.sin(ac), jnp.sin(ac)], axis=1)
    return cos, sin


def _layer(x2d, mod_l, lw, layer, *, b, t, latent, rope=None, cache=None, state=None, side=None):
    norm_w, w_in_t, b_if, conv_w, conv_b, q_norm, k_norm, ml_norm, w_fno, w_out = lw
    z, z_tail = _inproj(x2d, mod_l, norm_w, w_in_t, layer, conv_w, conv_b, per_row_mod=latent, t=t)
    o_att, *kv_new = _attention(z, q_norm, k_norm, b=b, t=t, rope=rope, cache=cache)
    o_ml, c_f, n_f, m_f, *side_mod = _mlstm(z, b_if, ml_norm, b=b, t=t, state=state, side=side)
    o_fo = _fourier(z, z_tail, w_fno, b=b, t=t)
    y = _outproj(o_att, o_ml, o_fo, w_out, layer, x2d, mod_l, per_row_mod=latent)
    return y, (*kv_new, c_f, n_f, m_f), side_mod


def kernel(x_prompt, x_sample, cache_k, cache_v, state_C, state_n, state_m, c, c_ctx, norm_w, w_mod, b_mod, w_in,
           b_if, conv_w, conv_b, q_norm, k_norm, ml_norm, w_fno, w_out):
    bp, tp, _ = x_prompt.shape
    bs, ts, _ = x_sample.shape
    assert tp == ML_CHUNK and ts == ROW_TILE and (bp * tp) % ROW_TILE == 0 and bs + 1 <= 8
    past = cache_k.shape[2]

    cond8 = jnp.concatenate([c_ctx[None, :], c, jnp.zeros((8 - 1 - bs, D_MODEL), F32)], axis=0)
    mod_l = _modulation(cond8, w_mod, b_mod, 0)

    rope = _rope_tables(ts)
    ck = cache_k.reshape(bs, DEPTH, past, ATT_KV_W)
    cv = cache_v.reshape(bs, DEPTH, past, ATT_KV_W)
    w_in_t = jnp.swapaxes(w_in, 1, 2)

    xp = x_prompt.reshape(bp * tp, D_MODEL)
    xs = x_sample.reshape(bs * ts, D_MODEL)
    new_k, new_v, new_c, new_n, new_m = [], [], [], [], []
    for l in range(DEPTH):
        lw = (norm_w[l], w_in_t, b_if[l], conv_w[l], conv_b[l], q_norm[l], k_norm[l], ml_norm[l], w_fno[l], w_out)
        side = (cond8, w_mod, b_mod, l + 1) if l + 1 < DEPTH else None
        xp, (k_l, v_l, c_l, n_l, m_l), mod_next = _layer(xp, mod_l, lw, l, b=bp, t=tp, latent=False, side=side)
        new_k.append(k_l)
        new_v.append(v_l)
        new_c.append(c_l)
        new_n.append(n_l)
        new_m.append(m_l)
        xs, _, _ = _layer(xs, mod_l, lw, l, b=bs, t=ts, latent=True, rope=rope,
                          cache=(ck, cv, l), state=(state_C, state_n, state_m, l))
        if mod_next:
            mod_l = mod_next[0]
    return (xp.reshape(bp, tp, D_MODEL), xs.reshape(bs, ts, D_MODEL), jnp.stack(new_k, axis=1),
            jnp.stack(new_v, axis=1), jnp.stack(new_c, axis=1), jnp.stack(new_n, axis=1), jnp.stack(new_m, axis=1))
```

```python
import functools

import numpy as np
import jax
import jax.numpy as jnp
from jax import lax
from jax.experimental import pallas as pl
from jax.experimental.pallas import tpu as pltpu

D_MODEL = 2048
DEPTH = 2
GRID_W = 64
HEAD_DIM = 128
ATT_HEADS = 8
ATT_KV_HEADS = 2
ATT_GROUPS = ATT_HEADS // ATT_KV_HEADS
ATT_W = ATT_HEADS * HEAD_DIM
ATT_KV_W = ATT_KV_HEADS * HEAD_DIM
ML_HEADS = 4
ML_DK = 128
ML_DV = 128
ML_W = ML_HEADS * ML_DV
FO_GROUPS = 4
FO_GC = 128
FO_W = FO_GROUPS * FO_GC
D_MIX = ATT_W + ML_W + FO_W
N_GATES = 4 * ML_HEADS
D_PROJ = 2 * ATT_W + 2 * ATT_KV_W + 5 * ML_W + N_GATES + 2 * FO_W
CONV_W = 3
ROPE_BASE = 10000.0
EPS = 1e-6

LANES = 128
MXU_N = 256
OFF_AQ = 0
OFF_AK = OFF_AQ + ATT_W
OFF_AV = OFF_AK + ATT_KV_W
OFF_AG = OFF_AV + ATT_KV_W
OFF_MQ = OFF_AG + ATT_W
OFF_MK = OFF_MQ + ML_W
OFF_MV = OFF_MK + ML_W
OFF_MO = OFF_MV + ML_W
OFF_MG = OFF_MO + ML_W
OFF_IF = OFF_MG + ML_W
OFF_FX = OFF_IF + N_GATES
OFF_FG = OFF_FX + FO_W
FO_SHIFT = OFF_FX % LANES

PROJ_TN = 1024
Z_W = (D_PROJ // PROJ_TN) * PROJ_TN
Z_TAIL = D_PROJ - Z_W
ROW_TILE = 1024
ML_CHUNK = 256
ML_EXT = ML_DV + 16
ATT_TQ = 512
ATT_SUB = 256
ATT_AHEAD = 1
VMEM_LIMIT = 56 * 1024 * 1024

BF16 = jnp.bfloat16
F32 = jnp.float32
NT_DIMS = (((1,), (1,)), ((), ()))


def _cparams(*sem):
    return pltpu.CompilerParams(dimension_semantics=sem, vmem_limit_bytes=VMEM_LIMIT)


def _silu(x):
    return x * (1.0 / (1.0 + jnp.exp(-x)))


def _sigmoid(x):
    return 1.0 / (1.0 + jnp.exp(-x))


def _rms(x, w):
    ms = jnp.mean(x * x, axis=-1, keepdims=True)
    return x * lax.rsqrt(ms + EPS) * w


MOD_TN = 768


def _mod_kernel(cond_ref, w_ref, b_ref, o_ref):
    a = _silu(cond_ref[...]).astype(BF16)
    o_ref[...] = jnp.dot(a, w_ref[...].astype(BF16), preferred_element_type=F32) + b_ref[...]


def _modulation(cond8, w_mod, b_mod, layer):
    n = 3 * D_MODEL
    return pl.pallas_call(
        _mod_kernel,
        grid=(n // MOD_TN,),
        in_specs=[
            pl.BlockSpec((8, D_MODEL), lambda j: (0, 0)),
            pl.BlockSpec((None, D_MODEL, MOD_TN), lambda j: (layer, 0, j)),
            pl.BlockSpec((None, 1, MOD_TN), lambda j: (layer, 0, j)),
        ],
        out_specs=pl.BlockSpec((8, MOD_TN), lambda j: (0, j)),
        out_shape=jax.ShapeDtypeStruct((8, n), F32),
        compiler_params=_cparams("parallel"),
        name="modulation",
    )(cond8, w_mod, b_mod.reshape(DEPTH, 1, n))


NORM_ROWS = 16


def _shift_rows(x, direction, period):
    n = x.shape[0]
    pos = lax.broadcasted_iota(jnp.int32, x.shape, 0) % period
    if direction > 0:
        return jnp.where(pos == 0, 0.0, pltpu.roll(x, 1, axis=0))
    return jnp.where(pos == period - 1, 0.0, pltpu.roll(x, n - 1, axis=0))


def _conv_silu(x, w, b, period):
    y = _shift_rows(x, 1, period) * w[0:1, :] + x * w[1:2, :] + _shift_rows(x, -1, period) * w[2:3, :] + b
    return _silu(y)


def _inproj_kernel(x_ref, mod_ref, nw_ref, w_ref, wt_ref, cw_ref, cb_ref, z_ref, zt_ref, h_ref, *, per_row_mod, t):
    i = pl.program_id(0)
    j = pl.program_id(1)

    @pl.when(j == 0)
    def _():
        row = 1 + i if per_row_mod else 0
        shift = mod_ref[pl.ds(row, 1), 0:D_MODEL]
        gain = nw_ref[...] * (1.0 + mod_ref[pl.ds(row, 1), D_MODEL:2 * D_MODEL])

        def body(r, carry):
            sl = pl.ds(pl.multiple_of(r * NORM_ROWS, NORM_ROWS), NORM_ROWS)
            x = x_ref[sl, :]
            inv = lax.rsqrt(jnp.mean(x * x, axis=-1, keepdims=True) + EPS)
            h_ref[sl, :] = (x * inv * gain + shift).astype(BF16)
            return carry

        lax.fori_loop(0, ROW_TILE // NORM_ROWS, body, 0, unroll=8)

    def project(cols=slice(0, PROJ_TN)):
        return lax.dot_general(h_ref[...], w_ref[cols, :].astype(BF16), NT_DIMS, preferred_element_type=F32)

    half = PROJ_TN // 2
    pieces = lambda start: [slice(c, c + MXU_N) for c in range(start, start + half, MXU_N)]
    j_q, j_k, j_o = OFF_MQ // PROJ_TN, OFF_MK // PROJ_TN, OFF_MO // PROJ_TN

    def conv_tile(start, raw_start, conv_off, scale):
        for cols, rcols in zip(pieces(start), pieces(raw_start)):
            cc = slice(conv_off + cols.start - start, conv_off + cols.stop - start)
            y = _conv_silu(project(cols), cw_ref[:, cc], cb_ref[:, cc], t)
            z_ref[:, cols] = y if scale is None else y * scale
            z_ref[:, rcols] = project(rcols)

    @pl.when(j == j_q)
    def _():
        conv_tile(half, 0, 0, None)

    @pl.when(j == j_k)
    def _():
        conv_tile(0, half, ML_W, ML_DK ** -0.5)

    @pl.when(j == j_o)
    def _():
        z_ref[...] = project()
        z_ref[:, 0:half] = _sigmoid(z_ref[:, 0:half]) * _silu(z_ref[:, half:PROJ_TN])

    @pl.when((j != j_q) & (j != j_k) & (j != j_o))
    def _():
        z_ref[...] = project()

    @pl.when(j == Z_W // PROJ_TN - 1)
    def _():
        zt = lax.dot_general(h_ref[...], wt_ref[...].astype(BF16), NT_DIMS, preferred_element_type=F32)
        col = lax.broadcasted_iota(jnp.int32, zt.shape, 1)
        zt_ref[...] = jnp.where(col < Z_TAIL, zt, 0.0)


def _inproj(x2d, mod_l, norm_w_l, w_in_t, layer, conv_w_l, conv_b_l, *, per_row_mod, t):
    m = x2d.shape[0]
    half = PROJ_TN // 2
    assert OFF_MQ % PROJ_TN == half and OFF_MK % PROJ_TN == 0 and OFF_MO % PROJ_TN == 0 and ML_W == half
    assert ROW_TILE % t == 0
    return pl.pallas_call(
        functools.partial(_inproj_kernel, per_row_mod=per_row_mod, t=t),
        grid=(m // ROW_TILE, Z_W // PROJ_TN),
        in_specs=[
            pl.BlockSpec((ROW_TILE, D_MODEL), lambda i, j: (i, 0)),
            pl.BlockSpec((8, 3 * D_MODEL), lambda i, j: (0, 0)),
            pl.BlockSpec((1, D_MODEL), lambda i, j: (0, 0)),
            pl.BlockSpec((None, PROJ_TN, D_MODEL), lambda i, j: (layer, j, 0)),
            pl.BlockSpec((None, LANES, D_MODEL), lambda i, j: (layer, Z_W // LANES, 0)),
            pl.BlockSpec((CONV_W, 2 * ML_W), lambda i, j: (0, 0)),
            pl.BlockSpec((1, 2 * ML_W), lambda i, j: (0, 0)),
        ],
        out_specs=[
            pl.BlockSpec((ROW_TILE, PROJ_TN), lambda i, j: (i, j)),
            pl.BlockSpec((ROW_TILE, LANES), lambda i, j: (i, 0)),
        ],
        out_shape=[jax.ShapeDtypeStruct((m, Z_W), F32), jax.ShapeDtypeStruct((m, LANES), F32)],
        scratch_shapes=[pltpu.VMEM((ROW_TILE, D_MODEL), BF16)],
        compiler_params=_cparams("parallel", "arbitrary"),
        name="inproj",
    )(x2d, mod_l, norm_w_l.reshape(1, D_MODEL), w_in_t, w_in_t, conv_w_l, conv_b_l.reshape(1, 2 * ML_W))


def _rope(x, cos, sin):
    lane = lax.broadcasted_iota(jnp.int32, x.shape, 1)
    quarter = HEAD_DIM // 4
    partner = jnp.where((lane % (2 * quarter)) < quarter,
                        pltpu.roll(x, HEAD_DIM - quarter, axis=1), pltpu.roll(x, quarter, axis=1))
    return x * cos + partner * sin


LOG2E = 1.4426950408889634
LN2 = 0.6931471805599453


def _attn_kernel(*refs, t, tq, bb, latent):
    hk = ATT_KV_HEADS
    zq_ref, zk_ref, zv_ref = refs[:3]
    zg_refs = refs[3:3 + hk]
    rest = refs[3 + hk:]
    if latent:
        qn_ref, kn_ref, cos_ref, sin_ref, ck_ref, cv_ref, o_ref, kall_ref, vall_ref = rest
        past = ck_ref.shape[0]
    else:
        qn_ref, kn_ref, o_ref, knew_ref, vnew_ref, kall_ref, vall_ref = rest
        past = 0
    qi = pl.program_id(1)

    @pl.when(qi == 0)
    def _():
        for bi in range(bb):
            seq = slice(bi * t, (bi + 1) * t)
            for j in range(hk):
                kc = slice(j * HEAD_DIM, (j + 1) * HEAD_DIM)
                slot = bi * hk + j
                k = _rms(zk_ref[seq, kc], kn_ref[...])
                v = zv_ref[seq, kc]
                if latent:
                    k = _rope(k, cos_ref[...], sin_ref[...])
                    kall_ref[slot, 0:past, :] = ck_ref[:, kc].astype(BF16)
                    vall_ref[slot, 0:past, :] = cv_ref[:, kc].astype(BF16)
                else:
                    knew_ref[bi, :, j, :] = k
                    vnew_ref[bi, :, j, :] = v
                kall_ref[slot, past:past + t, :] = k.astype(BF16)
                vall_ref[slot, past:past + t, :] = v.astype(BF16)

    q_scale = (HEAD_DIM ** -0.5) * LOG2E
    sub = min(t, ATT_SUB)

    def scores(r, hd):
        rows = slice(r * sub, (r + 1) * sub)
        cols = slice(hd * HEAD_DIM, (hd + 1) * HEAD_DIM)
        q = _rms(zq_ref[rows, cols], qn_ref[...])
        if latent:
            pos = pl.ds(pl.multiple_of(qi * tq + r * sub, sub), sub)
            q = _rope(q, cos_ref[pos, :], sin_ref[pos, :])
        qb = (q * q_scale).astype(BF16)
        slot = (r * sub // t) * hk + hd // ATT_GROUPS
        return lax.dot_general(qb, kall_ref[slot], NT_DIMS, preferred_element_type=F32)

    def finish(r, hd, s):
        j, g = divmod(hd, ATT_GROUPS)
        rows = slice(r * sub, (r + 1) * sub)
        cols = slice(hd * HEAD_DIM, (hd + 1) * HEAD_DIM)
        gcols = slice(g * HEAD_DIM, (g + 1) * HEAD_DIM)
        p = jnp.exp2(s - jnp.max(s, axis=-1, keepdims=True))
        den = jnp.sum(p, axis=-1, keepdims=True)
        o = jnp.dot(p.astype(BF16), vall_ref[(r * sub // t) * hk + j], preferred_element_type=F32) * (1.0 / den)
        o_ref[rows, cols] = (o * _silu(zg_refs[j][rows, gcols])).astype(BF16)

    items = [(r, hd) for r in range(tq // sub) for hd in range(hk * ATT_GROUPS)]
    pending = [scores(*it) for it in items[:ATT_AHEAD]]
    for n, it in enumerate(items):
        if n + ATT_AHEAD < len(items):
            pending.append(scores(*items[n + ATT_AHEAD]))
        finish(*it, pending.pop(0))


def _attention(z, q_norm_l, k_norm_l, *, b, t, rope=None, cache=None):
    latent = rope is not None
    hk = ATT_KV_HEADS
    bb = max(1, ATT_TQ // t)
    tq = min(t, ATT_TQ) * bb
    nq = bb * t // tq
    assert b % bb == 0 and (bb == 1 or nq == 1)
    gw = ATT_GROUPS * HEAD_DIM
    in_specs = [
        pl.BlockSpec((tq, hk * gw), lambda bi, qi: (bi * nq + qi, OFF_AQ // (hk * gw))),
        pl.BlockSpec((bb * t, hk * HEAD_DIM), lambda bi, qi: (bi, OFF_AK // (hk * HEAD_DIM))),
        pl.BlockSpec((bb * t, hk * HEAD_DIM), lambda bi, qi: (bi, OFF_AV // (hk * HEAD_DIM))),
    ]
    in_specs += [pl.BlockSpec((tq, gw), lambda bi, qi, j=j: (bi * nq + qi, OFF_AG // gw + j)) for j in range(hk)]
    in_specs += [pl.BlockSpec((1, HEAD_DIM), lambda bi, qi: (0, 0))] * 2
    args = [z] * (3 + hk) + [q_norm_l.reshape(1, HEAD_DIM), k_norm_l.reshape(1, HEAD_DIM)]
    tk = t
    if latent:
        cos, sin = rope
        ck, cv, layer = cache
        past = ck.shape[2]
        tk = past + t
        in_specs += [
            pl.BlockSpec((t, HEAD_DIM), lambda bi, qi: (0, 0)),
            pl.BlockSpec((t, HEAD_DIM), lambda bi, qi: (0, 0)),
            pl.BlockSpec((None, None, past, hk * HEAD_DIM), lambda bi, qi: (bi, layer, 0, 0)),
            pl.BlockSpec((None, None, past, hk * HEAD_DIM), lambda bi, qi: (bi, layer, 0, 0)),
        ]
        args += [cos, sin, ck, cv]
    out_specs = [pl.BlockSpec((tq, hk * gw), lambda bi, qi: (bi * nq + qi, 0))]
    out_shape = [jax.ShapeDtypeStruct((b * t, ATT_W), BF16)]
    if not latent:
        kv_spec = pl.BlockSpec((bb, t, ATT_KV_HEADS, HEAD_DIM), lambda bi, qi: (bi, 0, 0, 0))
        out_specs += [kv_spec, kv_spec]
        out_shape += [jax.ShapeDtypeStruct((b, t, ATT_KV_HEADS, HEAD_DIM), F32)] * 2
    return pl.pallas_call(
        functools.partial(_attn_kernel, t=t, tq=tq, bb=bb, latent=latent),
        grid=(b // bb, nq),
        in_specs=in_specs,
        out_specs=out_specs,
        out_shape=out_shape,
        scratch_shapes=[pltpu.VMEM((bb * hk, tk, HEAD_DIM), BF16), pltpu.VMEM((bb * hk, tk, HEAD_DIM), BF16)],
        compiler_params=_cparams("parallel", "arbitrary"),
        name="attention",
    )(*args)


def _split3_dot(tri, x):
    hi = x.astype(BF16)
    r1 = x - hi.astype(F32)
    mid = r1.astype(BF16)
    lo = (r1 - mid.astype(F32)).astype(BF16)
    d = lambda a: jnp.dot(tri, a, preferred_element_type=F32)
    return d(hi) + d(mid) + d(lo)


def _mlstm_kernel(*refs, t, zero_init, side_mod):
    if side_mod:
        *refs, mo_ref = refs
        cond_ref, wm_ref, bm_ref = refs[4:7]
        refs = refs[:4] + refs[7:]
        _mod_kernel(cond_ref, wm_ref, bm_ref, mo_ref)
    if zero_init:
        z_ref, zif_ref, bif_ref, nw_ref, o_ref, c_out, n_out, m_out = refs
    else:
        z_ref, zif_ref, bif_ref, nw_ref, c0_ref, n0_ref, m0_ref, o_ref, c_out, n_out, m_out = refs
    L = ML_CHUNK
    nc = t // L
    chunk = lambda a, c: a[c * L:(c + 1) * L]

    gates = zif_ref[...] + bif_ref[...]
    logsig = (jnp.minimum(gates, 0.0) - jnp.log1p(jnp.exp(-jnp.abs(gates)))) * LOG2E
    gates = gates * LOG2E
    gates_t = gates.T
    r_i = lax.broadcasted_iota(jnp.int32, (L, L), 0)
    c_i = lax.broadcasted_iota(jnp.int32, (L, L), 1)
    lower, upper = r_i >= c_i, r_i <= c_i
    cum_tris = (lower.astype(BF16), upper.astype(BF16))
    dmasks = (upper, lower)
    cums = [[_split3_dot(cum_tris[d], chunk(logsig, c)) for c in range(nc)] for d in range(2)]
    cums_t = [[x.T for x in row] for row in cums]
    ext_rows = (lax.broadcasted_iota(jnp.int32, (ML_EXT - ML_DV, t), 0) == 0).astype(F32)
    lchunk = lambda a, c: a[:, c * L:(c + 1) * L]

    def running_max(a, d):
        rows = a.shape[0]
        if rows < 8:
            a = jnp.concatenate([a] * (8 // rows), axis=0)
        lane = lax.broadcasted_iota(jnp.int32, a.shape, 1)
        k = 1
        while k < L:
            if d:
                shifted = jnp.where(lane < L - k, pltpu.roll(a, L - k, axis=1), -jnp.inf)
            else:
                shifted = jnp.where(lane >= k, pltpu.roll(a, k, axis=1), -jnp.inf)
            a = jnp.maximum(a, shifted)
            k *= 2
        return a[:rows]

    hsl = lambda d, which: slice((2 * d + which) * ML_HEADS, (2 * d + which + 1) * ML_HEADS)
    a_maxes = [running_max(jnp.concatenate([lchunk(gates_t, c)[hsl(d, 0)] - cums_t[d][c][hsl(d, 1)]
                                            for c in range(nc)], axis=0), d) for d in range(2)]

    def local_part(qk_c, kb_c, vt_c, vtb_c, b_row, b_col, i_row, i_col, a_max, d):
        m_loc = b_row + a_max
        s = (qk_c * jnp.exp2(jnp.where(dmasks[d], (i_col - b_col) - a_max, -jnp.inf))).astype(BF16)
        bm = jnp.dot(vtb_c, s, preferred_element_type=F32)
        b_end = b_row[:, 0:1] if d else b_row[:, L - 1:L]
        g = b_end - b_row + i_row
        g_max = jnp.max(g, axis=1, keepdims=True)
        vw = (vt_c * jnp.exp2(g - g_max)).astype(BF16)
        u = jnp.dot(vw, kb_c, preferred_element_type=F32)
        return m_loc, bm, b_end, g_max, u

    def scan_step(qt_c, b_row, loc, c_t, m):
        m_loc, bm, b_end, g_max, u = loc
        inter = b_row + m
        m_row = jnp.maximum(inter, m_loc)
        nd = jnp.exp2(m_loc - m_row) * bm
        if c_t is not None:
            nd = nd + jnp.exp2(inter - m_row) * jnp.dot(c_t.astype(BF16), qt_c, preferred_element_type=F32)
        h_t = nd[:ML_DV] / jnp.maximum(jnp.abs(nd[ML_DV:ML_DV + 1]), jnp.exp2(-m_row))
        m_new = jnp.maximum(b_end + m, g_max)
        c_new = jnp.exp2(g_max - m_new) * u
        if c_t is not None:
            c_new = c_new + jnp.exp2(b_end + m - m_new) * c_t
        return h_t, c_new, m_new

    chains = [(hd, d) for hd in range(ML_HEADS) for d in range(2)]
    qts, b_rows, locs, state = {}, {}, {}, {}
    for hd in range(ML_HEADS):
        cols = slice(hd * ML_DK, (hd + 1) * ML_DK)
        kcols = slice(ML_W + hd * ML_DK, ML_W + (hd + 1) * ML_DK)
        vcols = slice(2 * ML_W + hd * ML_DV, 2 * ML_W + (hd + 1) * ML_DV)
        qt = z_ref[:, cols].T.astype(BF16)
        kb = z_ref[:, kcols].astype(BF16)
        vt = jnp.concatenate([z_ref[:, vcols].T, ext_rows], axis=0)
        vtb = vt.astype(BF16)
        qk = [jnp.dot(chunk(kb, c), lchunk(qt, c), preferred_element_type=F32) for c in range(nc)]
        qts[hd] = qt
        for d in range(2):
            i_lane = 2 * d * ML_HEADS + hd
            f_lane = (2 * d + 1) * ML_HEADS + hd
            b_rows[hd, d] = [cums_t[d][c][f_lane:f_lane + 1, :] for c in range(nc)]
            locs[hd, d] = [local_part(qk[c], chunk(kb, c), lchunk(vt, c), lchunk(vtb, c), b_rows[hd, d][c],
                                      cums[d][c][:, f_lane:f_lane + 1], lchunk(gates_t, c)[i_lane:i_lane + 1, :],
                                      chunk(gates, c)[:, i_lane:i_lane + 1],
                                      a_maxes[d][c * ML_HEADS + hd:c * ML_HEADS + hd + 1], d) for c in range(nc)]
            if zero_init:
                state[hd, d] = (None, jnp.zeros((1, 1), F32))
            else:
                c_t = jnp.concatenate([c0_ref[d, hd].T, n0_ref[d, hd:hd + 1, :],
                                       jnp.zeros((ML_EXT - ML_DV - 1, ML_DK), F32)], axis=0)
                state[hd, d] = (c_t, m0_ref[d:d + 1, hd:hd + 1] * LOG2E)

    h_parts = {ch: [None] * nc for ch in chains}
    for step in range(nc):
        for hd, d in chains:
            c = nc - 1 - step if d else step
            c_t, m = state[hd, d]
            h_parts[hd, d][c], c_t, m = scan_step(lchunk(qts[hd], c), b_rows[hd, d][c], locs[hd, d][c], c_t, m)
            state[hd, d] = (c_t, m)

    for hd, d in chains:
        c_t, m = state[hd, d]
        c_out[d, hd] = c_t[:ML_DV].T
        n_out[d, hd:hd + 1, :] = c_t[ML_DV:ML_DV + 1]
        m_out[d:d + 1, hd:hd + 1] = m * LN2
    for hd in range(ML_HEADS):
        cols = slice(hd * ML_DK, (hd + 1) * ML_DK)
        ocols = slice(3 * ML_W + hd * ML_DV, 3 * ML_W + (hd + 1) * ML_DV)
        h_fwd, h_bwd = (jnp.concatenate(h_parts[hd, d], axis=1) if nc > 1 else h_parts[hd, d][0] for d in range(2))
        h_sum = h_fwd + h_bwd
        hm = _rms(h_sum.T, nw_ref[:, cols])
        o_ref[:, cols] = (hm * z_ref[:, ocols]).astype(BF16)


def _mlstm(z, b_if_l, ml_norm_l, *, b, t, state=None, side=None):
    zero_init = state is None
    zw = 5 * ML_W
    in_specs = [
        pl.BlockSpec((t, zw), lambda bi: (bi, OFF_MQ // zw)),
        pl.BlockSpec((t, LANES), lambda bi: (bi, OFF_IF // LANES)),
        pl.BlockSpec((1, LANES), lambda bi: (0, 0)),
        pl.BlockSpec((1, ML_W), lambda bi: (0, 0)),
    ]
    bif = jnp.pad(b_if_l, (0, LANES - N_GATES)).reshape(1, LANES)
    args = [z, z, bif, ml_norm_l.reshape(1, ML_W)]
    out_specs = [
        pl.BlockSpec((t, ML_W), lambda bi: (bi, 0)),
        pl.BlockSpec((None, 2, ML_HEADS, ML_DK, ML_DV), lambda bi: (bi, 0, 0, 0, 0)),
        pl.BlockSpec((None, 2, ML_HEADS, ML_DK), lambda bi: (bi, 0, 0, 0)),
        pl.BlockSpec((None, 2, ML_HEADS), lambda bi: (bi, 0, 0)),
    ]
    out_shape = [
        jax.ShapeDtypeStruct((b * t, ML_W), BF16),
        jax.ShapeDtypeStruct((b, 2, ML_HEADS, ML_DK, ML_DV), F32),
        jax.ShapeDtypeStruct((b, 2, ML_HEADS, ML_DK), F32),
        jax.ShapeDtypeStruct((b, 2, ML_HEADS), F32),
    ]
    if side is not None:
        cond8, w_mod, b_mod, side_layer = side
        n = 3 * D_MODEL
        tn = n // b
        assert tn % LANES == 0 and tn * b == n
        in_specs += [
            pl.BlockSpec((8, D_MODEL), lambda bi: (0, 0)),
            pl.BlockSpec((None, D_MODEL, tn), lambda bi: (side_layer, 0, bi)),
            pl.BlockSpec((None, 1, tn), lambda bi: (side_layer, 0, bi)),
        ]
        args += [cond8, w_mod, b_mod.reshape(DEPTH, 1, n)]
        out_specs.append(pl.BlockSpec((8, tn), lambda bi: (0, bi)))
        out_shape.append(jax.ShapeDtypeStruct((8, n), F32))
    if not zero_init:
        c0, n0, m0, layer = state
        in_specs += [
            pl.BlockSpec((None, None, 2, ML_HEADS, ML_DK, ML_DV), lambda bi: (bi, layer, 0, 0, 0, 0)),
            pl.BlockSpec((None, None, 2, ML_HEADS, ML_DK), lambda bi: (bi, layer, 0, 0, 0)),
            pl.BlockSpec((None, None, 2, ML_HEADS), lambda bi: (bi, layer, 0, 0)),
        ]
        args += [c0, n0, m0]
    return pl.pallas_call(
        functools.partial(_mlstm_kernel, t=t, zero_init=zero_init, side_mod=side is not None),
        grid=(b,),
        in_specs=in_specs,
        out_specs=out_specs,
        out_shape=out_shape,
        compiler_params=_cparams("parallel"),
        name="mlstm",
    )(*args)


def _dft_tables(t):
    def cs(n):
        idx = np.arange(n, dtype=np.int64)
        ang = 2.0 * np.pi * ((idx[:, None] * idx[None, :]) % n).astype(np.float64) / n
        return np.cos(ang), np.sin(ang)
    cc, sc = cs(FO_GC)
    ct, st = cs(t)
    return (np.concatenate([cc, sc], axis=1).astype(np.float32),
            np.concatenate([ct, -st], axis=1).astype(np.float32))


FO_RT = 256


def _fourier_kernel(zx_ref, zga_ref, zgb_ref, wc_ref, wt_ref, wf_ref, o_ref, y_ref, *, t, bb):
    wc = wc_ref[...]
    for bi in range(bb):
        x = zx_ref[bi * t:(bi + 1) * t, FO_SHIFT:FO_SHIFT + FO_W]
        for g in range(FO_GROUPS):
            cols = slice(g * FO_GC, (g + 1) * FO_GC)
            ycols = slice(bi * FO_W + g * FO_GC, bi * FO_W + (g + 1) * FO_GC)
            y = jnp.dot(x[:, cols].astype(BF16), wc, preferred_element_type=F32)
            y_ref[0:t, ycols] = y[:, :FO_GC].astype(BF16)
            y_ref[t:2 * t, ycols] = y[:, FO_GC:].astype(BF16)
    scale = (t * FO_GC) ** -0.5
    for r in range(t // FO_RT):
        rows = slice(r * FO_RT, (r + 1) * FO_RT)
        f = jnp.dot(wt_ref[rows, :], y_ref[...], preferred_element_type=F32) * scale
        for bi in range(bb):
            orows = slice(bi * t + r * FO_RT, bi * t + (r + 1) * FO_RT)
            fg = jnp.concatenate([zga_ref[orows, FO_SHIFT:], zgb_ref[orows, :FO_SHIFT]], axis=1)
            for g in range(FO_GROUPS):
                cols = slice(g * FO_GC, (g + 1) * FO_GC)
                fcols = slice(bi * FO_W + g * FO_GC, bi * FO_W + (g + 1) * FO_GC)
                og = jnp.dot(f[:, fcols].astype(BF16), wf_ref[g].astype(BF16), preferred_element_type=F32)
                o_ref[orows, cols] = (og * _silu(fg[:, cols])).astype(BF16)


def _fourier(z, z_tail, w_fno_l, *, b, t):
    assert OFF_FG - FO_SHIFT + FO_W == Z_W and Z_TAIL == FO_SHIFT
    wc, wt = _dft_tables(t)
    xw = FO_W + LANES
    bb = max(1, min(b, ROW_TILE // t))
    return pl.pallas_call(
        functools.partial(_fourier_kernel, t=t, bb=bb),
        grid=(b // bb,),
        in_specs=[
            pl.BlockSpec((bb * t, xw), lambda bi: (bi, (OFF_FX - FO_SHIFT) // xw)),
            pl.BlockSpec((bb * t, FO_W), lambda bi: (bi, (OFF_FG - FO_SHIFT) // FO_W)),
            pl.BlockSpec((bb * t, LANES), lambda bi: (bi, 0)),
            pl.BlockSpec((FO_GC, 2 * FO_GC), lambda bi: (0, 0)),
            pl.BlockSpec((t, 2 * t), lambda bi: (0, 0)),
            pl.BlockSpec((FO_GROUPS, FO_GC, FO_GC), lambda bi: (0, 0, 0)),
        ],
        out_specs=pl.BlockSpec((bb * t, FO_W), lambda bi: (bi, 0)),
        out_shape=jax.ShapeDtypeStruct((b * t, FO_W), BF16),
        scratch_shapes=[pltpu.VMEM((2 * t, bb * FO_W), BF16)],
        compiler_params=_cparams("parallel"),
        name="fourier",
    )(z, z, z_tail, jnp.asarray(wc).astype(BF16), jnp.asarray(wt).astype(BF16), w_fno_l)


OUT_TN = 1024
OUT_TM = 1024


def _outproj_kernel(oa_ref, om_ref, of_ref, wa_ref, wm_ref, wf_ref, x_ref, gate_ref, y_ref, wb_ref, *, per_row_mod):
    i = pl.program_id(1)

    @pl.when(i == 0)
    def _():
        wb_ref[0:ATT_W, :] = wa_ref[...].astype(BF16)
        wb_ref[ATT_W:ATT_W + ML_W, :] = wm_ref[...].astype(BF16)
        wb_ref[ATT_W + ML_W:D_MIX, :] = wf_ref[...].astype(BF16)

    row = 1 + (i * OUT_TM) // ROW_TILE if per_row_mod else 0
    gate = gate_ref[pl.ds(row, 1), :]
    y = (jnp.dot(oa_ref[...], wb_ref[0:ATT_W, :], preferred_element_type=F32)
         + jnp.dot(om_ref[...], wb_ref[ATT_W:ATT_W + ML_W, :], preferred_element_type=F32)
         + jnp.dot(of_ref[...], wb_ref[ATT_W + ML_W:D_MIX, :], preferred_element_type=F32))
    y_ref[...] = x_ref[...] + gate * y


def _outproj(o_att, o_ml, o_fo, w_out, layer, x2d, mod_l, *, per_row_mod):
    m = x2d.shape[0]
    return pl.pallas_call(
        functools.partial(_outproj_kernel, per_row_mod=per_row_mod),
        grid=(D_MODEL // OUT_TN, m // OUT_TM),
        in_specs=[
            pl.BlockSpec((OUT_TM, ATT_W), lambda n, i: (i, 0)),
            pl.BlockSpec((OUT_TM, ML_W), lambda n, i: (i, 0)),
            pl.BlockSpec((OUT_TM, FO_W), lambda n, i: (i, 0)),
            pl.BlockSpec((None, ATT_W, OUT_TN), lambda n, i: (layer, 0, n)),
            pl.BlockSpec((None, ML_W, OUT_TN), lambda n, i: (layer, ATT_W // ML_W, n)),
            pl.BlockSpec((None, FO_W, OUT_TN), lambda n, i: (layer, (ATT_W + ML_W) // FO_W, n)),
            pl.BlockSpec((OUT_TM, OUT_TN), lambda n, i: (i, n)),
            pl.BlockSpec((8, OUT_TN), lambda n, i: (0, 2 * D_MODEL // OUT_TN + n)),
        ],
        out_specs=pl.BlockSpec((OUT_TM, OUT_TN), lambda n, i: (i, n)),
        out_shape=jax.ShapeDtypeStruct((m, D_MODEL), F32),
        scratch_shapes=[pltpu.VMEM((D_MIX, OUT_TN), BF16)],
        compiler_params=_cparams("parallel", "arbitrary"),
        name="outproj",
    )(o_att, o_ml, o_fo, w_out, w_out, w_out, x2d, mod_l)


def _rope_tables(t):
    half = HEAD_DIM // 2
    inv_freq = ROPE_BASE ** (-jnp.arange(0, half, 2, dtype=F32) / half)
    n_rows = t // GRID_W
    rows = jnp.repeat(jnp.arange(n_rows, dtype=F32), GRID_W)
    cols = jnp.tile(jnp.arange(GRID_W, dtype=F32), n_rows)
    ar = rows[:, None] * inv_freq
    ac = cols[:, None] * inv_freq
    cos = jnp.concatenate([jnp.cos(ar), jnp.cos(ar), jnp.cos(ac), jnp.cos(ac)], axis=1)
    sin = jnp.concatenate([-jnp.sin(ar), jnp.sin(ar), -jnp.sin(ac), jnp.sin(ac)], axis=1)
    return cos, sin


def _layer(x2d, mod_l, lw, layer, *, b, t, latent, rope=None, cache=None, state=None, side=None):
    norm_w, w_in_t, b_if, conv_w, conv_b, q_norm, k_norm, ml_norm, w_fno, w_out = lw
    z, z_tail = _inproj(x2d, mod_l, norm_w, w_in_t, layer, conv_w, conv_b, per_row_mod=latent, t=t)
    o_att, *kv_new = _attention(z, q_norm, k_norm, b=b, t=t, rope=rope, cache=cache)
    o_ml, c_f, n_f, m_f, *side_mod = _mlstm(z, b_if, ml_norm, b=b, t=t, state=state, side=side)
    o_fo = _fourier(z, z_tail, w_fno, b=b, t=t)
    y = _outproj(o_att, o_ml, o_fo, w_out, layer, x2d, mod_l, per_row_mod=latent)
    return y, (*kv_new, c_f, n_f, m_f), side_mod


def kernel(x_prompt, x_sample, cache_k, cache_v, state_C, state_n, state_m, c, c_ctx, norm_w, w_mod, b_mod, w_in,
           b_if, conv_w, conv_b, q_norm, k_norm, ml_norm, w_fno, w_out):
    bp, tp, _ = x_prompt.shape
    bs, ts, _ = x_sample.shape
    assert tp % ML_CHUNK == 0 and ts == ROW_TILE and (bp * tp) % ROW_TILE == 0 and bs + 1 <= 8
    past = cache_k.shape[2]

    cond8 = jnp.concatenate([c_ctx[None, :], c, jnp.zeros((8 - 1 - bs, D_MODEL), F32)], axis=0)
    mod_l = _modulation(cond8, w_mod, b_mod, 0)

    rope = _rope_tables(ts)
    ck = cache_k.reshape(bs, DEPTH, past, ATT_KV_W)
    cv = cache_v.reshape(bs, DEPTH, past, ATT_KV_W)
    w_in_t = jnp.swapaxes(w_in, 1, 2)

    xp = x_prompt.reshape(bp * tp, D_MODEL)
    xs = x_sample.reshape(bs * ts, D_MODEL)
    new_k, new_v, new_c, new_n, new_m = [], [], [], [], []
    for l in range(DEPTH):
        lw = (norm_w[l], w_in_t, b_if[l], conv_w[l], conv_b[l], q_norm[l], k_norm[l], ml_norm[l], w_fno[l], w_out)
        side = (cond8, w_mod, b_mod, l + 1) if l + 1 < DEPTH else None
        xp, (k_l, v_l, c_l, n_l, m_l), mod_next = _layer(xp, mod_l, lw, l, b=bp, t=tp, latent=False, side=side)
        new_k.append(k_l)
        new_v.append(v_l)
        new_c.append(c_l)
        new_n.append(n_l)
        new_m.append(m_l)
        xs, _, _ = _layer(xs, mod_l, lw, l, b=bs, t=ts, latent=True, rope=rope,
                          cache=(ck, cv, l), state=(state_C, state_n, state_m, l))
        if mod_next:
            mod_l = mod_next[0]
    return (xp.reshape(bp, tp, D_MODEL), xs.reshape(bs, ts, D_MODEL), jnp.stack(new_k, axis=1),
            jnp.stack(new_v, axis=1), jnp.stack(new_c, axis=1), jnp.stack(new_n, axis=1), jnp.stack(new_m, axis=1))
```

```python
import functools

import numpy as np
import jax
import jax.numpy as jnp
from jax import lax
from jax.experimental import pallas as pl
from jax.experimental.pallas import tpu as pltpu

D_MODEL = 2048
DEPTH = 2
GRID_W = 64
HEAD_DIM = 128
ATT_HEADS = 8
ATT_KV_HEADS = 2
ATT_GROUPS = ATT_HEADS // ATT_KV_HEADS
ATT_W = ATT_HEADS * HEAD_DIM
ATT_KV_W = ATT_KV_HEADS * HEAD_DIM
ML_HEADS = 4
ML_DK = 128
ML_DV = 128
ML_W = ML_HEADS * ML_DV
FO_GROUPS = 4
FO_GC = 128
FO_W = FO_GROUPS * FO_GC
D_MIX = ATT_W + ML_W + FO_W
N_GATES = 4 * ML_HEADS
D_PROJ = 2 * ATT_W + 2 * ATT_KV_W + 5 * ML_W + N_GATES + 2 * FO_W
CONV_W = 3
ROPE_BASE = 10000.0
EPS = 1e-6

LANES = 128
MXU_N = 256
OFF_AQ = 0
OFF_AK = OFF_AQ + ATT_W
OFF_AV = OFF_AK + ATT_KV_W
OFF_AG = OFF_AV + ATT_KV_W
OFF_MQ = OFF_AG + ATT_W
OFF_MK = OFF_MQ + ML_W
OFF_MV = OFF_MK + ML_W
OFF_MO = OFF_MV + ML_W
OFF_MG = OFF_MO + ML_W
OFF_IF = OFF_MG + ML_W
OFF_FX = OFF_IF + N_GATES
OFF_FG = OFF_FX + FO_W
FO_SHIFT = OFF_FX % LANES

PROJ_TN = 1024
Z_W = (D_PROJ // PROJ_TN) * PROJ_TN
Z_TAIL = D_PROJ - Z_W
ROW_TILE = 1024
ML_CHUNK = 256
ML_EXT = ML_DV + 16
ATT_TQ = 512
ATT_SUB = 256
ATT_AHEAD = 1
VMEM_LIMIT = 56 * 1024 * 1024

BF16 = jnp.bfloat16
F32 = jnp.float32
NT_DIMS = (((1,), (1,)), ((), ()))


def _cparams(*sem):
    return pltpu.CompilerParams(dimension_semantics=sem, vmem_limit_bytes=VMEM_LIMIT)


def _silu(x):
    return x * (1.0 / (1.0 + jnp.exp(-x)))


def _sigmoid(x):
    return 1.0 / (1.0 + jnp.exp(-x))


def _rms(x, w):
    ms = jnp.mean(x * x, axis=-1, keepdims=True)
    return x * lax.rsqrt(ms + EPS) * w


MOD_TN = 768


def _mod_kernel(cond_ref, w_ref, b_ref, o_ref):
    a = _silu(cond_ref[...]).astype(BF16)
    o_ref[...] = jnp.dot(a, w_ref[...].astype(BF16), preferred_element_type=F32) + b_ref[...]


def _modulation(cond8, w_mod, b_mod, layer):
    n = 3 * D_MODEL
    return pl.pallas_call(
        _mod_kernel,
        grid=(n // MOD_TN,),
        in_specs=[
            pl.BlockSpec((8, D_MODEL), lambda j: (0, 0)),
            pl.BlockSpec((None, D_MODEL, MOD_TN), lambda j: (layer, 0, j)),
            pl.BlockSpec((None, 1, MOD_TN), lambda j: (layer, 0, j)),
        ],
        out_specs=pl.BlockSpec((8, MOD_TN), lambda j: (0, j)),
        out_shape=jax.ShapeDtypeStruct((8, n), F32),
        compiler_params=_cparams("parallel"),
        name="modulation",
    )(cond8, w_mod, b_mod.reshape(DEPTH, 1, n))


NORM_ROWS = 16


def _shift_rows(x, direction, period):
    n = x.shape[0]
    pos = lax.broadcasted_iota(jnp.int32, x.shape, 0) % period
    if direction > 0:
        return jnp.where(pos == 0, 0.0, pltpu.roll(x, 1, axis=0))
    return jnp.where(pos == period - 1, 0.0, pltpu.roll(x, n - 1, axis=0))


def _conv_silu(x, w, b, period):
    y = _shift_rows(x, 1, period) * w[0:1, :] + x * w[1:2, :] + _shift_rows(x, -1, period) * w[2:3, :] + b
    return _silu(y)


def _inproj_kernel(x_ref, mod_ref, nw_ref, w_ref, wt_ref, cw_ref, cb_ref, z_ref, zt_ref, h_ref, *, per_row_mod, t):
    i = pl.program_id(0)
    j = pl.program_id(1)

    @pl.when(j == 0)
    def _():
        row = 1 + i if per_row_mod else 0
        shift = mod_ref[pl.ds(row, 1), 0:D_MODEL]
        gain = nw_ref[...] * (1.0 + mod_ref[pl.ds(row, 1), D_MODEL:2 * D_MODEL])

        def body(r, carry):
            sl = pl.ds(pl.multiple_of(r * NORM_ROWS, NORM_ROWS), NORM_ROWS)
            x = x_ref[sl, :]
            inv = lax.rsqrt(jnp.mean(x * x, axis=-1, keepdims=True) + EPS)
            h_ref[sl, :] = (x * inv * gain + shift).astype(BF16)
            return carry

        lax.fori_loop(0, ROW_TILE // NORM_ROWS, body, 0, unroll=8)

    def project(cols=slice(0, PROJ_TN)):
        return lax.dot_general(h_ref[...], w_ref[cols, :].astype(BF16), NT_DIMS, preferred_element_type=F32)

    half = PROJ_TN // 2
    pieces = lambda start: [slice(c, c + MXU_N) for c in range(start, start + half, MXU_N)]
    j_q, j_k, j_o = OFF_MQ // PROJ_TN, OFF_MK // PROJ_TN, OFF_MO // PROJ_TN

    def conv_tile(start, raw_start, conv_off, scale):
        for cols, rcols in zip(pieces(start), pieces(raw_start)):
            cc = slice(conv_off + cols.start - start, conv_off + cols.stop - start)
            y = _conv_silu(project(cols), cw_ref[:, cc], cb_ref[:, cc], t)
            z_ref[:, cols] = y if scale is None else y * scale
            z_ref[:, rcols] = project(rcols)

    @pl.when(j == j_q)
    def _():
        conv_tile(half, 0, 0, None)

    @pl.when(j == j_k)
    def _():
        conv_tile(0, half, ML_W, ML_DK ** -0.5)

    @pl.when(j == j_o)
    def _():
        z_ref[...] = project()
        z_ref[:, 0:half] = _sigmoid(z_ref[:, 0:half]) * _silu(z_ref[:, half:PROJ_TN])

    @pl.when((j != j_q) & (j != j_k) & (j != j_o))
    def _():
        z_ref[...] = project()

    @pl.when(j == Z_W // PROJ_TN - 1)
    def _():
        zt = lax.dot_general(h_ref[...], wt_ref[...].astype(BF16), NT_DIMS, preferred_element_type=F32)
        col = lax.broadcasted_iota(jnp.int32, zt.shape, 1)
        zt_ref[...] = jnp.where(col < Z_TAIL, zt, 0.0)


def _inproj(x2d, mod_l, norm_w_l, w_in_t, layer, conv_w_l, conv_b_l, *, per_row_mod, t):
    m = x2d.shape[0]
    half = PROJ_TN // 2
    assert OFF_MQ % PROJ_TN == half and OFF_MK % PROJ_TN == 0 and OFF_MO % PROJ_TN == 0 and ML_W == half
    assert ROW_TILE % t == 0
    return pl.pallas_call(
        functools.partial(_inproj_kernel, per_row_mod=per_row_mod, t=t),
        grid=(m // ROW_TILE, Z_W // PROJ_TN),
        in_specs=[
            pl.BlockSpec((ROW_TILE, D_MODEL), lambda i, j: (i, 0)),
            pl.BlockSpec((8, 3 * D_MODEL), lambda i, j: (0, 0)),
            pl.BlockSpec((1, D_MODEL), lambda i, j: (0, 0)),
            pl.BlockSpec((None, PROJ_TN, D_MODEL), lambda i, j: (layer, j, 0)),
            pl.BlockSpec((None, LANES, D_MODEL), lambda i, j: (layer, Z_W // LANES, 0)),
            pl.BlockSpec((CONV_W, 2 * ML_W), lambda i, j: (0, 0)),
            pl.BlockSpec((1, 2 * ML_W), lambda i, j: (0, 0)),
        ],
        out_specs=[
            pl.BlockSpec((ROW_TILE, PROJ_TN), lambda i, j: (i, j)),
            pl.BlockSpec((ROW_TILE, LANES), lambda i, j: (i, 0)),
        ],
        out_shape=[jax.ShapeDtypeStruct((m, Z_W), F32), jax.ShapeDtypeStruct((m, LANES), F32)],
        scratch_shapes=[pltpu.VMEM((ROW_TILE, D_MODEL), BF16)],
        compiler_params=_cparams("parallel", "arbitrary"),
        name="inproj",
    )(x2d, mod_l, norm_w_l.reshape(1, D_MODEL), w_in_t, w_in_t, conv_w_l, conv_b_l.reshape(1, 2 * ML_W))


def _rope(x, cos, sin):
    lane = lax.broadcasted_iota(jnp.int32, x.shape, 1)
    quarter = HEAD_DIM // 4
    partner = jnp.where((lane % (2 * quarter)) < quarter,
                        pltpu.roll(x, HEAD_DIM - quarter, axis=1), pltpu.roll(x, quarter, axis=1))
    return x * cos + partner * sin


LOG2E = 1.4426950408889634
LN2 = 0.6931471805599453


def _attn_kernel(*refs, t, tq, bb, latent, n_prev):
    hk = ATT_KV_HEADS
    zq_ref, zk_ref, zv_ref = refs[:3]
    zg_refs = refs[3:3 + hk]
    rest = refs[3 + hk:]
    if latent:
        qn_ref, kn_ref, cos_ref, sin_ref, ck_ref, cv_ref, o_ref, kall_ref, vall_ref = rest
        past = ck_ref.shape[0]
    else:
        qn_ref, kn_ref, *prev_kv, o_ref, knew_ref, vnew_ref, kall_ref, vall_ref = rest
        past = 0
    qi = pl.program_id(1)

    @pl.when(qi == 0)
    def _():
        if n_prev:
            knew_ref[:, 0:n_prev] = prev_kv[0][...]
            vnew_ref[:, 0:n_prev] = prev_kv[1][...]
        for bi in range(bb):
            seq = slice(bi * t, (bi + 1) * t)
            for j in range(hk):
                kc = slice(j * HEAD_DIM, (j + 1) * HEAD_DIM)
                slot = bi * hk + j
                k = _rms(zk_ref[seq, kc], kn_ref[...])
                v = zv_ref[seq, kc]
                if latent:
                    k = _rope(k, cos_ref[...], sin_ref[...])
                    kall_ref[slot, 0:past, :] = ck_ref[:, kc].astype(BF16)
                    vall_ref[slot, 0:past, :] = cv_ref[:, kc].astype(BF16)
                else:
                    knew_ref[bi, n_prev, :, j, :] = k
                    vnew_ref[bi, n_prev, :, j, :] = v
                kall_ref[slot, past:past + t, :] = k.astype(BF16)
                vall_ref[slot, past:past + t, :] = v.astype(BF16)

    q_scale = (HEAD_DIM ** -0.5) * LOG2E
    sub = min(t, ATT_SUB)

    def scores(r, hd):
        rows = slice(r * sub, (r + 1) * sub)
        cols = slice(hd * HEAD_DIM, (hd + 1) * HEAD_DIM)
        q = _rms(zq_ref[rows, cols], qn_ref[...])
        if latent:
            pos = pl.ds(pl.multiple_of(qi * tq + r * sub, sub), sub)
            q = _rope(q, cos_ref[pos, :], sin_ref[pos, :])
        qb = (q * q_scale).astype(BF16)
        slot = (r * sub // t) * hk + hd // ATT_GROUPS
        return lax.dot_general(qb, kall_ref[slot], NT_DIMS, preferred_element_type=F32)

    def finish(r, hd, s):
        j, g = divmod(hd, ATT_GROUPS)
        rows = slice(r * sub, (r + 1) * sub)
        cols = slice(hd * HEAD_DIM, (hd + 1) * HEAD_DIM)
        gcols = slice(g * HEAD_DIM, (g + 1) * HEAD_DIM)
        p = jnp.exp2(s - jnp.max(s, axis=-1, keepdims=True))
        den = jnp.sum(p, axis=-1, keepdims=True)
        o = jnp.dot(p.astype(BF16), vall_ref[(r * sub // t) * hk + j], preferred_element_type=F32) * (1.0 / den)
        o_ref[rows, cols] = (o * _silu(zg_refs[j][rows, gcols])).astype(BF16)

    items = [(r, hd) for r in range(tq // sub) for hd in range(hk * ATT_GROUPS)]
    pending = [scores(*it) for it in items[:ATT_AHEAD]]
    for n, it in enumerate(items):
        if n + ATT_AHEAD < len(items):
            pending.append(scores(*items[n + ATT_AHEAD]))
        finish(*it, pending.pop(0))


def _attention(z, q_norm_l, k_norm_l, *, b, t, rope=None, cache=None, prev_kv=None):
    latent = rope is not None
    hk = ATT_KV_HEADS
    bb = max(1, ATT_TQ // t)
    tq = min(t, ATT_TQ) * bb
    nq = bb * t // tq
    assert b % bb == 0 and (bb == 1 or nq == 1)
    gw = ATT_GROUPS * HEAD_DIM
    in_specs = [
        pl.BlockSpec((tq, hk * gw), lambda bi, qi: (bi * nq + qi, OFF_AQ // (hk * gw))),
        pl.BlockSpec((bb * t, hk * HEAD_DIM), lambda bi, qi: (bi, OFF_AK // (hk * HEAD_DIM))),
        pl.BlockSpec((bb * t, hk * HEAD_DIM), lambda bi, qi: (bi, OFF_AV // (hk * HEAD_DIM))),
    ]
    in_specs += [pl.BlockSpec((tq, gw), lambda bi, qi, j=j: (bi * nq + qi, OFF_AG // gw + j)) for j in range(hk)]
    in_specs += [pl.BlockSpec((1, HEAD_DIM), lambda bi, qi: (0, 0))] * 2
    args = [z] * (3 + hk) + [q_norm_l.reshape(1, HEAD_DIM), k_norm_l.reshape(1, HEAD_DIM)]
    tk = t
    if latent:
        cos, sin = rope
        ck, cv, layer = cache
        past = ck.shape[2]
        tk = past + t
        in_specs += [
            pl.BlockSpec((t, HEAD_DIM), lambda bi, qi: (0, 0)),
            pl.BlockSpec((t, HEAD_DIM), lambda bi, qi: (0, 0)),
            pl.BlockSpec((None, None, past, hk * HEAD_DIM), lambda bi, qi: (bi, layer, 0, 0)),
            pl.BlockSpec((None, None, past, hk * HEAD_DIM), lambda bi, qi: (bi, layer, 0, 0)),
        ]
        args += [cos, sin, ck, cv]
    out_specs = [pl.BlockSpec((tq, hk * gw), lambda bi, qi: (bi * nq + qi, 0))]
    out_shape = [jax.ShapeDtypeStruct((b * t, ATT_W), BF16)]
    n_prev = 0 if prev_kv is None else prev_kv[0].shape[1]
    if not latent:
        kv_spec = lambda layers: pl.BlockSpec((bb, layers, t, ATT_KV_HEADS, HEAD_DIM), lambda bi, qi: (bi, 0, 0, 0, 0))
        if n_prev:
            in_specs += [kv_spec(n_prev)] * 2
            args += list(prev_kv)
        out_specs += [kv_spec(n_prev + 1)] * 2
        out_shape += [jax.ShapeDtypeStruct((b, n_prev + 1, t, ATT_KV_HEADS, HEAD_DIM), F32)] * 2
    return pl.pallas_call(
        functools.partial(_attn_kernel, t=t, tq=tq, bb=bb, latent=latent, n_prev=n_prev),
        grid=(b // bb, nq),
        in_specs=in_specs,
        out_specs=out_specs,
        out_shape=out_shape,
        scratch_shapes=[pltpu.VMEM((bb * hk, tk, HEAD_DIM), BF16), pltpu.VMEM((bb * hk, tk, HEAD_DIM), BF16)],
        compiler_params=_cparams("parallel", "arbitrary"),
        name="attention",
    )(*args)


def _split3_dot(tri, x):
    hi = x.astype(BF16)
    r1 = x - hi.astype(F32)
    mid = r1.astype(BF16)
    lo = (r1 - mid.astype(F32)).astype(BF16)
    d = lambda a: jnp.dot(tri, a, preferred_element_type=F32)
    return d(hi) + d(mid) + d(lo)


def _mlstm_kernel(*refs, t, zero_init, side_mod, n_prev):
    if side_mod:
        *refs, mo_ref = refs
        cond_ref, wm_ref, bm_ref = refs[4:7]
        refs = refs[:4] + refs[7:]
        _mod_kernel(cond_ref, wm_ref, bm_ref, mo_ref)
    refs = list(refs)
    if n_prev:
        c_prev, n_prev_ref, m_prev = refs[4:7]
        refs = refs[:4] + refs[7:]
    if zero_init:
        z_ref, zif_ref, bif_ref, nw_ref, o_ref, c_out, n_out, m_out = refs
    else:
        z_ref, zif_ref, bif_ref, nw_ref, c0_ref, n0_ref, m0_ref, o_ref, c_out, n_out, m_out = refs
    if n_prev:
        c_out[0:n_prev] = c_prev[...]
        n_out[0:n_prev] = n_prev_ref[...]
        m_out[0:n_prev] = m_prev[...]
    L = ML_CHUNK
    nc = t // L
    chunk = lambda a, c: a[c * L:(c + 1) * L]

    gates = zif_ref[...] + bif_ref[...]
    logsig = (jnp.minimum(gates, 0.0) - jnp.log1p(jnp.exp(-jnp.abs(gates)))) * LOG2E
    gates = gates * LOG2E
    gates_t = gates.T
    r_i = lax.broadcasted_iota(jnp.int32, (L, L), 0)
    c_i = lax.broadcasted_iota(jnp.int32, (L, L), 1)
    lower, upper = r_i >= c_i, r_i <= c_i
    cum_tris = (lower.astype(BF16), upper.astype(BF16))
    dmasks = (upper, lower)
    cums = [[_split3_dot(cum_tris[d], chunk(logsig, c)) for c in range(nc)] for d in range(2)]
    cums_t = [[x.T for x in row] for row in cums]
    ext_rows = (lax.broadcasted_iota(jnp.int32, (ML_EXT - ML_DV, t), 0) == 0).astype(F32)
    lchunk = lambda a, c: a[:, c * L:(c + 1) * L]

    def running_max(a, d):
        rows = a.shape[0]
        if rows < 8:
            a = jnp.concatenate([a] * (8 // rows), axis=0)
        lane = lax.broadcasted_iota(jnp.int32, a.shape, 1)
        k = 1
        while k < L:
            if d:
                shifted = jnp.where(lane < L - k, pltpu.roll(a, L - k, axis=1), -jnp.inf)
            else:
                shifted = jnp.where(lane >= k, pltpu.roll(a, k, axis=1), -jnp.inf)
            a = jnp.maximum(a, shifted)
            k *= 2
        return a[:rows]

    hsl = lambda d, which: slice((2 * d + which) * ML_HEADS, (2 * d + which + 1) * ML_HEADS)
    a_maxes = [running_max(jnp.concatenate([lchunk(gates_t, c)[hsl(d, 0)] - cums_t[d][c][hsl(d, 1)]
                                            for c in range(nc)], axis=0), d) for d in range(2)]

    def local_part(qk_c, kb_c, vt_c, vtb_c, b_row, b_col, i_row, i_col, a_max, d):
        m_loc = b_row + a_max
        s = (qk_c * jnp.exp2(jnp.where(dmasks[d], (i_col - b_col) - a_max, -jnp.inf))).astype(BF16)
        bm = jnp.dot(vtb_c, s, preferred_element_type=F32)
        b_end = b_row[:, 0:1] if d else b_row[:, L - 1:L]
        g = b_end - b_row + i_row
        g_max = jnp.max(g, axis=1, keepdims=True)
        vw = (vt_c * jnp.exp2(g - g_max)).astype(BF16)
        u = jnp.dot(vw, kb_c, preferred_element_type=F32)
        return m_loc, bm, b_end, g_max, u

    def scan_step(qt_c, b_row, loc, c_t, m):
        m_loc, bm, b_end, g_max, u = loc
        inter = b_row + m
        m_row = jnp.maximum(inter, m_loc)
        nd = jnp.exp2(m_loc - m_row) * bm
        if c_t is not None:
            nd = nd + jnp.exp2(inter - m_row) * jnp.dot(c_t.astype(BF16), qt_c, preferred_element_type=F32)
        h_t = nd[:ML_DV] / jnp.maximum(jnp.abs(nd[ML_DV:ML_DV + 1]), jnp.exp2(-m_row))
        m_new = jnp.maximum(b_end + m, g_max)
        c_new = jnp.exp2(g_max - m_new) * u
        if c_t is not None:
            c_new = c_new + jnp.exp2(b_end + m - m_new) * c_t
        return h_t, c_new, m_new

    chains = [(hd, d) for hd in range(ML_HEADS) for d in range(2)]
    qts, b_rows, locs, state = {}, {}, {}, {}
    for hd in range(ML_HEADS):
        cols = slice(hd * ML_DK, (hd + 1) * ML_DK)
        kcols = slice(ML_W + hd * ML_DK, ML_W + (hd + 1) * ML_DK)
        vcols = slice(2 * ML_W + hd * ML_DV, 2 * ML_W + (hd + 1) * ML_DV)
        qt = z_ref[:, cols].T.astype(BF16)
        kb = z_ref[:, kcols].astype(BF16)
        vt = jnp.concatenate([z_ref[:, vcols].T, ext_rows], axis=0)
        vtb = vt.astype(BF16)
        qk = [jnp.dot(chunk(kb, c), lchunk(qt, c), preferred_element_type=F32) for c in range(nc)]
        qts[hd] = qt
        for d in range(2):
            i_lane = 2 * d * ML_HEADS + hd
            f_lane = (2 * d + 1) * ML_HEADS + hd
            b_rows[hd, d] = [cums_t[d][c][f_lane:f_lane + 1, :] for c in range(nc)]
            locs[hd, d] = [local_part(qk[c], chunk(kb, c), lchunk(vt, c), lchunk(vtb, c), b_rows[hd, d][c],
                                      cums[d][c][:, f_lane:f_lane + 1], lchunk(gates_t, c)[i_lane:i_lane + 1, :],
                                      chunk(gates, c)[:, i_lane:i_lane + 1],
                                      a_maxes[d][c * ML_HEADS + hd:c * ML_HEADS + hd + 1], d) for c in range(nc)]
            if zero_init:
                state[hd, d] = (None, jnp.zeros((1, 1), F32))
            else:
                c_t = jnp.concatenate([c0_ref[d, hd].T, n0_ref[d, hd:hd + 1, :],
                                       jnp.zeros((ML_EXT - ML_DV - 1, ML_DK), F32)], axis=0)
                state[hd, d] = (c_t, m0_ref[d:d + 1, hd:hd + 1] * LOG2E)

    h_parts = {ch: [None] * nc for ch in chains}
    for step in range(nc):
        for hd, d in chains:
            c = nc - 1 - step if d else step
            c_t, m = state[hd, d]
            h_parts[hd, d][c], c_t, m = scan_step(lchunk(qts[hd], c), b_rows[hd, d][c], locs[hd, d][c], c_t, m)
            state[hd, d] = (c_t, m)

    for hd, d in chains:
        c_t, m = state[hd, d]
        c_out[n_prev, d, hd] = c_t[:ML_DV].T
        n_out[n_prev, d, hd:hd + 1, :] = c_t[ML_DV:ML_DV + 1]
        m_out[n_prev, d:d + 1, hd:hd + 1] = m * LN2
    for hd in range(ML_HEADS):
        cols = slice(hd * ML_DK, (hd + 1) * ML_DK)
        ocols = slice(3 * ML_W + hd * ML_DV, 3 * ML_W + (hd + 1) * ML_DV)
        h_fwd, h_bwd = (jnp.concatenate(h_parts[hd, d], axis=1) if nc > 1 else h_parts[hd, d][0] for d in range(2))
        h_sum = h_fwd + h_bwd
        hm = _rms(h_sum.T, nw_ref[:, cols])
        o_ref[:, cols] = (hm * z_ref[:, ocols]).astype(BF16)


def _mlstm(z, b_if_l, ml_norm_l, *, b, t, state=None, side=None, prev=None):
    zero_init = state is None
    n_prev = 0 if prev is None else prev[0].shape[1]
    zw = 5 * ML_W
    in_specs = [
        pl.BlockSpec((t, zw), lambda bi: (bi, OFF_MQ // zw)),
        pl.BlockSpec((t, LANES), lambda bi: (bi, OFF_IF // LANES)),
        pl.BlockSpec((1, LANES), lambda bi: (0, 0)),
        pl.BlockSpec((1, ML_W), lambda bi: (0, 0)),
    ]
    bif = jnp.pad(b_if_l, (0, LANES - N_GATES)).reshape(1, LANES)
    args = [z, z, bif, ml_norm_l.reshape(1, ML_W)]
    out_specs = [
        pl.BlockSpec((t, ML_W), lambda bi: (bi, 0)),
        pl.BlockSpec((None, n_prev + 1, 2, ML_HEADS, ML_DK, ML_DV), lambda bi: (bi, 0, 0, 0, 0, 0)),
        pl.BlockSpec((None, n_prev + 1, 2, ML_HEADS, ML_DK), lambda bi: (bi, 0, 0, 0, 0)),
        pl.BlockSpec((None, n_prev + 1, 2, ML_HEADS), lambda bi: (bi, 0, 0, 0)),
    ]
    out_shape = [
        jax.ShapeDtypeStruct((b * t, ML_W), BF16),
        jax.ShapeDtypeStruct((b, n_prev + 1, 2, ML_HEADS, ML_DK, ML_DV), F32),
        jax.ShapeDtypeStruct((b, n_prev + 1, 2, ML_HEADS, ML_DK), F32),
        jax.ShapeDtypeStruct((b, n_prev + 1, 2, ML_HEADS), F32),
    ]
    if side is not None:
        cond8, w_mod, b_mod, side_layer = side
        n = 3 * D_MODEL
        tn = n // b
        assert tn % LANES == 0 and tn * b == n
        in_specs += [
            pl.BlockSpec((8, D_MODEL), lambda bi: (0, 0)),
            pl.BlockSpec((None, D_MODEL, tn), lambda bi: (side_layer, 0, bi)),
            pl.BlockSpec((None, 1, tn), lambda bi: (side_layer, 0, bi)),
        ]
        args += [cond8, w_mod, b_mod.reshape(DEPTH, 1, n)]
        out_specs.append(pl.BlockSpec((8, tn), lambda bi: (0, bi)))
        out_shape.append(jax.ShapeDtypeStruct((8, n), F32))
    if n_prev:
        in_specs += [
            pl.BlockSpec((None, n_prev, 2, ML_HEADS, ML_DK, ML_DV), lambda bi: (bi, 0, 0, 0, 0, 0)),
            pl.BlockSpec((None, n_prev, 2, ML_HEADS, ML_DK), lambda bi: (bi, 0, 0, 0, 0)),
            pl.BlockSpec((None, n_prev, 2, ML_HEADS), lambda bi: (bi, 0, 0, 0)),
        ]
        args += list(prev)
    if not zero_init:
        c0, n0, m0, layer = state
        in_specs += [
            pl.BlockSpec((None, None, 2, ML_HEADS, ML_DK, ML_DV), lambda bi: (bi, layer, 0, 0, 0, 0)),
            pl.BlockSpec((None, None, 2, ML_HEADS, ML_DK), lambda bi: (bi, layer, 0, 0, 0)),
            pl.BlockSpec((None, None, 2, ML_HEADS), lambda bi: (bi, layer, 0, 0)),
        ]
        args += [c0, n0, m0]
    return pl.pallas_call(
        functools.partial(_mlstm_kernel, t=t, zero_init=zero_init, side_mod=side is not None, n_prev=n_prev),
        grid=(b,),
        in_specs=in_specs,
        out_specs=out_specs,
        out_shape=out_shape,
        compiler_params=_cparams("parallel"),
        name="mlstm",
    )(*args)


def _dft_tables(t):
    def cs(n):
        idx = np.arange(n, dtype=np.int64)
        ang = 2.0 * np.pi * ((idx[:, None] * idx[None, :]) % n).astype(np.float64) / n
        return np.cos(ang), np.sin(ang)
    cc, sc = cs(FO_GC)
    ct, st = cs(t)
    return (np.concatenate([cc, sc], axis=1).astype(np.float32),
            np.concatenate([ct, -st], axis=1).astype(np.float32))


FO_RT = 256


def _fourier_kernel(zx_ref, zga_ref, zgb_ref, wc_ref, wt_ref, wf_ref, o_ref, y_ref, *, t, bb):
    wc = wc_ref[...]
    for bi in range(bb):
        x = zx_ref[bi * t:(bi + 1) * t, FO_SHIFT:FO_SHIFT + FO_W]
        for g in range(FO_GROUPS):
            cols = slice(g * FO_GC, (g + 1) * FO_GC)
            ycols = slice(bi * FO_W + g * FO_GC, bi * FO_W + (g + 1) * FO_GC)
            y = jnp.dot(x[:, cols].astype(BF16), wc, preferred_element_type=F32)
            y_ref[0:t, ycols] = y[:, :FO_GC].astype(BF16)
            y_ref[t:2 * t, ycols] = y[:, FO_GC:].astype(BF16)
    scale = (t * FO_GC) ** -0.5
    for r in range(t // FO_RT):
        rows = slice(r * FO_RT, (r + 1) * FO_RT)
        f = jnp.dot(wt_ref[rows, :], y_ref[...], preferred_element_type=F32) * scale
        for bi in range(bb):
            orows = slice(bi * t + r * FO_RT, bi * t + (r + 1) * FO_RT)
            fg = jnp.concatenate([zga_ref[orows, FO_SHIFT:], zgb_ref[orows, :FO_SHIFT]], axis=1)
            for g in range(FO_GROUPS):
                cols = slice(g * FO_GC, (g + 1) * FO_GC)
                fcols = slice(bi * FO_W + g * FO_GC, bi * FO_W + (g + 1) * FO_GC)
                og = jnp.dot(f[:, fcols].astype(BF16), wf_ref[g].astype(BF16), preferred_element_type=F32)
                o_ref[orows, cols] = (og * _silu(fg[:, cols])).astype(BF16)


def _fourier(z, z_tail, w_fno_l, *, b, t):
    assert OFF_FG - FO_SHIFT + FO_W == Z_W and Z_TAIL == FO_SHIFT
    wc, wt = _dft_tables(t)
    xw = FO_W + LANES
    bb = max(1, min(b, ROW_TILE // t))
    return pl.pallas_call(
        functools.partial(_fourier_kernel, t=t, bb=bb),
        grid=(b // bb,),
        in_specs=[
            pl.BlockSpec((bb * t, xw), lambda bi: (bi, (OFF_FX - FO_SHIFT) // xw)),
            pl.BlockSpec((bb * t, FO_W), lambda bi: (bi, (OFF_FG - FO_SHIFT) // FO_W)),
            pl.BlockSpec((bb * t, LANES), lambda bi: (bi, 0)),
            pl.BlockSpec((FO_GC, 2 * FO_GC), lambda bi: (0, 0)),
            pl.BlockSpec((t, 2 * t), lambda bi: (0, 0)),
            pl.BlockSpec((FO_GROUPS, FO_GC, FO_GC), lambda bi: (0, 0, 0)),
        ],
        out_specs=pl.BlockSpec((bb * t, FO_W), lambda bi: (bi, 0)),
        out_shape=jax.ShapeDtypeStruct((b * t, FO_W), BF16),
        scratch_shapes=[pltpu.VMEM((2 * t, bb * FO_W), BF16)],
        compiler_params=_cparams("parallel"),
        name="fourier",
    )(z, z, z_tail, jnp.asarray(wc).astype(BF16), jnp.asarray(wt).astype(BF16), w_fno_l)


OUT_TN = 1024
OUT_TM = 1024


def _outproj_kernel(oa_ref, om_ref, of_ref, wa_ref, wm_ref, wf_ref, x_ref, gate_ref, y_ref, wb_ref, *, per_row_mod):
    i = pl.program_id(1)

    @pl.when(i == 0)
    def _():
        wb_ref[0:ATT_W, :] = wa_ref[...].astype(BF16)
        wb_ref[ATT_W:ATT_W + ML_W, :] = wm_ref[...].astype(BF16)
        wb_ref[ATT_W + ML_W:D_MIX, :] = wf_ref[...].astype(BF16)

    row = 1 + (i * OUT_TM) // ROW_TILE if per_row_mod else 0
    gate = gate_ref[pl.ds(row, 1), :]
    y = (jnp.dot(oa_ref[...], wb_ref[0:ATT_W, :], preferred_element_type=F32)
         + jnp.dot(om_ref[...], wb_ref[ATT_W:ATT_W + ML_W, :], preferred_element_type=F32)
         + jnp.dot(of_ref[...], wb_ref[ATT_W + ML_W:D_MIX, :], preferred_element_type=F32))
    y_ref[...] = x_ref[...] + gate * y


def _outproj(o_att, o_ml, o_fo, w_out, layer, x2d, mod_l, *, per_row_mod):
    m = x2d.shape[0]
    return pl.pallas_call(
        functools.partial(_outproj_kernel, per_row_mod=per_row_mod),
        grid=(D_MODEL // OUT_TN, m // OUT_TM),
        in_specs=[
            pl.BlockSpec((OUT_TM, ATT_W), lambda n, i: (i, 0)),
            pl.BlockSpec((OUT_TM, ML_W), lambda n, i: (i, 0)),
            pl.BlockSpec((OUT_TM, FO_W), lambda n, i: (i, 0)),
            pl.BlockSpec((None, ATT_W, OUT_TN), lambda n, i: (layer, 0, n)),
            pl.BlockSpec((None, ML_W, OUT_TN), lambda n, i: (layer, ATT_W // ML_W, n)),
            pl.BlockSpec((None, FO_W, OUT_TN), lambda n, i: (layer, (ATT_W + ML_W) // FO_W, n)),
            pl.BlockSpec((OUT_TM, OUT_TN), lambda n, i: (i, n)),
            pl.BlockSpec((8, OUT_TN), lambda n, i: (0, 2 * D_MODEL // OUT_TN + n)),
        ],
        out_specs=pl.BlockSpec((OUT_TM, OUT_TN), lambda n, i: (i, n)),
        out_shape=jax.ShapeDtypeStruct((m, D_MODEL), F32),
        scratch_shapes=[pltpu.VMEM((D_MIX, OUT_TN), BF16)],
        compiler_params=_cparams("parallel", "arbitrary"),
        name="outproj",
    )(o_att, o_ml, o_fo, w_out, w_out, w_out, x2d, mod_l)


def _rope_tables(t):
    half = HEAD_DIM // 2
    inv_freq = ROPE_BASE ** (-jnp.arange(0, half, 2, dtype=F32) / half)
    n_rows = t // GRID_W
    rows = jnp.repeat(jnp.arange(n_rows, dtype=F32), GRID_W)
    cols = jnp.tile(jnp.arange(GRID_W, dtype=F32), n_rows)
    ar = rows[:, None] * inv_freq
    ac = cols[:, None] * inv_freq
    cos = jnp.concatenate([jnp.cos(ar), jnp.cos(ar), jnp.cos(ac), jnp.cos(ac)], axis=1)
    sin = jnp.concatenate([-jnp.sin(ar), jnp.sin(ar), -jnp.sin(ac), jnp.sin(ac)], axis=1)
    return cos, sin


def _layer(x2d, mod_l, lw, layer, *, b, t, latent, rope=None, cache=None, state=None, side=None, prev=None):
    norm_w, w_in_t, b_if, conv_w, conv_b, q_norm, k_norm, ml_norm, w_fno, w_out = lw
    prev_kv, prev_state = (None, None) if prev is None else (prev[:2], prev[2:])
    z, z_tail = _inproj(x2d, mod_l, norm_w, w_in_t, layer, conv_w, conv_b, per_row_mod=latent, t=t)
    o_att, *kv_new = _attention(z, q_norm, k_norm, b=b, t=t, rope=rope, cache=cache, prev_kv=prev_kv)
    o_ml, c_f, n_f, m_f, *side_mod = _mlstm(z, b_if, ml_norm, b=b, t=t, state=state, side=side, prev=prev_state)
    o_fo = _fourier(z, z_tail, w_fno, b=b, t=t)
    y = _outproj(o_att, o_ml, o_fo, w_out, layer, x2d, mod_l, per_row_mod=latent)
    return y, (*kv_new, c_f, n_f, m_f), side_mod


def kernel(x_prompt, x_sample, cache_k, cache_v, state_C, state_n, state_m, c, c_ctx, norm_w, w_mod, b_mod, w_in,
           b_if, conv_w, conv_b, q_norm, k_norm, ml_norm, w_fno, w_out):
    bp, tp, _ = x_prompt.shape
    bs, ts, _ = x_sample.shape
    assert tp % ML_CHUNK == 0 and ts == ROW_TILE and (bp * tp) % ROW_TILE == 0 and bs + 1 <= 8
    past = cache_k.shape[2]

    cond8 = jnp.concatenate([c_ctx[None, :], c, jnp.zeros((8 - 1 - bs, D_MODEL), F32)], axis=0)
    mod_l = _modulation(cond8, w_mod, b_mod, 0)

    rope = _rope_tables(ts)
    ck = cache_k.reshape(bs, DEPTH, past, ATT_KV_W)
    cv = cache_v.reshape(bs, DEPTH, past, ATT_KV_W)
    w_in_t = jnp.swapaxes(w_in, 1, 2)

    xp = x_prompt.reshape(bp * tp, D_MODEL)
    xs = x_sample.reshape(bs * ts, D_MODEL)
    new = None
    for l in range(DEPTH):
        lw = (norm_w[l], w_in_t, b_if[l], conv_w[l], conv_b[l], q_norm[l], k_norm[l], ml_norm[l], w_fno[l], w_out)
        side = (cond8, w_mod, b_mod, l + 1) if l + 1 < DEPTH else None
        xp, new, mod_next = _layer(xp, mod_l, lw, l, b=bp, t=tp, latent=False, side=side, prev=new)
        xs, _, _ = _layer(xs, mod_l, lw, l, b=bs, t=ts, latent=True, rope=rope,
                          cache=(ck, cv, l), state=(state_C, state_n, state_m, l))
        if mod_next:
            mod_l = mod_next[0]
    return (xp.reshape(bp, tp, D_MODEL), xs.reshape(bs, ts, D_MODEL), *new)
```

```python
import functools

import numpy as np
import jax
import jax.numpy as jnp
from jax import lax
from jax.experimental import pallas as pl
from jax.experimental.pallas import tpu as pltpu

D_MODEL = 2048
DEPTH = 2
GRID_W = 64
HEAD_DIM = 128
ATT_HEADS = 8
ATT_KV_HEADS = 2
ATT_GROUPS = ATT_HEADS // ATT_KV_HEADS
ATT_W = ATT_HEADS * HEAD_DIM
ATT_KV_W = ATT_KV_HEADS * HEAD_DIM
ML_HEADS = 4
ML_DK = 128
ML_DV = 128
ML_W = ML_HEADS * ML_DV
FO_GROUPS = 4
FO_GC = 128
FO_W = FO_GROUPS * FO_GC
D_MIX = ATT_W + ML_W + FO_W
N_GATES = 4 * ML_HEADS
D_PROJ = 2 * ATT_W + 2 * ATT_KV_W + 5 * ML_W + N_GATES + 2 * FO_W
CONV_W = 3
ROPE_BASE = 10000.0
EPS = 1e-6

LANES = 128
MXU_N = 256
OFF_AQ = 0
OFF_AK = OFF_AQ + ATT_W
OFF_AV = OFF_AK + ATT_KV_W
OFF_AG = OFF_AV + ATT_KV_W
OFF_MQ = OFF_AG + ATT_W
OFF_MK = OFF_MQ + ML_W
OFF_MV = OFF_MK + ML_W
OFF_MO = OFF_MV + ML_W
OFF_MG = OFF_MO + ML_W
OFF_IF = OFF_MG + ML_W
OFF_FX = OFF_IF + N_GATES
OFF_FG = OFF_FX + FO_W
FO_SHIFT = OFF_FX % LANES

PROJ_TN = 1024
Z_W = (D_PROJ // PROJ_TN) * PROJ_TN
Z_TAIL = D_PROJ - Z_W
ROW_TILE = 1024
ML_CHUNK = 256
ML_EXT = ML_DV + 16
ML_ROWS = 512
ATT_TQ = 512
ATT_SUB = 256
ATT_AHEAD = 1
VMEM_LIMIT = 56 * 1024 * 1024

BF16 = jnp.bfloat16
F32 = jnp.float32
NT_DIMS = (((1,), (1,)), ((), ()))


def _cparams(*sem):
    return pltpu.CompilerParams(dimension_semantics=sem, vmem_limit_bytes=VMEM_LIMIT)


def _silu(x):
    return x * (1.0 / (1.0 + jnp.exp(-x)))


def _sigmoid(x):
    return 1.0 / (1.0 + jnp.exp(-x))


def _rms(x, w):
    ms = jnp.mean(x * x, axis=-1, keepdims=True)
    return x * lax.rsqrt(ms + EPS) * w


MOD_TN = 768


def _mod_kernel(cond_ref, w_ref, b_ref, o_ref):
    a = _silu(cond_ref[...]).astype(BF16)
    o_ref[...] = jnp.dot(a, w_ref[...].astype(BF16), preferred_element_type=F32) + b_ref[...]


def _modulation(cond8, w_mod, b_mod, layer):
    n = 3 * D_MODEL
    return pl.pallas_call(
        _mod_kernel,
        grid=(n // MOD_TN,),
        in_specs=[
            pl.BlockSpec((8, D_MODEL), lambda j: (0, 0)),
            pl.BlockSpec((None, D_MODEL, MOD_TN), lambda j: (layer, 0, j)),
            pl.BlockSpec((None, 1, MOD_TN), lambda j: (layer, 0, j)),
        ],
        out_specs=pl.BlockSpec((8, MOD_TN), lambda j: (0, j)),
        out_shape=jax.ShapeDtypeStruct((8, n), F32),
        compiler_params=_cparams("parallel"),
        name="modulation",
    )(cond8, w_mod, b_mod.reshape(DEPTH, 1, n))


NORM_ROWS = 16


def _shift_rows(x, direction, period):
    n = x.shape[0]
    pos = lax.broadcasted_iota(jnp.int32, x.shape, 0) % period
    if direction > 0:
        return jnp.where(pos == 0, 0.0, pltpu.roll(x, 1, axis=0))
    return jnp.where(pos == period - 1, 0.0, pltpu.roll(x, n - 1, axis=0))


def _conv_silu(x, w, b, period):
    y = _shift_rows(x, 1, period) * w[0:1, :] + x * w[1:2, :] + _shift_rows(x, -1, period) * w[2:3, :] + b
    return _silu(y)


def _inproj_kernel(x_ref, mod_ref, nw_ref, w_ref, wt_ref, cw_ref, cb_ref, z_ref, zt_ref, h_ref, *, per_row_mod, t):
    i = pl.program_id(0)
    j = pl.program_id(1)

    @pl.when(j == 0)
    def _():
        row = 1 + i if per_row_mod else 0
        shift = mod_ref[pl.ds(row, 1), 0:D_MODEL]
        gain = nw_ref[...] * (1.0 + mod_ref[pl.ds(row, 1), D_MODEL:2 * D_MODEL])

        def body(r, carry):
            sl = pl.ds(pl.multiple_of(r * NORM_ROWS, NORM_ROWS), NORM_ROWS)
            x = x_ref[sl, :]
            inv = lax.rsqrt(jnp.mean(x * x, axis=-1, keepdims=True) + EPS)
            h_ref[sl, :] = (x * inv * gain + shift).astype(BF16)
            return carry

        lax.fori_loop(0, ROW_TILE // NORM_ROWS, body, 0, unroll=8)

    def project(cols=slice(0, PROJ_TN)):
        return lax.dot_general(h_ref[...], w_ref[cols, :].astype(BF16), NT_DIMS, preferred_element_type=F32)

    half = PROJ_TN // 2
    pieces = lambda start: [slice(c, c + MXU_N) for c in range(start, start + half, MXU_N)]
    j_q, j_k, j_o = OFF_MQ // PROJ_TN, OFF_MK // PROJ_TN, OFF_MO // PROJ_TN

    def conv_tile(start, raw_start, conv_off, scale):
        for cols, rcols in zip(pieces(start), pieces(raw_start)):
            cc = slice(conv_off + cols.start - start, conv_off + cols.stop - start)
            y = _conv_silu(project(cols), cw_ref[:, cc], cb_ref[:, cc], t)
            z_ref[:, cols] = y if scale is None else y * scale
            z_ref[:, rcols] = project(rcols)

    @pl.when(j == j_q)
    def _():
        conv_tile(half, 0, 0, None)

    @pl.when(j == j_k)
    def _():
        conv_tile(0, half, ML_W, ML_DK ** -0.5)

    @pl.when(j == j_o)
    def _():
        z_ref[...] = project()
        z_ref[:, 0:half] = _sigmoid(z_ref[:, 0:half]) * _silu(z_ref[:, half:PROJ_TN])

    @pl.when((j != j_q) & (j != j_k) & (j != j_o))
    def _():
        z_ref[...] = project()

    @pl.when(j == Z_W // PROJ_TN - 1)
    def _():
        zt = lax.dot_general(h_ref[...], wt_ref[...].astype(BF16), NT_DIMS, preferred_element_type=F32)
        col = lax.broadcasted_iota(jnp.int32, zt.shape, 1)
        zt_ref[...] = jnp.where(col < Z_TAIL, zt, 0.0)


def _inproj(x2d, mod_l, norm_w_l, w_in_t, layer, conv_w_l, conv_b_l, *, per_row_mod, t):
    m = x2d.shape[0]
    half = PROJ_TN // 2
    assert OFF_MQ % PROJ_TN == half and OFF_MK % PROJ_TN == 0 and OFF_MO % PROJ_TN == 0 and ML_W == half
    assert ROW_TILE % t == 0
    return pl.pallas_call(
        functools.partial(_inproj_kernel, per_row_mod=per_row_mod, t=t),
        grid=(m // ROW_TILE, Z_W // PROJ_TN),
        in_specs=[
            pl.BlockSpec((ROW_TILE, D_MODEL), lambda i, j: (i, 0)),
            pl.BlockSpec((8, 3 * D_MODEL), lambda i, j: (0, 0)),
            pl.BlockSpec((1, D_MODEL), lambda i, j: (0, 0)),
            pl.BlockSpec((None, PROJ_TN, D_MODEL), lambda i, j: (layer, j, 0)),
            pl.BlockSpec((None, LANES, D_MODEL), lambda i, j: (layer, Z_W // LANES, 0)),
            pl.BlockSpec((CONV_W, 2 * ML_W), lambda i, j: (0, 0)),
            pl.BlockSpec((1, 2 * ML_W), lambda i, j: (0, 0)),
        ],
        out_specs=[
            pl.BlockSpec((ROW_TILE, PROJ_TN), lambda i, j: (i, j)),
            pl.BlockSpec((ROW_TILE, LANES), lambda i, j: (i, 0)),
        ],
        out_shape=[jax.ShapeDtypeStruct((m, Z_W), F32), jax.ShapeDtypeStruct((m, LANES), F32)],
        scratch_shapes=[pltpu.VMEM((ROW_TILE, D_MODEL), BF16)],
        compiler_params=_cparams("parallel", "arbitrary"),
        name="inproj",
    )(x2d, mod_l, norm_w_l.reshape(1, D_MODEL), w_in_t, w_in_t, conv_w_l, conv_b_l.reshape(1, 2 * ML_W))


def _rope(x, cos, sin):
    lane = lax.broadcasted_iota(jnp.int32, x.shape, 1)
    quarter = HEAD_DIM // 4
    partner = jnp.where((lane % (2 * quarter)) < quarter,
                        pltpu.roll(x, HEAD_DIM - quarter, axis=1), pltpu.roll(x, quarter, axis=1))
    return x * cos + partner * sin


LOG2E = 1.4426950408889634
LN2 = 0.6931471805599453


def _attn_kernel(*refs, t, tq, bb, latent, n_prev):
    hk = ATT_KV_HEADS
    zq_ref, zk_ref, zv_ref = refs[:3]
    zg_refs = refs[3:3 + hk]
    rest = refs[3 + hk:]
    if latent:
        qn_ref, kn_ref, cos_ref, sin_ref, ck_ref, cv_ref, o_ref, kall_ref, vall_ref = rest
        past = ck_ref.shape[0]
    else:
        qn_ref, kn_ref, *prev_kv, o_ref, knew_ref, vnew_ref, kall_ref, vall_ref = rest
        past = 0
    qi = pl.program_id(1)

    @pl.when(qi == 0)
    def _():
        if n_prev:
            knew_ref[:, 0:n_prev] = prev_kv[0][...]
            vnew_ref[:, 0:n_prev] = prev_kv[1][...]
        for bi in range(bb):
            seq = slice(bi * t, (bi + 1) * t)
            for j in range(hk):
                kc = slice(j * HEAD_DIM, (j + 1) * HEAD_DIM)
                slot = bi * hk + j
                k = _rms(zk_ref[seq, kc], kn_ref[...])
                v = zv_ref[seq, kc]
                if latent:
                    k = _rope(k, cos_ref[...], sin_ref[...])
                    kall_ref[slot, 0:past, :] = ck_ref[:, kc].astype(BF16)
                    vall_ref[slot, 0:past, :] = cv_ref[:, kc].astype(BF16)
                else:
                    knew_ref[bi, n_prev, :, j, :] = k
                    vnew_ref[bi, n_prev, :, j, :] = v
                kall_ref[slot, past:past + t, :] = k.astype(BF16)
                vall_ref[slot, past:past + t, :] = v.astype(BF16)

    q_scale = (HEAD_DIM ** -0.5) * LOG2E
    sub = min(t, ATT_SUB)

    def scores(r, hd):
        rows = slice(r * sub, (r + 1) * sub)
        cols = slice(hd * HEAD_DIM, (hd + 1) * HEAD_DIM)
        q = _rms(zq_ref[rows, cols], qn_ref[...])
        if latent:
            pos = pl.ds(pl.multiple_of(qi * tq + r * sub, sub), sub)
            q = _rope(q, cos_ref[pos, :], sin_ref[pos, :])
        qb = (q * q_scale).astype(BF16)
        slot = (r * sub // t) * hk + hd // ATT_GROUPS
        return lax.dot_general(qb, kall_ref[slot], NT_DIMS, preferred_element_type=F32)

    def finish(r, hd, s):
        j, g = divmod(hd, ATT_GROUPS)
        rows = slice(r * sub, (r + 1) * sub)
        cols = slice(hd * HEAD_DIM, (hd + 1) * HEAD_DIM)
        gcols = slice(g * HEAD_DIM, (g + 1) * HEAD_DIM)
        p = jnp.exp2(s - jnp.max(s, axis=-1, keepdims=True))
        den = jnp.sum(p, axis=-1, keepdims=True)
        o = jnp.dot(p.astype(BF16), vall_ref[(r * sub // t) * hk + j], preferred_element_type=F32) * (1.0 / den)
        o_ref[rows, cols] = (o * _silu(zg_refs[j][rows, gcols])).astype(BF16)

    items = [(r, hd) for r in range(tq // sub) for hd in range(hk * ATT_GROUPS)]
    pending = [scores(*it) for it in items[:ATT_AHEAD]]
    for n, it in enumerate(items):
        if n + ATT_AHEAD < len(items):
            pending.append(scores(*items[n + ATT_AHEAD]))
        finish(*it, pending.pop(0))


def _attention(z, q_norm_l, k_norm_l, *, b, t, rope=None, cache=None, prev_kv=None):
    latent = rope is not None
    hk = ATT_KV_HEADS
    bb = max(1, ATT_TQ // t)
    tq = min(t, ATT_TQ) * bb
    nq = bb * t // tq
    assert b % bb == 0 and (bb == 1 or nq == 1)
    gw = ATT_GROUPS * HEAD_DIM
    in_specs = [
        pl.BlockSpec((tq, hk * gw), lambda bi, qi: (bi * nq + qi, OFF_AQ // (hk * gw))),
        pl.BlockSpec((bb * t, hk * HEAD_DIM), lambda bi, qi: (bi, OFF_AK // (hk * HEAD_DIM))),
        pl.BlockSpec((bb * t, hk * HEAD_DIM), lambda bi, qi: (bi, OFF_AV // (hk * HEAD_DIM))),
    ]
    in_specs += [pl.BlockSpec((tq, gw), lambda bi, qi, j=j: (bi * nq + qi, OFF_AG // gw + j)) for j in range(hk)]
    in_specs += [pl.BlockSpec((1, HEAD_DIM), lambda bi, qi: (0, 0))] * 2
    args = [z] * (3 + hk) + [q_norm_l.reshape(1, HEAD_DIM), k_norm_l.reshape(1, HEAD_DIM)]
    tk = t
    if latent:
        cos, sin = rope
        ck, cv, layer = cache
        past = ck.shape[2]
        tk = past + t
        in_specs += [
            pl.BlockSpec((t, HEAD_DIM), lambda bi, qi: (0, 0)),
            pl.BlockSpec((t, HEAD_DIM), lambda bi, qi: (0, 0)),
            pl.BlockSpec((None, None, past, hk * HEAD_DIM), lambda bi, qi: (bi, layer, 0, 0)),
            pl.BlockSpec((None, None, past, hk * HEAD_DIM), lambda bi, qi: (bi, layer, 0, 0)),
        ]
        args += [cos, sin, ck, cv]
    out_specs = [pl.BlockSpec((tq, hk * gw), lambda bi, qi: (bi * nq + qi, 0))]
    out_shape = [jax.ShapeDtypeStruct((b * t, ATT_W), BF16)]
    n_prev = 0 if prev_kv is None else prev_kv[0].shape[1]
    if not latent:
        kv_spec = lambda layers: pl.BlockSpec((bb, layers, t, ATT_KV_HEADS, HEAD_DIM), lambda bi, qi: (bi, 0, 0, 0, 0))
        if n_prev:
            in_specs += [kv_spec(n_prev)] * 2
            args += list(prev_kv)
        out_specs += [kv_spec(n_prev + 1)] * 2
        out_shape += [jax.ShapeDtypeStruct((b, n_prev + 1, t, ATT_KV_HEADS, HEAD_DIM), F32)] * 2
    return pl.pallas_call(
        functools.partial(_attn_kernel, t=t, tq=tq, bb=bb, latent=latent, n_prev=n_prev),
        grid=(b // bb, nq),
        in_specs=in_specs,
        out_specs=out_specs,
        out_shape=out_shape,
        scratch_shapes=[pltpu.VMEM((bb * hk, tk, HEAD_DIM), BF16), pltpu.VMEM((bb * hk, tk, HEAD_DIM), BF16)],
        compiler_params=_cparams("parallel", "arbitrary"),
        name="attention",
    )(*args)


def _split3_dot(tri, x):
    hi = x.astype(BF16)
    r1 = x - hi.astype(F32)
    mid = r1.astype(BF16)
    lo = (r1 - mid.astype(F32)).astype(BF16)
    d = lambda a: jnp.dot(tri, a, preferred_element_type=F32)
    return d(hi) + d(mid) + d(lo)


def _mlstm_kernel(*refs, t, bb, zero_init, side_mod, n_prev):
    refs = list(refs)
    if side_mod:
        mo_ref = refs.pop()
        cond_ref, wm_ref, bm_ref = refs[4:7]
        refs = refs[:4] + refs[7:]
        _mod_kernel(cond_ref, wm_ref, bm_ref, mo_ref)
    if n_prev:
        c_prev, n_prev_ref, m_prev = refs[4:7]
        refs = refs[:4] + refs[7:]
        c_out, n_out, m_out = refs[-3:]
        c_out[:, 0:n_prev] = c_prev[...]
        n_out[:, 0:n_prev] = n_prev_ref[...]
        m_out[:, 0:n_prev] = m_prev[...]
    for bi in range(bb):
        _mlstm_one(bi, refs, t=t, zero_init=zero_init, n_prev=n_prev)


def _mlstm_one(bi, refs, *, t, zero_init, n_prev):
    if zero_init:
        z_ref, zif_ref, bif_ref, nw_ref, o_ref, c_out, n_out, m_out = refs
    else:
        z_ref, zif_ref, bif_ref, nw_ref, c0_ref, n0_ref, m0_ref, o_ref, c_out, n_out, m_out = refs
    seq = slice(bi * t, (bi + 1) * t)
    L = ML_CHUNK
    nc = t // L
    chunk = lambda a, c: a[c * L:(c + 1) * L]

    gates = zif_ref[seq, :] + bif_ref[...]
    logsig = (jnp.minimum(gates, 0.0) - jnp.log1p(jnp.exp(-jnp.abs(gates)))) * LOG2E
    gates = gates * LOG2E
    gates_t = gates.T
    r_i = lax.broadcasted_iota(jnp.int32, (L, L), 0)
    c_i = lax.broadcasted_iota(jnp.int32, (L, L), 1)
    lower, upper = r_i >= c_i, r_i <= c_i
    cum_tris = (lower.astype(BF16), upper.astype(BF16))
    dmasks = (upper, lower)
    cums = [[_split3_dot(cum_tris[d], chunk(logsig, c)) for c in range(nc)] for d in range(2)]
    cums_t = [[x.T for x in row] for row in cums]
    ext_rows = (lax.broadcasted_iota(jnp.int32, (ML_EXT - ML_DV, t), 0) == 0).astype(F32)
    lchunk = lambda a, c: a[:, c * L:(c + 1) * L]

    def running_max(a, d):
        rows = a.shape[0]
        if rows < 8:
            a = jnp.concatenate([a] * (8 // rows), axis=0)
        lane = lax.broadcasted_iota(jnp.int32, a.shape, 1)
        k = 1
        while k < L:
            if d:
                shifted = jnp.where(lane < L - k, pltpu.roll(a, L - k, axis=1), -jnp.inf)
            else:
                shifted = jnp.where(lane >= k, pltpu.roll(a, k, axis=1), -jnp.inf)
            a = jnp.maximum(a, shifted)
            k *= 2
        return a[:rows]

    hsl = lambda d, which: slice((2 * d + which) * ML_HEADS, (2 * d + which + 1) * ML_HEADS)
    a_maxes = [running_max(jnp.concatenate([lchunk(gates_t, c)[hsl(d, 0)] - cums_t[d][c][hsl(d, 1)]
                                            for c in range(nc)], axis=0), d) for d in range(2)]

    def local_part(qk_c, kb_c, vt_c, vtb_c, b_row, b_col, i_row, i_col, a_max, d):
        m_loc = b_row + a_max
        s = (qk_c * jnp.exp2(jnp.where(dmasks[d], (i_col - b_col) - a_max, -jnp.inf))).astype(BF16)
        bm = jnp.dot(vtb_c, s, preferred_element_type=F32)
        b_end = b_row[:, 0:1] if d else b_row[:, L - 1:L]
        g = b_end - b_row + i_row
        g_max = jnp.max(g, axis=1, keepdims=True)
        vw = (vt_c * jnp.exp2(g - g_max)).astype(BF16)
        u = jnp.dot(vw, kb_c, preferred_element_type=F32)
        return m_loc, bm, b_end, g_max, u

    def scan_step(qt_c, b_row, loc, c_t, m):
        m_loc, bm, b_end, g_max, u = loc
        inter = b_row + m
        m_row = jnp.maximum(inter, m_loc)
        nd = jnp.exp2(m_loc - m_row) * bm
        if c_t is not None:
            nd = nd + jnp.exp2(inter - m_row) * jnp.dot(c_t.astype(BF16), qt_c, preferred_element_type=F32)
        h_t = nd[:ML_DV] / jnp.maximum(jnp.abs(nd[ML_DV:ML_DV + 1]), jnp.exp2(-m_row))
        m_new = jnp.maximum(b_end + m, g_max)
        c_new = jnp.exp2(g_max - m_new) * u
        if c_t is not None:
            c_new = c_new + jnp.exp2(b_end + m - m_new) * c_t
        return h_t, c_new, m_new

    chains = [(hd, d) for hd in range(ML_HEADS) for d in range(2)]
    qts, b_rows, locs, state = {}, {}, {}, {}
    for hd in range(ML_HEADS):
        cols = slice(hd * ML_DK, (hd + 1) * ML_DK)
        kcols = slice(ML_W + hd * ML_DK, ML_W + (hd + 1) * ML_DK)
        vcols = slice(2 * ML_W + hd * ML_DV, 2 * ML_W + (hd + 1) * ML_DV)
        qt = z_ref[seq, cols].T.astype(BF16)
        kb = z_ref[seq, kcols].astype(BF16)
        vt = jnp.concatenate([z_ref[seq, vcols].T, ext_rows], axis=0)
        vtb = vt.astype(BF16)
        qk = [jnp.dot(chunk(kb, c), lchunk(qt, c), preferred_element_type=F32) for c in range(nc)]
        qts[hd] = qt
        for d in range(2):
            i_lane = 2 * d * ML_HEADS + hd
            f_lane = (2 * d + 1) * ML_HEADS + hd
            b_rows[hd, d] = [cums_t[d][c][f_lane:f_lane + 1, :] for c in range(nc)]
            locs[hd, d] = [local_part(qk[c], chunk(kb, c), lchunk(vt, c), lchunk(vtb, c), b_rows[hd, d][c],
                                      cums[d][c][:, f_lane:f_lane + 1], lchunk(gates_t, c)[i_lane:i_lane + 1, :],
                                      chunk(gates, c)[:, i_lane:i_lane + 1],
                                      a_maxes[d][c * ML_HEADS + hd:c * ML_HEADS + hd + 1], d) for c in range(nc)]
            if zero_init:
                state[hd, d] = (None, jnp.zeros((1, 1), F32))
            else:
                c_t = jnp.concatenate([c0_ref[d, hd].T, n0_ref[d, hd:hd + 1, :],
                                       jnp.zeros((ML_EXT - ML_DV - 1, ML_DK), F32)], axis=0)
                state[hd, d] = (c_t, m0_ref[d:d + 1, hd:hd + 1] * LOG2E)

    h_parts = {ch: [None] * nc for ch in chains}
    for step in range(nc):
        for hd, d in chains:
            c = nc - 1 - step if d else step
            c_t, m = state[hd, d]
            h_parts[hd, d][c], c_t, m = scan_step(lchunk(qts[hd], c), b_rows[hd, d][c], locs[hd, d][c], c_t, m)
            state[hd, d] = (c_t, m)

    for hd, d in chains:
        c_t, m = state[hd, d]
        c_out[bi, n_prev, d, hd] = c_t[:ML_DV].T
        n_out[bi, n_prev, d, hd:hd + 1, :] = c_t[ML_DV:ML_DV + 1]
        m_out[bi, n_prev, d:d + 1, hd:hd + 1] = m * LN2
    for hd in range(ML_HEADS):
        cols = slice(hd * ML_DK, (hd + 1) * ML_DK)
        ocols = slice(3 * ML_W + hd * ML_DV, 3 * ML_W + (hd + 1) * ML_DV)
        h_fwd, h_bwd = (jnp.concatenate(h_parts[hd, d], axis=1) if nc > 1 else h_parts[hd, d][0] for d in range(2))
        h_sum = h_fwd + h_bwd
        hm = _rms(h_sum.T, nw_ref[:, cols])
        o_ref[seq, cols] = (hm * z_ref[seq, ocols]).astype(BF16)


def _mlstm(z, b_if_l, ml_norm_l, *, b, t, state=None, side=None, prev=None):
    zero_init = state is None
    n_prev = 0 if prev is None else prev[0].shape[1]
    bb = max(1, ML_ROWS // t) if zero_init else 1
    assert b % bb == 0
    steps = b // bb
    zw = 5 * ML_W
    in_specs = [
        pl.BlockSpec((bb * t, zw), lambda bi: (bi, OFF_MQ // zw)),
        pl.BlockSpec((bb * t, LANES), lambda bi: (bi, OFF_IF // LANES)),
        pl.BlockSpec((1, LANES), lambda bi: (0, 0)),
        pl.BlockSpec((1, ML_W), lambda bi: (0, 0)),
    ]
    bif = jnp.pad(b_if_l, (0, LANES - N_GATES)).reshape(1, LANES)
    args = [z, z, bif, ml_norm_l.reshape(1, ML_W)]
    out_specs = [
        pl.BlockSpec((bb * t, ML_W), lambda bi: (bi, 0)),
        pl.BlockSpec((bb, n_prev + 1, 2, ML_HEADS, ML_DK, ML_DV), lambda bi: (bi, 0, 0, 0, 0, 0)),
        pl.BlockSpec((bb, n_prev + 1, 2, ML_HEADS, ML_DK), lambda bi: (bi, 0, 0, 0, 0)),
        pl.BlockSpec((bb, n_prev + 1, 2, ML_HEADS), lambda bi: (bi, 0, 0, 0)),
    ]
    out_shape = [
        jax.ShapeDtypeStruct((b * t, ML_W), BF16),
        jax.ShapeDtypeStruct((b, n_prev + 1, 2, ML_HEADS, ML_DK, ML_DV), F32),
        jax.ShapeDtypeStruct((b, n_prev + 1, 2, ML_HEADS, ML_DK), F32),
        jax.ShapeDtypeStruct((b, n_prev + 1, 2, ML_HEADS), F32),
    ]
    if side is not None:
        cond8, w_mod, b_mod, side_layer = side
        n = 3 * D_MODEL
        tn = n // steps
        assert tn % LANES == 0 and tn * steps == n
        in_specs += [
            pl.BlockSpec((8, D_MODEL), lambda bi: (0, 0)),
            pl.BlockSpec((None, D_MODEL, tn), lambda bi: (side_layer, 0, bi)),
            pl.BlockSpec((None, 1, tn), lambda bi: (side_layer, 0, bi)),
        ]
        args += [cond8, w_mod, b_mod.reshape(DEPTH, 1, n)]
        out_specs.append(pl.BlockSpec((8, tn), lambda bi: (0, bi)))
        out_shape.append(jax.ShapeDtypeStruct((8, n), F32))
    if n_prev:
        in_specs += [
            pl.BlockSpec((bb, n_prev, 2, ML_HEADS, ML_DK, ML_DV), lambda bi: (bi, 0, 0, 0, 0, 0)),
            pl.BlockSpec((bb, n_prev, 2, ML_HEADS, ML_DK), lambda bi: (bi, 0, 0, 0, 0)),
            pl.BlockSpec((bb, n_prev, 2, ML_HEADS), lambda bi: (bi, 0, 0, 0)),
        ]
        args += list(prev)
    if not zero_init:
        c0, n0, m0, layer = state
        in_specs += [
            pl.BlockSpec((None, None, 2, ML_HEADS, ML_DK, ML_DV), lambda bi: (bi, layer, 0, 0, 0, 0)),
            pl.BlockSpec((None, None, 2, ML_HEADS, ML_DK), lambda bi: (bi, layer, 0, 0, 0)),
            pl.BlockSpec((None, None, 2, ML_HEADS), lambda bi: (bi, layer, 0, 0)),
        ]
        args += [c0, n0, m0]
    return pl.pallas_call(
        functools.partial(_mlstm_kernel, t=t, bb=bb, zero_init=zero_init, side_mod=side is not None, n_prev=n_prev),
        grid=(steps,),
        in_specs=in_specs,
        out_specs=out_specs,
        out_shape=out_shape,
        compiler_params=_cparams("parallel"),
        name="mlstm",
    )(*args)


def _dft_tables(t):
    def cs(n):
        idx = np.arange(n, dtype=np.int64)
        ang = 2.0 * np.pi * ((idx[:, None] * idx[None, :]) % n).astype(np.float64) / n
        return np.cos(ang), np.sin(ang)
    cc, sc = cs(FO_GC)
    ct, st = cs(t)
    return (np.concatenate([cc, sc], axis=1).astype(np.float32),
            np.concatenate([ct, -st], axis=1).astype(np.float32))


FO_RT = 256


def _fourier_kernel(zx_ref, zga_ref, zgb_ref, wc_ref, wt_ref, wf_ref, o_ref, y_ref, *, t, bb):
    wc = wc_ref[...]
    for bi in range(bb):
        x = zx_ref[bi * t:(bi + 1) * t, FO_SHIFT:FO_SHIFT + FO_W]
        for g in range(FO_GROUPS):
            cols = slice(g * FO_GC, (g + 1) * FO_GC)
            ycols = slice(bi * FO_W + g * FO_GC, bi * FO_W + (g + 1) * FO_GC)
            y = jnp.dot(x[:, cols].astype(BF16), wc, preferred_element_type=F32)
            y_ref[0:t, ycols] = y[:, :FO_GC].astype(BF16)
            y_ref[t:2 * t, ycols] = y[:, FO_GC:].astype(BF16)
    scale = (t * FO_GC) ** -0.5
    for r in range(t // FO_RT):
        rows = slice(r * FO_RT, (r + 1) * FO_RT)
        f = jnp.dot(wt_ref[rows, :], y_ref[...], preferred_element_type=F32) * scale
        for bi in range(bb):
            orows = slice(bi * t + r * FO_RT, bi * t + (r + 1) * FO_RT)
            fg = jnp.concatenate([zga_ref[orows, FO_SHIFT:], zgb_ref[orows, :FO_SHIFT]], axis=1)
            for g in range(FO_GROUPS):
                cols = slice(g * FO_GC, (g + 1) * FO_GC)
                fcols = slice(bi * FO_W + g * FO_GC, bi * FO_W + (g + 1) * FO_GC)
                og = jnp.dot(f[:, fcols].astype(BF16), wf_ref[g].astype(BF16), preferred_element_type=F32)
                o_ref[orows, cols] = (og * _silu(fg[:, cols])).astype(BF16)


def _fourier(z, z_tail, w_fno_l, *, b, t):
    assert OFF_FG - FO_SHIFT + FO_W == Z_W and Z_TAIL == FO_SHIFT
    wc, wt = _dft_tables(t)
    xw = FO_W + LANES
    bb = max(1, min(b, ROW_TILE // t))
    return pl.pallas_call(
        functools.partial(_fourier_kernel, t=t, bb=bb),
        grid=(b // bb,),
        in_specs=[
            pl.BlockSpec((bb * t, xw), lambda bi: (bi, (OFF_FX - FO_SHIFT) // xw)),
            pl.BlockSpec((bb * t, FO_W), lambda bi: (bi, (OFF_FG - FO_SHIFT) // FO_W)),
            pl.BlockSpec((bb * t, LANES), lambda bi: (bi, 0)),
            pl.BlockSpec((FO_GC, 2 * FO_GC), lambda bi: (0, 0)),
            pl.BlockSpec((t, 2 * t), lambda bi: (0, 0)),
            pl.BlockSpec((FO_GROUPS, FO_GC, FO_GC), lambda bi: (0, 0, 0)),
        ],
        out_specs=pl.BlockSpec((bb * t, FO_W), lambda bi: (bi, 0)),
        out_shape=jax.ShapeDtypeStruct((b * t, FO_W), BF16),
        scratch_shapes=[pltpu.VMEM((2 * t, bb * FO_W), BF16)],
        compiler_params=_cparams("parallel"),
        name="fourier",
    )(z, z, z_tail, jnp.asarray(wc).astype(BF16), jnp.asarray(wt).astype(BF16), w_fno_l)


OUT_TN = 1024
OUT_TM = 1024


def _outproj_kernel(oa_ref, om_ref, of_ref, wa_ref, wm_ref, wf_ref, x_ref, gate_ref, y_ref, wb_ref, *, per_row_mod):
    i = pl.program_id(1)

    @pl.when(i == 0)
    def _():
        wb_ref[0:ATT_W, :] = wa_ref[...].astype(BF16)
        wb_ref[ATT_W:ATT_W + ML_W, :] = wm_ref[...].astype(BF16)
        wb_ref[ATT_W + ML_W:D_MIX, :] = wf_ref[...].astype(BF16)

    row = 1 + (i * OUT_TM) // ROW_TILE if per_row_mod else 0
    gate = gate_ref[pl.ds(row, 1), :]
    y = (jnp.dot(oa_ref[...], wb_ref[0:ATT_W, :], preferred_element_type=F32)
         + jnp.dot(om_ref[...], wb_ref[ATT_W:ATT_W + ML_W, :], preferred_element_type=F32)
         + jnp.dot(of_ref[...], wb_ref[ATT_W + ML_W:D_MIX, :], preferred_element_type=F32))
    y_ref[...] = x_ref[...] + gate * y


def _outproj(o_att, o_ml, o_fo, w_out, layer, x2d, mod_l, *, per_row_mod):
    m = x2d.shape[0]
    return pl.pallas_call(
        functools.partial(_outproj_kernel, per_row_mod=per_row_mod),
        grid=(D_MODEL // OUT_TN, m // OUT_TM),
        in_specs=[
            pl.BlockSpec((OUT_TM, ATT_W), lambda n, i: (i, 0)),
            pl.BlockSpec((OUT_TM, ML_W), lambda n, i: (i, 0)),
            pl.BlockSpec((OUT_TM, FO_W), lambda n, i: (i, 0)),
            pl.BlockSpec((None, ATT_W, OUT_TN), lambda n, i: (layer, 0, n)),
            pl.BlockSpec((None, ML_W, OUT_TN), lambda n, i: (layer, ATT_W // ML_W, n)),
            pl.BlockSpec((None, FO_W, OUT_TN), lambda n, i: (layer, (ATT_W + ML_W) // FO_W, n)),
            pl.BlockSpec((OUT_TM, OUT_TN), lambda n, i: (i, n)),
            pl.BlockSpec((8, OUT_TN), lambda n, i: (0, 2 * D_MODEL // OUT_TN + n)),
        ],
        out_specs=pl.BlockSpec((OUT_TM, OUT_TN), lambda n, i: (i, n)),
        out_shape=jax.ShapeDtypeStruct((m, D_MODEL), F32),
        scratch_shapes=[pltpu.VMEM((D_MIX, OUT_TN), BF16)],
        compiler_params=_cparams("parallel", "arbitrary"),
        name="outproj",
    )(o_att, o_ml, o_fo, w_out, w_out, w_out, x2d, mod_l)


def _rope_tables(t):
    half = HEAD_DIM // 2
    inv_freq = ROPE_BASE ** (-jnp.arange(0, half, 2, dtype=F32) / half)
    n_rows = t // GRID_W
    rows = jnp.repeat(jnp.arange(n_rows, dtype=F32), GRID_W)
    cols = jnp.tile(jnp.arange(GRID_W, dtype=F32), n_rows)
    ar = rows[:, None] * inv_freq
    ac = cols[:, None] * inv_freq
    cos = jnp.concatenate([jnp.cos(ar), jnp.cos(ar), jnp.cos(ac), jnp.cos(ac)], axis=1)
    sin = jnp.concatenate([-jnp.sin(ar), jnp.sin(ar), -jnp.sin(ac), jnp.sin(ac)], axis=1)
    return cos, sin


def _layer(x2d, mod_l, lw, layer, *, b, t, latent, rope=None, cache=None, state=None, side=None, prev=None):
    norm_w, w_in_t, b_if, conv_w, conv_b, q_norm, k_norm, ml_norm, w_fno, w_out = lw
    prev_kv, prev_state = (None, None) if prev is None else (prev[:2], prev[2:])
    z, z_tail = _inproj(x2d, mod_l, norm_w, w_in_t, layer, conv_w, conv_b, per_row_mod=latent, t=t)
    o_att, *kv_new = _attention(z, q_norm, k_norm, b=b, t=t, rope=rope, cache=cache, prev_kv=prev_kv)
    o_ml, c_f, n_f, m_f, *side_mod = _mlstm(z, b_if, ml_norm, b=b, t=t, state=state, side=side, prev=prev_state)
    o_fo = _fourier(z, z_tail, w_fno, b=b, t=t)
    y = _outproj(o_att, o_ml, o_fo, w_out, layer, x2d, mod_l, per_row_mod=latent)
    return y, (*kv_new, c_f, n_f, m_f), side_mod


def kernel(x_prompt, x_sample, cache_k, cache_v, state_C, state_n, state_m, c, c_ctx, norm_w, w_mod, b_mod, w_in,
           b_if, conv_w, conv_b, q_norm, k_norm, ml_norm, w_fno, w_out):
    bp, tp, _ = x_prompt.shape
    bs, ts, _ = x_sample.shape
    assert tp % ML_CHUNK == 0 and ts == ROW_TILE and (bp * tp) % ROW_TILE == 0 and bs + 1 <= 8
    past = cache_k.shape[2]

    cond8 = jnp.concatenate([c_ctx[None, :], c, jnp.zeros((8 - 1 - bs, D_MODEL), F32)], axis=0)
    mod_l = _modulation(cond8, w_mod, b_mod, 0)

    rope = _rope_tables(ts)
    ck = cache_k.reshape(bs, DEPTH, past, ATT_KV_W)
    cv = cache_v.reshape(bs, DEPTH, past, ATT_KV_W)
    w_in_t = jnp.swapaxes(w_in, 1, 2)

    xp = x_prompt.reshape(bp * tp, D_MODEL)
    xs = x_sample.reshape(bs * ts, D_MODEL)
    new = None
    for l in range(DEPTH):
        lw = (norm_w[l], w_in_t, b_if[l], conv_w[l], conv_b[l], q_norm[l], k_norm[l], ml_norm[l], w_fno[l], w_out)
        side = (cond8, w_mod, b_mod, l + 1) if l + 1 < DEPTH else None
        xp, new, mod_next = _layer(xp, mod_l, lw, l, b=bp, t=tp, latent=False, side=side, prev=new)
        xs, _, _ = _layer(xs, mod_l, lw, l, b=bs, t=ts, latent=True, rope=rope,
                          cache=(ck, cv, l), state=(state_C, state_n, state_m, l))
        if mod_next:
            mod_l = mod_next[0]
    return (xp.reshape(bp, tp, D_MODEL), xs.reshape(bs, ts, D_MODEL), *new)
```

```python
import functools

import numpy as np
import jax
import jax.numpy as jnp
from jax import lax
from jax.experimental import pallas as pl
from jax.experimental.pallas import tpu as pltpu

D_MODEL = 2048
DEPTH = 2
GRID_W = 64
HEAD_DIM = 128
ATT_HEADS = 8
ATT_KV_HEADS = 2
ATT_GROUPS = ATT_HEADS // ATT_KV_HEADS
ATT_W = ATT_HEADS * HEAD_DIM
ATT_KV_W = ATT_KV_HEADS * HEAD_DIM
ML_HEADS = 4
ML_DK = 128
ML_DV = 128
ML_W = ML_HEADS * ML_DV
FO_GROUPS = 4
FO_GC = 128
FO_W = FO_GROUPS * FO_GC
D_MIX = ATT_W + ML_W + FO_W
N_GATES = 4 * ML_HEADS
D_PROJ = 2 * ATT_W + 2 * ATT_KV_W + 5 * ML_W + N_GATES + 2 * FO_W
CONV_W = 3
ROPE_BASE = 10000.0
EPS = 1e-6

LANES = 128
MXU_N = 256
OFF_AQ = 0
OFF_AK = OFF_AQ + ATT_W
OFF_AV = OFF_AK + ATT_KV_W
OFF_AG = OFF_AV + ATT_KV_W
OFF_MQ = OFF_AG + ATT_W
OFF_MK = OFF_MQ + ML_W
OFF_MV = OFF_MK + ML_W
OFF_MO = OFF_MV + ML_W
OFF_MG = OFF_MO + ML_W
OFF_IF = OFF_MG + ML_W
OFF_FX = OFF_IF + N_GATES
OFF_FG = OFF_FX + FO_W
FO_SHIFT = OFF_FX % LANES

PROJ_TN = 1024
Z_W = (D_PROJ // PROJ_TN) * PROJ_TN
Z_TAIL = D_PROJ - Z_W
ROW_TILE = 1024
ML_CHUNK = 256
ML_EXT = ML_DV + 16
ML_ROWS = 512
ATT_TQ = 512
ATT_SUB = 256
ATT_AHEAD = 1
VMEM_LIMIT = 56 * 1024 * 1024

BF16 = jnp.bfloat16
F32 = jnp.float32
NT_DIMS = (((1,), (1,)), ((), ()))


def _cparams(*sem):
    return pltpu.CompilerParams(dimension_semantics=sem, vmem_limit_bytes=VMEM_LIMIT)


def _silu(x):
    return x * (1.0 / (1.0 + jnp.exp(-x)))


def _sigmoid(x):
    return 1.0 / (1.0 + jnp.exp(-x))


def _rms(x, w):
    ms = jnp.mean(x * x, axis=-1, keepdims=True)
    return x * lax.rsqrt(ms + EPS) * w


MOD_TN = 768


def _mod_kernel(cond_ref, w_ref, b_ref, o_ref):
    a = _silu(cond_ref[...]).astype(BF16)
    o_ref[...] = jnp.dot(a, w_ref[...].astype(BF16), preferred_element_type=F32) + b_ref[...]


def _modulation(cond8, w_mod, b_mod, layer):
    n = 3 * D_MODEL
    return pl.pallas_call(
        _mod_kernel,
        grid=(n // MOD_TN,),
        in_specs=[
            pl.BlockSpec((8, D_MODEL), lambda j: (0, 0)),
            pl.BlockSpec((None, D_MODEL, MOD_TN), lambda j: (layer, 0, j)),
            pl.BlockSpec((None, 1, MOD_TN), lambda j: (layer, 0, j)),
        ],
        out_specs=pl.BlockSpec((8, MOD_TN), lambda j: (0, j)),
        out_shape=jax.ShapeDtypeStruct((8, n), F32),
        compiler_params=_cparams("parallel"),
        name="modulation",
    )(cond8, w_mod, b_mod.reshape(DEPTH, 1, n))


NORM_ROWS = 16


def _shift_rows(x, direction, period):
    n = x.shape[0]
    pos = lax.broadcasted_iota(jnp.int32, x.shape, 0) % period
    if direction > 0:
        return jnp.where(pos == 0, 0.0, pltpu.roll(x, 1, axis=0))
    return jnp.where(pos == period - 1, 0.0, pltpu.roll(x, n - 1, axis=0))


def _conv_silu(x, w, b, period):
    y = _shift_rows(x, 1, period) * w[0:1, :] + x * w[1:2, :] + _shift_rows(x, -1, period) * w[2:3, :] + b
    return _silu(y)


def _inproj_kernel(x_ref, mod_ref, nw_ref, w_ref, wt_ref, cw_ref, cb_ref, z_ref, zt_ref, h_ref, *, per_row_mod, t):
    i = pl.program_id(0)
    j = pl.program_id(1)

    @pl.when(j == 0)
    def _():
        row = 1 + i if per_row_mod else 0
        shift = mod_ref[pl.ds(row, 1), 0:D_MODEL]
        gain = nw_ref[...] * (1.0 + mod_ref[pl.ds(row, 1), D_MODEL:2 * D_MODEL])

        def body(r, carry):
            sl = pl.ds(pl.multiple_of(r * NORM_ROWS, NORM_ROWS), NORM_ROWS)
            x = x_ref[sl, :]
            inv = lax.rsqrt(jnp.mean(x * x, axis=-1, keepdims=True) + EPS)
            h_ref[sl, :] = (x * inv * gain + shift).astype(BF16)
            return carry

        lax.fori_loop(0, ROW_TILE // NORM_ROWS, body, 0, unroll=8)

    def project(cols=slice(0, PROJ_TN)):
        return lax.dot_general(h_ref[...], w_ref[cols, :].astype(BF16), NT_DIMS, preferred_element_type=F32)

    half = PROJ_TN // 2
    pieces = lambda start: [slice(c, c + MXU_N) for c in range(start, start + half, MXU_N)]
    j_q, j_k, j_o = OFF_MQ // PROJ_TN, OFF_MK // PROJ_TN, OFF_MO // PROJ_TN

    def conv_tile(start, raw_start, conv_off, scale):
        for cols, rcols in zip(pieces(start), pieces(raw_start)):
            cc = slice(conv_off + cols.start - start, conv_off + cols.stop - start)
            y = _conv_silu(project(cols), cw_ref[:, cc], cb_ref[:, cc], t)
            z_ref[:, cols] = y if scale is None else y * scale
            z_ref[:, rcols] = project(rcols)

    def tail():
        zt = lax.dot_general(h_ref[...], wt_ref[...].astype(BF16), NT_DIMS, preferred_element_type=F32)
        col = lax.broadcasted_iota(jnp.int32, zt.shape, 1)
        zt_ref[...] = jnp.where(col < Z_TAIL, zt, 0.0)

    @pl.when(j == j_q)
    def _():
        tail()
        conv_tile(half, 0, 0, None)

    @pl.when(j == j_k)
    def _():
        conv_tile(0, half, ML_W, ML_DK ** -0.5)

    @pl.when(j == j_o)
    def _():
        z_ref[...] = project()
        z_ref[:, 0:half] = _sigmoid(z_ref[:, 0:half]) * _silu(z_ref[:, half:PROJ_TN])

    @pl.when((j != j_q) & (j != j_k) & (j != j_o))
    def _():
        z_ref[...] = project()


def _inproj(x2d, mod_l, norm_w_l, w_in_t, layer, conv_w_l, conv_b_l, *, per_row_mod, t):
    m = x2d.shape[0]
    half = PROJ_TN // 2
    assert OFF_MQ % PROJ_TN == half and OFF_MK % PROJ_TN == 0 and OFF_MO % PROJ_TN == 0 and ML_W == half
    assert ROW_TILE % t == 0
    return pl.pallas_call(
        functools.partial(_inproj_kernel, per_row_mod=per_row_mod, t=t),
        grid=(m // ROW_TILE, Z_W // PROJ_TN),
        in_specs=[
            pl.BlockSpec((ROW_TILE, D_MODEL), lambda i, j: (i, 0)),
            pl.BlockSpec((8, 3 * D_MODEL), lambda i, j: (0, 0)),
            pl.BlockSpec((1, D_MODEL), lambda i, j: (0, 0)),
            pl.BlockSpec((None, PROJ_TN, D_MODEL), lambda i, j: (layer, j, 0)),
            pl.BlockSpec((None, LANES, D_MODEL), lambda i, j: (layer, Z_W // LANES, 0)),
            pl.BlockSpec((CONV_W, 2 * ML_W), lambda i, j: (0, 0)),
            pl.BlockSpec((1, 2 * ML_W), lambda i, j: (0, 0)),
        ],
        out_specs=[
            pl.BlockSpec((ROW_TILE, PROJ_TN), lambda i, j: (i, j)),
            pl.BlockSpec((ROW_TILE, LANES), lambda i, j: (i, 0)),
        ],
        out_shape=[jax.ShapeDtypeStruct((m, Z_W), F32), jax.ShapeDtypeStruct((m, LANES), F32)],
        scratch_shapes=[pltpu.VMEM((ROW_TILE, D_MODEL), BF16)],
        compiler_params=_cparams("parallel", "arbitrary"),
        name="inproj",
    )(x2d, mod_l, norm_w_l.reshape(1, D_MODEL), w_in_t, w_in_t, conv_w_l, conv_b_l.reshape(1, 2 * ML_W))


def _rope(x, cos, sin):
    lane = lax.broadcasted_iota(jnp.int32, x.shape, 1)
    quarter = HEAD_DIM // 4
    partner = jnp.where((lane % (2 * quarter)) < quarter,
                        pltpu.roll(x, HEAD_DIM - quarter, axis=1), pltpu.roll(x, quarter, axis=1))
    return x * cos + partner * sin


LOG2E = 1.4426950408889634
LN2 = 0.6931471805599453


def _attn_kernel(*refs, t, tq, bb, latent, n_prev):
    hk = ATT_KV_HEADS
    zq_ref, zk_ref, zv_ref = refs[:3]
    zg_refs = refs[3:3 + hk]
    rest = refs[3 + hk:]
    if latent:
        qn_ref, kn_ref, cos_ref, sin_ref, ck_ref, cv_ref, o_ref, kall_ref, vall_ref = rest
        past = ck_ref.shape[0]
    else:
        qn_ref, kn_ref, *prev_kv, o_ref, knew_ref, vnew_ref, kall_ref, vall_ref = rest
        past = 0
    qi = pl.program_id(1)

    @pl.when(qi == 0)
    def _():
        if n_prev:
            knew_ref[:, 0:n_prev] = prev_kv[0][...]
            vnew_ref[:, 0:n_prev] = prev_kv[1][...]
        for bi in range(bb):
            seq = slice(bi * t, (bi + 1) * t)
            for j in range(hk):
                kc = slice(j * HEAD_DIM, (j + 1) * HEAD_DIM)
                slot = bi * hk + j
                k = _rms(zk_ref[seq, kc], kn_ref[...])
                v = zv_ref[seq, kc]
                if latent:
                    k = _rope(k, cos_ref[...], sin_ref[...])
                    kall_ref[slot, 0:past, :] = ck_ref[:, kc].astype(BF16)
                    vall_ref[slot, 0:past, :] = cv_ref[:, kc].astype(BF16)
                else:
                    knew_ref[bi, n_prev, :, j, :] = k
                    vnew_ref[bi, n_prev, :, j, :] = v
                kall_ref[slot, past:past + t, :] = k.astype(BF16)
                vall_ref[slot, past:past + t, :] = v.astype(BF16)

    q_scale = (HEAD_DIM ** -0.5) * LOG2E
    sub = min(t, ATT_SUB)

    def scores(r, hd):
        rows = slice(r * sub, (r + 1) * sub)
        cols = slice(hd * HEAD_DIM, (hd + 1) * HEAD_DIM)
        q = _rms(zq_ref[rows, cols], qn_ref[...])
        if latent:
            pos = pl.ds(pl.multiple_of(qi * tq + r * sub, sub), sub)
            q = _rope(q, cos_ref[pos, :], sin_ref[pos, :])
        qb = (q * q_scale).astype(BF16)
        slot = (r * sub // t) * hk + hd // ATT_GROUPS
        return lax.dot_general(qb, kall_ref[slot], NT_DIMS, preferred_element_type=F32)

    def finish(r, hd, s):
        j, g = divmod(hd, ATT_GROUPS)
        rows = slice(r * sub, (r + 1) * sub)
        cols = slice(hd * HEAD_DIM, (hd + 1) * HEAD_DIM)
        gcols = slice(g * HEAD_DIM, (g + 1) * HEAD_DIM)
        p = jnp.exp2(s - jnp.max(s, axis=-1, keepdims=True))
        den = jnp.sum(p, axis=-1, keepdims=True)
        o = jnp.dot(p.astype(BF16), vall_ref[(r * sub // t) * hk + j], preferred_element_type=F32) * (1.0 / den)
        o_ref[rows, cols] = (o * _silu(zg_refs[j][rows, gcols])).astype(BF16)

    items = [(r, hd) for r in range(tq // sub) for hd in range(hk * ATT_GROUPS)]
    pending = [scores(*it) for it in items[:ATT_AHEAD]]
    for n, it in enumerate(items):
        if n + ATT_AHEAD < len(items):
            pending.append(scores(*items[n + ATT_AHEAD]))
        finish(*it, pending.pop(0))


def _attention(z, q_norm_l, k_norm_l, *, b, t, rope=None, cache=None, prev_kv=None):
    latent = rope is not None
    hk = ATT_KV_HEADS
    bb = max(1, ATT_TQ // t)
    tq = min(t, ATT_TQ) * bb
    nq = bb * t // tq
    assert b % bb == 0 and (bb == 1 or nq == 1)
    gw = ATT_GROUPS * HEAD_DIM
    in_specs = [
        pl.BlockSpec((tq, hk * gw), lambda bi, qi: (bi * nq + qi, OFF_AQ // (hk * gw))),
        pl.BlockSpec((bb * t, hk * HEAD_DIM), lambda bi, qi: (bi, OFF_AK // (hk * HEAD_DIM))),
        pl.BlockSpec((bb * t, hk * HEAD_DIM), lambda bi, qi: (bi, OFF_AV // (hk * HEAD_DIM))),
    ]
    in_specs += [pl.BlockSpec((tq, gw), lambda bi, qi, j=j: (bi * nq + qi, OFF_AG // gw + j)) for j in range(hk)]
    in_specs += [pl.BlockSpec((1, HEAD_DIM), lambda bi, qi: (0, 0))] * 2
    args = [z] * (3 + hk) + [q_norm_l.reshape(1, HEAD_DIM), k_norm_l.reshape(1, HEAD_DIM)]
    tk = t
    if latent:
        cos, sin = rope
        ck, cv, layer = cache
        past = ck.shape[2]
        tk = past + t
        in_specs += [
            pl.BlockSpec((t, HEAD_DIM), lambda bi, qi: (0, 0)),
            pl.BlockSpec((t, HEAD_DIM), lambda bi, qi: (0, 0)),
            pl.BlockSpec((None, None, past, hk * HEAD_DIM), lambda bi, qi: (bi, layer, 0, 0)),
            pl.BlockSpec((None, None, past, hk * HEAD_DIM), lambda bi, qi: (bi, layer, 0, 0)),
        ]
        args += [cos, sin, ck, cv]
    out_specs = [pl.BlockSpec((tq, hk * gw), lambda bi, qi: (bi * nq + qi, 0))]
    out_shape = [jax.ShapeDtypeStruct((b * t, ATT_W), BF16)]
    n_prev = 0 if prev_kv is None else prev_kv[0].shape[1]
    if not latent:
        kv_spec = lambda layers: pl.BlockSpec((bb, layers, t, ATT_KV_HEADS, HEAD_DIM), lambda bi, qi: (bi, 0, 0, 0, 0))
        if n_prev:
            in_specs += [kv_spec(n_prev)] * 2
            args += list(prev_kv)
        out_specs += [kv_spec(n_prev + 1)] * 2
        out_shape += [jax.ShapeDtypeStruct((b, n_prev + 1, t, ATT_KV_HEADS, HEAD_DIM), F32)] * 2
    return pl.pallas_call(
        functools.partial(_attn_kernel, t=t, tq=tq, bb=bb, latent=latent, n_prev=n_prev),
        grid=(b // bb, nq),
        in_specs=in_specs,
        out_specs=out_specs,
        out_shape=out_shape,
        scratch_shapes=[pltpu.VMEM((bb * hk, tk, HEAD_DIM), BF16), pltpu.VMEM((bb * hk, tk, HEAD_DIM), BF16)],
        compiler_params=_cparams("parallel", "arbitrary"),
        name="attention",
    )(*args)


def _split3_dot(tri, x):
    hi = x.astype(BF16)
    r1 = x - hi.astype(F32)
    mid = r1.astype(BF16)
    lo = (r1 - mid.astype(F32)).astype(BF16)
    d = lambda a: jnp.dot(tri, a, preferred_element_type=F32)
    return d(hi) + d(mid) + d(lo)


def _mlstm_kernel(*refs, t, bb, zero_init, side_mod, n_prev):
    refs = list(refs)
    if side_mod:
        mo_ref = refs.pop()
        cond_ref, wm_ref, bm_ref = refs[4:7]
        refs = refs[:4] + refs[7:]
        _mod_kernel(cond_ref, wm_ref, bm_ref, mo_ref)
    if n_prev:
        c_prev, n_prev_ref, m_prev = refs[4:7]
        refs = refs[:4] + refs[7:]
        c_out, n_out, m_out = refs[-3:]
        c_out[:, 0:n_prev] = c_prev[...]
        n_out[:, 0:n_prev] = n_prev_ref[...]
        m_out[:, 0:n_prev] = m_prev[...]
    for bi in range(bb):
        _mlstm_one(bi, refs, t=t, zero_init=zero_init, n_prev=n_prev)


def _mlstm_one(bi, refs, *, t, zero_init, n_prev):
    if zero_init:
        z_ref, zif_ref, bif_ref, nw_ref, o_ref, c_out, n_out, m_out = refs
    else:
        z_ref, zif_ref, bif_ref, nw_ref, c0_ref, n0_ref, m0_ref, o_ref, c_out, n_out, m_out = refs
    seq = slice(bi * t, (bi + 1) * t)
    L = ML_CHUNK
    nc = t // L
    chunk = lambda a, c: a[c * L:(c + 1) * L]

    gates = zif_ref[seq, :] + bif_ref[...]
    logsig = (jnp.minimum(gates, 0.0) - jnp.log1p(jnp.exp(-jnp.abs(gates)))) * LOG2E
    gates = gates * LOG2E
    gates_t = gates.T
    r_i = lax.broadcasted_iota(jnp.int32, (L, L), 0)
    c_i = lax.broadcasted_iota(jnp.int32, (L, L), 1)
    lower, upper = r_i >= c_i, r_i <= c_i
    cum_tris = (lower.astype(BF16), upper.astype(BF16))
    dmasks = (upper, lower)
    cums = [[_split3_dot(cum_tris[d], chunk(logsig, c)) for c in range(nc)] for d in range(2)]
    cums_t = [[x.T for x in row] for row in cums]
    ext_rows = (lax.broadcasted_iota(jnp.int32, (ML_EXT - ML_DV, t), 0) == 0).astype(F32)
    lchunk = lambda a, c: a[:, c * L:(c + 1) * L]

    def running_max(a, d):
        rows = a.shape[0]
        if rows < 8:
            a = jnp.concatenate([a] * (8 // rows), axis=0)
        lane = lax.broadcasted_iota(jnp.int32, a.shape, 1)
        k = 1
        while k < L:
            if d:
                shifted = jnp.where(lane < L - k, pltpu.roll(a, L - k, axis=1), -jnp.inf)
            else:
                shifted = jnp.where(lane >= k, pltpu.roll(a, k, axis=1), -jnp.inf)
            a = jnp.maximum(a, shifted)
            k *= 2
        return a[:rows]

    hsl = lambda d, which: slice((2 * d + which) * ML_HEADS, (2 * d + which + 1) * ML_HEADS)
    a_maxes = [running_max(jnp.concatenate([lchunk(gates_t, c)[hsl(d, 0)] - cums_t[d][c][hsl(d, 1)]
                                            for c in range(nc)], axis=0), d) for d in range(2)]

    def local_part(qk_c, kb_c, vt_c, vtb_c, b_row, b_col, i_row, i_col, a_max, d):
        m_loc = b_row + a_max
        s = (qk_c * jnp.exp2(jnp.where(dmasks[d], (i_col - b_col) - a_max, -jnp.inf))).astype(BF16)
        bm = jnp.dot(vtb_c, s, preferred_element_type=F32)
        b_end = b_row[:, 0:1] if d else b_row[:, L - 1:L]
        g = b_end - b_row + i_row
        g_max = jnp.max(g, axis=1, keepdims=True)
        vw = (vt_c * jnp.exp2(g - g_max)).astype(BF16)
        u = jnp.dot(vw, kb_c, preferred_element_type=F32)
        return m_loc, bm, b_end, g_max, u

    def scan_step(qt_c, b_row, loc, c_t, m):
        m_loc, bm, b_end, g_max, u = loc
        inter = b_row + m
        m_row = jnp.maximum(inter, m_loc)
        nd = jnp.exp2(m_loc - m_row) * bm
        if c_t is not None:
            nd = nd + jnp.exp2(inter - m_row) * jnp.dot(c_t.astype(BF16), qt_c, preferred_element_type=F32)
        h_t = nd[:ML_DV] / jnp.maximum(jnp.abs(nd[ML_DV:ML_DV + 1]), jnp.exp2(-m_row))
        m_new = jnp.maximum(b_end + m, g_max)
        c_new = jnp.exp2(g_max - m_new) * u
        if c_t is not None:
            c_new = c_new + jnp.exp2(b_end + m - m_new) * c_t
        return h_t, c_new, m_new

    chains = [(hd, d) for hd in range(ML_HEADS) for d in range(2)]
    qts, b_rows, locs, state = {}, {}, {}, {}
    for hd in range(ML_HEADS):
        cols = slice(hd * ML_DK, (hd + 1) * ML_DK)
        kcols = slice(ML_W + hd * ML_DK, ML_W + (hd + 1) * ML_DK)
        vcols = slice(2 * ML_W + hd * ML_DV, 2 * ML_W + (hd + 1) * ML_DV)
        qt = z_ref[seq, cols].T.astype(BF16)
        kb = z_ref[seq, kcols].astype(BF16)
        vt = jnp.concatenate([z_ref[seq, vcols].T, ext_rows], axis=0)
        vtb = vt.astype(BF16)
        qk = [jnp.dot(chunk(kb, c), lchunk(qt, c), preferred_element_type=F32) for c in range(nc)]
        qts[hd] = qt
        for d in range(2):
            i_lane = 2 * d * ML_HEADS + hd
            f_lane = (2 * d + 1) * ML_HEADS + hd
            b_rows[hd, d] = [cums_t[d][c][f_lane:f_lane + 1, :] for c in range(nc)]
            locs[hd, d] = [local_part(qk[c], chunk(kb, c), lchunk(vt, c), lchunk(vtb, c), b_rows[hd, d][c],
                                      cums[d][c][:, f_lane:f_lane + 1], lchunk(gates_t, c)[i_lane:i_lane + 1, :],
                                      chunk(gates, c)[:, i_lane:i_lane + 1],
                                      a_maxes[d][c * ML_HEADS + hd:c * ML_HEADS + hd + 1], d) for c in range(nc)]
            if zero_init:
                state[hd, d] = (None, jnp.zeros((1, 1), F32))
            else:
                c_t = jnp.concatenate([c0_ref[d, hd].T, n0_ref[d, hd:hd + 1, :],
                                       jnp.zeros((ML_EXT - ML_DV - 1, ML_DK), F32)], axis=0)
                state[hd, d] = (c_t, m0_ref[d:d + 1, hd:hd + 1] * LOG2E)

    h_parts = {ch: [None] * nc for ch in chains}
    for step in range(nc):
        for hd, d in chains:
            c = nc - 1 - step if d else step
            c_t, m = state[hd, d]
            h_parts[hd, d][c], c_t, m = scan_step(lchunk(qts[hd], c), b_rows[hd, d][c], locs[hd, d][c], c_t, m)
            state[hd, d] = (c_t, m)

    for hd, d in chains:
        c_t, m = state[hd, d]
        c_out[bi, n_prev, d, hd] = c_t[:ML_DV].T
        n_out[bi, n_prev, d, hd:hd + 1, :] = c_t[ML_DV:ML_DV + 1]
        m_out[bi, n_prev, d:d + 1, hd:hd + 1] = m * LN2
    for hd in range(ML_HEADS):
        cols = slice(hd * ML_DK, (hd + 1) * ML_DK)
        ocols = slice(3 * ML_W + hd * ML_DV, 3 * ML_W + (hd + 1) * ML_DV)
        h_fwd, h_bwd = (jnp.concatenate(h_parts[hd, d], axis=1) if nc > 1 else h_parts[hd, d][0] for d in range(2))
        h_sum = h_fwd + h_bwd
        hm = _rms(h_sum.T, nw_ref[:, cols])
        o_ref[seq, cols] = (hm * z_ref[seq, ocols]).astype(BF16)


def _mlstm(z, b_if_l, ml_norm_l, *, b, t, bb, state=None, side=None, prev=None):
    zero_init = state is None
    n_prev = 0 if prev is None else prev[0].shape[1]
    assert b % bb == 0 and (zero_init or bb == 1)
    steps = b // bb
    zw = 5 * ML_W
    in_specs = [
        pl.BlockSpec((bb * t, zw), lambda bi: (bi, OFF_MQ // zw)),
        pl.BlockSpec((bb * t, LANES), lambda bi: (bi, OFF_IF // LANES)),
        pl.BlockSpec((1, LANES), lambda bi: (0, 0)),
        pl.BlockSpec((1, ML_W), lambda bi: (0, 0)),
    ]
    bif = jnp.pad(b_if_l, (0, LANES - N_GATES)).reshape(1, LANES)
    args = [z, z, bif, ml_norm_l.reshape(1, ML_W)]
    out_specs = [
        pl.BlockSpec((bb * t, ML_W), lambda bi: (bi, 0)),
        pl.BlockSpec((bb, n_prev + 1, 2, ML_HEADS, ML_DK, ML_DV), lambda bi: (bi, 0, 0, 0, 0, 0)),
        pl.BlockSpec((bb, n_prev + 1, 2, ML_HEADS, ML_DK), lambda bi: (bi, 0, 0, 0, 0)),
        pl.BlockSpec((bb, n_prev + 1, 2, ML_HEADS), lambda bi: (bi, 0, 0, 0)),
    ]
    out_shape = [
        jax.ShapeDtypeStruct((b * t, ML_W), BF16),
        jax.ShapeDtypeStruct((b, n_prev + 1, 2, ML_HEADS, ML_DK, ML_DV), F32),
        jax.ShapeDtypeStruct((b, n_prev + 1, 2, ML_HEADS, ML_DK), F32),
        jax.ShapeDtypeStruct((b, n_prev + 1, 2, ML_HEADS), F32),
    ]
    if side is not None:
        cond8, w_mod, b_mod, side_layer = side
        n = 3 * D_MODEL
        tn = n // steps
        assert tn % LANES == 0 and tn * steps == n
        in_specs += [
            pl.BlockSpec((8, D_MODEL), lambda bi: (0, 0)),
            pl.BlockSpec((None, D_MODEL, tn), lambda bi: (side_layer, 0, bi)),
            pl.BlockSpec((None, 1, tn), lambda bi: (side_layer, 0, bi)),
        ]
        args += [cond8, w_mod, b_mod.reshape(DEPTH, 1, n)]
        out_specs.append(pl.BlockSpec((8, tn), lambda bi: (0, bi)))
        out_shape.append(jax.ShapeDtypeStruct((8, n), F32))
    if n_prev:
        in_specs += [
            pl.BlockSpec((bb, n_prev, 2, ML_HEADS, ML_DK, ML_DV), lambda bi: (bi, 0, 0, 0, 0, 0)),
            pl.BlockSpec((bb, n_prev, 2, ML_HEADS, ML_DK), lambda bi: (bi, 0, 0, 0, 0)),
            pl.BlockSpec((bb, n_prev, 2, ML_HEADS), lambda bi: (bi, 0, 0, 0)),
        ]
        args += list(prev)
    if not zero_init:
        c0, n0, m0, layer = state
        in_specs += [
            pl.BlockSpec((None, None, 2, ML_HEADS, ML_DK, ML_DV), lambda bi: (bi, layer, 0, 0, 0, 0)),
            pl.BlockSpec((None, None, 2, ML_HEADS, ML_DK), lambda bi: (bi, layer, 0, 0, 0)),
            pl.BlockSpec((None, None, 2, ML_HEADS), lambda bi: (bi, layer, 0, 0)),
        ]
        args += [c0, n0, m0]
    kern = functools.partial(_mlstm_kernel, t=t, bb=bb, zero_init=zero_init, side_mod=side is not None, n_prev=n_prev)
    return kern, in_specs, args, out_specs, out_shape, []


def _dft_tables(t):
    def cs(n):
        idx = np.arange(n, dtype=np.int64)
        ang = 2.0 * np.pi * ((idx[:, None] * idx[None, :]) % n).astype(np.float64) / n
        return np.cos(ang), np.sin(ang)
    cc, sc = cs(FO_GC)
    ct, st = cs(t)
    return (np.concatenate([cc, sc], axis=1).astype(np.float32),
            np.concatenate([ct, -st], axis=1).astype(np.float32))


FO_RT = 256


def _fourier_kernel(zx_ref, zga_ref, zgb_ref, wc_ref, wt_ref, wf_ref, o_ref, y_ref, *, t, bb):
    wc = wc_ref[...]
    for bi in range(bb):
        x = zx_ref[bi * t:(bi + 1) * t, FO_SHIFT:FO_SHIFT + FO_W]
        for g in range(FO_GROUPS):
            cols = slice(g * FO_GC, (g + 1) * FO_GC)
            ycols = slice(bi * FO_W + g * FO_GC, bi * FO_W + (g + 1) * FO_GC)
            y = jnp.dot(x[:, cols].astype(BF16), wc, preferred_element_type=F32)
            y_ref[0:t, ycols] = y[:, :FO_GC].astype(BF16)
            y_ref[t:2 * t, ycols] = y[:, FO_GC:].astype(BF16)
    scale = (t * FO_GC) ** -0.5
    for r in range(t // FO_RT):
        rows = slice(r * FO_RT, (r + 1) * FO_RT)
        f = jnp.dot(wt_ref[rows, :], y_ref[...], preferred_element_type=F32) * scale
        for bi in range(bb):
            orows = slice(bi * t + r * FO_RT, bi * t + (r + 1) * FO_RT)
            fg = jnp.concatenate([zga_ref[orows, FO_SHIFT:], zgb_ref[orows, :FO_SHIFT]], axis=1)
            for g in range(FO_GROUPS):
                cols = slice(g * FO_GC, (g + 1) * FO_GC)
                fcols = slice(bi * FO_W + g * FO_GC, bi * FO_W + (g + 1) * FO_GC)
                og = jnp.dot(f[:, fcols].astype(BF16), wf_ref[g].astype(BF16), preferred_element_type=F32)
                o_ref[orows, cols] = (og * _silu(fg[:, cols])).astype(BF16)


def _fourier(z, z_tail, w_fno_l, *, b, t, bb):
    assert OFF_FG - FO_SHIFT + FO_W == Z_W and Z_TAIL == FO_SHIFT
    wc, wt = _dft_tables(t)
    xw = FO_W + LANES
    in_specs = [
        pl.BlockSpec((bb * t, xw), lambda bi: (bi, (OFF_FX - FO_SHIFT) // xw)),
        pl.BlockSpec((bb * t, FO_W), lambda bi: (bi, (OFF_FG - FO_SHIFT) // FO_W)),
        pl.BlockSpec((bb * t, LANES), lambda bi: (bi, 0)),
        pl.BlockSpec((FO_GC, 2 * FO_GC), lambda bi: (0, 0)),
        pl.BlockSpec((t, 2 * t), lambda bi: (0, 0)),
        pl.BlockSpec((FO_GROUPS, FO_GC, FO_GC), lambda bi: (0, 0, 0)),
    ]
    args = [z, z, z_tail, jnp.asarray(wc).astype(BF16), jnp.asarray(wt).astype(BF16), w_fno_l]
    return (functools.partial(_fourier_kernel, t=t, bb=bb), in_specs, args,
            [pl.BlockSpec((bb * t, FO_W), lambda bi: (bi, 0))], [jax.ShapeDtypeStruct((b * t, FO_W), BF16)],
            [pltpu.VMEM((2 * t, bb * FO_W), BF16)])


OUT_TN = 1024
OUT_TM = 1024


def _outproj_kernel(oa_ref, om_ref, of_ref, wa_ref, wm_ref, wf_ref, x_ref, gate_ref, y_ref, wb_ref, *, per_row_mod):
    i = pl.program_id(1)

    @pl.when(i == 0)
    def _():
        wb_ref[0:ATT_W, :] = wa_ref[...].astype(BF16)
        wb_ref[ATT_W:ATT_W + ML_W, :] = wm_ref[...].astype(BF16)
        wb_ref[ATT_W + ML_W:D_MIX, :] = wf_ref[...].astype(BF16)

    row = 1 + (i * OUT_TM) // ROW_TILE if per_row_mod else 0
    gate = gate_ref[pl.ds(row, 1), :]
    y = (jnp.dot(oa_ref[...], wb_ref[0:ATT_W, :], preferred_element_type=F32)
         + jnp.dot(om_ref[...], wb_ref[ATT_W:ATT_W + ML_W, :], preferred_element_type=F32)
         + jnp.dot(of_ref[...], wb_ref[ATT_W + ML_W:D_MIX, :], preferred_element_type=F32))
    y_ref[...] = x_ref[...] + gate * y


def _outproj(o_att, o_ml, o_fo, w_out, layer, x2d, mod_l, *, per_row_mod):
    m = x2d.shape[0]
    return pl.pallas_call(
        functools.partial(_outproj_kernel, per_row_mod=per_row_mod),
        grid=(D_MODEL // OUT_TN, m // OUT_TM),
        in_specs=[
            pl.BlockSpec((OUT_TM, ATT_W), lambda n, i: (i, 0)),
            pl.BlockSpec((OUT_TM, ML_W), lambda n, i: (i, 0)),
            pl.BlockSpec((OUT_TM, FO_W), lambda n, i: (i, 0)),
            pl.BlockSpec((None, ATT_W, OUT_TN), lambda n, i: (layer, 0, n)),
            pl.BlockSpec((None, ML_W, OUT_TN), lambda n, i: (layer, ATT_W // ML_W, n)),
            pl.BlockSpec((None, FO_W, OUT_TN), lambda n, i: (layer, (ATT_W + ML_W) // FO_W, n)),
            pl.BlockSpec((OUT_TM, OUT_TN), lambda n, i: (i, n)),
            pl.BlockSpec((8, OUT_TN), lambda n, i: (0, 2 * D_MODEL // OUT_TN + n)),
        ],
        out_specs=pl.BlockSpec((OUT_TM, OUT_TN), lambda n, i: (i, n)),
        out_shape=jax.ShapeDtypeStruct((m, D_MODEL), F32),
        scratch_shapes=[pltpu.VMEM((D_MIX, OUT_TN), BF16)],
        compiler_params=_cparams("parallel", "arbitrary"),
        name="outproj",
    )(o_att, o_ml, o_fo, w_out, w_out, w_out, x2d, mod_l)


def _rope_tables(t):
    half = HEAD_DIM // 2
    inv_freq = ROPE_BASE ** (-jnp.arange(0, half, 2, dtype=F32) / half)
    n_rows = t // GRID_W
    rows = jnp.repeat(jnp.arange(n_rows, dtype=F32), GRID_W)
    cols = jnp.tile(jnp.arange(GRID_W, dtype=F32), n_rows)
    ar = rows[:, None] * inv_freq
    ac = cols[:, None] * inv_freq
    cos = jnp.concatenate([jnp.cos(ar), jnp.cos(ar), jnp.cos(ac), jnp.cos(ac)], axis=1)
    sin = jnp.concatenate([-jnp.sin(ar), jnp.sin(ar), -jnp.sin(ac), jnp.sin(ac)], axis=1)
    return cos, sin


def _run_parts(parts, steps, name):
    n_in = [len(p[1]) for p in parts]
    n_out = [len(p[3]) for p in parts]
    n_scr = [len(p[5]) for p in parts]

    def split(seq, counts):
        out, k = [], 0
        for c in counts:
            out.append(list(seq[k:k + c]))
            k += c
        return out

    def body(*refs):
        ins = split(refs[:sum(n_in)], n_in)
        outs = split(refs[sum(n_in):sum(n_in) + sum(n_out)], n_out)
        scrs = split(refs[sum(n_in) + sum(n_out):], n_scr)
        for part, i, o, s in zip(parts, ins, outs, scrs):
            part[0](*i, *o, *s)

    cat = lambda k: [x for p in parts for x in p[k]]
    results = pl.pallas_call(
        body, grid=(steps,), in_specs=cat(1), out_specs=cat(3), out_shape=cat(4), scratch_shapes=cat(5),
        compiler_params=_cparams("parallel"), name=name,
    )(*cat(2))
    return split(results, n_out)


def _layer(x2d, mod_l, lw, layer, *, b, t, latent, rope=None, cache=None, state=None, side=None, prev=None):
    norm_w, w_in_t, b_if, conv_w, conv_b, q_norm, k_norm, ml_norm, w_fno, w_out = lw
    prev_kv, prev_state = (None, None) if prev is None else (prev[:2], prev[2:])
    z, z_tail = _inproj(x2d, mod_l, norm_w, w_in_t, layer, conv_w, conv_b, per_row_mod=latent, t=t)
    o_att, *kv_new = _attention(z, q_norm, k_norm, b=b, t=t, rope=rope, cache=cache, prev_kv=prev_kv)
    bb = 1 if latent else max(1, ML_ROWS // t)
    (o_ml, c_f, n_f, m_f, *side_mod), (o_fo,) = _run_parts(
        [_mlstm(z, b_if, ml_norm, b=b, t=t, bb=bb, state=state, side=side, prev=prev_state),
         _fourier(z, z_tail, w_fno, b=b, t=t, bb=bb)], b // bb, "mixers")
    y = _outproj(o_att, o_ml, o_fo, w_out, layer, x2d, mod_l, per_row_mod=latent)
    return y, (*kv_new, c_f, n_f, m_f), side_mod


def kernel(x_prompt, x_sample, cache_k, cache_v, state_C, state_n, state_m, c, c_ctx, norm_w, w_mod, b_mod, w_in,
           b_if, conv_w, conv_b, q_norm, k_norm, ml_norm, w_fno, w_out):
    bp, tp, _ = x_prompt.shape
    bs, ts, _ = x_sample.shape
    assert tp % ML_CHUNK == 0 and ts == ROW_TILE and (bp * tp) % ROW_TILE == 0 and bs + 1 <= 8
    past = cache_k.shape[2]

    cond8 = jnp.concatenate([c_ctx[None, :], c, jnp.zeros((8 - 1 - bs, D_MODEL), F32)], axis=0)
    mod_l = _modulation(cond8, w_mod, b_mod, 0)

    rope = _rope_tables(ts)
    ck = cache_k.reshape(bs, DEPTH, past, ATT_KV_W)
    cv = cache_v.reshape(bs, DEPTH, past, ATT_KV_W)
    w_in_t = jnp.swapaxes(w_in, 1, 2)

    xp = x_prompt.reshape(bp * tp, D_MODEL)
    xs = x_sample.reshape(bs * ts, D_MODEL)
    new = None
    for l in range(DEPTH):
        lw = (norm_w[l], w_in_t, b_if[l], conv_w[l], conv_b[l], q_norm[l], k_norm[l], ml_norm[l], w_fno[l], w_out)
        side = (cond8, w_mod, b_mod, l + 1) if l + 1 < DEPTH else None
        xp, new, mod_next = _layer(xp, mod_l, lw, l, b=bp, t=tp, latent=False, side=side, prev=new)
        xs, _, _ = _layer(xs, mod_l, lw, l, b=bs, t=ts, latent=True, rope=rope,
                          cache=(ck, cv, l), state=(state_C, state_n, state_m, l))
        if mod_next:
            mod_l = mod_next[0]
    return (xp.reshape(bp, tp, D_MODEL), xs.reshape(bs, ts, D_MODEL), *new)
```

```python
import functools

import numpy as np
import jax
import jax.numpy as jnp
from jax import lax
from jax.experimental import pallas as pl
from jax.experimental.pallas import tpu as pltpu

D_MODEL = 2048
DEPTH = 2
GRID_W = 64
HEAD_DIM = 128
ATT_HEADS = 8
ATT_KV_HEADS = 2
ATT_GROUPS = ATT_HEADS // ATT_KV_HEADS
ATT_W = ATT_HEADS * HEAD_DIM
ATT_KV_W = ATT_KV_HEADS * HEAD_DIM
ML_HEADS = 4
ML_DK = 128
ML_DV = 128
ML_W = ML_HEADS * ML_DV
FO_GROUPS = 4
FO_GC = 128
FO_W = FO_GROUPS * FO_GC
D_MIX = ATT_W + ML_W + FO_W
N_GATES = 4 * ML_HEADS
D_PROJ = 2 * ATT_W + 2 * ATT_KV_W + 5 * ML_W + N_GATES + 2 * FO_W
CONV_W = 3
ROPE_BASE = 10000.0
EPS = 1e-6

LANES = 128
MXU_N = 256
OFF_AQ = 0
OFF_AK = OFF_AQ + ATT_W
OFF_AV = OFF_AK + ATT_KV_W
OFF_AG = OFF_AV + ATT_KV_W
OFF_MQ = OFF_AG + ATT_W
OFF_MK = OFF_MQ + ML_W
OFF_MV = OFF_MK + ML_W
OFF_MO = OFF_MV + ML_W
OFF_MG = OFF_MO + ML_W
OFF_IF = OFF_MG + ML_W
OFF_FX = OFF_IF + N_GATES
OFF_FG = OFF_FX + FO_W
FO_SHIFT = OFF_FX % LANES

PROJ_TN = 1024
Z_W = (D_PROJ // PROJ_TN) * PROJ_TN
Z_TAIL = D_PROJ - Z_W
ROW_TILE = 1024
ML_CHUNK = 256
ML_EXT = ML_DV + 16
ML_ROWS = 512
ATT_TQ = 512
ATT_SUB = 256
ATT_AHEAD = 1
VMEM_LIMIT = 56 * 1024 * 1024

BF16 = jnp.bfloat16
F32 = jnp.float32
NT_DIMS = (((1,), (1,)), ((), ()))


def _cparams(*sem):
    return pltpu.CompilerParams(dimension_semantics=sem, vmem_limit_bytes=VMEM_LIMIT)


def _silu(x):
    return x * (1.0 / (1.0 + jnp.exp(-x)))


def _sigmoid(x):
    return 1.0 / (1.0 + jnp.exp(-x))


def _rms(x, w):
    ms = jnp.mean(x * x, axis=-1, keepdims=True)
    return x * lax.rsqrt(ms + EPS) * w


MOD_TN = 768


def _mod_kernel(cond_ref, w_ref, b_ref, o_ref):
    a = _silu(cond_ref[...]).astype(BF16)
    o_ref[...] = jnp.dot(a, w_ref[...].astype(BF16), preferred_element_type=F32) + b_ref[...]


def _modulation(cond8, w_mod, b_mod, layer):
    n = 3 * D_MODEL
    return pl.pallas_call(
        _mod_kernel,
        grid=(n // MOD_TN,),
        in_specs=[
            pl.BlockSpec((8, D_MODEL), lambda j: (0, 0)),
            pl.BlockSpec((None, D_MODEL, MOD_TN), lambda j: (layer, 0, j)),
            pl.BlockSpec((None, 1, MOD_TN), lambda j: (layer, 0, j)),
        ],
        out_specs=pl.BlockSpec((8, MOD_TN), lambda j: (0, j)),
        out_shape=jax.ShapeDtypeStruct((8, n), F32),
        compiler_params=_cparams("parallel"),
        name="modulation",
    )(cond8, w_mod, b_mod.reshape(DEPTH, 1, n))


NORM_ROWS = 16


def _shift_rows(x, direction, period):
    n = x.shape[0]
    pos = lax.broadcasted_iota(jnp.int32, x.shape, 0) % period
    if direction > 0:
        return jnp.where(pos == 0, 0.0, pltpu.roll(x, 1, axis=0))
    return jnp.where(pos == period - 1, 0.0, pltpu.roll(x, n - 1, axis=0))


def _conv_silu(x, w, b, period):
    y = _shift_rows(x, 1, period) * w[0:1, :] + x * w[1:2, :] + _shift_rows(x, -1, period) * w[2:3, :] + b
    return _silu(y)


def _inproj_kernel(x_ref, mod_ref, nw_ref, w_ref, wt_ref, cw_ref, cb_ref, z_ref, zt_ref, h_ref, *, per_row_mod, t):
    i = pl.program_id(0)
    j = pl.program_id(1)

    @pl.when(j == 0)
    def _():
        row = 1 + i if per_row_mod else 0
        shift = mod_ref[pl.ds(row, 1), 0:D_MODEL]
        gain = nw_ref[...] * (1.0 + mod_ref[pl.ds(row, 1), D_MODEL:2 * D_MODEL])

        def body(r, carry):
            sl = pl.ds(pl.multiple_of(r * NORM_ROWS, NORM_ROWS), NORM_ROWS)
            x = x_ref[sl, :]
            inv = lax.rsqrt(jnp.mean(x * x, axis=-1, keepdims=True) + EPS)
            h_ref[sl, :] = (x * inv * gain + shift).astype(BF16)
            return carry

        lax.fori_loop(0, ROW_TILE // NORM_ROWS, body, 0, unroll=8)

    def project(cols=slice(0, PROJ_TN)):
        return lax.dot_general(h_ref[...], w_ref[cols, :].astype(BF16), NT_DIMS, preferred_element_type=F32)

    half = PROJ_TN // 2
    pieces = lambda start: [slice(c, c + MXU_N) for c in range(start, start + half, MXU_N)]
    j_q, j_k, j_o = OFF_MQ // PROJ_TN, OFF_MK // PROJ_TN, OFF_MO // PROJ_TN

    def conv_tile(start, raw_start, conv_off, scale):
        for cols, rcols in zip(pieces(start), pieces(raw_start)):
            cc = slice(conv_off + cols.start - start, conv_off + cols.stop - start)
            y = _conv_silu(project(cols), cw_ref[:, cc], cb_ref[:, cc], t)
            z_ref[:, cols] = y if scale is None else y * scale
            z_ref[:, rcols] = project(rcols)

    @pl.when(j == j_q)
    def _():
        conv_tile(half, 0, 0, None)

    @pl.when(j == j_k)
    def _():
        conv_tile(0, half, ML_W, ML_DK ** -0.5)

    @pl.when(j == j_o)
    def _():
        z_ref[...] = project()
        z_ref[:, 0:half] = _sigmoid(z_ref[:, 0:half]) * _silu(z_ref[:, half:PROJ_TN])

    @pl.when((j != j_q) & (j != j_k) & (j != j_o))
    def _():
        z_ref[...] = project()

    @pl.when(j == Z_W // PROJ_TN - 1)
    def _():
        zt = lax.dot_general(h_ref[...], wt_ref[...].astype(BF16), NT_DIMS, preferred_element_type=F32)
        col = lax.broadcasted_iota(jnp.int32, zt.shape, 1)
        zt_ref[...] = jnp.where(col < Z_TAIL, zt, 0.0)


def _inproj(x2d, mod_l, norm_w_l, w_in_t, layer, conv_w_l, conv_b_l, *, per_row_mod, t):
    m = x2d.shape[0]
    half = PROJ_TN // 2
    assert OFF_MQ % PROJ_TN == half and OFF_MK % PROJ_TN == 0 and OFF_MO % PROJ_TN == 0 and ML_W == half
    assert ROW_TILE % t == 0
    return pl.pallas_call(
        functools.partial(_inproj_kernel, per_row_mod=per_row_mod, t=t),
        grid=(m // ROW_TILE, Z_W // PROJ_TN),
        in_specs=[
            pl.BlockSpec((ROW_TILE, D_MODEL), lambda i, j: (i, 0)),
            pl.BlockSpec((8, 3 * D_MODEL), lambda i, j: (0, 0)),
            pl.BlockSpec((1, D_MODEL), lambda i, j: (0, 0)),
            pl.BlockSpec((None, PROJ_TN, D_MODEL), lambda i, j: (layer, j, 0)),
            pl.BlockSpec((None, LANES, D_MODEL), lambda i, j: (layer, Z_W // LANES, 0)),
            pl.BlockSpec((CONV_W, 2 * ML_W), lambda i, j: (0, 0)),
            pl.BlockSpec((1, 2 * ML_W), lambda i, j: (0, 0)),
        ],
        out_specs=[
            pl.BlockSpec((ROW_TILE, PROJ_TN), lambda i, j: (i, j)),
            pl.BlockSpec((ROW_TILE, LANES), lambda i, j: (i, 0)),
        ],
        out_shape=[jax.ShapeDtypeStruct((m, Z_W), F32), jax.ShapeDtypeStruct((m, LANES), F32)],
        scratch_shapes=[pltpu.VMEM((ROW_TILE, D_MODEL), BF16)],
        compiler_params=_cparams("parallel", "arbitrary"),
        name="inproj",
    )(x2d, mod_l, norm_w_l.reshape(1, D_MODEL), w_in_t, w_in_t, conv_w_l, conv_b_l.reshape(1, 2 * ML_W))


def _rope(x, cos, sin):
    lane = lax.broadcasted_iota(jnp.int32, x.shape, 1)
    quarter = HEAD_DIM // 4
    partner = jnp.where((lane % (2 * quarter)) < quarter,
                        pltpu.roll(x, HEAD_DIM - quarter, axis=1), pltpu.roll(x, quarter, axis=1))
    return x * cos + partner * sin


LOG2E = 1.4426950408889634
LN2 = 0.6931471805599453


def _attn_kernel(*refs, t, tq, bb, latent, n_prev):
    hk = ATT_KV_HEADS
    zq_ref, zk_ref, zv_ref = refs[:3]
    zg_refs = refs[3:3 + hk]
    rest = refs[3 + hk:]
    if latent:
        qn_ref, kn_ref, cos_ref, sin_ref, ck_ref, cv_ref, o_ref, kall_ref, vall_ref = rest
        past = ck_ref.shape[0]
    else:
        qn_ref, kn_ref, *prev_kv, o_ref, knew_ref, vnew_ref, kall_ref, vall_ref = rest
        past = 0
    qi = pl.program_id(1)

    @pl.when(qi == 0)
    def _():
        if n_prev:
            knew_ref[:, 0:n_prev] = prev_kv[0][...]
            vnew_ref[:, 0:n_prev] = prev_kv[1][...]
        for bi in range(bb):
            seq = slice(bi * t, (bi + 1) * t)
            for j in range(hk):
                kc = slice(j * HEAD_DIM, (j + 1) * HEAD_DIM)
                slot = bi * hk + j
                k = _rms(zk_ref[seq, kc], kn_ref[...])
                v = zv_ref[seq, kc]
                if latent:
                    k = _rope(k, cos_ref[...], sin_ref[...])
                    kall_ref[slot, 0:past, :] = ck_ref[:, kc].astype(BF16)
                    vall_ref[slot, 0:past, :] = cv_ref[:, kc].astype(BF16)
                else:
                    knew_ref[bi, n_prev, :, j, :] = k
                    vnew_ref[bi, n_prev, :, j, :] = v
                kall_ref[slot, past:past + t, :] = k.astype(BF16)
                vall_ref[slot, past:past + t, :] = v.astype(BF16)

    q_scale = (HEAD_DIM ** -0.5) * LOG2E
    sub = min(t, ATT_SUB)

    def scores(r, hd):
        rows = slice(r * sub, (r + 1) * sub)
        cols = slice(hd * HEAD_DIM, (hd + 1) * HEAD_DIM)
        q = _rms(zq_ref[rows, cols], qn_ref[...])
        if latent:
            pos = pl.ds(pl.multiple_of(qi * tq + r * sub, sub), sub)
            q = _rope(q, cos_ref[pos, :], sin_ref[pos, :])
        qb = (q * q_scale).astype(BF16)
        slot = (r * sub // t) * hk + hd // ATT_GROUPS
        return lax.dot_general(qb, kall_ref[slot], NT_DIMS, preferred_element_type=F32)

    def finish(r, hd, s):
        j, g = divmod(hd, ATT_GROUPS)
        rows = slice(r * sub, (r + 1) * sub)
        cols = slice(hd * HEAD_DIM, (hd + 1) * HEAD_DIM)
        gcols = slice(g * HEAD_DIM, (g + 1) * HEAD_DIM)
        p = jnp.exp2(s - jnp.max(s, axis=-1, keepdims=True))
        den = jnp.sum(p, axis=-1, keepdims=True)
        o = jnp.dot(p.astype(BF16), vall_ref[(r * sub // t) * hk + j], preferred_element_type=F32) * (1.0 / den)
        o_ref[rows, cols] = (o * _silu(zg_refs[j][rows, gcols])).astype(BF16)

    items = [(r, hd) for r in range(tq // sub) for hd in range(hk * ATT_GROUPS)]
    pending = [scores(*it) for it in items[:ATT_AHEAD]]
    for n, it in enumerate(items):
        if n + ATT_AHEAD < len(items):
            pending.append(scores(*items[n + ATT_AHEAD]))
        finish(*it, pending.pop(0))


def _attention(z, q_norm_l, k_norm_l, *, b, t, rope=None, cache=None, prev_kv=None):
    latent = rope is not None
    hk = ATT_KV_HEADS
    bb = max(1, ATT_TQ // t)
    tq = min(t, ATT_TQ) * bb
    nq = bb * t // tq
    assert b % bb == 0 and (bb == 1 or nq == 1)
    gw = ATT_GROUPS * HEAD_DIM
    in_specs = [
        pl.BlockSpec((tq, hk * gw), lambda bi, qi: (bi * nq + qi, OFF_AQ // (hk * gw))),
        pl.BlockSpec((bb * t, hk * HEAD_DIM), lambda bi, qi: (bi, OFF_AK // (hk * HEAD_DIM))),
        pl.BlockSpec((bb * t, hk * HEAD_DIM), lambda bi, qi: (bi, OFF_AV // (hk * HEAD_DIM))),
    ]
    in_specs += [pl.BlockSpec((tq, gw), lambda bi, qi, j=j: (bi * nq + qi, OFF_AG // gw + j)) for j in range(hk)]
    in_specs += [pl.BlockSpec((1, HEAD_DIM), lambda bi, qi: (0, 0))] * 2
    args = [z] * (3 + hk) + [q_norm_l.reshape(1, HEAD_DIM), k_norm_l.reshape(1, HEAD_DIM)]
    tk = t
    if latent:
        cos, sin = rope
        ck, cv, layer = cache
        past = ck.shape[2]
        tk = past + t
        in_specs += [
            pl.BlockSpec((t, HEAD_DIM), lambda bi, qi: (0, 0)),
            pl.BlockSpec((t, HEAD_DIM), lambda bi, qi: (0, 0)),
            pl.BlockSpec((None, None, past, hk * HEAD_DIM), lambda bi, qi: (bi, layer, 0, 0)),
            pl.BlockSpec((None, None, past, hk * HEAD_DIM), lambda bi, qi: (bi, layer, 0, 0)),
        ]
        args += [cos, sin, ck, cv]
    out_specs = [pl.BlockSpec((tq, hk * gw), lambda bi, qi: (bi * nq + qi, 0))]
    out_shape = [jax.ShapeDtypeStruct((b * t, ATT_W), BF16)]
    n_prev = 0 if prev_kv is None else prev_kv[0].shape[1]
    if not latent:
        kv_spec = lambda layers: pl.BlockSpec((bb, layers, t, ATT_KV_HEADS, HEAD_DIM), lambda bi, qi: (bi, 0, 0, 0, 0))
        if n_prev:
            in_specs += [kv_spec(n_prev)] * 2
            args += list(prev_kv)
        out_specs += [kv_spec(n_prev + 1)] * 2
        out_shape += [jax.ShapeDtypeStruct((b, n_prev + 1, t, ATT_KV_HEADS, HEAD_DIM), F32)] * 2
    return pl.pallas_call(
        functools.partial(_attn_kernel, t=t, tq=tq, bb=bb, latent=latent, n_prev=n_prev),
        grid=(b // bb, nq),
        in_specs=in_specs,
        out_specs=out_specs,
        out_shape=out_shape,
        scratch_shapes=[pltpu.VMEM((bb * hk, tk, HEAD_DIM), BF16), pltpu.VMEM((bb * hk, tk, HEAD_DIM), BF16)],
        compiler_params=_cparams("parallel", "arbitrary"),
        name="attention",
    )(*args)


def _split3_dot(tri, x):
    hi = x.astype(BF16)
    r1 = x - hi.astype(F32)
    mid = r1.astype(BF16)
    lo = (r1 - mid.astype(F32)).astype(BF16)
    d = lambda a: jnp.dot(tri, a, preferred_element_type=F32)
    return d(hi) + d(mid) + d(lo)


def _mlstm_kernel(*refs, t, bb, zero_init, side_mod, n_prev):
    refs = list(refs)
    if side_mod:
        mo_ref = refs.pop()
        cond_ref, wm_ref, bm_ref = refs[4:7]
        refs = refs[:4] + refs[7:]
        _mod_kernel(cond_ref, wm_ref, bm_ref, mo_ref)
    if n_prev:
        c_prev, n_prev_ref, m_prev = refs[4:7]
        refs = refs[:4] + refs[7:]
        c_out, n_out, m_out = refs[-3:]
        c_out[:, 0:n_prev] = c_prev[...]
        n_out[:, 0:n_prev] = n_prev_ref[...]
        m_out[:, 0:n_prev] = m_prev[...]
    for bi in range(bb):
        _mlstm_one(bi, refs, t=t, zero_init=zero_init, n_prev=n_prev)


def _mlstm_one(bi, refs, *, t, zero_init, n_prev):
    if zero_init:
        z_ref, zif_ref, bif_ref, nw_ref, o_ref, c_out, n_out, m_out = refs
    else:
        z_ref, zif_ref, bif_ref, nw_ref, c0_ref, n0_ref, m0_ref, o_ref, c_out, n_out, m_out = refs
    seq = slice(bi * t, (bi + 1) * t)
    L = ML_CHUNK
    nc = t // L
    chunk = lambda a, c: a[c * L:(c + 1) * L]

    gates = zif_ref[seq, :] + bif_ref[...]
    logsig = (jnp.minimum(gates, 0.0) - jnp.log1p(jnp.exp(-jnp.abs(gates)))) * LOG2E
    gates = gates * LOG2E
    gates_t = gates.T
    r_i = lax.broadcasted_iota(jnp.int32, (L, L), 0)
    c_i = lax.broadcasted_iota(jnp.int32, (L, L), 1)
    lower, upper = r_i >= c_i, r_i <= c_i
    cum_tris = (lower.astype(BF16), upper.astype(BF16))
    dmasks = (upper, lower)
    cums = [[_split3_dot(cum_tris[d], chunk(logsig, c)) for c in range(nc)] for d in range(2)]
    cums_t = [[x.T for x in row] for row in cums]
    ext_rows = (lax.broadcasted_iota(jnp.int32, (ML_EXT - ML_DV, t), 0) == 0).astype(F32)
    lchunk = lambda a, c: a[:, c * L:(c + 1) * L]

    def running_max(a, d):
        rows = a.shape[0]
        if rows < 8:
            a = jnp.concatenate([a] * (8 // rows), axis=0)
        lane = lax.broadcasted_iota(jnp.int32, a.shape, 1)
        k = 1
        while k < L:
            if d:
                shifted = jnp.where(lane < L - k, pltpu.roll(a, L - k, axis=1), -jnp.inf)
            else:
                shifted = jnp.where(lane >= k, pltpu.roll(a, k, axis=1), -jnp.inf)
            a = jnp.maximum(a, shifted)
            k *= 2
        return a[:rows]

    hsl = lambda d, which: slice((2 * d + which) * ML_HEADS, (2 * d + which + 1) * ML_HEADS)
    a_maxes = [running_max(jnp.concatenate([lchunk(gates_t, c)[hsl(d, 0)] - cums_t[d][c][hsl(d, 1)]
                                            for c in range(nc)], axis=0), d) for d in range(2)]

    def local_part(qk_c, kb_c, vt_c, vtb_c, b_row, b_col, i_row, i_col, a_max, d):
        m_loc = b_row + a_max
        s = (qk_c * jnp.exp2(jnp.where(dmasks[d], (i_col - b_col) - a_max, -jnp.inf))).astype(BF16)
        bm = jnp.dot(vtb_c, s, preferred_element_type=F32)
        b_end = b_row[:, 0:1] if d else b_row[:, L - 1:L]
        g = b_end - b_row + i_row
        g_max = jnp.max(g, axis=1, keepdims=True)
        vw = (vt_c * jnp.exp2(g - g_max)).astype(BF16)
        u = jnp.dot(vw, kb_c, preferred_element_type=F32)
        return m_loc, bm, b_end, g_max, u

    def scan_step(qt_c, b_row, loc, c_t, m):
        m_loc, bm, b_end, g_max, u = loc
        inter = b_row + m
        m_row = jnp.maximum(inter, m_loc)
        nd = jnp.exp2(m_loc - m_row) * bm
        if c_t is not None:
            nd = nd + jnp.exp2(inter - m_row) * jnp.dot(c_t.astype(BF16), qt_c, preferred_element_type=F32)
        h_t = nd[:ML_DV] / jnp.maximum(jnp.abs(nd[ML_DV:ML_DV + 1]), jnp.exp2(-m_row))
        m_new = jnp.maximum(b_end + m, g_max)
        c_new = jnp.exp2(g_max - m_new) * u
        if c_t is not None:
            c_new = c_new + jnp.exp2(b_end + m - m_new) * c_t
        return h_t, c_new, m_new

    chains = [(hd, d) for hd in range(ML_HEADS) for d in range(2)]
    qts, b_rows, locs, state = {}, {}, {}, {}
    for hd in range(ML_HEADS):
        cols = slice(hd * ML_DK, (hd + 1) * ML_DK)
        kcols = slice(ML_W + hd * ML_DK, ML_W + (hd + 1) * ML_DK)
        vcols = slice(2 * ML_W + hd * ML_DV, 2 * ML_W + (hd + 1) * ML_DV)
        qt = z_ref[seq, cols].T.astype(BF16)
        kb = z_ref[seq, kcols].astype(BF16)
        vt = jnp.concatenate([z_ref[seq, vcols].T, ext_rows], axis=0)
        vtb = vt.astype(BF16)
        qk = [jnp.dot(chunk(kb, c), lchunk(qt, c), preferred_element_type=F32) for c in range(nc)]
        qts[hd] = qt
        for d in range(2):
            i_lane = 2 * d * ML_HEADS + hd
            f_lane = (2 * d + 1) * ML_HEADS + hd
            b_rows[hd, d] = [cums_t[d][c][f_lane:f_lane + 1, :] for c in range(nc)]
            locs[hd, d] = [local_part(qk[c], chunk(kb, c), lchunk(vt, c), lchunk(vtb, c), b_rows[hd, d][c],
                                      cums[d][c][:, f_lane:f_lane + 1], lchunk(gates_t, c)[i_lane:i_lane + 1, :],
                                      chunk(gates, c)[:, i_lane:i_lane + 1],
                                      a_maxes[d][c * ML_HEADS + hd:c * ML_HEADS + hd + 1], d) for c in range(nc)]
            if zero_init:
                state[hd, d] = (None, jnp.zeros((1, 1), F32))
            else:
                c_t = jnp.concatenate([c0_ref[d, hd].T, n0_ref[d, hd:hd + 1, :],
                                       jnp.zeros((ML_EXT - ML_DV - 1, ML_DK), F32)], axis=0)
                state[hd, d] = (c_t, m0_ref[d:d + 1, hd:hd + 1] * LOG2E)

    h_parts = {ch: [None] * nc for ch in chains}
    for step in range(nc):
        for hd, d in chains:
            c = nc - 1 - step if d else step
            c_t, m = state[hd, d]
            h_parts[hd, d][c], c_t, m = scan_step(lchunk(qts[hd], c), b_rows[hd, d][c], locs[hd, d][c], c_t, m)
            state[hd, d] = (c_t, m)

    for hd, d in chains:
        c_t, m = state[hd, d]
        c_out[bi, n_prev, d, hd] = c_t[:ML_DV].T
        n_out[bi, n_prev, d, hd:hd + 1, :] = c_t[ML_DV:ML_DV + 1]
        m_out[bi, n_prev, d:d + 1, hd:hd + 1] = m * LN2
    for hd in range(ML_HEADS):
        cols = slice(hd * ML_DK, (hd + 1) * ML_DK)
        ocols = slice(3 * ML_W + hd * ML_DV, 3 * ML_W + (hd + 1) * ML_DV)
        h_fwd, h_bwd = (jnp.concatenate(h_parts[hd, d], axis=1) if nc > 1 else h_parts[hd, d][0] for d in range(2))
        h_sum = h_fwd + h_bwd
        hm = _rms(h_sum.T, nw_ref[:, cols])
        o_ref[seq, cols] = (hm * z_ref[seq, ocols]).astype(BF16)


def _mlstm(z, b_if_l, ml_norm_l, *, b, t, bb, state=None, side=None, prev=None):
    zero_init = state is None
    n_prev = 0 if prev is None else prev[0].shape[1]
    assert b % bb == 0 and (zero_init or bb == 1)
    steps = b // bb
    zw = 5 * ML_W
    in_specs = [
        pl.BlockSpec((bb * t, zw), lambda bi: (bi, OFF_MQ // zw)),
        pl.BlockSpec((bb * t, LANES), lambda bi: (bi, OFF_IF // LANES)),
        pl.BlockSpec((1, LANES), lambda bi: (0, 0)),
        pl.BlockSpec((1, ML_W), lambda bi: (0, 0)),
    ]
    bif = jnp.pad(b_if_l, (0, LANES - N_GATES)).reshape(1, LANES)
    args = [z, z, bif, ml_norm_l.reshape(1, ML_W)]
    out_specs = [
        pl.BlockSpec((bb * t, ML_W), lambda bi: (bi, 0)),
        pl.BlockSpec((bb, n_prev + 1, 2, ML_HEADS, ML_DK, ML_DV), lambda bi: (bi, 0, 0, 0, 0, 0)),
        pl.BlockSpec((bb, n_prev + 1, 2, ML_HEADS, ML_DK), lambda bi: (bi, 0, 0, 0, 0)),
        pl.BlockSpec((bb, n_prev + 1, 2, ML_HEADS), lambda bi: (bi, 0, 0, 0)),
    ]
    out_shape = [
        jax.ShapeDtypeStruct((b * t, ML_W), BF16),
        jax.ShapeDtypeStruct((b, n_prev + 1, 2, ML_HEADS, ML_DK, ML_DV), F32),
        jax.ShapeDtypeStruct((b, n_prev + 1, 2, ML_HEADS, ML_DK), F32),
        jax.ShapeDtypeStruct((b, n_prev + 1, 2, ML_HEADS), F32),
    ]
    if side is not None:
        cond8, w_mod, b_mod, side_layer = side
        n = 3 * D_MODEL
        tn = n // steps
        assert tn % LANES == 0 and tn * steps == n
        in_specs += [
            pl.BlockSpec((8, D_MODEL), lambda bi: (0, 0)),
            pl.BlockSpec((None, D_MODEL, tn), lambda bi: (side_layer, 0, bi)),
            pl.BlockSpec((None, 1, tn), lambda bi: (side_layer, 0, bi)),
        ]
        args += [cond8, w_mod, b_mod.reshape(DEPTH, 1, n)]
        out_specs.append(pl.BlockSpec((8, tn), lambda bi: (0, bi)))
        out_shape.append(jax.ShapeDtypeStruct((8, n), F32))
    if n_prev:
        in_specs += [
            pl.BlockSpec((bb, n_prev, 2, ML_HEADS, ML_DK, ML_DV), lambda bi: (bi, 0, 0, 0, 0, 0)),
            pl.BlockSpec((bb, n_prev, 2, ML_HEADS, ML_DK), lambda bi: (bi, 0, 0, 0, 0)),
            pl.BlockSpec((bb, n_prev, 2, ML_HEADS), lambda bi: (bi, 0, 0, 0)),
        ]
        args += list(prev)
    if not zero_init:
        c0, n0, m0, layer = state
        in_specs += [
            pl.BlockSpec((None, None, 2, ML_HEADS, ML_DK, ML_DV), lambda bi: (bi, layer, 0, 0, 0, 0)),
            pl.BlockSpec((None, None, 2, ML_HEADS, ML_DK), lambda bi: (bi, layer, 0, 0, 0)),
            pl.BlockSpec((None, None, 2, ML_HEADS), lambda bi: (bi, layer, 0, 0)),
        ]
        args += [c0, n0, m0]
    kern = functools.partial(_mlstm_kernel, t=t, bb=bb, zero_init=zero_init, side_mod=side is not None, n_prev=n_prev)
    return kern, in_specs, args, out_specs, out_shape, []


def _dft_tables(t):
    def cs(n):
        idx = np.arange(n, dtype=np.int64)
        ang = 2.0 * np.pi * ((idx[:, None] * idx[None, :]) % n).astype(np.float64) / n
        return np.cos(ang), np.sin(ang)
    cc, sc = cs(FO_GC)
    ct, st = cs(t)
    return (np.concatenate([cc, sc], axis=1).astype(np.float32),
            np.concatenate([ct, -st], axis=1).astype(np.float32))


FO_RT = 256


def _fourier_kernel(zx_ref, zga_ref, zgb_ref, wc_ref, wt_ref, wf_ref, o_ref, y_ref, *, t, bb):
    wc = wc_ref[...]
    for bi in range(bb):
        x = zx_ref[bi * t:(bi + 1) * t, FO_SHIFT:FO_SHIFT + FO_W]
        for g in range(FO_GROUPS):
            cols = slice(g * FO_GC, (g + 1) * FO_GC)
            ycols = slice(bi * FO_W + g * FO_GC, bi * FO_W + (g + 1) * FO_GC)
            y = jnp.dot(x[:, cols].astype(BF16), wc, preferred_element_type=F32)
            y_ref[0:t, ycols] = y[:, :FO_GC].astype(BF16)
            y_ref[t:2 * t, ycols] = y[:, FO_GC:].astype(BF16)
    scale = (t * FO_GC) ** -0.5
    for r in range(t // FO_RT):
        rows = slice(r * FO_RT, (r + 1) * FO_RT)
        f = jnp.dot(wt_ref[rows, :], y_ref[...], preferred_element_type=F32) * scale
        for bi in range(bb):
            orows = slice(bi * t + r * FO_RT, bi * t + (r + 1) * FO_RT)
            fg = jnp.concatenate([zga_ref[orows, FO_SHIFT:], zgb_ref[orows, :FO_SHIFT]], axis=1)
            for g in range(FO_GROUPS):
                cols = slice(g * FO_GC, (g + 1) * FO_GC)
                fcols = slice(bi * FO_W + g * FO_GC, bi * FO_W + (g + 1) * FO_GC)
                og = jnp.dot(f[:, fcols].astype(BF16), wf_ref[g].astype(BF16), preferred_element_type=F32)
                o_ref[orows, cols] = (og * _silu(fg[:, cols])).astype(BF16)


def _fourier(z, z_tail, w_fno_l, *, b, t, bb):
    assert OFF_FG - FO_SHIFT + FO_W == Z_W and Z_TAIL == FO_SHIFT
    wc, wt = _dft_tables(t)
    xw = FO_W + LANES
    in_specs = [
        pl.BlockSpec((bb * t, xw), lambda bi: (bi, (OFF_FX - FO_SHIFT) // xw)),
        pl.BlockSpec((bb * t, FO_W), lambda bi: (bi, (OFF_FG - FO_SHIFT) // FO_W)),
        pl.BlockSpec((bb * t, LANES), lambda bi: (bi, 0)),
        pl.BlockSpec((FO_GC, 2 * FO_GC), lambda bi: (0, 0)),
        pl.BlockSpec((t, 2 * t), lambda bi: (0, 0)),
        pl.BlockSpec((FO_GROUPS, FO_GC, FO_GC), lambda bi: (0, 0, 0)),
    ]
    args = [z, z, z_tail, jnp.asarray(wc).astype(BF16), jnp.asarray(wt).astype(BF16), w_fno_l]
    return (functools.partial(_fourier_kernel, t=t, bb=bb), in_specs, args,
            [pl.BlockSpec((bb * t, FO_W), lambda bi: (bi, 0))], [jax.ShapeDtypeStruct((b * t, FO_W), BF16)],
            [pltpu.VMEM((2 * t, bb * FO_W), BF16)])


OUT_TN = 1024
OUT_TM = 1024


def _outproj_kernel(oa_ref, om_ref, of_ref, wa_ref, wm_ref, wf_ref, x_ref, gate_ref, y_ref, wb_ref, *, per_row_mod):
    i = pl.program_id(1)

    @pl.when(i == 0)
    def _():
        wb_ref[0:ATT_W, :] = wa_ref[...].astype(BF16)
        wb_ref[ATT_W:ATT_W + ML_W, :] = wm_ref[...].astype(BF16)
        wb_ref[ATT_W + ML_W:D_MIX, :] = wf_ref[...].astype(BF16)

    row = 1 + (i * OUT_TM) // ROW_TILE if per_row_mod else 0
    gate = gate_ref[pl.ds(row, 1), :]
    y = (jnp.dot(oa_ref[...], wb_ref[0:ATT_W, :], preferred_element_type=F32)
         + jnp.dot(om_ref[...], wb_ref[ATT_W:ATT_W + ML_W, :], preferred_element_type=F32)
         + jnp.dot(of_ref[...], wb_ref[ATT_W + ML_W:D_MIX, :], preferred_element_type=F32))
    y_ref[...] = x_ref[...] + gate * y


def _outproj(o_att, o_ml, o_fo, w_out, layer, x2d, mod_l, *, per_row_mod):
    m = x2d.shape[0]
    return pl.pallas_call(
        functools.partial(_outproj_kernel, per_row_mod=per_row_mod),
        grid=(D_MODEL // OUT_TN, m // OUT_TM),
        in_specs=[
            pl.BlockSpec((OUT_TM, ATT_W), lambda n, i: (i, 0)),
            pl.BlockSpec((OUT_TM, ML_W), lambda n, i: (i, 0)),
            pl.BlockSpec((OUT_TM, FO_W), lambda n, i: (i, 0)),
            pl.BlockSpec((None, ATT_W, OUT_TN), lambda n, i: (layer, 0, n)),
            pl.BlockSpec((None, ML_W, OUT_TN), lambda n, i: (layer, ATT_W // ML_W, n)),
            pl.BlockSpec((None, FO_W, OUT_TN), lambda n, i: (layer, (ATT_W + ML_W) // FO_W, n)),
            pl.BlockSpec((OUT_TM, OUT_TN), lambda n, i: (i, n)),
            pl.BlockSpec((8, OUT_TN), lambda n, i: (0, 2 * D_MODEL // OUT_TN + n)),
        ],
        out_specs=pl.BlockSpec((OUT_TM, OUT_TN), lambda n, i: (i, n)),
        out_shape=jax.ShapeDtypeStruct((m, D_MODEL), F32),
        scratch_shapes=[pltpu.VMEM((D_MIX, OUT_TN), BF16)],
        compiler_params=_cparams("parallel", "arbitrary"),
        name="outproj",
    )(o_att, o_ml, o_fo, w_out, w_out, w_out, x2d, mod_l)


def _rope_tables(t):
    half = HEAD_DIM // 2
    inv_freq = ROPE_BASE ** (-jnp.arange(0, half, 2, dtype=F32) / half)
    n_rows = t // GRID_W
    rows = jnp.repeat(jnp.arange(n_rows, dtype=F32), GRID_W)
    cols = jnp.tile(jnp.arange(GRID_W, dtype=F32), n_rows)
    ar = rows[:, None] * inv_freq
    ac = cols[:, None] * inv_freq
    cos = jnp.concatenate([jnp.cos(ar), jnp.cos(ar), jnp.cos(ac), jnp.cos(ac)], axis=1)
    sin = jnp.concatenate([-jnp.sin(ar), jnp.sin(ar), -jnp.sin(ac), jnp.sin(ac)], axis=1)
    return cos, sin


def _run_parts(parts, steps, name):
    n_in = [len(p[1]) for p in parts]
    n_out = [len(p[3]) for p in parts]
    n_scr = [len(p[5]) for p in parts]

    def split(seq, counts):
        out, k = [], 0
        for c in counts:
            out.append(list(seq[k:k + c]))
            k += c
        return out

    def body(*refs):
        ins = split(refs[:sum(n_in)], n_in)
        outs = split(refs[sum(n_in):sum(n_in) + sum(n_out)], n_out)
        scrs = split(refs[sum(n_in) + sum(n_out):], n_scr)
        for part, i, o, s in zip(parts, ins, outs, scrs):
            part[0](*i, *o, *s)

    cat = lambda k: [x for p in parts for x in p[k]]
    results = pl.pallas_call(
        body, grid=(steps,), in_specs=cat(1), out_specs=cat(3), out_shape=cat(4), scratch_shapes=cat(5),
        compiler_params=_cparams("parallel"), name=name,
    )(*cat(2))
    return split(results, n_out)


def _layer(x2d, mod_l, lw, layer, *, b, t, latent, rope=None, cache=None, state=None, side=None, prev=None):
    norm_w, w_in_t, b_if, conv_w, conv_b, q_norm, k_norm, ml_norm, w_fno, w_out = lw
    prev_kv, prev_state = (None, None) if prev is None else (prev[:2], prev[2:])
    z, z_tail = _inproj(x2d, mod_l, norm_w, w_in_t, layer, conv_w, conv_b, per_row_mod=latent, t=t)
    o_att, *kv_new = _attention(z, q_norm, k_norm, b=b, t=t, rope=rope, cache=cache, prev_kv=prev_kv)
    bb = 1 if latent else max(1, ML_ROWS // t)
    (o_ml, c_f, n_f, m_f, *side_mod), (o_fo,) = _run_parts(
        [_mlstm(z, b_if, ml_norm, b=b, t=t, bb=bb, state=state, side=side, prev=prev_state),
         _fourier(z, z_tail, w_fno, b=b, t=t, bb=bb)], b // bb, "mixers")
    y = _outproj(o_att, o_ml, o_fo, w_out, layer, x2d, mod_l, per_row_mod=latent)
    return y, (*kv_new, c_f, n_f, m_f), side_mod


def kernel(x_prompt, x_sample, cache_k, cache_v, state_C, state_n, state_m, c, c_ctx, norm_w, w_mod, b_mod, w_in,
           b_if, conv_w, conv_b, q_norm, k_norm, ml_norm, w_fno, w_out):
    bp, tp, _ = x_prompt.shape
    bs, ts, _ = x_sample.shape
    assert tp % ML_CHUNK == 0 and ts == ROW_TILE and (bp * tp) % ROW_TILE == 0 and bs + 1 <= 8
    past = cache_k.shape[2]

    cond8 = jnp.concatenate([c_ctx[None, :], c, jnp.zeros((8 - 1 - bs, D_MODEL), F32)], axis=0)
    mod_l = _modulation(cond8, w_mod, b_mod, 0)

    rope = _rope_tables(ts)
    ck = cache_k.reshape(bs, DEPTH, past, ATT_KV_W)
    cv = cache_v.reshape(bs, DEPTH, past, ATT_KV_W)
    w_in_t = jnp.swapaxes(w_in, 1, 2)

    xp = x_prompt.reshape(bp * tp, D_MODEL)
    xs = x_sample.reshape(bs * ts, D_MODEL)
    new = None
    for l in range(DEPTH):
        lw = (norm_w[l], w_in_t, b_if[l], conv_w[l], conv_b[l], q_norm[l], k_norm[l], ml_norm[l], w_fno[l], w_out)
        side = (cond8, w_mod, b_mod, l + 1) if l + 1 < DEPTH else None
        xp, new, mod_next = _layer(xp, mod_l, lw, l, b=bp, t=tp, latent=False, side=side, prev=new)
        xs, _, _ = _layer(xs, mod_l, lw, l, b=bs, t=ts, latent=True, rope=rope,
                          cache=(ck, cv, l), state=(state_C, state_n, state_m, l))
        if mod_next:
            mod_l = mod_next[0]
    return (xp.reshape(bp, tp, D_MODEL), xs.reshape(bs, ts, D_MODEL), *new)
```

```python
import functools

import numpy as np
import jax
import jax.numpy as jnp
from jax import lax
from jax.experimental import pallas as pl
from jax.experimental.pallas import tpu as pltpu

D_MODEL = 2048
DEPTH = 2
GRID_W = 64
HEAD_DIM = 128
ATT_HEADS = 8
ATT_KV_HEADS = 2
ATT_GROUPS = ATT_HEADS // ATT_KV_HEADS
ATT_W = ATT_HEADS * HEAD_DIM
ATT_KV_W = ATT_KV_HEADS * HEAD_DIM
ML_HEADS = 4
ML_DK = 128
ML_DV = 128
ML_W = ML_HEADS * ML_DV
FO_GROUPS = 4
FO_GC = 128
FO_W = FO_GROUPS * FO_GC
D_MIX = ATT_W + ML_W + FO_W
N_GATES = 4 * ML_HEADS
D_PROJ = 2 * ATT_W + 2 * ATT_KV_W + 5 * ML_W + N_GATES + 2 * FO_W
CONV_W = 3
ROPE_BASE = 10000.0
EPS = 1e-6

LANES = 128
MXU_N = 256
OFF_AQ = 0
OFF_AK = OFF_AQ + ATT_W
OFF_AV = OFF_AK + ATT_KV_W
OFF_AG = OFF_AV + ATT_KV_W
OFF_MQ = OFF_AG + ATT_W
OFF_MK = OFF_MQ + ML_W
OFF_MV = OFF_MK + ML_W
OFF_MO = OFF_MV + ML_W
OFF_MG = OFF_MO + ML_W
OFF_IF = OFF_MG + ML_W
OFF_FX = OFF_IF + N_GATES
OFF_FG = OFF_FX + FO_W
FO_SHIFT = OFF_FX % LANES

PROJ_TN = 1024
Z_W = (D_PROJ // PROJ_TN) * PROJ_TN
Z_TAIL = D_PROJ - Z_W
ROW_TILE = 1024
ML_CHUNK = 256
ML_EXT = ML_DV + 16
ML_ROWS = 512
ATT_TQ = 512
ATT_SUB = 256
ATT_AHEAD = 1
VMEM_LIMIT = 56 * 1024 * 1024

BF16 = jnp.bfloat16
F32 = jnp.float32
NT_DIMS = (((1,), (1,)), ((), ()))


def _cparams(*sem):
    return pltpu.CompilerParams(dimension_semantics=sem, vmem_limit_bytes=VMEM_LIMIT)


def _silu(x):
    return x * (1.0 / (1.0 + jnp.exp(-x)))


def _sigmoid(x):
    return 1.0 / (1.0 + jnp.exp(-x))


def _rms(x, w):
    ms = jnp.mean(x * x, axis=-1, keepdims=True)
    return x * lax.rsqrt(ms + EPS) * w


MOD_TN = 768


def _mod_kernel(cond_ref, w_ref, b_ref, o_ref):
    a = _silu(cond_ref[...]).astype(BF16)
    o_ref[...] = jnp.dot(a, w_ref[...].astype(BF16), preferred_element_type=F32) + b_ref[...]


def _modulation(cond8, w_mod, b_mod, layer):
    n = 3 * D_MODEL
    return pl.pallas_call(
        _mod_kernel,
        grid=(n // MOD_TN,),
        in_specs=[
            pl.BlockSpec((8, D_MODEL), lambda j: (0, 0)),
            pl.BlockSpec((None, D_MODEL, MOD_TN), lambda j: (layer, 0, j)),
            pl.BlockSpec((None, 1, MOD_TN), lambda j: (layer, 0, j)),
        ],
        out_specs=pl.BlockSpec((8, MOD_TN), lambda j: (0, j)),
        out_shape=jax.ShapeDtypeStruct((8, n), F32),
        compiler_params=_cparams("parallel"),
        name="modulation",
    )(cond8, w_mod, b_mod.reshape(DEPTH, 1, n))


NORM_ROWS = 16
NORM_PARTS = 4


def _shift_rows(x, direction, period):
    n = x.shape[0]
    pos = lax.broadcasted_iota(jnp.int32, x.shape, 0) % period
    if direction > 0:
        return jnp.where(pos == 0, 0.0, pltpu.roll(x, 1, axis=0))
    return jnp.where(pos == period - 1, 0.0, pltpu.roll(x, n - 1, axis=0))


def _conv_silu(x, w, b, period):
    y = _shift_rows(x, 1, period) * w[0:1, :] + x * w[1:2, :] + _shift_rows(x, -1, period) * w[2:3, :] + b
    return _silu(y)


def _inproj_kernel(x_ref, mod_ref, nw_ref, w_ref, wt_ref, cw_ref, cb_ref, z_ref, zt_ref, h_ref, *, per_row_mod, t):
    i = pl.program_id(0)
    j = pl.program_id(1)

    @pl.when(j == 0)
    def _():
        row = 1 + i if per_row_mod else 0
        shift = mod_ref[pl.ds(row, 1), 0:D_MODEL]
        gain = nw_ref[...] * (1.0 + mod_ref[pl.ds(row, 1), D_MODEL:2 * D_MODEL])

        wb = w_ref[...].astype(BF16)
        part = ROW_TILE // NORM_PARTS
        for p in range(NORM_PARTS):
            for r in range(p * part, (p + 1) * part, NORM_ROWS):
                x = x_ref[r:r + NORM_ROWS, :]
                inv = lax.rsqrt(jnp.mean(x * x, axis=-1, keepdims=True) + EPS)
                h_ref[r:r + NORM_ROWS, :] = (x * inv * gain + shift).astype(BF16)
            rows = slice(p * part, (p + 1) * part)
            z_ref[rows, :] = lax.dot_general(h_ref[rows, :], wb, NT_DIMS, preferred_element_type=F32)

    def project(cols=slice(0, PROJ_TN)):
        return lax.dot_general(h_ref[...], w_ref[cols, :].astype(BF16), NT_DIMS, preferred_element_type=F32)

    half = PROJ_TN // 2
    pieces = lambda start: [slice(c, c + MXU_N) for c in range(start, start + half, MXU_N)]
    j_q, j_k, j_o = OFF_MQ // PROJ_TN, OFF_MK // PROJ_TN, OFF_MO // PROJ_TN

    def conv_tile(start, raw_start, conv_off, scale):
        for cols, rcols in zip(pieces(start), pieces(raw_start)):
            cc = slice(conv_off + cols.start - start, conv_off + cols.stop - start)
            y = _conv_silu(project(cols), cw_ref[:, cc], cb_ref[:, cc], t)
            z_ref[:, cols] = y if scale is None else y * scale
            z_ref[:, rcols] = project(rcols)

    @pl.when(j == j_q)
    def _():
        conv_tile(half, 0, 0, None)

    @pl.when(j == j_k)
    def _():
        conv_tile(0, half, ML_W, ML_DK ** -0.5)

    @pl.when(j == j_o)
    def _():
        z_ref[...] = project()
        z_ref[:, 0:half] = _sigmoid(z_ref[:, 0:half]) * _silu(z_ref[:, half:PROJ_TN])

    @pl.when((j != 0) & (j != j_q) & (j != j_k) & (j != j_o))
    def _():
        z_ref[...] = project()

    @pl.when(j == Z_W // PROJ_TN - 1)
    def _():
        zt = lax.dot_general(h_ref[...], wt_ref[...].astype(BF16), NT_DIMS, preferred_element_type=F32)
        col = lax.broadcasted_iota(jnp.int32, zt.shape, 1)
        zt_ref[...] = jnp.where(col < Z_TAIL, zt, 0.0)


def _inproj(x2d, mod_l, norm_w_l, w_in_t, layer, conv_w_l, conv_b_l, *, per_row_mod, t):
    m = x2d.shape[0]
    half = PROJ_TN // 2
    assert OFF_MQ % PROJ_TN == half and OFF_MK % PROJ_TN == 0 and OFF_MO % PROJ_TN == 0 and ML_W == half
    assert ROW_TILE % t == 0
    return pl.pallas_call(
        functools.partial(_inproj_kernel, per_row_mod=per_row_mod, t=t),
        grid=(m // ROW_TILE, Z_W // PROJ_TN),
        in_specs=[
            pl.BlockSpec((ROW_TILE, D_MODEL), lambda i, j: (i, 0)),
            pl.BlockSpec((8, 3 * D_MODEL), lambda i, j: (0, 0)),
            pl.BlockSpec((1, D_MODEL), lambda i, j: (0, 0)),
            pl.BlockSpec((None, PROJ_TN, D_MODEL), lambda i, j: (layer, j, 0)),
            pl.BlockSpec((None, LANES, D_MODEL), lambda i, j: (layer, Z_W // LANES, 0)),
            pl.BlockSpec((CONV_W, 2 * ML_W), lambda i, j: (0, 0)),
            pl.BlockSpec((1, 2 * ML_W), lambda i, j: (0, 0)),
        ],
        out_specs=[
            pl.BlockSpec((ROW_TILE, PROJ_TN), lambda i, j: (i, j)),
            pl.BlockSpec((ROW_TILE, LANES), lambda i, j: (i, 0)),
        ],
        out_shape=[jax.ShapeDtypeStruct((m, Z_W), F32), jax.ShapeDtypeStruct((m, LANES), F32)],
        scratch_shapes=[pltpu.VMEM((ROW_TILE, D_MODEL), BF16)],
        compiler_params=_cparams("parallel", "arbitrary"),
        name="inproj",
    )(x2d, mod_l, norm_w_l.reshape(1, D_MODEL), w_in_t, w_in_t, conv_w_l, conv_b_l.reshape(1, 2 * ML_W))


def _rope(x, cos, sin):
    lane = lax.broadcasted_iota(jnp.int32, x.shape, 1)
    quarter = HEAD_DIM // 4
    partner = jnp.where((lane % (2 * quarter)) < quarter,
                        pltpu.roll(x, HEAD_DIM - quarter, axis=1), pltpu.roll(x, quarter, axis=1))
    return x * cos + partner * sin


LOG2E = 1.4426950408889634
LN2 = 0.6931471805599453


def _attn_kernel(*refs, t, tq, nq, bb, latent, n_prev):
    hk = ATT_KV_HEADS
    zq_ref, zk_ref, zv_ref = refs[:3]
    zg_refs = refs[3:3 + hk]
    rest = refs[3 + hk:]
    if latent:
        qn_ref, kn_ref, cos_ref, sin_ref, ck_ref, cv_ref, o_ref, kall_ref, vall_ref = rest
        past = ck_ref.shape[0]
    else:
        qn_ref, kn_ref, *prev_kv, o_ref, knew_ref, vnew_ref, kall_ref, vall_ref = rest
        past = 0
    qi = pl.program_id(1) if nq > 1 else 0

    def stage():
        if n_prev:
            knew_ref[:, 0:n_prev] = prev_kv[0][...]
            vnew_ref[:, 0:n_prev] = prev_kv[1][...]
        for bi in range(bb):
            seq = slice(bi * t, (bi + 1) * t)
            for j in range(hk):
                kc = slice(j * HEAD_DIM, (j + 1) * HEAD_DIM)
                slot = bi * hk + j
                k = _rms(zk_ref[seq, kc], kn_ref[...])
                v = zv_ref[seq, kc]
                if latent:
                    k = _rope(k, cos_ref[...], sin_ref[...])
                    kall_ref[slot, 0:past, :] = ck_ref[:, kc].astype(BF16)
                    vall_ref[slot, 0:past, :] = cv_ref[:, kc].astype(BF16)
                else:
                    knew_ref[bi, n_prev, :, j, :] = k
                    vnew_ref[bi, n_prev, :, j, :] = v
                kall_ref[slot, past:past + t, :] = k.astype(BF16)
                vall_ref[slot, past:past + t, :] = v.astype(BF16)

    if nq > 1:
        pl.when(qi == 0)(stage)
    else:
        stage()

    q_scale = (HEAD_DIM ** -0.5) * LOG2E
    sub = min(t, ATT_SUB)

    def scores(r, hd):
        rows = slice(r * sub, (r + 1) * sub)
        cols = slice(hd * HEAD_DIM, (hd + 1) * HEAD_DIM)
        q = _rms(zq_ref[rows, cols], qn_ref[...])
        if latent:
            pos = pl.ds(pl.multiple_of(qi * tq + r * sub, sub), sub)
            q = _rope(q, cos_ref[pos, :], sin_ref[pos, :])
        qb = (q * q_scale).astype(BF16)
        slot = (r * sub // t) * hk + hd // ATT_GROUPS
        return lax.dot_general(qb, kall_ref[slot], NT_DIMS, preferred_element_type=F32)

    def finish(r, hd, s):
        j, g = divmod(hd, ATT_GROUPS)
        rows = slice(r * sub, (r + 1) * sub)
        cols = slice(hd * HEAD_DIM, (hd + 1) * HEAD_DIM)
        gcols = slice(g * HEAD_DIM, (g + 1) * HEAD_DIM)
        p = jnp.exp2(s - jnp.max(s, axis=-1, keepdims=True))
        den = jnp.sum(p, axis=-1, keepdims=True)
        o = jnp.dot(p.astype(BF16), vall_ref[(r * sub // t) * hk + j], preferred_element_type=F32) * (1.0 / den)
        o_ref[rows, cols] = (o * _silu(zg_refs[j][rows, gcols])).astype(BF16)

    items = [(r, hd) for r in range(tq // sub) for hd in range(hk * ATT_GROUPS)]
    pending = [scores(*it) for it in items[:ATT_AHEAD]]
    for n, it in enumerate(items):
        if n + ATT_AHEAD < len(items):
            pending.append(scores(*items[n + ATT_AHEAD]))
        finish(*it, pending.pop(0))


def _attention(z, q_norm_l, k_norm_l, *, b, t, rope=None, cache=None, prev_kv=None):
    latent = rope is not None
    hk = ATT_KV_HEADS
    bb = max(1, ATT_TQ // t)
    tq = min(t, ATT_TQ) * bb
    nq = bb * t // tq
    assert b % bb == 0 and (bb == 1 or nq == 1)
    gw = ATT_GROUPS * HEAD_DIM
    in_specs = [
        pl.BlockSpec((tq, hk * gw), lambda bi, qi=0: (bi * nq + qi, OFF_AQ // (hk * gw))),
        pl.BlockSpec((bb * t, hk * HEAD_DIM), lambda bi, qi=0: (bi, OFF_AK // (hk * HEAD_DIM))),
        pl.BlockSpec((bb * t, hk * HEAD_DIM), lambda bi, qi=0: (bi, OFF_AV // (hk * HEAD_DIM))),
    ]
    in_specs += [pl.BlockSpec((tq, gw), lambda bi, qi=0, j=j: (bi * nq + qi, OFF_AG // gw + j)) for j in range(hk)]
    in_specs += [pl.BlockSpec((1, HEAD_DIM), lambda bi, qi=0: (0, 0))] * 2
    args = [z] * (3 + hk) + [q_norm_l.reshape(1, HEAD_DIM), k_norm_l.reshape(1, HEAD_DIM)]
    tk = t
    if latent:
        cos, sin = rope
        ck, cv, layer = cache
        past = ck.shape[2]
        tk = past + t
        in_specs += [
            pl.BlockSpec((t, HEAD_DIM), lambda bi, qi=0: (0, 0)),
            pl.BlockSpec((t, HEAD_DIM), lambda bi, qi=0: (0, 0)),
            pl.BlockSpec((None, None, past, hk * HEAD_DIM), lambda bi, qi=0: (bi, layer, 0, 0)),
            pl.BlockSpec((None, None, past, hk * HEAD_DIM), lambda bi, qi=0: (bi, layer, 0, 0)),
        ]
        args += [cos, sin, ck, cv]
    out_specs = [pl.BlockSpec((tq, hk * gw), lambda bi, qi=0: (bi * nq + qi, 0))]
    out_shape = [jax.ShapeDtypeStruct((b * t, ATT_W), BF16)]
    n_prev = 0 if prev_kv is None else prev_kv[0].shape[1]
    if not latent:
        kv_spec = lambda layers: pl.BlockSpec((bb, layers, t, ATT_KV_HEADS, HEAD_DIM), lambda bi, qi=0: (bi, 0, 0, 0, 0))
        if n_prev:
            in_specs += [kv_spec(n_prev)] * 2
            args += list(prev_kv)
        out_specs += [kv_spec(n_prev + 1)] * 2
        out_shape += [jax.ShapeDtypeStruct((b, n_prev + 1, t, ATT_KV_HEADS, HEAD_DIM), F32)] * 2
    kern = functools.partial(_attn_kernel, t=t, tq=tq, nq=nq, bb=bb, latent=latent, n_prev=n_prev)
    scratch = [pltpu.VMEM((bb * hk, tk, HEAD_DIM), BF16), pltpu.VMEM((bb * hk, tk, HEAD_DIM), BF16)]
    grid = (b // bb, nq) if nq > 1 else (b // bb,)
    return (kern, in_specs, args, out_specs, out_shape, scratch), grid


def _split3_dot(tri, x):
    hi = x.astype(BF16)
    r1 = x - hi.astype(F32)
    mid = r1.astype(BF16)
    lo = (r1 - mid.astype(F32)).astype(BF16)
    d = lambda a: jnp.dot(tri, a, preferred_element_type=F32)
    return d(hi) + d(mid) + d(lo)


def _mlstm_kernel(*refs, t, bb, zero_init, side_mod, n_prev):
    refs = list(refs)
    if side_mod:
        mo_ref = refs.pop()
        cond_ref, wm_ref, bm_ref = refs[4:7]
        refs = refs[:4] + refs[7:]
        _mod_kernel(cond_ref, wm_ref, bm_ref, mo_ref)
    if n_prev:
        c_prev, n_prev_ref, m_prev = refs[4:7]
        refs = refs[:4] + refs[7:]
        c_out, n_out, m_out = refs[-3:]
        c_out[:, 0:n_prev] = c_prev[...]
        n_out[:, 0:n_prev] = n_prev_ref[...]
        m_out[:, 0:n_prev] = m_prev[...]
    for bi in range(bb):
        _mlstm_one(bi, refs, t=t, zero_init=zero_init, n_prev=n_prev)


def _mlstm_one(bi, refs, *, t, zero_init, n_prev):
    if zero_init:
        z_ref, zif_ref, bif_ref, nw_ref, o_ref, c_out, n_out, m_out = refs
    else:
        z_ref, zif_ref, bif_ref, nw_ref, c0_ref, n0_ref, m0_ref, o_ref, c_out, n_out, m_out = refs
    seq = slice(bi * t, (bi + 1) * t)
    L = ML_CHUNK
    nc = t // L
    chunk = lambda a, c: a[c * L:(c + 1) * L]

    gates = zif_ref[seq, :] + bif_ref[...]
    logsig = (jnp.minimum(gates, 0.0) - jnp.log1p(jnp.exp(-jnp.abs(gates)))) * LOG2E
    gates = gates * LOG2E
    gates_t = gates.T
    r_i = lax.broadcasted_iota(jnp.int32, (L, L), 0)
    c_i = lax.broadcasted_iota(jnp.int32, (L, L), 1)
    lower, upper = r_i >= c_i, r_i <= c_i
    cum_tris = (lower.astype(BF16), upper.astype(BF16))
    dmasks = (upper, lower)
    cums = [[_split3_dot(cum_tris[d], chunk(logsig, c)) for c in range(nc)] for d in range(2)]
    cums_t = [[x.T for x in row] for row in cums]
    ext_rows = (lax.broadcasted_iota(jnp.int32, (ML_EXT - ML_DV, t), 0) == 0).astype(F32)
    lchunk = lambda a, c: a[:, c * L:(c + 1) * L]

    def running_max(a, d):
        rows = a.shape[0]
        if rows < 8:
            a = jnp.concatenate([a] * (8 // rows), axis=0)
        lane = lax.broadcasted_iota(jnp.int32, a.shape, 1)
        k = 1
        while k < L:
            if d:
                shifted = jnp.where(lane < L - k, pltpu.roll(a, L - k, axis=1), -jnp.inf)
            else:
                shifted = jnp.where(lane >= k, pltpu.roll(a, k, axis=1), -jnp.inf)
            a = jnp.maximum(a, shifted)
            k *= 2
        return a[:rows]

    hsl = lambda d, which: slice((2 * d + which) * ML_HEADS, (2 * d + which + 1) * ML_HEADS)
    a_maxes = [running_max(jnp.concatenate([lchunk(gates_t, c)[hsl(d, 0)] - cums_t[d][c][hsl(d, 1)]
                                            for c in range(nc)], axis=0), d) for d in range(2)]

    def local_part(qk_c, kb_c, vt_c, vtb_c, b_row, b_col, i_row, i_col, a_max, d):
        m_loc = b_row + a_max
        s = (qk_c * jnp.exp2(jnp.where(dmasks[d], (i_col - b_col) - a_max, -jnp.inf))).astype(BF16)
        bm = jnp.dot(vtb_c, s, preferred_element_type=F32)
        b_end = b_row[:, 0:1] if d else b_row[:, L - 1:L]
        g = b_end - b_row + i_row
        g_max = jnp.max(g, axis=1, keepdims=True)
        vw = (vt_c * jnp.exp2(g - g_max)).astype(BF16)
        u = jnp.dot(vw, kb_c, preferred_element_type=F32)
        return m_loc, bm, b_end, g_max, u

    def scan_step(qt_c, b_row, loc, c_t, m):
        m_loc, bm, b_end, g_max, u = loc
        inter = b_row + m
        m_row = jnp.maximum(inter, m_loc)
        nd = jnp.exp2(m_loc - m_row) * bm
        if c_t is not None:
            nd = nd + jnp.exp2(inter - m_row) * jnp.dot(c_t.astype(BF16), qt_c, preferred_element_type=F32)
        h_t = nd[:ML_DV] / jnp.maximum(jnp.abs(nd[ML_DV:ML_DV + 1]), jnp.exp2(-m_row))
        m_new = jnp.maximum(b_end + m, g_max)
        c_new = jnp.exp2(g_max - m_new) * u
        if c_t is not None:
            c_new = c_new + jnp.exp2(b_end + m - m_new) * c_t
        return h_t, c_new, m_new

    chains = [(hd, d) for hd in range(ML_HEADS) for d in range(2)]
    qts, b_rows, locs, state = {}, {}, {}, {}
    for hd in range(ML_HEADS):
        cols = slice(hd * ML_DK, (hd + 1) * ML_DK)
        kcols = slice(ML_W + hd * ML_DK, ML_W + (hd + 1) * ML_DK)
        vcols = slice(2 * ML_W + hd * ML_DV, 2 * ML_W + (hd + 1) * ML_DV)
        qt = z_ref[seq, cols].T.astype(BF16)
        kb = z_ref[seq, kcols].astype(BF16)
        vt = jnp.concatenate([z_ref[seq, vcols].T, ext_rows], axis=0)
        vtb = vt.astype(BF16)
        qk = [jnp.dot(chunk(kb, c), lchunk(qt, c), preferred_element_type=F32) for c in range(nc)]
        qts[hd] = qt
        for d in range(2):
            i_lane = 2 * d * ML_HEADS + hd
            f_lane = (2 * d + 1) * ML_HEADS + hd
            b_rows[hd, d] = [cums_t[d][c][f_lane:f_lane + 1, :] for c in range(nc)]
            locs[hd, d] = [local_part(qk[c], chunk(kb, c), lchunk(vt, c), lchunk(vtb, c), b_rows[hd, d][c],
                                      cums[d][c][:, f_lane:f_lane + 1], lchunk(gates_t, c)[i_lane:i_lane + 1, :],
                                      chunk(gates, c)[:, i_lane:i_lane + 1],
                                      a_maxes[d][c * ML_HEADS + hd:c * ML_HEADS + hd + 1], d) for c in range(nc)]
            if zero_init:
                state[hd, d] = (None, jnp.zeros((1, 1), F32))
            else:
                c_t = jnp.concatenate([c0_ref[d, hd].T, n0_ref[d, hd:hd + 1, :],
                                       jnp.zeros((ML_EXT - ML_DV - 1, ML_DK), F32)], axis=0)
                state[hd, d] = (c_t, m0_ref[d:d + 1, hd:hd + 1] * LOG2E)

    h_parts = {ch: [None] * nc for ch in chains}
    for step in range(nc):
        for hd, d in chains:
            c = nc - 1 - step if d else step
            c_t, m = state[hd, d]
            h_parts[hd, d][c], c_t, m = scan_step(lchunk(qts[hd], c), b_rows[hd, d][c], locs[hd, d][c], c_t, m)
            state[hd, d] = (c_t, m)

    for hd, d in chains:
        c_t, m = state[hd, d]
        c_out[bi, n_prev, d, hd] = c_t[:ML_DV].T
        n_out[bi, n_prev, d, hd:hd + 1, :] = c_t[ML_DV:ML_DV + 1]
        m_out[bi, n_prev, d:d + 1, hd:hd + 1] = m * LN2
    for hd in range(ML_HEADS):
        cols = slice(hd * ML_DK, (hd + 1) * ML_DK)
        ocols = slice(3 * ML_W + hd * ML_DV, 3 * ML_W + (hd + 1) * ML_DV)
        h_fwd, h_bwd = (jnp.concatenate(h_parts[hd, d], axis=1) if nc > 1 else h_parts[hd, d][0] for d in range(2))
        h_sum = h_fwd + h_bwd
        hm = _rms(h_sum.T, nw_ref[:, cols])
        o_ref[seq, cols] = (hm * z_ref[seq, ocols]).astype(BF16)


def _mlstm(z, b_if_l, ml_norm_l, *, b, t, bb, state=None, side=None, prev=None):
    zero_init = state is None
    n_prev = 0 if prev is None else prev[0].shape[1]
    assert b % bb == 0 and (zero_init or bb == 1)
    steps = b // bb
    zw = 5 * ML_W
    in_specs = [
        pl.BlockSpec((bb * t, zw), lambda bi: (bi, OFF_MQ // zw)),
        pl.BlockSpec((bb * t, LANES), lambda bi: (bi, OFF_IF // LANES)),
        pl.BlockSpec((1, LANES), lambda bi: (0, 0)),
        pl.BlockSpec((1, ML_W), lambda bi: (0, 0)),
    ]
    bif = jnp.pad(b_if_l, (0, LANES - N_GATES)).reshape(1, LANES)
    args = [z, z, bif, ml_norm_l.reshape(1, ML_W)]
    out_specs = [
        pl.BlockSpec((bb * t, ML_W), lambda bi: (bi, 0)),
        pl.BlockSpec((bb, n_prev + 1, 2, ML_HEADS, ML_DK, ML_DV), lambda bi: (bi, 0, 0, 0, 0, 0)),
        pl.BlockSpec((bb, n_prev + 1, 2, ML_HEADS, ML_DK), lambda bi: (bi, 0, 0, 0, 0)),
        pl.BlockSpec((bb, n_prev + 1, 2, ML_HEADS), lambda bi: (bi, 0, 0, 0)),
    ]
    out_shape = [
        jax.ShapeDtypeStruct((b * t, ML_W), BF16),
        jax.ShapeDtypeStruct((b, n_prev + 1, 2, ML_HEADS, ML_DK, ML_DV), F32),
        jax.ShapeDtypeStruct((b, n_prev + 1, 2, ML_HEADS, ML_DK), F32),
        jax.ShapeDtypeStruct((b, n_prev + 1, 2, ML_HEADS), F32),
    ]
    if side is not None:
        cond8, w_mod, b_mod, side_layer = side
        n = 3 * D_MODEL
        tn = n // steps
        assert tn % LANES == 0 and tn * steps == n
        in_specs += [
            pl.BlockSpec((8, D_MODEL), lambda bi: (0, 0)),
            pl.BlockSpec((None, D_MODEL, tn), lambda bi: (side_layer, 0, bi)),
            pl.BlockSpec((None, 1, tn), lambda bi: (side_layer, 0, bi)),
        ]
        args += [cond8, w_mod, b_mod.reshape(DEPTH, 1, n)]
        out_specs.append(pl.BlockSpec((8, tn), lambda bi: (0, bi)))
        out_shape.append(jax.ShapeDtypeStruct((8, n), F32))
    if n_prev:
        in_specs += [
            pl.BlockSpec((bb, n_prev, 2, ML_HEADS, ML_DK, ML_DV), lambda bi: (bi, 0, 0, 0, 0, 0)),
            pl.BlockSpec((bb, n_prev, 2, ML_HEADS, ML_DK), lambda bi: (bi, 0, 0, 0, 0)),
            pl.BlockSpec((bb, n_prev, 2, ML_HEADS), lambda bi: (bi, 0, 0, 0)),
        ]
        args += list(prev)
    if not zero_init:
        c0, n0, m0, layer = state
        in_specs += [
            pl.BlockSpec((None, None, 2, ML_HEADS, ML_DK, ML_DV), lambda bi: (bi, layer, 0, 0, 0, 0)),
            pl.BlockSpec((None, None, 2, ML_HEADS, ML_DK), lambda bi: (bi, layer, 0, 0, 0)),
            pl.BlockSpec((None, None, 2, ML_HEADS), lambda bi: (bi, layer, 0, 0)),
        ]
        args += [c0, n0, m0]
    kern = functools.partial(_mlstm_kernel, t=t, bb=bb, zero_init=zero_init, side_mod=side is not None, n_prev=n_prev)
    return kern, in_specs, args, out_specs, out_shape, []


def _dft_tables(t):
    def cs(n):
        idx = np.arange(n, dtype=np.int64)
        ang = 2.0 * np.pi * ((idx[:, None] * idx[None, :]) % n).astype(np.float64) / n
        return np.cos(ang), np.sin(ang)
    cc, sc = cs(FO_GC)
    ct, st = cs(t)
    return (np.concatenate([cc, sc], axis=1).astype(np.float32),
            np.concatenate([ct, -st], axis=1).astype(np.float32))


FO_RT = 256


def _fourier_kernel(zx_ref, zga_ref, zgb_ref, wc_ref, wt_ref, wf_ref, o_ref, y_ref, *, t, bb):
    wc = wc_ref[...]
    for bi in range(bb):
        x = zx_ref[bi * t:(bi + 1) * t, FO_SHIFT:FO_SHIFT + FO_W]
        for g in range(FO_GROUPS):
            cols = slice(g * FO_GC, (g + 1) * FO_GC)
            ycols = slice(bi * FO_W + g * FO_GC, bi * FO_W + (g + 1) * FO_GC)
            y = jnp.dot(x[:, cols].astype(BF16), wc, preferred_element_type=F32)
            y_ref[0:t, ycols] = y[:, :FO_GC].astype(BF16)
            y_ref[t:2 * t, ycols] = y[:, FO_GC:].astype(BF16)
    scale = (t * FO_GC) ** -0.5
    for r in range(t // FO_RT):
        rows = slice(r * FO_RT, (r + 1) * FO_RT)
        f = jnp.dot(wt_ref[rows, :], y_ref[...], preferred_element_type=F32) * scale
        for bi in range(bb):
            orows = slice(bi * t + r * FO_RT, bi * t + (r + 1) * FO_RT)
            fg = jnp.concatenate([zga_ref[orows, FO_SHIFT:], zgb_ref[orows, :FO_SHIFT]], axis=1)
            for g in range(FO_GROUPS):
                cols = slice(g * FO_GC, (g + 1) * FO_GC)
                fcols = slice(bi * FO_W + g * FO_GC, bi * FO_W + (g + 1) * FO_GC)
                og = jnp.dot(f[:, fcols].astype(BF16), wf_ref[g].astype(BF16), preferred_element_type=F32)
                o_ref[orows, cols] = (og * _silu(fg[:, cols])).astype(BF16)


def _fourier(z, z_tail, w_fno_l, *, b, t, bb):
    assert OFF_FG - FO_SHIFT + FO_W == Z_W and Z_TAIL == FO_SHIFT
    wc, wt = _dft_tables(t)
    xw = FO_W + LANES
    in_specs = [
        pl.BlockSpec((bb * t, xw), lambda bi: (bi, (OFF_FX - FO_SHIFT) // xw)),
        pl.BlockSpec((bb * t, FO_W), lambda bi: (bi, (OFF_FG - FO_SHIFT) // FO_W)),
        pl.BlockSpec((bb * t, LANES), lambda bi: (bi, 0)),
        pl.BlockSpec((FO_GC, 2 * FO_GC), lambda bi: (0, 0)),
        pl.BlockSpec((t, 2 * t), lambda bi: (0, 0)),
        pl.BlockSpec((FO_GROUPS, FO_GC, FO_GC), lambda bi: (0, 0, 0)),
    ]
    args = [z, z, z_tail, jnp.asarray(wc).astype(BF16), jnp.asarray(wt).astype(BF16), w_fno_l]
    return (functools.partial(_fourier_kernel, t=t, bb=bb), in_specs, args,
            [pl.BlockSpec((bb * t, FO_W), lambda bi: (bi, 0))], [jax.ShapeDtypeStruct((b * t, FO_W), BF16)],
            [pltpu.VMEM((2 * t, bb * FO_W), BF16)])


OUT_TN = 1024
OUT_TM = 1024


def _outproj_kernel(oa_ref, om_ref, of_ref, wa_ref, wm_ref, wf_ref, x_ref, gate_ref, y_ref, wb_ref, *, per_row_mod):
    i = pl.program_id(1)

    @pl.when(i == 0)
    def _():
        wb_ref[0:ATT_W, :] = wa_ref[...].astype(BF16)
        wb_ref[ATT_W:ATT_W + ML_W, :] = wm_ref[...].astype(BF16)
        wb_ref[ATT_W + ML_W:D_MIX, :] = wf_ref[...].astype(BF16)

    row = 1 + (i * OUT_TM) // ROW_TILE if per_row_mod else 0
    gate = gate_ref[pl.ds(row, 1), :]
    y = (jnp.dot(oa_ref[...], wb_ref[0:ATT_W, :], preferred_element_type=F32)
         + jnp.dot(om_ref[...], wb_ref[ATT_W:ATT_W + ML_W, :], preferred_element_type=F32)
         + jnp.dot(of_ref[...], wb_ref[ATT_W + ML_W:D_MIX, :], preferred_element_type=F32))
    y_ref[...] = x_ref[...] + gate * y


def _outproj(o_att, o_ml, o_fo, w_out, layer, x2d, mod_l, *, per_row_mod):
    m = x2d.shape[0]
    return pl.pallas_call(
        functools.partial(_outproj_kernel, per_row_mod=per_row_mod),
        grid=(D_MODEL // OUT_TN, m // OUT_TM),
        in_specs=[
            pl.BlockSpec((OUT_TM, ATT_W), lambda n, i: (i, 0)),
            pl.BlockSpec((OUT_TM, ML_W), lambda n, i: (i, 0)),
            pl.BlockSpec((OUT_TM, FO_W), lambda n, i: (i, 0)),
            pl.BlockSpec((None, ATT_W, OUT_TN), lambda n, i: (layer, 0, n)),
            pl.BlockSpec((None, ML_W, OUT_TN), lambda n, i: (layer, ATT_W // ML_W, n)),
            pl.BlockSpec((None, FO_W, OUT_TN), lambda n, i: (layer, (ATT_W + ML_W) // FO_W, n)),
            pl.BlockSpec((OUT_TM, OUT_TN), lambda n, i: (i, n)),
            pl.BlockSpec((8, OUT_TN), lambda n, i: (0, 2 * D_MODEL // OUT_TN + n)),
        ],
        out_specs=pl.BlockSpec((OUT_TM, OUT_TN), lambda n, i: (i, n)),
        out_shape=jax.ShapeDtypeStruct((m, D_MODEL), F32),
        scratch_shapes=[pltpu.VMEM((D_MIX, OUT_TN), BF16)],
        compiler_params=_cparams("parallel", "arbitrary"),
        name="outproj",
    )(o_att, o_ml, o_fo, w_out, w_out, w_out, x2d, mod_l)


def _rope_tables(t):
    half = HEAD_DIM // 2
    inv_freq = ROPE_BASE ** (-jnp.arange(0, half, 2, dtype=F32) / half)
    n_rows = t // GRID_W
    rows = jnp.repeat(jnp.arange(n_rows, dtype=F32), GRID_W)
    cols = jnp.tile(jnp.arange(GRID_W, dtype=F32), n_rows)
    ar = rows[:, None] * inv_freq
    ac = cols[:, None] * inv_freq
    cos = jnp.concatenate([jnp.cos(ar), jnp.cos(ar), jnp.cos(ac), jnp.cos(ac)], axis=1)
    sin = jnp.concatenate([-jnp.sin(ar), jnp.sin(ar), -jnp.sin(ac), jnp.sin(ac)], axis=1)
    return cos, sin


def _run_parts(parts, grid, name):
    n_in = [len(p[1]) for p in parts]
    n_out = [len(p[3]) for p in parts]
    n_scr = [len(p[5]) for p in parts]

    def split(seq, counts):
        out, k = [], 0
        for c in counts:
            out.append(list(seq[k:k + c]))
            k += c
        return out

    def body(*refs):
        ins = split(refs[:sum(n_in)], n_in)
        outs = split(refs[sum(n_in):sum(n_in) + sum(n_out)], n_out)
        scrs = split(refs[sum(n_in) + sum(n_out):], n_scr)
        for part, i, o, s in zip(parts, ins, outs, scrs):
            part[0](*i, *o, *s)

    cat = lambda k: [x for p in parts for x in p[k]]
    results = pl.pallas_call(
        body, grid=grid, in_specs=cat(1), out_specs=cat(3), out_shape=cat(4), scratch_shapes=cat(5),
        compiler_params=_cparams("parallel", *["arbitrary"] * (len(grid) - 1)), name=name,
    )(*cat(2))
    return split(results, n_out)


def _layer(x2d, mod_l, lw, layer, *, b, t, latent, rope=None, cache=None, state=None, side=None, prev=None):
    norm_w, w_in_t, b_if, conv_w, conv_b, q_norm, k_norm, ml_norm, w_fno, w_out = lw
    prev_kv, prev_state = (None, None) if prev is None else (prev[:2], prev[2:])
    z, z_tail = _inproj(x2d, mod_l, norm_w, w_in_t, layer, conv_w, conv_b, per_row_mod=latent, t=t)
    bb = 1 if latent else max(1, ML_ROWS // t)
    att, att_grid = _attention(z, q_norm, k_norm, b=b, t=t, rope=rope, cache=cache, prev_kv=prev_kv)
    mixers = [_mlstm(z, b_if, ml_norm, b=b, t=t, bb=bb, state=state, side=side, prev=prev_state),
              _fourier(z, z_tail, w_fno, b=b, t=t, bb=bb)]
    (o_att, *kv_new), = _run_parts([att], att_grid, "attention")
    (o_ml, c_f, n_f, m_f, *side_mod), (o_fo,) = _run_parts(mixers, (b // bb,), "mixers")
    y = _outproj(o_att, o_ml, o_fo, w_out, layer, x2d, mod_l, per_row_mod=latent)
    return y, (*kv_new, c_f, n_f, m_f), side_mod


def kernel(x_prompt, x_sample, cache_k, cache_v, state_C, state_n, state_m, c, c_ctx, norm_w, w_mod, b_mod, w_in,
           b_if, conv_w, conv_b, q_norm, k_norm, ml_norm, w_fno, w_out):
    bp, tp, _ = x_prompt.shape
    bs, ts, _ = x_sample.shape
    assert tp % ML_CHUNK == 0 and ts == ROW_TILE and (bp * tp) % ROW_TILE == 0 and bs + 1 <= 8
    past = cache_k.shape[2]

    cond8 = jnp.concatenate([c_ctx[None, :], c, jnp.zeros((8 - 1 - bs, D_MODEL), F32)], axis=0)
    mod_l = _modulation(cond8, w_mod, b_mod, 0)

    rope = _rope_tables(ts)
    ck = cache_k.reshape(bs, DEPTH, past, ATT_KV_W)
    cv = cache_v.reshape(bs, DEPTH, past, ATT_KV_W)
    w_in_t = jnp.swapaxes(w_in, 1, 2)

    xp = x_prompt.reshape(bp * tp, D_MODEL)
    xs = x_sample.reshape(bs * ts, D_MODEL)
    new = None
    for l in range(DEPTH):
        lw = (norm_w[l], w_in_t, b_if[l], conv_w[l], conv_b[l], q_norm[l], k_norm[l], ml_norm[l], w_fno[l], w_out)
        side = (cond8, w_mod, b_mod, l + 1) if l + 1 < DEPTH else None
        xp, new, mod_next = _layer(xp, mod_l, lw, l, b=bp, t=tp, latent=False, side=side, prev=new)
        xs, _, _ = _layer(xs, mod_l, lw, l, b=bs, t=ts, latent=True, rope=rope,
                          cache=(ck, cv, l), state=(state_C, state_n, state_m, l))
        if mod_next:
            mod_l = mod_next[0]
    return (xp.reshape(bp, tp, D_MODEL), xs.reshape(bs, ts, D_MODEL), *new)
```

```python
import functools

import numpy as np
import jax
import jax.numpy as jnp
from jax import lax
from jax.experimental import pallas as pl
from jax.experimental.pallas import tpu as pltpu

D_MODEL = 2048
DEPTH = 2
GRID_W = 64
HEAD_DIM = 128
ATT_HEADS = 8
ATT_KV_HEADS = 2
ATT_GROUPS = ATT_HEADS // ATT_KV_HEADS
ATT_W = ATT_HEADS * HEAD_DIM
ATT_KV_W = ATT_KV_HEADS * HEAD_DIM
ML_HEADS = 4
ML_DK = 128
ML_DV = 128
ML_W = ML_HEADS * ML_DV
FO_GROUPS = 4
FO_GC = 128
FO_W = FO_GROUPS * FO_GC
D_MIX = ATT_W + ML_W + FO_W
N_GATES = 4 * ML_HEADS
D_PROJ = 2 * ATT_W + 2 * ATT_KV_W + 5 * ML_W + N_GATES + 2 * FO_W
CONV_W = 3
ROPE_BASE = 10000.0
EPS = 1e-6

LANES = 128
MXU_N = 256
OFF_AQ = 0
OFF_AK = OFF_AQ + ATT_W
OFF_AV = OFF_AK + ATT_KV_W
OFF_AG = OFF_AV + ATT_KV_W
OFF_MQ = OFF_AG + ATT_W
OFF_MK = OFF_MQ + ML_W
OFF_MV = OFF_MK + ML_W
OFF_MO = OFF_MV + ML_W
OFF_MG = OFF_MO + ML_W
OFF_IF = OFF_MG + ML_W
OFF_FX = OFF_IF + N_GATES
OFF_FG = OFF_FX + FO_W
FO_SHIFT = OFF_FX % LANES

PROJ_TN = 1024
Z_W = (D_PROJ // PROJ_TN) * PROJ_TN
Z_TAIL = D_PROJ - Z_W
ROW_TILE = 1024
ML_CHUNK = 256
ML_EXT = ML_DV + 16
ML_ROWS = 512
ATT_TQ = 512
ATT_SHORT_ROWS = 512
ATT_SUB = 256
ATT_AHEAD = 1
VMEM_LIMIT = 56 * 1024 * 1024

BF16 = jnp.bfloat16
F32 = jnp.float32
NT_DIMS = (((1,), (1,)), ((), ()))


def _cparams(*sem):
    return pltpu.CompilerParams(dimension_semantics=sem, vmem_limit_bytes=VMEM_LIMIT)


def _silu(x):
    return x * (1.0 / (1.0 + jnp.exp(-x)))


def _sigmoid(x):
    return 1.0 / (1.0 + jnp.exp(-x))


def _rms(x, w):
    ms = jnp.mean(x * x, axis=-1, keepdims=True)
    return x * lax.rsqrt(ms + EPS) * w


MOD_TN = 768


def _mod_kernel(cond_ref, w_ref, b_ref, o_ref):
    a = _silu(cond_ref[...]).astype(BF16)
    o_ref[...] = jnp.dot(a, w_ref[...].astype(BF16), preferred_element_type=F32) + b_ref[...]


def _modulation(cond8, w_mod, b_mod, layer):
    n = 3 * D_MODEL
    return pl.pallas_call(
        _mod_kernel,
        grid=(n // MOD_TN,),
        in_specs=[
            pl.BlockSpec((8, D_MODEL), lambda j: (0, 0)),
            pl.BlockSpec((None, D_MODEL, MOD_TN), lambda j: (layer, 0, j)),
            pl.BlockSpec((None, 1, MOD_TN), lambda j: (layer, 0, j)),
        ],
        out_specs=pl.BlockSpec((8, MOD_TN), lambda j: (0, j)),
        out_shape=jax.ShapeDtypeStruct((8, n), F32),
        compiler_params=_cparams("parallel"),
        name="modulation",
    )(cond8, w_mod, b_mod.reshape(DEPTH, 1, n))


NORM_ROWS = 16
NORM_PARTS = 4


def _shift_rows(x, direction, period):
    n = x.shape[0]
    pos = lax.broadcasted_iota(jnp.int32, x.shape, 0) % period
    if direction > 0:
        return jnp.where(pos == 0, 0.0, pltpu.roll(x, 1, axis=0))
    return jnp.where(pos == period - 1, 0.0, pltpu.roll(x, n - 1, axis=0))


def _conv_silu(x, w, b, period):
    y = _shift_rows(x, 1, period) * w[0:1, :] + x * w[1:2, :] + _shift_rows(x, -1, period) * w[2:3, :] + b
    return _silu(y)


def _inproj_kernel(x_ref, mod_ref, nw_ref, w_ref, wt_ref, cw_ref, cb_ref, z_ref, zt_ref, h_ref, *, per_row_mod, t):
    i = pl.program_id(0)
    j = pl.program_id(1)

    @pl.when(j == 0)
    def _():
        row = 1 + i if per_row_mod else 0
        shift = mod_ref[pl.ds(row, 1), 0:D_MODEL]
        gain = nw_ref[...] * (1.0 + mod_ref[pl.ds(row, 1), D_MODEL:2 * D_MODEL])

        wb = w_ref[...].astype(BF16)
        part = ROW_TILE // NORM_PARTS
        for p in range(NORM_PARTS):
            for r in range(p * part, (p + 1) * part, NORM_ROWS):
                x = x_ref[r:r + NORM_ROWS, :]
                inv = lax.rsqrt(jnp.mean(x * x, axis=-1, keepdims=True) + EPS)
                h_ref[r:r + NORM_ROWS, :] = (x * inv * gain + shift).astype(BF16)
            rows = slice(p * part, (p + 1) * part)
            z_ref[rows, :] = lax.dot_general(h_ref[rows, :], wb, NT_DIMS, preferred_element_type=F32)

    def project(cols=slice(0, PROJ_TN)):
        return lax.dot_general(h_ref[...], w_ref[cols, :].astype(BF16), NT_DIMS, preferred_element_type=F32)

    half = PROJ_TN // 2
    pieces = lambda start: [slice(c, c + MXU_N) for c in range(start, start + half, MXU_N)]
    j_q, j_k, j_o = OFF_MQ // PROJ_TN, OFF_MK // PROJ_TN, OFF_MO // PROJ_TN

    def conv_tile(start, raw_start, conv_off, scale):
        for cols, rcols in zip(pieces(start), pieces(raw_start)):
            cc = slice(conv_off + cols.start - start, conv_off + cols.stop - start)
            y = _conv_silu(project(cols), cw_ref[:, cc], cb_ref[:, cc], t)
            z_ref[:, cols] = y if scale is None else y * scale
            z_ref[:, rcols] = project(rcols)

    @pl.when(j == j_q)
    def _():
        conv_tile(half, 0, 0, None)

    @pl.when(j == j_k)
    def _():
        conv_tile(0, half, ML_W, ML_DK ** -0.5)

    @pl.when(j == j_o)
    def _():
        z_ref[...] = project()
        z_ref[:, 0:half] = _sigmoid(z_ref[:, 0:half]) * _silu(z_ref[:, half:PROJ_TN])

    @pl.when((j != 0) & (j != j_q) & (j != j_k) & (j != j_o))
    def _():
        z_ref[...] = project()

    @pl.when(j == Z_W // PROJ_TN - 1)
    def _():
        zt = lax.dot_general(h_ref[...], wt_ref[...].astype(BF16), NT_DIMS, preferred_element_type=F32)
        col = lax.broadcasted_iota(jnp.int32, zt.shape, 1)
        zt_ref[...] = jnp.where(col < Z_TAIL, zt, 0.0)


def _inproj(x2d, mod_l, norm_w_l, w_in_t, layer, conv_w_l, conv_b_l, *, per_row_mod, t):
    m = x2d.shape[0]
    half = PROJ_TN // 2
    assert OFF_MQ % PROJ_TN == half and OFF_MK % PROJ_TN == 0 and OFF_MO % PROJ_TN == 0 and ML_W == half
    assert ROW_TILE % t == 0
    return pl.pallas_call(
        functools.partial(_inproj_kernel, per_row_mod=per_row_mod, t=t),
        grid=(m // ROW_TILE, Z_W // PROJ_TN),
        in_specs=[
            pl.BlockSpec((ROW_TILE, D_MODEL), lambda i, j: (i, 0)),
            pl.BlockSpec((8, 3 * D_MODEL), lambda i, j: (0, 0)),
            pl.BlockSpec((1, D_MODEL), lambda i, j: (0, 0)),
            pl.BlockSpec((None, PROJ_TN, D_MODEL), lambda i, j: (layer, j, 0)),
            pl.BlockSpec((None, LANES, D_MODEL), lambda i, j: (layer, Z_W // LANES, 0)),
            pl.BlockSpec((CONV_W, 2 * ML_W), lambda i, j: (0, 0)),
            pl.BlockSpec((1, 2 * ML_W), lambda i, j: (0, 0)),
        ],
        out_specs=[
            pl.BlockSpec((ROW_TILE, PROJ_TN), lambda i, j: (i, j)),
            pl.BlockSpec((ROW_TILE, LANES), lambda i, j: (i, 0)),
        ],
        out_shape=[jax.ShapeDtypeStruct((m, Z_W), F32), jax.ShapeDtypeStruct((m, LANES), F32)],
        scratch_shapes=[pltpu.VMEM((ROW_TILE, D_MODEL), BF16)],
        compiler_params=_cparams("parallel", "arbitrary"),
        name="inproj",
    )(x2d, mod_l, norm_w_l.reshape(1, D_MODEL), w_in_t, w_in_t, conv_w_l, conv_b_l.reshape(1, 2 * ML_W))


def _rope(x, cos, sin):
    lane = lax.broadcasted_iota(jnp.int32, x.shape, 1)
    quarter = HEAD_DIM // 4
    partner = jnp.where((lane % (2 * quarter)) < quarter,
                        pltpu.roll(x, HEAD_DIM - quarter, axis=1), pltpu.roll(x, quarter, axis=1))
    return x * cos + partner * sin


LOG2E = 1.4426950408889634
LN2 = 0.6931471805599453


def _attn_kernel(*refs, t, tq, nq, bb, latent, n_prev):
    hk = ATT_KV_HEADS
    zq_ref, zk_ref, zv_ref = refs[:3]
    zg_refs = refs[3:3 + hk]
    rest = refs[3 + hk:]
    if latent:
        qn_ref, kn_ref, cos_ref, sin_ref, ck_ref, cv_ref, o_ref, kall_ref, vall_ref = rest
        past = ck_ref.shape[0]
    else:
        qn_ref, kn_ref, *prev_kv, o_ref, knew_ref, vnew_ref, kall_ref, vall_ref = rest
        past = 0
    qi = pl.program_id(1) if nq > 1 else 0

    def stage():
        if n_prev:
            knew_ref[:, 0:n_prev] = prev_kv[0][...]
            vnew_ref[:, 0:n_prev] = prev_kv[1][...]
        for bi in range(bb):
            seq = slice(bi * t, (bi + 1) * t)
            for j in range(hk):
                kc = slice(j * HEAD_DIM, (j + 1) * HEAD_DIM)
                slot = bi * hk + j
                k = _rms(zk_ref[seq, kc], kn_ref[...])
                v = zv_ref[seq, kc]
                if latent:
                    k = _rope(k, cos_ref[...], sin_ref[...])
                    kall_ref[slot, 0:past, :] = ck_ref[:, kc].astype(BF16)
                    vall_ref[slot, 0:past, :] = cv_ref[:, kc].astype(BF16)
                else:
                    knew_ref[bi, n_prev, :, j, :] = k
                    vnew_ref[bi, n_prev, :, j, :] = v
                kall_ref[slot, past:past + t, :] = k.astype(BF16)
                vall_ref[slot, past:past + t, :] = v.astype(BF16)

    if nq > 1:
        pl.when(qi == 0)(stage)
    else:
        stage()

    q_scale = (HEAD_DIM ** -0.5) * LOG2E
    sub = min(t, ATT_SUB)

    def scores(r, hd):
        rows = slice(r * sub, (r + 1) * sub)
        cols = slice(hd * HEAD_DIM, (hd + 1) * HEAD_DIM)
        q = _rms(zq_ref[rows, cols], qn_ref[...])
        if latent:
            pos = pl.ds(pl.multiple_of(qi * tq + r * sub, sub), sub)
            q = _rope(q, cos_ref[pos, :], sin_ref[pos, :])
        qb = (q * q_scale).astype(BF16)
        slot = (r * sub // t) * hk + hd // ATT_GROUPS
        return lax.dot_general(qb, kall_ref[slot], NT_DIMS, preferred_element_type=F32)

    def finish(r, hd, s):
        j, g = divmod(hd, ATT_GROUPS)
        rows = slice(r * sub, (r + 1) * sub)
        cols = slice(hd * HEAD_DIM, (hd + 1) * HEAD_DIM)
        gcols = slice(g * HEAD_DIM, (g + 1) * HEAD_DIM)
        p = jnp.exp2(s - jnp.max(s, axis=-1, keepdims=True))
        den = jnp.sum(p, axis=-1, keepdims=True)
        o = jnp.dot(p.astype(BF16), vall_ref[(r * sub // t) * hk + j], preferred_element_type=F32) * (1.0 / den)
        o_ref[rows, cols] = (o * _silu(zg_refs[j][rows, gcols])).astype(BF16)

    items = [(r, hd) for r in range(tq // sub) for hd in range(hk * ATT_GROUPS)]
    pending = [scores(*it) for it in items[:ATT_AHEAD]]
    for n, it in enumerate(items):
        if n + ATT_AHEAD < len(items):
            pending.append(scores(*items[n + ATT_AHEAD]))
        finish(*it, pending.pop(0))


def _attention(z, q_norm_l, k_norm_l, *, b, t, rope=None, cache=None, prev_kv=None):
    latent = rope is not None
    hk = ATT_KV_HEADS
    bb = max(1, ATT_SHORT_ROWS // t) if t < ATT_TQ else 1
    tq = min(t, ATT_TQ) * bb
    nq = bb * t // tq
    assert b % bb == 0 and (bb == 1 or nq == 1)
    gw = ATT_GROUPS * HEAD_DIM
    in_specs = [
        pl.BlockSpec((tq, hk * gw), lambda bi, qi=0: (bi * nq + qi, OFF_AQ // (hk * gw))),
        pl.BlockSpec((bb * t, hk * HEAD_DIM), lambda bi, qi=0: (bi, OFF_AK // (hk * HEAD_DIM))),
        pl.BlockSpec((bb * t, hk * HEAD_DIM), lambda bi, qi=0: (bi, OFF_AV // (hk * HEAD_DIM))),
    ]
    in_specs += [pl.BlockSpec((tq, gw), lambda bi, qi=0, j=j: (bi * nq + qi, OFF_AG // gw + j)) for j in range(hk)]
    in_specs += [pl.BlockSpec((1, HEAD_DIM), lambda bi, qi=0: (0, 0))] * 2
    args = [z] * (3 + hk) + [q_norm_l.reshape(1, HEAD_DIM), k_norm_l.reshape(1, HEAD_DIM)]
    tk = t
    if latent:
        cos, sin = rope
        ck, cv, layer = cache
        past = ck.shape[2]
        tk = past + t
        in_specs += [
            pl.BlockSpec((t, HEAD_DIM), lambda bi, qi=0: (0, 0)),
            pl.BlockSpec((t, HEAD_DIM), lambda bi, qi=0: (0, 0)),
            pl.BlockSpec((None, None, past, hk * HEAD_DIM), lambda bi, qi=0: (bi, layer, 0, 0)),
            pl.BlockSpec((None, None, past, hk * HEAD_DIM), lambda bi, qi=0: (bi, layer, 0, 0)),
        ]
        args += [cos, sin, ck, cv]
    out_specs = [pl.BlockSpec((tq, hk * gw), lambda bi, qi=0: (bi * nq + qi, 0))]
    out_shape = [jax.ShapeDtypeStruct((b * t, ATT_W), BF16)]
    n_prev = 0 if prev_kv is None else prev_kv[0].shape[1]
    if not latent:
        kv_spec = lambda layers: pl.BlockSpec((bb, layers, t, ATT_KV_HEADS, HEAD_DIM), lambda bi, qi=0: (bi, 0, 0, 0, 0))
        if n_prev:
            in_specs += [kv_spec(n_prev)] * 2
            args += list(prev_kv)
        out_specs += [kv_spec(n_prev + 1)] * 2
        out_shape += [jax.ShapeDtypeStruct((b, n_prev + 1, t, ATT_KV_HEADS, HEAD_DIM), F32)] * 2
    kern = functools.partial(_attn_kernel, t=t, tq=tq, nq=nq, bb=bb, latent=latent, n_prev=n_prev)
    scratch = [pltpu.VMEM((bb * hk, tk, HEAD_DIM), BF16), pltpu.VMEM((bb * hk, tk, HEAD_DIM), BF16)]
    grid = (b // bb, nq) if nq > 1 else (b // bb,)
    return (kern, in_specs, args, out_specs, out_shape, scratch), grid


def _split3_dot(tri, x):
    hi = x.astype(BF16)
    r1 = x - hi.astype(F32)
    mid = r1.astype(BF16)
    lo = (r1 - mid.astype(F32)).astype(BF16)
    d = lambda a: jnp.dot(tri, a, preferred_element_type=F32)
    return d(hi) + d(mid) + d(lo)


def _mlstm_kernel(*refs, t, bb, zero_init, side_mod, n_prev):
    refs = list(refs)
    if side_mod:
        mo_ref = refs.pop()
        cond_ref, wm_ref, bm_ref = refs[4:7]
        refs = refs[:4] + refs[7:]
        _mod_kernel(cond_ref, wm_ref, bm_ref, mo_ref)
    if n_prev:
        c_prev, n_prev_ref, m_prev = refs[4:7]
        refs = refs[:4] + refs[7:]
        c_out, n_out, m_out = refs[-3:]
        c_out[:, 0:n_prev] = c_prev[...]
        n_out[:, 0:n_prev] = n_prev_ref[...]
        m_out[:, 0:n_prev] = m_prev[...]
    for bi in range(bb):
        _mlstm_one(bi, refs, t=t, zero_init=zero_init, n_prev=n_prev)


def _mlstm_one(bi, refs, *, t, zero_init, n_prev):
    if zero_init:
        z_ref, zif_ref, bif_ref, nw_ref, o_ref, c_out, n_out, m_out = refs
    else:
        z_ref, zif_ref, bif_ref, nw_ref, c0_ref, n0_ref, m0_ref, o_ref, c_out, n_out, m_out = refs
    seq = slice(bi * t, (bi + 1) * t)
    L = ML_CHUNK
    nc = t // L
    chunk = lambda a, c: a[c * L:(c + 1) * L]

    gates = zif_ref[seq, :] + bif_ref[...]
    logsig = (jnp.minimum(gates, 0.0) - jnp.log1p(jnp.exp(-jnp.abs(gates)))) * LOG2E
    gates = gates * LOG2E
    gates_t = gates.T
    r_i = lax.broadcasted_iota(jnp.int32, (L, L), 0)
    c_i = lax.broadcasted_iota(jnp.int32, (L, L), 1)
    lower, upper = r_i >= c_i, r_i <= c_i
    cum_tris = (lower.astype(BF16), upper.astype(BF16))
    dmasks = (upper, lower)
    cums = [[_split3_dot(cum_tris[d], chunk(logsig, c)) for c in range(nc)] for d in range(2)]
    cums_t = [[x.T for x in row] for row in cums]
    ext_rows = (lax.broadcasted_iota(jnp.int32, (ML_EXT - ML_DV, t), 0) == 0).astype(F32)
    lchunk = lambda a, c: a[:, c * L:(c + 1) * L]

    def running_max(a, d):
        rows = a.shape[0]
        if rows < 8:
            a = jnp.concatenate([a] * (8 // rows), axis=0)
        lane = lax.broadcasted_iota(jnp.int32, a.shape, 1)
        k = 1
        while k < L:
            if d:
                shifted = jnp.where(lane < L - k, pltpu.roll(a, L - k, axis=1), -jnp.inf)
            else:
                shifted = jnp.where(lane >= k, pltpu.roll(a, k, axis=1), -jnp.inf)
            a = jnp.maximum(a, shifted)
            k *= 2
        return a[:rows]

    hsl = lambda d, which: slice((2 * d + which) * ML_HEADS, (2 * d + which + 1) * ML_HEADS)
    a_maxes = [running_max(jnp.concatenate([lchunk(gates_t, c)[hsl(d, 0)] - cums_t[d][c][hsl(d, 1)]
                                            for c in range(nc)], axis=0), d) for d in range(2)]

    def local_part(qk_c, kb_c, vt_c, vtb_c, b_row, b_col, i_row, i_col, a_max, d):
        m_loc = b_row + a_max
        s = (qk_c * jnp.exp2(jnp.where(dmasks[d], (i_col - b_col) - a_max, -jnp.inf))).astype(BF16)
        bm = jnp.dot(vtb_c, s, preferred_element_type=F32)
        b_end = b_row[:, 0:1] if d else b_row[:, L - 1:L]
        g = b_end - b_row + i_row
        g_max = jnp.max(g, axis=1, keepdims=True)
        vw = (vt_c * jnp.exp2(g - g_max)).astype(BF16)
        u = jnp.dot(vw, kb_c, preferred_element_type=F32)
        return m_loc, bm, b_end, g_max, u

    def scan_step(qt_c, b_row, loc, c_t, m):
        m_loc, bm, b_end, g_max, u = loc
        inter = b_row + m
        m_row = jnp.maximum(inter, m_loc)
        nd = jnp.exp2(m_loc - m_row) * bm
        if c_t is not None:
            nd = nd + jnp.exp2(inter - m_row) * jnp.dot(c_t.astype(BF16), qt_c, preferred_element_type=F32)
        h_t = nd[:ML_DV] / jnp.maximum(jnp.abs(nd[ML_DV:ML_DV + 1]), jnp.exp2(-m_row))
        m_new = jnp.maximum(b_end + m, g_max)
        c_new = jnp.exp2(g_max - m_new) * u
        if c_t is not None:
            c_new = c_new + jnp.exp2(b_end + m - m_new) * c_t
        return h_t, c_new, m_new

    chains = [(hd, d) for hd in range(ML_HEADS) for d in range(2)]
    qts, b_rows, locs, state = {}, {}, {}, {}
    for hd in range(ML_HEADS):
        cols = slice(hd * ML_DK, (hd + 1) * ML_DK)
        kcols = slice(ML_W + hd * ML_DK, ML_W + (hd + 1) * ML_DK)
        vcols = slice(2 * ML_W + hd * ML_DV, 2 * ML_W + (hd + 1) * ML_DV)
        qt = z_ref[seq, cols].T.astype(BF16)
        kb = z_ref[seq, kcols].astype(BF16)
        vt = jnp.concatenate([z_ref[seq, vcols].T, ext_rows], axis=0)
        vtb = vt.astype(BF16)
        qk = [jnp.dot(chunk(kb, c), lchunk(qt, c), preferred_element_type=F32) for c in range(nc)]
        qts[hd] = qt
        for d in range(2):
            i_lane = 2 * d * ML_HEADS + hd
            f_lane = (2 * d + 1) * ML_HEADS + hd
            b_rows[hd, d] = [cums_t[d][c][f_lane:f_lane + 1, :] for c in range(nc)]
            locs[hd, d] = [local_part(qk[c], chunk(kb, c), lchunk(vt, c), lchunk(vtb, c), b_rows[hd, d][c],
                                      cums[d][c][:, f_lane:f_lane + 1], lchunk(gates_t, c)[i_lane:i_lane + 1, :],
                                      chunk(gates, c)[:, i_lane:i_lane + 1],
                                      a_maxes[d][c * ML_HEADS + hd:c * ML_HEADS + hd + 1], d) for c in range(nc)]
            if zero_init:
                state[hd, d] = (None, jnp.zeros((1, 1), F32))
            else:
                c_t = jnp.concatenate([c0_ref[d, hd].T, n0_ref[d, hd:hd + 1, :],
                                       jnp.zeros((ML_EXT - ML_DV - 1, ML_DK), F32)], axis=0)
                state[hd, d] = (c_t, m0_ref[d:d + 1, hd:hd + 1] * LOG2E)

    h_parts = {ch: [None] * nc for ch in chains}
    for step in range(nc):
        for hd, d in chains:
            c = nc - 1 - step if d else step
            c_t, m = state[hd, d]
            h_parts[hd, d][c], c_t, m = scan_step(lchunk(qts[hd], c), b_rows[hd, d][c], locs[hd, d][c], c_t, m)
            state[hd, d] = (c_t, m)

    for hd, d in chains:
        c_t, m = state[hd, d]
        c_out[bi, n_prev, d, hd] = c_t[:ML_DV].T
        n_out[bi, n_prev, d, hd:hd + 1, :] = c_t[ML_DV:ML_DV + 1]
        m_out[bi, n_prev, d:d + 1, hd:hd + 1] = m * LN2
    for hd in range(ML_HEADS):
        cols = slice(hd * ML_DK, (hd + 1) * ML_DK)
        ocols = slice(3 * ML_W + hd * ML_DV, 3 * ML_W + (hd + 1) * ML_DV)
        h_fwd, h_bwd = (jnp.concatenate(h_parts[hd, d], axis=1) if nc > 1 else h_parts[hd, d][0] for d in range(2))
        h_sum = h_fwd + h_bwd
        hm = _rms(h_sum.T, nw_ref[:, cols])
        o_ref[seq, cols] = (hm * z_ref[seq, ocols]).astype(BF16)


def _mlstm(z, b_if_l, ml_norm_l, *, b, t, bb, state=None, side=None, prev=None):
    zero_init = state is None
    n_prev = 0 if prev is None else prev[0].shape[1]
    assert b % bb == 0 and (zero_init or bb == 1)
    steps = b // bb
    zw = 5 * ML_W
    in_specs = [
        pl.BlockSpec((bb * t, zw), lambda bi: (bi, OFF_MQ // zw)),
        pl.BlockSpec((bb * t, LANES), lambda bi: (bi, OFF_IF // LANES)),
        pl.BlockSpec((1, LANES), lambda bi: (0, 0)),
        pl.BlockSpec((1, ML_W), lambda bi: (0, 0)),
    ]
    bif = jnp.pad(b_if_l, (0, LANES - N_GATES)).reshape(1, LANES)
    args = [z, z, bif, ml_norm_l.reshape(1, ML_W)]
    out_specs = [
        pl.BlockSpec((bb * t, ML_W), lambda bi: (bi, 0)),
        pl.BlockSpec((bb, n_prev + 1, 2, ML_HEADS, ML_DK, ML_DV), lambda bi: (bi, 0, 0, 0, 0, 0)),
        pl.BlockSpec((bb, n_prev + 1, 2, ML_HEADS, ML_DK), lambda bi: (bi, 0, 0, 0, 0)),
        pl.BlockSpec((bb, n_prev + 1, 2, ML_HEADS), lambda bi: (bi, 0, 0, 0)),
    ]
    out_shape = [
        jax.ShapeDtypeStruct((b * t, ML_W), BF16),
        jax.ShapeDtypeStruct((b, n_prev + 1, 2, ML_HEADS, ML_DK, ML_DV), F32),
        jax.ShapeDtypeStruct((b, n_prev + 1, 2, ML_HEADS, ML_DK), F32),
        jax.ShapeDtypeStruct((b, n_prev + 1, 2, ML_HEADS), F32),
    ]
    if side is not None:
        cond8, w_mod, b_mod, side_layer = side
        n = 3 * D_MODEL
        tn = n // steps
        assert tn % LANES == 0 and tn * steps == n
        in_specs += [
            pl.BlockSpec((8, D_MODEL), lambda bi: (0, 0)),
            pl.BlockSpec((None, D_MODEL, tn), lambda bi: (side_layer, 0, bi)),
            pl.BlockSpec((None, 1, tn), lambda bi: (side_layer, 0, bi)),
        ]
        args += [cond8, w_mod, b_mod.reshape(DEPTH, 1, n)]
        out_specs.append(pl.BlockSpec((8, tn), lambda bi: (0, bi)))
        out_shape.append(jax.ShapeDtypeStruct((8, n), F32))
    if n_prev:
        in_specs += [
            pl.BlockSpec((bb, n_prev, 2, ML_HEADS, ML_DK, ML_DV), lambda bi: (bi, 0, 0, 0, 0, 0)),
            pl.BlockSpec((bb, n_prev, 2, ML_HEADS, ML_DK), lambda bi: (bi, 0, 0, 0, 0)),
            pl.BlockSpec((bb, n_prev, 2, ML_HEADS), lambda bi: (bi, 0, 0, 0)),
        ]
        args += list(prev)
    if not zero_init:
        c0, n0, m0, layer = state
        in_specs += [
            pl.BlockSpec((None, None, 2, ML_HEADS, ML_DK, ML_DV), lambda bi: (bi, layer, 0, 0, 0, 0)),
            pl.BlockSpec((None, None, 2, ML_HEADS, ML_DK), lambda bi: (bi, layer, 0, 0, 0)),
            pl.BlockSpec((None, None, 2, ML_HEADS), lambda bi: (bi, layer, 0, 0)),
        ]
        args += [c0, n0, m0]
    kern = functools.partial(_mlstm_kernel, t=t, bb=bb, zero_init=zero_init, side_mod=side is not None, n_prev=n_prev)
    return kern, in_specs, args, out_specs, out_shape, []


def _dft_tables(t):
    def cs(n):
        idx = np.arange(n, dtype=np.int64)
        ang = 2.0 * np.pi * ((idx[:, None] * idx[None, :]) % n).astype(np.float64) / n
        return np.cos(ang), np.sin(ang)
    cc, sc = cs(FO_GC)
    ct, st = cs(t)
    return (np.concatenate([cc, sc], axis=1).astype(np.float32),
            np.concatenate([ct, -st], axis=1).astype(np.float32))


FO_RT = 256


def _fourier_kernel(zx_ref, zga_ref, zgb_ref, wc_ref, wt_ref, wf_ref, o_ref, y_ref, *, t, bb):
    wc = wc_ref[...]
    for bi in range(bb):
        x = zx_ref[bi * t:(bi + 1) * t, FO_SHIFT:FO_SHIFT + FO_W]
        for g in range(FO_GROUPS):
            cols = slice(g * FO_GC, (g + 1) * FO_GC)
            ycols = slice(bi * FO_W + g * FO_GC, bi * FO_W + (g + 1) * FO_GC)
            y = jnp.dot(x[:, cols].astype(BF16), wc, preferred_element_type=F32)
            y_ref[0:t, ycols] = y[:, :FO_GC].astype(BF16)
            y_ref[t:2 * t, ycols] = y[:, FO_GC:].astype(BF16)
    scale = (t * FO_GC) ** -0.5
    for r in range(t // FO_RT):
        rows = slice(r * FO_RT, (r + 1) * FO_RT)
        f = jnp.dot(wt_ref[rows, :], y_ref[...], preferred_element_type=F32) * scale
        for bi in range(bb):
            orows = slice(bi * t + r * FO_RT, bi * t + (r + 1) * FO_RT)
            fg = jnp.concatenate([zga_ref[orows, FO_SHIFT:], zgb_ref[orows, :FO_SHIFT]], axis=1)
            for g in range(FO_GROUPS):
                cols = slice(g * FO_GC, (g + 1) * FO_GC)
                fcols = slice(bi * FO_W + g * FO_GC, bi * FO_W + (g + 1) * FO_GC)
                og = jnp.dot(f[:, fcols].astype(BF16), wf_ref[g].astype(BF16), preferred_element_type=F32)
                o_ref[orows, cols] = (og * _silu(fg[:, cols])).astype(BF16)


def _fourier(z, z_tail, w_fno_l, *, b, t, bb):
    assert OFF_FG - FO_SHIFT + FO_W == Z_W and Z_TAIL == FO_SHIFT
    wc, wt = _dft_tables(t)
    xw = FO_W + LANES
    in_specs = [
        pl.BlockSpec((bb * t, xw), lambda bi: (bi, (OFF_FX - FO_SHIFT) // xw)),
        pl.BlockSpec((bb * t, FO_W), lambda bi: (bi, (OFF_FG - FO_SHIFT) // FO_W)),
        pl.BlockSpec((bb * t, LANES), lambda bi: (bi, 0)),
        pl.BlockSpec((FO_GC, 2 * FO_GC), lambda bi: (0, 0)),
        pl.BlockSpec((t, 2 * t), lambda bi: (0, 0)),
        pl.BlockSpec((FO_GROUPS, FO_GC, FO_GC), lambda bi: (0, 0, 0)),
    ]
    args = [z, z, z_tail, jnp.asarray(wc).astype(BF16), jnp.asarray(wt).astype(BF16), w_fno_l]
    return (functools.partial(_fourier_kernel, t=t, bb=bb), in_specs, args,
            [pl.BlockSpec((bb * t, FO_W), lambda bi: (bi, 0))], [jax.ShapeDtypeStruct((b * t, FO_W), BF16)],
            [pltpu.VMEM((2 * t, bb * FO_W), BF16)])


OUT_TN = 1024
OUT_TM = 1024


def _outproj_kernel(oa_ref, om_ref, of_ref, wa_ref, wm_ref, wf_ref, x_ref, gate_ref, y_ref, wb_ref, *, per_row_mod):
    i = pl.program_id(1)

    @pl.when(i == 0)
    def _():
        wb_ref[0:ATT_W, :] = wa_ref[...].astype(BF16)
        wb_ref[ATT_W:ATT_W + ML_W, :] = wm_ref[...].astype(BF16)
        wb_ref[ATT_W + ML_W:D_MIX, :] = wf_ref[...].astype(BF16)

    row = 1 + (i * OUT_TM) // ROW_TILE if per_row_mod else 0
    gate = gate_ref[pl.ds(row, 1), :]
    y = (jnp.dot(oa_ref[...], wb_ref[0:ATT_W, :], preferred_element_type=F32)
         + jnp.dot(om_ref[...], wb_ref[ATT_W:ATT_W + ML_W, :], preferred_element_type=F32)
         + jnp.dot(of_ref[...], wb_ref[ATT_W + ML_W:D_MIX, :], preferred_element_type=F32))
    y_ref[...] = x_ref[...] + gate * y


def _outproj(o_att, o_ml, o_fo, w_out, layer, x2d, mod_l, *, per_row_mod):
    m = x2d.shape[0]
    return pl.pallas_call(
        functools.partial(_outproj_kernel, per_row_mod=per_row_mod),
        grid=(D_MODEL // OUT_TN, m // OUT_TM),
        in_specs=[
            pl.BlockSpec((OUT_TM, ATT_W), lambda n, i: (i, 0)),
            pl.BlockSpec((OUT_TM, ML_W), lambda n, i: (i, 0)),
            pl.BlockSpec((OUT_TM, FO_W), lambda n, i: (i, 0)),
            pl.BlockSpec((None, ATT_W, OUT_TN), lambda n, i: (layer, 0, n)),
            pl.BlockSpec((None, ML_W, OUT_TN), lambda n, i: (layer, ATT_W // ML_W, n)),
            pl.BlockSpec((None, FO_W, OUT_TN), lambda n, i: (layer, (ATT_W + ML_W) // FO_W, n)),
            pl.BlockSpec((OUT_TM, OUT_TN), lambda n, i: (i, n)),
            pl.BlockSpec((8, OUT_TN), lambda n, i: (0, 2 * D_MODEL // OUT_TN + n)),
        ],
        out_specs=pl.BlockSpec((OUT_TM, OUT_TN), lambda n, i: (i, n)),
        out_shape=jax.ShapeDtypeStruct((m, D_MODEL), F32),
        scratch_shapes=[pltpu.VMEM((D_MIX, OUT_TN), BF16)],
        compiler_params=_cparams("parallel", "arbitrary"),
        name="outproj",
    )(o_att, o_ml, o_fo, w_out, w_out, w_out, x2d, mod_l)


def _rope_tables(t):
    half = HEAD_DIM // 2
    inv_freq = ROPE_BASE ** (-jnp.arange(0, half, 2, dtype=F32) / half)
    n_rows = t // GRID_W
    rows = jnp.repeat(jnp.arange(n_rows, dtype=F32), GRID_W)
    cols = jnp.tile(jnp.arange(GRID_W, dtype=F32), n_rows)
    ar = rows[:, None] * inv_freq
    ac = cols[:, None] * inv_freq
    cos = jnp.concatenate([jnp.cos(ar), jnp.cos(ar), jnp.cos(ac), jnp.cos(ac)], axis=1)
    sin = jnp.concatenate([-jnp.sin(ar), jnp.sin(ar), -jnp.sin(ac), jnp.sin(ac)], axis=1)
    return cos, sin


def _run_parts(parts, grid, name):
    n_in = [len(p[1]) for p in parts]
    n_out = [len(p[3]) for p in parts]
    n_scr = [len(p[5]) for p in parts]

    def split(seq, counts):
        out, k = [], 0
        for c in counts:
            out.append(list(seq[k:k + c]))
            k += c
        return out

    def body(*refs):
        ins = split(refs[:sum(n_in)], n_in)
        outs = split(refs[sum(n_in):sum(n_in) + sum(n_out)], n_out)
        scrs = split(refs[sum(n_in) + sum(n_out):], n_scr)
        for part, i, o, s in zip(parts, ins, outs, scrs):
            part[0](*i, *o, *s)

    cat = lambda k: [x for p in parts for x in p[k]]
    results = pl.pallas_call(
        body, grid=grid, in_specs=cat(1), out_specs=cat(3), out_shape=cat(4), scratch_shapes=cat(5),
        compiler_params=_cparams("parallel", *["arbitrary"] * (len(grid) - 1)), name=name,
    )(*cat(2))
    return split(results, n_out)


def _layer(x2d, mod_l, lw, layer, *, b, t, latent, rope=None, cache=None, state=None, side=None, prev=None):
    norm_w, w_in_t, b_if, conv_w, conv_b, q_norm, k_norm, ml_norm, w_fno, w_out = lw
    prev_kv, prev_state = (None, None) if prev is None else (prev[:2], prev[2:])
    z, z_tail = _inproj(x2d, mod_l, norm_w, w_in_t, layer, conv_w, conv_b, per_row_mod=latent, t=t)
    bb = 1 if latent else max(1, ML_ROWS // t)
    att, att_grid = _attention(z, q_norm, k_norm, b=b, t=t, rope=rope, cache=cache, prev_kv=prev_kv)
    mixers = [_mlstm(z, b_if, ml_norm, b=b, t=t, bb=bb, state=state, side=side, prev=prev_state),
              _fourier(z, z_tail, w_fno, b=b, t=t, bb=bb)]
    (o_att, *kv_new), = _run_parts([att], att_grid, "attention")
    (o_ml, c_f, n_f, m_f, *side_mod), (o_fo,) = _run_parts(mixers, (b // bb,), "mixers")
    y = _outproj(o_att, o_ml, o_fo, w_out, layer, x2d, mod_l, per_row_mod=latent)
    return y, (*kv_new, c_f, n_f, m_f), side_mod


def kernel(x_prompt, x_sample, cache_k, cache_v, state_C, state_n, state_m, c, c_ctx, norm_w, w_mod, b_mod, w_in,
           b_if, conv_w, conv_b, q_norm, k_norm, ml_norm, w_fno, w_out):
    bp, tp, _ = x_prompt.shape
    bs, ts, _ = x_sample.shape
    assert tp % ML_CHUNK == 0 and ts == ROW_TILE and (bp * tp) % ROW_TILE == 0 and bs + 1 <= 8
    past = cache_k.shape[2]

    cond8 = jnp.concatenate([c_ctx[None, :], c, jnp.zeros((8 - 1 - bs, D_MODEL), F32)], axis=0)
    mod_l = _modulation(cond8, w_mod, b_mod, 0)

    rope = _rope_tables(ts)
    ck = cache_k.reshape(bs, DEPTH, past, ATT_KV_W)
    cv = cache_v.reshape(bs, DEPTH, past, ATT_KV_W)
    w_in_t = jnp.swapaxes(w_in, 1, 2)

    xp = x_prompt.reshape(bp * tp, D_MODEL)
    xs = x_sample.reshape(bs * ts, D_MODEL)
    new = None
    for l in range(DEPTH):
        lw = (norm_w[l], w_in_t, b_if[l], conv_w[l], conv_b[l], q_norm[l], k_norm[l], ml_norm[l], w_fno[l], w_out)
        side = (cond8, w_mod, b_mod, l + 1) if l + 1 < DEPTH else None
        xp, new, mod_next = _layer(xp, mod_l, lw, l, b=bp, t=tp, latent=False, side=side, prev=new)
        xs, _, _ = _layer(xs, mod_l, lw, l, b=bs, t=ts, latent=True, rope=rope,
                          cache=(ck, cv, l), state=(state_C, state_n, state_m, l))
        if mod_next:
            mod_l = mod_next[0]
    return (xp.reshape(bp, tp, D_MODEL), xs.reshape(bs, ts, D_MODEL), *new)
```

```python
import functools

import numpy as np
import jax
import jax.numpy as jnp
from jax import lax
from jax.experimental import pallas as pl
from jax.experimental.pallas import tpu as pltpu

D_MODEL = 2048
DEPTH = 2
GRID_W = 64
HEAD_DIM = 128
ATT_HEADS = 8
ATT_KV_HEADS = 2
ATT_GROUPS = ATT_HEADS // ATT_KV_HEADS
ATT_W = ATT_HEADS * HEAD_DIM
ATT_KV_W = ATT_KV_HEADS * HEAD_DIM
ML_HEADS = 4
ML_DK = 128
ML_DV = 128
ML_W = ML_HEADS * ML_DV
FO_GROUPS = 4
FO_GC = 128
FO_W = FO_GROUPS * FO_GC
D_MIX = ATT_W + ML_W + FO_W
N_GATES = 4 * ML_HEADS
D_PROJ = 2 * ATT_W + 2 * ATT_KV_W + 5 * ML_W + N_GATES + 2 * FO_W
CONV_W = 3
ROPE_BASE = 10000.0
EPS = 1e-6

LANES = 128
MXU_N = 256
OFF_AQ = 0
OFF_AK = OFF_AQ + ATT_W
OFF_AV = OFF_AK + ATT_KV_W
OFF_AG = OFF_AV + ATT_KV_W
OFF_MQ = OFF_AG + ATT_W
OFF_MK = OFF_MQ + ML_W
OFF_MV = OFF_MK + ML_W
OFF_MO = OFF_MV + ML_W
OFF_MG = OFF_MO + ML_W
OFF_IF = OFF_MG + ML_W
OFF_FX = OFF_IF + N_GATES
OFF_FG = OFF_FX + FO_W
FO_SHIFT = OFF_FX % LANES

PROJ_TN = 1024
Z_W = (D_PROJ // PROJ_TN) * PROJ_TN
Z_TAIL = D_PROJ - Z_W
ROW_TILE = 1024
ML_CHUNK = 256
ML_EXT = ML_DV + 16
ML_ROWS = 512
ATT_TQ = 512
ATT_SHORT_ROWS = 512
ATT_SUB = 256
ATT_AHEAD = 1
VMEM_LIMIT = 56 * 1024 * 1024

BF16 = jnp.bfloat16
F32 = jnp.float32
NT_DIMS = (((1,), (1,)), ((), ()))


def _cparams(*sem):
    return pltpu.CompilerParams(dimension_semantics=sem, vmem_limit_bytes=VMEM_LIMIT)


def _silu(x):
    return x * (1.0 / (1.0 + jnp.exp(-x)))


def _sigmoid(x):
    return 1.0 / (1.0 + jnp.exp(-x))


def _rms(x, w):
    ms = jnp.mean(x * x, axis=-1, keepdims=True)
    return x * lax.rsqrt(ms + EPS) * w


MOD_TN = 768


def _mod_kernel(cond_ref, w_ref, b_ref, o_ref):
    a = _silu(cond_ref[...]).astype(BF16)
    o_ref[...] = jnp.dot(a, w_ref[...].astype(BF16), preferred_element_type=F32) + b_ref[...]


def _modulation(cond8, w_mod, b_mod, layer):
    n = 3 * D_MODEL
    return pl.pallas_call(
        _mod_kernel,
        grid=(n // MOD_TN,),
        in_specs=[
            pl.BlockSpec((8, D_MODEL), lambda j: (0, 0)),
            pl.BlockSpec((None, D_MODEL, MOD_TN), lambda j: (layer, 0, j)),
            pl.BlockSpec((None, 1, MOD_TN), lambda j: (layer, 0, j)),
        ],
        out_specs=pl.BlockSpec((8, MOD_TN), lambda j: (0, j)),
        out_shape=jax.ShapeDtypeStruct((8, n), F32),
        compiler_params=_cparams("parallel"),
        name="modulation",
    )(cond8, w_mod, b_mod.reshape(DEPTH, 1, n))


NORM_ROWS = 16
NORM_PARTS = 4


def _shift_rows(x, direction, period):
    n = x.shape[0]
    pos = lax.broadcasted_iota(jnp.int32, x.shape, 0) % period
    if direction > 0:
        return jnp.where(pos == 0, 0.0, pltpu.roll(x, 1, axis=0))
    return jnp.where(pos == period - 1, 0.0, pltpu.roll(x, n - 1, axis=0))


def _conv_silu(x, w, b, period):
    y = _shift_rows(x, 1, period) * w[0:1, :] + x * w[1:2, :] + _shift_rows(x, -1, period) * w[2:3, :] + b
    return _silu(y)


def _inproj_kernel(x_ref, mod_ref, nw_ref, w_ref, wt_ref, cw_ref, cb_ref, z_ref, zt_ref, h_ref, *, per_row_mod, t):
    i = pl.program_id(0)
    j = pl.program_id(1)

    @pl.when(j == 0)
    def _():
        row = 1 + i if per_row_mod else 0
        shift = mod_ref[pl.ds(row, 1), 0:D_MODEL]
        gain = nw_ref[...] * (1.0 + mod_ref[pl.ds(row, 1), D_MODEL:2 * D_MODEL])

        wb = w_ref[...].astype(BF16)
        part = ROW_TILE // NORM_PARTS
        for p in range(NORM_PARTS):
            for r in range(p * part, (p + 1) * part, NORM_ROWS):
                x = x_ref[r:r + NORM_ROWS, :]
                inv = lax.rsqrt(jnp.mean(x * x, axis=-1, keepdims=True) + EPS)
                h_ref[r:r + NORM_ROWS, :] = (x * inv * gain + shift).astype(BF16)
            rows = slice(p * part, (p + 1) * part)
            z_ref[rows, :] = lax.dot_general(h_ref[rows, :], wb, NT_DIMS, preferred_element_type=F32)

    def project(cols=slice(0, PROJ_TN)):
        return lax.dot_general(h_ref[...], w_ref[cols, :].astype(BF16), NT_DIMS, preferred_element_type=F32)

    half = PROJ_TN // 2
    pieces = lambda start: [slice(c, c + MXU_N) for c in range(start, start + half, MXU_N)]
    j_q, j_k, j_o = OFF_MQ // PROJ_TN, OFF_MK // PROJ_TN, OFF_MO // PROJ_TN

    def conv_tile(start, raw_start, conv_off, scale):
        for cols, rcols in zip(pieces(start), pieces(raw_start)):
            cc = slice(conv_off + cols.start - start, conv_off + cols.stop - start)
            y = _conv_silu(project(cols), cw_ref[:, cc], cb_ref[:, cc], t)
            z_ref[:, cols] = y if scale is None else y * scale
            z_ref[:, rcols] = project(rcols)

    @pl.when(j == j_q)
    def _():
        conv_tile(half, 0, 0, None)

    @pl.when(j == j_k)
    def _():
        conv_tile(0, half, ML_W, ML_DK ** -0.5)

    @pl.when(j == j_o)
    def _():
        wb = w_ref[...].astype(BF16)
        part = ROW_TILE // NORM_PARTS
        for p in range(NORM_PARTS):
            rows = slice(p * part, (p + 1) * part)
            z = lax.dot_general(h_ref[rows, :], wb, NT_DIMS, preferred_element_type=F32)
            z_ref[rows, 0:half] = _sigmoid(z[:, 0:half]) * _silu(z[:, half:PROJ_TN])
            z_ref[rows, half:PROJ_TN] = z[:, half:PROJ_TN]

    @pl.when((j != 0) & (j != j_q) & (j != j_k) & (j != j_o))
    def _():
        z_ref[...] = project()

    @pl.when(j == Z_W // PROJ_TN - 1)
    def _():
        zt = lax.dot_general(h_ref[...], wt_ref[...].astype(BF16), NT_DIMS, preferred_element_type=F32)
        col = lax.broadcasted_iota(jnp.int32, zt.shape, 1)
        zt_ref[...] = jnp.where(col < Z_TAIL, zt, 0.0)


def _inproj(x2d, mod_l, norm_w_l, w_in_t, layer, conv_w_l, conv_b_l, *, per_row_mod, t):
    m = x2d.shape[0]
    half = PROJ_TN // 2
    assert OFF_MQ % PROJ_TN == half and OFF_MK % PROJ_TN == 0 and OFF_MO % PROJ_TN == 0 and ML_W == half
    assert ROW_TILE % t == 0
    return pl.pallas_call(
        functools.partial(_inproj_kernel, per_row_mod=per_row_mod, t=t),
        grid=(m // ROW_TILE, Z_W // PROJ_TN),
        in_specs=[
            pl.BlockSpec((ROW_TILE, D_MODEL), lambda i, j: (i, 0)),
            pl.BlockSpec((8, 3 * D_MODEL), lambda i, j: (0, 0)),
            pl.BlockSpec((1, D_MODEL), lambda i, j: (0, 0)),
            pl.BlockSpec((None, PROJ_TN, D_MODEL), lambda i, j: (layer, j, 0)),
            pl.BlockSpec((None, LANES, D_MODEL), lambda i, j: (layer, Z_W // LANES, 0)),
            pl.BlockSpec((CONV_W, 2 * ML_W), lambda i, j: (0, 0)),
            pl.BlockSpec((1, 2 * ML_W), lambda i, j: (0, 0)),
        ],
        out_specs=[
            pl.BlockSpec((ROW_TILE, PROJ_TN), lambda i, j: (i, j)),
            pl.BlockSpec((ROW_TILE, LANES), lambda i, j: (i, 0)),
        ],
        out_shape=[jax.ShapeDtypeStruct((m, Z_W), F32), jax.ShapeDtypeStruct((m, LANES), F32)],
        scratch_shapes=[pltpu.VMEM((ROW_TILE, D_MODEL), BF16)],
        compiler_params=_cparams("parallel", "arbitrary"),
        name="inproj",
    )(x2d, mod_l, norm_w_l.reshape(1, D_MODEL), w_in_t, w_in_t, conv_w_l, conv_b_l.reshape(1, 2 * ML_W))


def _rope(x, cos, sin):
    lane = lax.broadcasted_iota(jnp.int32, x.shape, 1)
    quarter = HEAD_DIM // 4
    partner = jnp.where((lane % (2 * quarter)) < quarter,
                        pltpu.roll(x, HEAD_DIM - quarter, axis=1), pltpu.roll(x, quarter, axis=1))
    return x * cos + partner * sin


LOG2E = 1.4426950408889634
LN2 = 0.6931471805599453


def _attn_kernel(*refs, t, tq, nq, bb, latent, n_prev):
    hk = ATT_KV_HEADS
    zq_ref, zk_ref, zv_ref = refs[:3]
    zg_refs = refs[3:3 + hk]
    rest = refs[3 + hk:]
    if latent:
        qn_ref, kn_ref, cos_ref, sin_ref, ck_ref, cv_ref, o_ref, kall_ref, vall_ref = rest
        past = ck_ref.shape[0]
    else:
        qn_ref, kn_ref, *prev_kv, o_ref, knew_ref, vnew_ref, kall_ref, vall_ref = rest
        past = 0
    qi = pl.program_id(1) if nq > 1 else 0

    def stage():
        if n_prev:
            knew_ref[:, 0:n_prev] = prev_kv[0][...]
            vnew_ref[:, 0:n_prev] = prev_kv[1][...]
        for bi in range(bb):
            seq = slice(bi * t, (bi + 1) * t)
            for j in range(hk):
                kc = slice(j * HEAD_DIM, (j + 1) * HEAD_DIM)
                slot = bi * hk + j
                k = _rms(zk_ref[seq, kc], kn_ref[...])
                v = zv_ref[seq, kc]
                if latent:
                    k = _rope(k, cos_ref[...], sin_ref[...])
                    kall_ref[slot, 0:past, :] = ck_ref[:, kc].astype(BF16)
                    vall_ref[slot, 0:past, :] = cv_ref[:, kc].astype(BF16)
                else:
                    knew_ref[bi, n_prev, :, j, :] = k
                    vnew_ref[bi, n_prev, :, j, :] = v
                kall_ref[slot, past:past + t, :] = k.astype(BF16)
                vall_ref[slot, past:past + t, :] = v.astype(BF16)

    if nq > 1:
        pl.when(qi == 0)(stage)
    else:
        stage()

    q_scale = (HEAD_DIM ** -0.5) * LOG2E
    sub = min(t, ATT_SUB)

    def scores(r, hd):
        rows = slice(r * sub, (r + 1) * sub)
        cols = slice(hd * HEAD_DIM, (hd + 1) * HEAD_DIM)
        q = _rms(zq_ref[rows, cols], qn_ref[...])
        if latent:
            pos = pl.ds(pl.multiple_of(qi * tq + r * sub, sub), sub)
            q = _rope(q, cos_ref[pos, :], sin_ref[pos, :])
        qb = (q * q_scale).astype(BF16)
        slot = (r * sub // t) * hk + hd // ATT_GROUPS
        return lax.dot_general(qb, kall_ref[slot], NT_DIMS, preferred_element_type=F32)

    def finish(r, hd, s):
        j, g = divmod(hd, ATT_GROUPS)
        rows = slice(r * sub, (r + 1) * sub)
        cols = slice(hd * HEAD_DIM, (hd + 1) * HEAD_DIM)
        gcols = slice(g * HEAD_DIM, (g + 1) * HEAD_DIM)
        p = jnp.exp2(s - jnp.max(s, axis=-1, keepdims=True))
        den = jnp.sum(p, axis=-1, keepdims=True)
        o = jnp.dot(p.astype(BF16), vall_ref[(r * sub // t) * hk + j], preferred_element_type=F32) * (1.0 / den)
        o_ref[rows, cols] = (o * _silu(zg_refs[j][rows, gcols])).astype(BF16)

    items = [(r, hd) for r in range(tq // sub) for hd in range(hk * ATT_GROUPS)]
    pending = [scores(*it) for it in items[:ATT_AHEAD]]
    for n, it in enumerate(items):
        if n + ATT_AHEAD < len(items):
            pending.append(scores(*items[n + ATT_AHEAD]))
        finish(*it, pending.pop(0))


def _attention(z, q_norm_l, k_norm_l, *, b, t, rope=None, cache=None, prev_kv=None):
    latent = rope is not None
    hk = ATT_KV_HEADS
    bb = max(1, ATT_SHORT_ROWS // t) if t < ATT_TQ else 1
    tq = min(t, ATT_TQ) * bb
    nq = bb * t // tq
    assert b % bb == 0 and (bb == 1 or nq == 1)
    gw = ATT_GROUPS * HEAD_DIM
    in_specs = [
        pl.BlockSpec((tq, hk * gw), lambda bi, qi=0: (bi * nq + qi, OFF_AQ // (hk * gw))),
        pl.BlockSpec((bb * t, hk * HEAD_DIM), lambda bi, qi=0: (bi, OFF_AK // (hk * HEAD_DIM))),
        pl.BlockSpec((bb * t, hk * HEAD_DIM), lambda bi, qi=0: (bi, OFF_AV // (hk * HEAD_DIM))),
    ]
    in_specs += [pl.BlockSpec((tq, gw), lambda bi, qi=0, j=j: (bi * nq + qi, OFF_AG // gw + j)) for j in range(hk)]
    in_specs += [pl.BlockSpec((1, HEAD_DIM), lambda bi, qi=0: (0, 0))] * 2
    args = [z] * (3 + hk) + [q_norm_l.reshape(1, HEAD_DIM), k_norm_l.reshape(1, HEAD_DIM)]
    tk = t
    if latent:
        cos, sin = rope
        ck, cv, layer = cache
        past = ck.shape[2]
        tk = past + t
        in_specs += [
            pl.BlockSpec((t, HEAD_DIM), lambda bi, qi=0: (0, 0)),
            pl.BlockSpec((t, HEAD_DIM), lambda bi, qi=0: (0, 0)),
            pl.BlockSpec((None, None, past, hk * HEAD_DIM), lambda bi, qi=0: (bi, layer, 0, 0)),
            pl.BlockSpec((None, None, past, hk * HEAD_DIM), lambda bi, qi=0: (bi, layer, 0, 0)),
        ]
        args += [cos, sin, ck, cv]
    out_specs = [pl.BlockSpec((tq, hk * gw), lambda bi, qi=0: (bi * nq + qi, 0))]
    out_shape = [jax.ShapeDtypeStruct((b * t, ATT_W), BF16)]
    n_prev = 0 if prev_kv is None else prev_kv[0].shape[1]
    if not latent:
        kv_spec = lambda layers: pl.BlockSpec((bb, layers, t, ATT_KV_HEADS, HEAD_DIM), lambda bi, qi=0: (bi, 0, 0, 0, 0))
        if n_prev:
            in_specs += [kv_spec(n_prev)] * 2
            args += list(prev_kv)
        out_specs += [kv_spec(n_prev + 1)] * 2
        out_shape += [jax.ShapeDtypeStruct((b, n_prev + 1, t, ATT_KV_HEADS, HEAD_DIM), F32)] * 2
    kern = functools.partial(_attn_kernel, t=t, tq=tq, nq=nq, bb=bb, latent=latent, n_prev=n_prev)
    scratch = [pltpu.VMEM((bb * hk, tk, HEAD_DIM), BF16), pltpu.VMEM((bb * hk, tk, HEAD_DIM), BF16)]
    grid = (b // bb, nq) if nq > 1 else (b // bb,)
    return (kern, in_specs, args, out_specs, out_shape, scratch), grid


def _split3_dot(tri, x):
    hi = x.astype(BF16)
    r1 = x - hi.astype(F32)
    mid = r1.astype(BF16)
    lo = (r1 - mid.astype(F32)).astype(BF16)
    d = lambda a: jnp.dot(tri, a, preferred_element_type=F32)
    return d(hi) + d(mid) + d(lo)


def _mlstm_kernel(*refs, t, bb, zero_init, side_mod, n_prev):
    refs = list(refs)
    if side_mod:
        mo_ref = refs.pop()
        cond_ref, wm_ref, bm_ref = refs[4:7]
        refs = refs[:4] + refs[7:]
        _mod_kernel(cond_ref, wm_ref, bm_ref, mo_ref)
    if n_prev:
        c_prev, n_prev_ref, m_prev = refs[4:7]
        refs = refs[:4] + refs[7:]
        c_out, n_out, m_out = refs[-3:]
        c_out[:, 0:n_prev] = c_prev[...]
        n_out[:, 0:n_prev] = n_prev_ref[...]
        m_out[:, 0:n_prev] = m_prev[...]
    for bi in range(bb):
        _mlstm_one(bi, refs, t=t, zero_init=zero_init, n_prev=n_prev)


def _mlstm_one(bi, refs, *, t, zero_init, n_prev):
    if zero_init:
        z_ref, zif_ref, bif_ref, nw_ref, o_ref, c_out, n_out, m_out = refs
    else:
        z_ref, zif_ref, bif_ref, nw_ref, c0_ref, n0_ref, m0_ref, o_ref, c_out, n_out, m_out = refs
    seq = slice(bi * t, (bi + 1) * t)
    L = ML_CHUNK
    nc = t // L
    chunk = lambda a, c: a[c * L:(c + 1) * L]

    gates = zif_ref[seq, :] + bif_ref[...]
    logsig = (jnp.minimum(gates, 0.0) - jnp.log1p(jnp.exp(-jnp.abs(gates)))) * LOG2E
    gates = gates * LOG2E
    gates_t = gates.T
    r_i = lax.broadcasted_iota(jnp.int32, (L, L), 0)
    c_i = lax.broadcasted_iota(jnp.int32, (L, L), 1)
    lower, upper = r_i >= c_i, r_i <= c_i
    cum_tris = (lower.astype(BF16), upper.astype(BF16))
    dmasks = (upper, lower)
    cums = [[_split3_dot(cum_tris[d], chunk(logsig, c)) for c in range(nc)] for d in range(2)]
    cums_t = [[x.T for x in row] for row in cums]
    ext_rows = (lax.broadcasted_iota(jnp.int32, (ML_EXT - ML_DV, t), 0) == 0).astype(F32)
    lchunk = lambda a, c: a[:, c * L:(c + 1) * L]

    def running_max(a, d):
        rows = a.shape[0]
        if rows < 8:
            a = jnp.concatenate([a] * (8 // rows), axis=0)
        lane = lax.broadcasted_iota(jnp.int32, a.shape, 1)
        k = 1
        while k < L:
            if d:
                shifted = jnp.where(lane < L - k, pltpu.roll(a, L - k, axis=1), -jnp.inf)
            else:
                shifted = jnp.where(lane >= k, pltpu.roll(a, k, axis=1), -jnp.inf)
            a = jnp.maximum(a, shifted)
            k *= 2
        return a[:rows]

    hsl = lambda d, which: slice((2 * d + which) * ML_HEADS, (2 * d + which + 1) * ML_HEADS)
    a_maxes = [running_max(jnp.concatenate([lchunk(gates_t, c)[hsl(d, 0)] - cums_t[d][c][hsl(d, 1)]
                                            for c in range(nc)], axis=0), d) for d in range(2)]

    def local_part(qk_c, kb_c, vt_c, vtb_c, b_row, b_col, i_row, i_col, a_max, d):
        m_loc = b_row + a_max
        s = (qk_c * jnp.exp2(jnp.where(dmasks[d], (i_col - b_col) - a_max, -jnp.inf))).astype(BF16)
        bm = jnp.dot(vtb_c, s, preferred_element_type=F32)
        b_end = b_row[:, 0:1] if d else b_row[:, L - 1:L]
        g = b_end - b_row + i_row
        g_max = jnp.max(g, axis=1, keepdims=True)
        vw = (vt_c * jnp.exp2(g - g_max)).astype(BF16)
        u = jnp.dot(vw, kb_c, preferred_element_type=F32)
        return m_loc, bm, b_end, g_max, u

    def scan_step(qt_c, b_row, loc, c_t, m):
        m_loc, bm, b_end, g_max, u = loc
        inter = b_row + m
        m_row = jnp.maximum(inter, m_loc)
        nd = jnp.exp2(m_loc - m_row) * bm
        if c_t is not None:
            nd = nd + jnp.exp2(inter - m_row) * jnp.dot(c_t.astype(BF16), qt_c, preferred_element_type=F32)
        h_t = nd[:ML_DV] / jnp.maximum(jnp.abs(nd[ML_DV:ML_DV + 1]), jnp.exp2(-m_row))
        m_new = jnp.maximum(b_end + m, g_max)
        c_new = jnp.exp2(g_max - m_new) * u
        if c_t is not None:
            c_new = c_new + jnp.exp2(b_end + m - m_new) * c_t
        return h_t, c_new, m_new

    chains = [(hd, d) for hd in range(ML_HEADS) for d in range(2)]
    qts, b_rows, locs, state = {}, {}, {}, {}
    for hd in range(ML_HEADS):
        cols = slice(hd * ML_DK, (hd + 1) * ML_DK)
        kcols = slice(ML_W + hd * ML_DK, ML_W + (hd + 1) * ML_DK)
        vcols = slice(2 * ML_W + hd * ML_DV, 2 * ML_W + (hd + 1) * ML_DV)
        qt = z_ref[seq, cols].T.astype(BF16)
        kb = z_ref[seq, kcols].astype(BF16)
        vt = jnp.concatenate([z_ref[seq, vcols].T, ext_rows], axis=0)
        vtb = vt.astype(BF16)
        qk = [jnp.dot(chunk(kb, c), lchunk(qt, c), preferred_element_type=F32) for c in range(nc)]
        qts[hd] = qt
        for d in range(2):
            i_lane = 2 * d * ML_HEADS + hd
            f_lane = (2 * d + 1) * ML_HEADS + hd
            b_rows[hd, d] = [cums_t[d][c][f_lane:f_lane + 1, :] for c in range(nc)]
            locs[hd, d] = [local_part(qk[c], chunk(kb, c), lchunk(vt, c), lchunk(vtb, c), b_rows[hd, d][c],
                                      cums[d][c][:, f_lane:f_lane + 1], lchunk(gates_t, c)[i_lane:i_lane + 1, :],
                                      chunk(gates, c)[:, i_lane:i_lane + 1],
                                      a_maxes[d][c * ML_HEADS + hd:c * ML_HEADS + hd + 1], d) for c in range(nc)]
            if zero_init:
                state[hd, d] = (None, jnp.zeros((1, 1), F32))
            else:
                c_t = jnp.concatenate([c0_ref[d, hd].T, n0_ref[d, hd:hd + 1, :],
                                       jnp.zeros((ML_EXT - ML_DV - 1, ML_DK), F32)], axis=0)
                state[hd, d] = (c_t, m0_ref[d:d + 1, hd:hd + 1] * LOG2E)

    h_parts = {ch: [None] * nc for ch in chains}
    for step in range(nc):
        for hd, d in chains:
            c = nc - 1 - step if d else step
            c_t, m = state[hd, d]
            h_parts[hd, d][c], c_t, m = scan_step(lchunk(qts[hd], c), b_rows[hd, d][c], locs[hd, d][c], c_t, m)
            state[hd, d] = (c_t, m)

    for hd, d in chains:
        c_t, m = state[hd, d]
        c_out[bi, n_prev, d, hd] = c_t[:ML_DV].T
        n_out[bi, n_prev, d, hd:hd + 1, :] = c_t[ML_DV:ML_DV + 1]
        m_out[bi, n_prev, d:d + 1, hd:hd + 1] = m * LN2
    for hd in range(ML_HEADS):
        cols = slice(hd * ML_DK, (hd + 1) * ML_DK)
        ocols = slice(3 * ML_W + hd * ML_DV, 3 * ML_W + (hd + 1) * ML_DV)
        h_fwd, h_bwd = (jnp.concatenate(h_parts[hd, d], axis=1) if nc > 1 else h_parts[hd, d][0] for d in range(2))
        h_sum = h_fwd + h_bwd
        hm = _rms(h_sum.T, nw_ref[:, cols])
        o_ref[seq, cols] = (hm * z_ref[seq, ocols]).astype(BF16)


def _mlstm(z, b_if_l, ml_norm_l, *, b, t, bb, state=None, side=None, prev=None):
    zero_init = state is None
    n_prev = 0 if prev is None else prev[0].shape[1]
    assert b % bb == 0 and (zero_init or bb == 1)
    steps = b // bb
    zw = 5 * ML_W
    in_specs = [
        pl.BlockSpec((bb * t, zw), lambda bi: (bi, OFF_MQ // zw)),
        pl.BlockSpec((bb * t, LANES), lambda bi: (bi, OFF_IF // LANES)),
        pl.BlockSpec((1, LANES), lambda bi: (0, 0)),
        pl.BlockSpec((1, ML_W), lambda bi: (0, 0)),
    ]
    bif = jnp.pad(b_if_l, (0, LANES - N_GATES)).reshape(1, LANES)
    args = [z, z, bif, ml_norm_l.reshape(1, ML_W)]
    out_specs = [
        pl.BlockSpec((bb * t, ML_W), lambda bi: (bi, 0)),
        pl.BlockSpec((bb, n_prev + 1, 2, ML_HEADS, ML_DK, ML_DV), lambda bi: (bi, 0, 0, 0, 0, 0)),
        pl.BlockSpec((bb, n_prev + 1, 2, ML_HEADS, ML_DK), lambda bi: (bi, 0, 0, 0, 0)),
        pl.BlockSpec((bb, n_prev + 1, 2, ML_HEADS), lambda bi: (bi, 0, 0, 0)),
    ]
    out_shape = [
        jax.ShapeDtypeStruct((b * t, ML_W), BF16),
        jax.ShapeDtypeStruct((b, n_prev + 1, 2, ML_HEADS, ML_DK, ML_DV), F32),
        jax.ShapeDtypeStruct((b, n_prev + 1, 2, ML_HEADS, ML_DK), F32),
        jax.ShapeDtypeStruct((b, n_prev + 1, 2, ML_HEADS), F32),
    ]
    if side is not None:
        cond8, w_mod, b_mod, side_layer = side
        n = 3 * D_MODEL
        tn = n // steps
        assert tn % LANES == 0 and tn * steps == n
        in_specs += [
            pl.BlockSpec((8, D_MODEL), lambda bi: (0, 0)),
            pl.BlockSpec((None, D_MODEL, tn), lambda bi: (side_layer, 0, bi)),
            pl.BlockSpec((None, 1, tn), lambda bi: (side_layer, 0, bi)),
        ]
        args += [cond8, w_mod, b_mod.reshape(DEPTH, 1, n)]
        out_specs.append(pl.BlockSpec((8, tn), lambda bi: (0, bi)))
        out_shape.append(jax.ShapeDtypeStruct((8, n), F32))
    if n_prev:
        in_specs += [
            pl.BlockSpec((bb, n_prev, 2, ML_HEADS, ML_DK, ML_DV), lambda bi: (bi, 0, 0, 0, 0, 0)),
            pl.BlockSpec((bb, n_prev, 2, ML_HEADS, ML_DK), lambda bi: (bi, 0, 0, 0, 0)),
            pl.BlockSpec((bb, n_prev, 2, ML_HEADS), lambda bi: (bi, 0, 0, 0)),
        ]
        args += list(prev)
    if not zero_init:
        c0, n0, m0, layer = state
        in_specs += [
            pl.BlockSpec((None, None, 2, ML_HEADS, ML_DK, ML_DV), lambda bi: (bi, layer, 0, 0, 0, 0)),
            pl.BlockSpec((None, None, 2, ML_HEADS, ML_DK), lambda bi: (bi, layer, 0, 0, 0)),
            pl.BlockSpec((None, None, 2, ML_HEADS), lambda bi: (bi, layer, 0, 0)),
        ]
        args += [c0, n0, m0]
    kern = functools.partial(_mlstm_kernel, t=t, bb=bb, zero_init=zero_init, side_mod=side is not None, n_prev=n_prev)
    return kern, in_specs, args, out_specs, out_shape, []


def _dft_tables(t):
    def cs(n):
        idx = np.arange(n, dtype=np.int64)
        ang = 2.0 * np.pi * ((idx[:, None] * idx[None, :]) % n).astype(np.float64) / n
        return np.cos(ang), np.sin(ang)
    cc, sc = cs(FO_GC)
    ct, st = cs(t)
    return (np.concatenate([cc, sc], axis=1).astype(np.float32),
            np.concatenate([ct, -st], axis=1).astype(np.float32))


FO_RT = 256


def _fourier_kernel(zx_ref, zga_ref, zgb_ref, wc_ref, wt_ref, wf_ref, o_ref, y_ref, *, t, bb):
    wc = wc_ref[...]
    for bi in range(bb):
        x = zx_ref[bi * t:(bi + 1) * t, FO_SHIFT:FO_SHIFT + FO_W]
        for g in range(FO_GROUPS):
            cols = slice(g * FO_GC, (g + 1) * FO_GC)
            ycols = slice(bi * FO_W + g * FO_GC, bi * FO_W + (g + 1) * FO_GC)
            y = jnp.dot(x[:, cols].astype(BF16), wc, preferred_element_type=F32)
            y_ref[0:t, ycols] = y[:, :FO_GC].astype(BF16)
            y_ref[t:2 * t, ycols] = y[:, FO_GC:].astype(BF16)
    scale = (t * FO_GC) ** -0.5
    for r in range(t // FO_RT):
        rows = slice(r * FO_RT, (r + 1) * FO_RT)
        f = jnp.dot(wt_ref[rows, :], y_ref[...], preferred_element_type=F32) * scale
        for bi in range(bb):
            orows = slice(bi * t + r * FO_RT, bi * t + (r + 1) * FO_RT)
            fg = jnp.concatenate([zga_ref[orows, FO_SHIFT:], zgb_ref[orows, :FO_SHIFT]], axis=1)
            for g in range(FO_GROUPS):
                cols = slice(g * FO_GC, (g + 1) * FO_GC)
                fcols = slice(bi * FO_W + g * FO_GC, bi * FO_W + (g + 1) * FO_GC)
                og = jnp.dot(f[:, fcols].astype(BF16), wf_ref[g].astype(BF16), preferred_element_type=F32)
                o_ref[orows, cols] = (og * _silu(fg[:, cols])).astype(BF16)


def _fourier(z, z_tail, w_fno_l, *, b, t, bb):
    assert OFF_FG - FO_SHIFT + FO_W == Z_W and Z_TAIL == FO_SHIFT
    wc, wt = _dft_tables(t)
    xw = FO_W + LANES
    in_specs = [
        pl.BlockSpec((bb * t, xw), lambda bi: (bi, (OFF_FX - FO_SHIFT) // xw)),
        pl.BlockSpec((bb * t, FO_W), lambda bi: (bi, (OFF_FG - FO_SHIFT) // FO_W)),
        pl.BlockSpec((bb * t, LANES), lambda bi: (bi, 0)),
        pl.BlockSpec((FO_GC, 2 * FO_GC), lambda bi: (0, 0)),
        pl.BlockSpec((t, 2 * t), lambda bi: (0, 0)),
        pl.BlockSpec((FO_GROUPS, FO_GC, FO_GC), lambda bi: (0, 0, 0)),
    ]
    args = [z, z, z_tail, jnp.asarray(wc).astype(BF16), jnp.asarray(wt).astype(BF16), w_fno_l]
    return (functools.partial(_fourier_kernel, t=t, bb=bb), in_specs, args,
            [pl.BlockSpec((bb * t, FO_W), lambda bi: (bi, 0))], [jax.ShapeDtypeStruct((b * t, FO_W), BF16)],
            [pltpu.VMEM((2 * t, bb * FO_W), BF16)])


OUT_TN = 1024
OUT_TM = 1024


def _outproj_kernel(oa_ref, om_ref, of_ref, wa_ref, wm_ref, wf_ref, x_ref, gate_ref, y_ref, wb_ref, *, per_row_mod):
    i = pl.program_id(1)

    @pl.when(i == 0)
    def _():
        wb_ref[0:ATT_W, :] = wa_ref[...].astype(BF16)
        wb_ref[ATT_W:ATT_W + ML_W, :] = wm_ref[...].astype(BF16)
        wb_ref[ATT_W + ML_W:D_MIX, :] = wf_ref[...].astype(BF16)

    row = 1 + (i * OUT_TM) // ROW_TILE if per_row_mod else 0
    gate = gate_ref[pl.ds(row, 1), :]
    y = (jnp.dot(oa_ref[...], wb_ref[0:ATT_W, :], preferred_element_type=F32)
         + jnp.dot(om_ref[...], wb_ref[ATT_W:ATT_W + ML_W, :], preferred_element_type=F32)
         + jnp.dot(of_ref[...], wb_ref[ATT_W + ML_W:D_MIX, :], preferred_element_type=F32))
    y_ref[...] = x_ref[...] + gate * y


def _outproj(o_att, o_ml, o_fo, w_out, layer, x2d, mod_l, *, per_row_mod):
    m = x2d.shape[0]
    return pl.pallas_call(
        functools.partial(_outproj_kernel, per_row_mod=per_row_mod),
        grid=(D_MODEL // OUT_TN, m // OUT_TM),
        in_specs=[
            pl.BlockSpec((OUT_TM, ATT_W), lambda n, i: (i, 0)),
            pl.BlockSpec((OUT_TM, ML_W), lambda n, i: (i, 0)),
            pl.BlockSpec((OUT_TM, FO_W), lambda n, i: (i, 0)),
            pl.BlockSpec((None, ATT_W, OUT_TN), lambda n, i: (layer, 0, n)),
            pl.BlockSpec((None, ML_W, OUT_TN), lambda n, i: (layer, ATT_W // ML_W, n)),
            pl.BlockSpec((None, FO_W, OUT_TN), lambda n, i: (layer, (ATT_W + ML_W) // FO_W, n)),
            pl.BlockSpec((OUT_TM, OUT_TN), lambda n, i: (i, n)),
            pl.BlockSpec((8, OUT_TN), lambda n, i: (0, 2 * D_MODEL // OUT_TN + n)),
        ],
        out_specs=pl.BlockSpec((OUT_TM, OUT_TN), lambda n, i: (i, n)),
        out_shape=jax.ShapeDtypeStruct((m, D_MODEL), F32),
        scratch_shapes=[pltpu.VMEM((D_MIX, OUT_TN), BF16)],
        compiler_params=_cparams("parallel", "arbitrary"),
        name="outproj",
    )(o_att, o_ml, o_fo, w_out, w_out, w_out, x2d, mod_l)


def _rope_tables(t):
    half = HEAD_DIM // 2
    inv_freq = ROPE_BASE ** (-np.arange(0, half, 2, dtype=np.float64) / half)
    n_rows = t // GRID_W
    rows = np.repeat(np.arange(n_rows, dtype=np.float64), GRID_W)
    cols = np.tile(np.arange(GRID_W, dtype=np.float64), n_rows)
    ar = rows[:, None] * inv_freq
    ac = cols[:, None] * inv_freq
    cos = np.concatenate([np.cos(ar), np.cos(ar), np.cos(ac), np.cos(ac)], axis=1)
    sin = np.concatenate([-np.sin(ar), np.sin(ar), -np.sin(ac), np.sin(ac)], axis=1)
    return jnp.asarray(cos.astype(np.float32)), jnp.asarray(sin.astype(np.float32))


def _run_parts(parts, grid, name):
    n_in = [len(p[1]) for p in parts]
    n_out = [len(p[3]) for p in parts]
    n_scr = [len(p[5]) for p in parts]

    def split(seq, counts):
        out, k = [], 0
        for c in counts:
            out.append(list(seq[k:k + c]))
            k += c
        return out

    def body(*refs):
        ins = split(refs[:sum(n_in)], n_in)
        outs = split(refs[sum(n_in):sum(n_in) + sum(n_out)], n_out)
        scrs = split(refs[sum(n_in) + sum(n_out):], n_scr)
        for part, i, o, s in zip(parts, ins, outs, scrs):
            part[0](*i, *o, *s)

    cat = lambda k: [x for p in parts for x in p[k]]
    results = pl.pallas_call(
        body, grid=grid, in_specs=cat(1), out_specs=cat(3), out_shape=cat(4), scratch_shapes=cat(5),
        compiler_params=_cparams("parallel", *["arbitrary"] * (len(grid) - 1)), name=name,
    )(*cat(2))
    return split(results, n_out)


def _layer(x2d, mod_l, lw, layer, *, b, t, latent, rope=None, cache=None, state=None, side=None, prev=None):
    norm_w, w_in_t, b_if, conv_w, conv_b, q_norm, k_norm, ml_norm, w_fno, w_out = lw
    prev_kv, prev_state = (None, None) if prev is None else (prev[:2], prev[2:])
    z, z_tail = _inproj(x2d, mod_l, norm_w, w_in_t, layer, conv_w, conv_b, per_row_mod=latent, t=t)
    bb = 1 if latent else max(1, ML_ROWS // t)
    att, att_grid = _attention(z, q_norm, k_norm, b=b, t=t, rope=rope, cache=cache, prev_kv=prev_kv)
    mixers = [_mlstm(z, b_if, ml_norm, b=b, t=t, bb=bb, state=state, side=side, prev=prev_state),
              _fourier(z, z_tail, w_fno, b=b, t=t, bb=bb)]
    (o_att, *kv_new), = _run_parts([att], att_grid, "attention")
    (o_ml, c_f, n_f, m_f, *side_mod), (o_fo,) = _run_parts(mixers, (b // bb,), "mixers")
    y = _outproj(o_att, o_ml, o_fo, w_out, layer, x2d, mod_l, per_row_mod=latent)
    return y, (*kv_new, c_f, n_f, m_f), side_mod


def kernel(x_prompt, x_sample, cache_k, cache_v, state_C, state_n, state_m, c, c_ctx, norm_w, w_mod, b_mod, w_in,
           b_if, conv_w, conv_b, q_norm, k_norm, ml_norm, w_fno, w_out):
    bp, tp, _ = x_prompt.shape
    bs, ts, _ = x_sample.shape
    assert tp % ML_CHUNK == 0 and ts == ROW_TILE and (bp * tp) % ROW_TILE == 0 and bs + 1 <= 8
    past = cache_k.shape[2]

    cond8 = jnp.concatenate([c_ctx[None, :], c, jnp.zeros((8 - 1 - bs, D_MODEL), F32)], axis=0)
    mod_l = _modulation(cond8, w_mod, b_mod, 0)

    rope = _rope_tables(ts)
    ck = cache_k.reshape(bs, DEPTH, past, ATT_KV_W)
    cv = cache_v.reshape(bs, DEPTH, past, ATT_KV_W)
    w_in_t = jnp.swapaxes(w_in, 1, 2)

    xp = x_prompt.reshape(bp * tp, D_MODEL)
    xs = x_sample.reshape(bs * ts, D_MODEL)
    new = None
    for l in range(DEPTH):
        lw = (norm_w[l], w_in_t, b_if[l], conv_w[l], conv_b[l], q_norm[l], k_norm[l], ml_norm[l], w_fno[l], w_out)
        side = (cond8, w_mod, b_mod, l + 1) if l + 1 < DEPTH else None
        xp, new, mod_next = _layer(xp, mod_l, lw, l, b=bp, t=tp, latent=False, side=side, prev=new)
        xs, _, _ = _layer(xs, mod_l, lw, l, b=bs, t=ts, latent=True, rope=rope,
                          cache=(ck, cv, l), state=(state_C, state_n, state_m, l))
        if mod_next:
            mod_l = mod_next[0]
    return (xp.reshape(bp, tp, D_MODEL), xs.reshape(bs, ts, D_MODEL), *new)
```
